```python
import math
import jax, jax.numpy as jnp
from jax import lax
import numpy as np

D_MODEL = 2048
BATCH = 2
SEQ = 4096
DEPTH = 2

N_MIXERS = 2
A_HEADS = 8
A_HEAD_DIM = 128
A_V_DIM = 2 * A_HEAD_DIM
A_IN = 2 * A_HEADS * 2 * A_HEAD_DIM + A_HEADS * A_V_DIM
A_Q_BLOCK = 128
B_HEADS = 16
B_GROUPS = 4
B_HPG = B_HEADS // B_GROUPS
B_DK = 128
B_DV = 128
B_IN = B_HEADS * B_DK + 3 * B_GROUPS * (B_DK + B_DV) + 3 * B_HEADS
CMP_LEN = 32
CMP_STRIDE = 16
CMP_HID = 256
SLC_BLOCK = 64
SLC_TOPN = 16
WINDOW = 512
B_Q_BLOCK = 64
FORCE = 1e4
N_GROUPS = 4
EXPERTS_PER_GROUP = 8
N_EXPERTS = N_GROUPS * EXPERTS_PER_GROUP
TOP_K = 2
D_EXPERT = 1024
MOE_ROW_BLOCK = 128
EPS = 1e-6
NEG = -1e30

kernel_name = 'hybrid_diffattn_nsa_hmoe'


def rmsnorm(x, w):
    xf = x.astype(jnp.float32)
    y = xf * lax.rsqrt(jnp.mean(xf * xf, axis=-1, keepdims=True) + EPS)
    return (y * w.astype(jnp.float32)).astype(x.dtype)


def alibi_slopes(n):
    return jnp.asarray(np.array([2.0 ** (-8.0 * (i + 1) / n) for i in range(n)], dtype=np.float32))


def diff_attention(xn, w_in, lam, subln_w, w_out, lambda_init):
    B, S, _ = xn.shape
    H, dh = A_HEADS, A_HEAD_DIM
    qkv = xn @ w_in
    q, k, v = jnp.split(qkv, [H * 2 * dh, 2 * H * 2 * dh], axis=-1)
    q = q.reshape(B, S, H, 2, dh) * (dh ** -0.5)
    k = k.reshape(B, S, H, 2, dh)
    v = v.reshape(B, S, H, A_V_DIM)
    lf = lam.astype(jnp.float32)
    lmbda = jnp.exp(jnp.sum(lf[0] * lf[1])) - jnp.exp(jnp.sum(lf[2] * lf[3])) + lambda_init
    slopes = alibi_slopes(H)[:, None, None, None]
    kpos = jnp.arange(S)

    def q_block(i):
        start = i * A_Q_BLOCK
        qb = lax.dynamic_slice_in_dim(q, start, A_Q_BLOCK, axis=1)
        qpos = start + jnp.arange(A_Q_BLOCK)
        s = jnp.einsum('bqhcd,bkhcd->bhcqk', qb, k, preferred_element_type=jnp.float32)
        dist = (qpos[:, None] - kpos[None, :]).astype(jnp.float32)
        s = jnp.where(dist >= 0, s - slopes * dist, NEG)
        p = jax.nn.softmax(s, axis=-1)
        a = p[:, :, 0] - lmbda * p[:, :, 1]
        return jnp.einsum('bhqk,bkhv->bqhv', a.astype(v.dtype), v)

    o = lax.map(q_block, jnp.arange(S // A_Q_BLOCK))
    o = jnp.moveaxis(o, 0, 1).reshape(B, S, H, A_V_DIM)
    o = rmsnorm(o, subln_w) * (1.0 - lambda_init)
    return o.reshape(B, S, H * A_V_DIM) @ w_out


def nsa_attention(xn, w_in, cmp_pos, cmp_w1, cmp_w2, w_out):
    B, S, _ = xn.shape
    H, G, HPG, dk, dv = B_HEADS, B_GROUPS, B_HPG, B_DK, B_DV
    proj = xn @ w_in
    splits = np.cumsum([H * dk, G * dk, G * dv, G * dk, G * dv, G * dk, G * dv]).tolist()
    q, kc, vc, ks, vs, kw, vw, gate = jnp.split(proj, splits, axis=-1)
    q = q.reshape(B, S, G, HPG, dk) * (dk ** -0.5)
    kc, ks, kw = (t.reshape(B, S, G, dk) for t in (kc, ks, kw))
    vc, vs, vw = (t.reshape(B, S, G, dv) for t in (vc, vs, vw))
    gate = jax.nn.sigmoid(gate.astype(jnp.float32)).reshape(B, S, G, HPG, 3)
    slopes = alibi_slopes(H).reshape(G, HPG)[:, :, None, None]
    pos = jnp.arange(S)

    nc = (S - CMP_LEN) // CMP_STRIDE + 1
    blk_idx = np.arange(nc)[:, None] * CMP_STRIDE + np.arange(CMP_LEN)[None, :]

    def compress(t, pe, w1, w2):
        tb = t[:, blk_idx] + pe[None, None, :, None, :]
        tb = jnp.moveaxis(tb, 3, 2).reshape(B, nc, G, CMP_LEN * t.shape[-1])
        return jax.nn.gelu(tb @ w1) @ w2

    kcmp = compress(kc, cmp_pos[0], cmp_w1[0], cmp_w2[0])
    vcmp = compress(vc, cmp_pos[1], cmp_w1[1], cmp_w2[1])
    cmp_end = jnp.arange(nc) * CMP_STRIDE + CMP_LEN - 1
    s = jnp.einsum('bsghd,bcgd->bghsc', q, kcmp, preferred_element_type=jnp.float32)
    dist = (pos[:, None] - cmp_end[None, :]).astype(jnp.float32)
    valid = dist >= 0
    s = jnp.where(valid, s - slopes * dist, NEG)
    p_cmp = jax.nn.softmax(s, axis=-1) * jnp.any(valid, axis=-1)[:, None]
    o_cmp = jnp.einsum('bghsc,bcgd->bsghd', p_cmp.astype(vcmp.dtype), vcmp)

    nsb = S // SLC_BLOCK
    cs = np.arange(nc) * CMP_STRIDE
    ss = np.arange(nsb) * SLC_BLOCK
    overlap = np.clip(np.minimum(cs[:, None] + CMP_LEN, ss[None, :] + SLC_BLOCK)
                      - np.maximum(cs[:, None], ss[None, :]), 0, None) / CMP_LEN
    imp = jnp.einsum('bghsc,cj->bgsj', p_cmp, jnp.asarray(overlap, dtype=jnp.float32))
    jb = jnp.arange(nsb)[None, :]
    cur = (pos // SLC_BLOCK)[:, None]
    forced = (jb == 0) | (jb == cur) | (jb == cur - 1)
    causal_blk = jb * SLC_BLOCK <= pos[:, None]
    score = jnp.where(forced, FORCE, jnp.where(causal_blk, imp, -FORCE))
    n_sel = min(SLC_TOPN, nsb)
    _, sel = lax.top_k(score, n_sel)
    sel = jnp.moveaxis(sel, 1, 2)

    kw_pad = jnp.pad(kw, ((0, 0), (WINDOW, 0), (0, 0), (0, 0)))
    vw_pad = jnp.pad(vw, ((0, 0), (WINDOW, 0), (0, 0), (0, 0)))
    bidx = jnp.arange(B)[:, None, None, None]
    gidx = jnp.arange(G)[None, None, :, None]
    offs = jnp.arange(SLC_BLOCK)
    n_keys = n_sel * SLC_BLOCK

    def q_block(i):
        start = i * B_Q_BLOCK
        qb = lax.dynamic_slice_in_dim(q, start, B_Q_BLOCK, axis=1)
        qpos = start + jnp.arange(B_Q_BLOCK)
        selb = lax.dynamic_slice_in_dim(sel, start, B_Q_BLOCK, axis=1)
        kpos = (selb[..., None] * SLC_BLOCK + offs).reshape(B, B_Q_BLOCK, G, n_keys)
        ksel = ks[bidx, kpos, gidx]
        vsel = vs[bidx, kpos, gidx]
        s1 = jnp.einsum('bqghd,bqgld->bghql', qb, ksel, preferred_element_type=jnp.float32)
        d1 = jnp.moveaxis(qpos[None, :, None, None] - kpos, 2, 1).astype(jnp.float32)[:, :, None]
        s1 = jnp.where(d1 >= 0, s1 - slopes * d1, NEG)
        p1 = jax.nn.softmax(s1, axis=-1)
        o_slc = jnp.einsum('bghql,bqgld->bqghd', p1.astype(vsel.dtype), vsel)
        kwb = lax.dynamic_slice_in_dim(kw_pad, start, WINDOW + B_Q_BLOCK, axis=1)
        vwb = lax.dynamic_slice_in_dim(vw_pad, start, WINDOW + B_Q_BLOCK, axis=1)
        wpos = start - WINDOW + jnp.arange(WINDOW + B_Q_BLOCK)
        dw = qpos[:, None] - wpos[None, :]
        vis = (dw >= 0) & (dw < WINDOW) & (wpos[None, :] >= 0)
        s2 = jnp.einsum('bqghd,bkgd->bghqk', qb, kwb, preferred_element_type=jnp.float32)
        s2 = jnp.where(vis, s2 - slopes * dw.astype(jnp.float32), NEG)
        p2 = jax.nn.softmax(s2, axis=-1)
        o_win = jnp.einsum('bghqk,bkgd->bqghd', p2.astype(vwb.dtype), vwb)
        return o_slc, o_win

    o_slc, o_win = lax.map(q_block, jnp.arange(S // B_Q_BLOCK))
    o_slc = jnp.moveaxis(o_slc, 0, 1).reshape(B, S, G, HPG, dv)
    o_win = jnp.moveaxis(o_win, 0, 1).reshape(B, S, G, HPG, dv)
    o = gate[..., 0:1] * o_cmp + gate[..., 1:2] * o_slc + gate[..., 2:3] * o_win
    return o.reshape(B, S, H * dv).astype(xn.dtype) @ w_out


def hier_moe(xn, w_group, b_group, w_expert, b_expert, w_gate, w_up, w_down):
    B, S, D = xn.shape
    T = B * S
    xt = xn.reshape(T, D)
    lg = (xt @ w_group).astype(jnp.float32) + b_group.astype(jnp.float32)
    pg = jax.nn.softmax(lg, axis=-1)
    g_top = jnp.argmax(lg, axis=-1)
    wg = jnp.take_along_axis(pg, g_top[:, None], axis=-1)
    le = ((xt @ w_expert).astype(jnp.float32) + b_expert.astype(jnp.float32)).reshape(T, N_GROUPS, EXPERTS_PER_GROUP)
    le = jnp.take_along_axis(le, g_top[:, None, None], axis=1)[:, 0]
    pe = jax.nn.softmax(le, axis=-1)
    pk, ik = lax.top_k(pe, TOP_K)
    wk = wg * pk / jnp.sum(pk, axis=-1, keepdims=True)
    eid = (g_top[:, None] * EXPERTS_PER_GROUP + ik).reshape(-1)
    tok = jnp.repeat(jnp.arange(T, dtype=jnp.int32), TOP_K)
    wslot = wk.reshape(-1)
    n_slots = T * TOP_K
    RB = MOE_ROW_BLOCK
    n_rows = -(-(n_slots + N_EXPERTS * (RB - 1)) // RB) * RB
    n_blk = n_rows // RB
    order = jnp.argsort(eid)
    e_sorted = eid[order]
    counts = jnp.bincount(eid, length=N_EXPERTS)
    padded = (counts + RB - 1) // RB * RB
    start_unpadded = jnp.cumsum(counts) - counts
    end_padded = jnp.cumsum(padded)
    start_padded = end_padded - padded
    dest = start_padded[e_sorted] + (jnp.arange(n_slots) - start_unpadded[e_sorted])
    row_tok = jnp.full((n_rows,), T, dtype=jnp.int32).at[dest].set(tok[order])
    row_w = jnp.zeros((n_rows,), jnp.float32).at[dest].set(wslot[order])
    blk_expert = jnp.clip(jnp.searchsorted(end_padded, jnp.arange(n_blk) * RB, side='right'), 0, N_EXPERTS - 1)
    x_pad = jnp.concatenate([xt, jnp.zeros((1, D), xt.dtype)], axis=0)

    def run(args):
        rows, e = args
        xr = x_pad[rows]
        h = jax.nn.silu(xr @ w_gate[e]) * (xr @ w_up[e])
        return h @ w_down[e]

    y = lax.map(run, (row_tok.reshape(n_blk, RB), blk_expert)).reshape(n_rows, D)
    y = y * row_w[:, None].astype(y.dtype)
    out = jax.ops.segment_sum(y, row_tok, num_segments=T + 1)[:T]
    return out.reshape(B, S, D).astype(xn.dtype)


def setup_inputs(seed: int = 0) -> dict:
    key = jax.random.key(seed)
    ks = jax.random.split(key, 20)
    n_a = (DEPTH + 1) // 2
    n_b = DEPTH // 2
    f = jnp.float32

    def nrm(k, shape, scale):
        return jax.random.normal(k, shape, f) * scale

    return {
        'x': nrm(ks[0], (BATCH, SEQ, D_MODEL), 1.0),
        'attn_norm_w': 1.0 + nrm(ks[1], (DEPTH, D_MODEL), 0.02),
        'ffn_norm_w': 1.0 + nrm(ks[2], (DEPTH, D_MODEL), 0.02),
        'final_norm_w': 1.0 + nrm(ks[3], (D_MODEL,), 0.02),
        'a_w_in': nrm(ks[4], (n_a, D_MODEL, A_IN), D_MODEL ** -0.5),
        'a_lambda': nrm(ks[5], (n_a, 4, A_HEAD_DIM), 0.1),
        'a_subln_w': 1.0 + nrm(ks[6], (n_a, A_V_DIM), 0.02),
        'a_w_out': nrm(ks[7], (n_a, A_HEADS * A_V_DIM, D_MODEL), (A_HEADS * A_V_DIM) ** -0.5),
        'b_w_in': nrm(ks[8], (n_b, D_MODEL, B_IN), D_MODEL ** -0.5),
        'b_cmp_pos': nrm(ks[9], (n_b, 2, CMP_LEN, B_DK), 0.1),
        'b_cmp_w1': nrm(ks[10], (n_b, 2, CMP_LEN * B_DK, CMP_HID), (CMP_LEN * B_DK) ** -0.5),
        'b_cmp_w2': nrm(ks[11], (n_b, 2, CMP_HID, B_DK), CMP_HID ** -0.5),
        'b_w_out': nrm(ks[12], (n_b, B_HEADS * B_DV, D_MODEL), (B_HEADS * B_DV) ** -0.5),
        'moe_w_group': nrm(ks[13], (DEPTH, D_MODEL, N_GROUPS), D_MODEL ** -0.5),
        'moe_b_group': nrm(ks[14], (DEPTH, N_GROUPS), 0.01),
        'moe_w_expert': nrm(ks[15], (DEPTH, D_MODEL, N_EXPERTS), D_MODEL ** -0.5),
        'moe_b_expert': nrm(ks[16], (DEPTH, N_EXPERTS), 0.01),
        'moe_w_gate': nrm(ks[17], (DEPTH, N_EXPERTS, D_MODEL, D_EXPERT), D_MODEL ** -0.5),
        'moe_w_up': nrm(ks[18], (DEPTH, N_EXPERTS, D_MODEL, D_EXPERT), D_MODEL ** -0.5),
        'moe_w_down': nrm(ks[19], (DEPTH, N_EXPERTS, D_EXPERT, D_MODEL), D_EXPERT ** -0.5),
    }


def reference(x, attn_norm_w, ffn_norm_w, final_norm_w, a_w_in, a_lambda, a_subln_w, a_w_out,
              b_w_in, b_cmp_pos, b_cmp_w1, b_cmp_w2, b_w_out, moe_w_group, moe_b_group,
              moe_w_expert, moe_b_expert, moe_w_gate, moe_w_up, moe_w_down):
    h = x
    for i in range(DEPTH):
        xn = rmsnorm(h, attn_norm_w[i])
        j = i // N_MIXERS
        if i % N_MIXERS == 0:
            lambda_init = 0.8 - 0.6 * math.exp(-0.3 * i)
            h = h + diff_attention(xn, a_w_in[j], a_lambda[j], a_subln_w[j], a_w_out[j], lambda_init)
        else:
            h = h + nsa_attention(xn, b_w_in[j], b_cmp_pos[j], b_cmp_w1[j], b_cmp_w2[j], b_w_out[j])
        xn = rmsnorm(h, ffn_norm_w[i])
        h = h + hier_moe(xn, moe_w_group[i], moe_b_group[i], moe_w_expert[i], moe_b_expert[i],
                         moe_w_gate[i], moe_w_up[i], moe_w_down[i])
    return rmsnorm(h, final_norm_w)
```

```python
import functools
import math

import numpy as np
import jax
import jax.numpy as jnp
from jax import lax
from jax.experimental import pallas as pl
from jax.experimental.pallas import tpu as pltpu

F32 = jnp.float32
BF16 = jnp.bfloat16
I32 = jnp.int32

EPS = 1e-6
NEG = -1e30
FORCE = 1e4

A_HEADS = 8
A_DH = 128
A_DV = 256
B_HEADS = 16
B_GROUPS = 4
B_HPG = 4
B_D = 128
CMP_LEN = 32
CMP_STRIDE = 16
SLC_BLOCK = 64
SLC_TOPN = 16
WINDOW = 512
N_GROUPS = 4
EPG = 8
N_EXPERTS = 32
D_EXPERT = 1024
ROUTER_LANES = 128
MOE_ROWS = 256

VMEM_LIMIT = 56 * 1024 * 1024


def _cparams(sem):
    return pltpu.CompilerParams(dimension_semantics=sem, vmem_limit_bytes=VMEM_LIMIT)


def _alibi_slopes(n):
    return jnp.asarray(np.array([2.0 ** (-8.0 * (i + 1) / n) for i in range(n)], dtype=np.float32))


def _nt_dot(a, b):
    return lax.dot_general(a, b, (((1,), (1,)), ((), ())), preferred_element_type=F32)


def _rms_kernel(x_ref, w_ref, o_ref):
    x = x_ref[...]
    y = x * lax.rsqrt(jnp.mean(x * x, axis=-1, keepdims=True) + EPS)
    o_ref[...] = (y * w_ref[...]).astype(o_ref.dtype)


def _rmsnorm(x, w, out_dtype, tm=512):
    T, D = x.shape
    return pl.pallas_call(
        _rms_kernel,
        out_shape=jax.ShapeDtypeStruct((T, D), out_dtype),
        grid=(T // tm,),
        in_specs=[pl.BlockSpec((tm, D), lambda i: (i, 0)),
                  pl.BlockSpec((1, D), lambda i: (0, 0))],
        out_specs=pl.BlockSpec((tm, D), lambda i: (i, 0)),
        compiler_params=_cparams(("parallel",)),
        name="rmsnorm",
    )(x, w.reshape(1, D))


def _mm_kernel(*refs, has_scale, has_resid):
    a_ref, w_ref = refs[0], refs[1]
    k = 2
    scale_ref = resid_ref = None
    if has_scale:
        scale_ref = refs[k]
        k += 1
    if has_resid:
        resid_ref = refs[k]
        k += 1
    o_ref, wbf_ref = refs[k], refs[k + 1]

    @pl.when(pl.program_id(1) == 0)
    def _():
        wbf_ref[...] = w_ref[...].astype(BF16)

    acc = jnp.dot(a_ref[...], wbf_ref[...], preferred_element_type=F32)
    if has_scale:
        acc = acc * scale_ref[...]
    if has_resid:
        acc = acc + resid_ref[...]
    o_ref[...] = acc.astype(o_ref.dtype)


def _matmul(a, w, n_out, out_dtype, *, col_scale=None, resid=None, tm=1024, tn=512):
    M, K = a.shape
    assert M % tm == 0 and n_out % tn == 0 and w.shape[0] == K and w.shape[1] >= n_out
    in_specs = [pl.BlockSpec((tm, K), lambda j, i: (i, 0)),
                pl.BlockSpec((K, tn), lambda j, i: (0, j))]
    args = [a, w]
    if col_scale is not None:
        in_specs.append(pl.BlockSpec((1, tn), lambda j, i: (0, j)))
        args.append(col_scale.reshape(1, n_out))
    if resid is not None:
        in_specs.append(pl.BlockSpec((tm, tn), lambda j, i: (i, j)))
        args.append(resid)
    return pl.pallas_call(
        functools.partial(_mm_kernel, has_scale=col_scale is not None, has_resid=resid is not None),
        out_shape=jax.ShapeDtypeStruct((M, n_out), out_dtype),
        grid=(n_out // tn, M // tm),
        in_specs=in_specs,
        out_specs=pl.BlockSpec((tm, tn), lambda j, i: (i, j)),
        scratch_shapes=[pltpu.VMEM((K, tn), BF16)],
        compiler_params=_cparams(("parallel", "arbitrary")),
        name="matmul",
    )(*args)


def _causal_pairs(n):
    qi, kj = [], []
    for i in range(n):
        for j in range(i + 1):
            qi.append(i)
            kj.append(j)
    return jnp.asarray(qi, I32), jnp.asarray(kj, I32)


def _rel_matrix(tq, tk):
    return jnp.asarray(np.arange(tq)[:, None] - np.arange(tk)[None, :], F32)


def _diff_attn_kernel(qi_ref, kj_ref, slopes_ref, q1_ref, q2_ref, k1_ref, k2_ref, v_ref, rel_ref,
                      lam_ref, sw_ref, o_ref, m_ref, l_ref, acc_ref, *, tq, lambda_init):
    h = pl.program_id(1)
    p = pl.program_id(2)
    i = qi_ref[p]
    j = kj_ref[p]
    slope = slopes_ref[h]

    @pl.when(j == 0)
    def _():
        m_ref[...] = jnp.full(m_ref.shape, NEG, F32)
        l_ref[...] = jnp.zeros(l_ref.shape, F32)
        acc_ref[...] = jnp.zeros(acc_ref.shape, F32)

    def step(masked):
        dist = rel_ref[...] + ((i - j) * tq).astype(F32)
        bias = slope * dist
        v = v_ref[...]
        for c, (q_ref, k_ref) in enumerate(((q1_ref, k1_ref), (q2_ref, k2_ref))):
            s = _nt_dot(q_ref[...], k_ref[...]) - bias
            if masked:
                s = jnp.where(dist >= 0, s, NEG)
            m_old = m_ref[c]
            m_new = jnp.maximum(m_old, jnp.max(s, axis=-1, keepdims=True))
            alpha = jnp.exp(m_old - m_new)
            e = jnp.exp(s - m_new)
            l_ref[c] = alpha * l_ref[c] + jnp.sum(e, axis=-1, keepdims=True)
            acc_ref[c] = alpha * acc_ref[c] + jnp.dot(e.astype(BF16), v, preferred_element_type=F32)
            m_ref[c] = m_new

    @pl.when(j < i)
    def _():
        step(False)

    @pl.when(j == i)
    def _():
        step(True)
        lam = lam_ref[...]
        lmbda = (jnp.exp(jnp.sum(lam[0:1] * lam[1:2], axis=-1, keepdims=True))
                 - jnp.exp(jnp.sum(lam[2:3] * lam[3:4], axis=-1, keepdims=True)) + lambda_init)
        o = acc_ref[0] / l_ref[0] - lmbda * (acc_ref[1] / l_ref[1])
        y = o * lax.rsqrt(jnp.mean(o * o, axis=-1, keepdims=True) + EPS)
        y = (y * sw_ref[...]) * (1.0 - lambda_init)
        o_ref[...] = y.astype(o_ref.dtype)


def _diff_attention(qkv, lam, subln_w, lambda_init, B, S, tq=512):
    T = B * S
    H, dh, dv = A_HEADS, A_DH, A_DV
    n = S // tq
    qi, kj = _causal_pairs(n)
    slopes = _alibi_slopes(H)
    kb = H * 2
    vb = (2 * H * 2 * dh) // dv
    grid_spec = pltpu.PrefetchScalarGridSpec(
        num_scalar_prefetch=3,
        grid=(B, H, int(qi.shape[0])),
        in_specs=[
            pl.BlockSpec((tq, dh), lambda b, h, p, qi, kj, sl: (b * n + qi[p], 2 * h)),
            pl.BlockSpec((tq, dh), lambda b, h, p, qi, kj, sl: (b * n + qi[p], 2 * h + 1)),
            pl.BlockSpec((tq, dh), lambda b, h, p, qi, kj, sl: (b * n + kj[p], kb + 2 * h)),
            pl.BlockSpec((tq, dh), lambda b, h, p, qi, kj, sl: (b * n + kj[p], kb + 2 * h + 1)),
            pl.BlockSpec((tq, dv), lambda b, h, p, qi, kj, sl: (b * n + kj[p], vb + h)),
            pl.BlockSpec((tq, tq), lambda b, h, p, qi, kj, sl: (0, 0)),
            pl.BlockSpec((4, dh), lambda b, h, p, qi, kj, sl: (0, 0)),
            pl.BlockSpec((1, dv), lambda b, h, p, qi, kj, sl: (0, 0)),
        ],
        out_specs=pl.BlockSpec((tq, dv), lambda b, h, p, qi, kj, sl: (b * n + qi[p], h)),
        scratch_shapes=[pltpu.VMEM((2, tq, 1), F32), pltpu.VMEM((2, tq, 1), F32),
                        pltpu.VMEM((2, tq, dv), F32)],
    )
    return pl.pallas_call(
        functools.partial(_diff_attn_kernel, tq=tq, lambda_init=lambda_init),
        out_shape=jax.ShapeDtypeStruct((T, H * dv), BF16),
        grid_spec=grid_spec,
        compiler_params=_cparams(("parallel", "parallel", "arbitrary")),
        name="diff_attention",
    )(qi, kj, slopes, qkv, qkv, qkv, qkv, qkv, _rel_matrix(tq, tq), lam, subln_w.reshape(1, dv))


def _router_kernel(h_ref, nw_ref, wr_ref, br_ref, tri_ref, xn_ref, ids_ref, wts_ref, cnt_ref, base_ref):
    @pl.when(pl.program_id(0) == 0)
    def _():
        base_ref[...] = jnp.zeros(base_ref.shape, F32)

    x = h_ref[...]
    xn = (x * lax.rsqrt(jnp.mean(x * x, axis=-1, keepdims=True) + EPS)) * nw_ref[...]
    xn_ref[...] = xn
    logits = jnp.dot(xn, wr_ref[...], preferred_element_type=F32,
                     precision=lax.Precision.HIGHEST) + br_ref[...]
    lane = lax.broadcasted_iota(I32, logits.shape, 1)
    big = jnp.int32(1 << 20)
    is_g = lane < N_GROUPS
    is_e = (lane >= N_GROUPS) & (lane < N_GROUPS + N_EXPERTS)

    lg = jnp.where(is_g, logits, NEG)
    mg = jnp.max(lg, axis=-1, keepdims=True)
    g_top = jnp.min(jnp.where(lg == mg, lane, big), axis=-1, keepdims=True)
    wg = 1.0 / jnp.sum(jnp.where(is_g, jnp.exp(lg - mg), 0.0), axis=-1, keepdims=True)

    in_grp = is_e & (lax.shift_right_logical(lane - N_GROUPS, 3) == g_top)
    le = jnp.where(in_grp, logits, NEG)
    me = jnp.max(le, axis=-1, keepdims=True)
    ee = jnp.where(in_grp, jnp.exp(le - me), 0.0)
    pe = ee / jnp.sum(ee, axis=-1, keepdims=True)
    pe = jnp.where(in_grp, pe, -1.0)
    p1 = jnp.max(pe, axis=-1, keepdims=True)
    i1 = jnp.min(jnp.where(pe == p1, lane, big), axis=-1, keepdims=True)
    pe2 = jnp.where(lane == i1, -1.0, pe)
    p2 = jnp.max(pe2, axis=-1, keepdims=True)
    i2 = jnp.min(jnp.where(pe2 == p2, lane, big), axis=-1, keepdims=True)
    den = p1 + p2
    w0 = wg * p1 / den
    w1 = wg * p2 / den

    oh1 = lane == i1
    oh2 = lane == i2
    both = jnp.where(oh1 | oh2, 1.0, 0.0)
    before = jnp.dot(tri_ref[...], both.astype(BF16), preferred_element_type=F32) + base_ref[...]
    pos0 = jnp.sum(jnp.where(oh1, before, 0.0), axis=-1, keepdims=True).astype(I32)
    pos1 = jnp.sum(jnp.where(oh2, before, 0.0), axis=-1, keepdims=True).astype(I32)
    total = base_ref[...] + jnp.sum(both, axis=0, keepdims=True)
    base_ref[...] = total
    cnt_ref[...] = total

    ids_ref[...] = jnp.where(lane == 0, i1 - N_GROUPS,
                             jnp.where(lane == 1, i2 - N_GROUPS,
                                       jnp.where(lane == 2, pos0, jnp.where(lane == 3, pos1, 0))))
    wts_ref[...] = jnp.where(lane == 0, w0, jnp.where(lane == 1, w1, 0.0))


def _router(h, norm_w, w_group, b_group, w_expert, b_expert, tm=512):
    T, D = h.shape
    pad = ROUTER_LANES - N_GROUPS - N_EXPERTS
    wr = jnp.concatenate([w_group, w_expert, jnp.zeros((D, pad), F32)], axis=1)
    br = jnp.concatenate([b_group, b_expert, jnp.zeros((pad,), F32)]).reshape(1, ROUTER_LANES)
    tri = jnp.asarray(np.tril(np.ones((tm, tm), np.float32), -1), BF16)
    return pl.pallas_call(
        _router_kernel,
        out_shape=(jax.ShapeDtypeStruct((T, D), F32),
                   jax.ShapeDtypeStruct((T, ROUTER_LANES), I32),
                   jax.ShapeDtypeStruct((T, ROUTER_LANES), F32),
                   jax.ShapeDtypeStruct((1, ROUTER_LANES), F32)),
        grid=(T // tm,),
        in_specs=[pl.BlockSpec((tm, D), lambda i: (i, 0)),
                  pl.BlockSpec((1, D), lambda i: (0, 0)),
                  pl.BlockSpec((D, ROUTER_LANES), lambda i: (0, 0)),
                  pl.BlockSpec((1, ROUTER_LANES), lambda i: (0, 0)),
                  pl.BlockSpec((tm, tm), lambda i: (0, 0))],
        out_specs=(pl.BlockSpec((tm, D), lambda i: (i, 0)),
                   pl.BlockSpec((tm, ROUTER_LANES), lambda i: (i, 0)),
                   pl.BlockSpec((tm, ROUTER_LANES), lambda i: (i, 0)),
                   pl.BlockSpec((1, ROUTER_LANES), lambda i: (0, 0))),
        scratch_shapes=[pltpu.VMEM((1, ROUTER_LANES), F32)],
        compiler_params=_cparams(("arbitrary",)),
        name="moe_router",
    )(h, norm_w.reshape(1, D), wr, br, tri)


def _dispatch_kernel(dest_ref, xn_ref, xs_in_ref, xs_ref, sem, *, tm):
    del xs_in_ref
    base = pl.program_id(0) * (2 * tm)

    def row_copy(t, slot):
        d = dest_ref[base + 2 * t + slot]
        return pltpu.make_async_copy(xn_ref.at[pl.ds(t, 1)], xs_ref.at[pl.ds(d, 1)], sem)

    def start(t, c):
        row_copy(t, 0).start()
        row_copy(t, 1).start()
        return c

    def wait(t, c):
        row_copy(t, 0).wait()
        row_copy(t, 1).wait()
        return c

    lax.fori_loop(0, tm, start, 0)
    lax.fori_loop(0, tm, wait, 0)


def _dispatch(xn, dest_flat, n_rows, tm=256):
    T, D = xn.shape
    grid_spec = pltpu.PrefetchScalarGridSpec(
        num_scalar_prefetch=1,
        grid=(T // tm,),
        in_specs=[pl.BlockSpec((tm, D), lambda i, d: (i, 0)),
                  pl.BlockSpec(memory_space=pl.ANY)],
        out_specs=pl.BlockSpec(memory_space=pl.ANY),
        scratch_shapes=[pltpu.SemaphoreType.DMA],
    )
    return pl.pallas_call(
        functools.partial(_dispatch_kernel, tm=tm),
        out_shape=jax.ShapeDtypeStruct((n_rows, D), F32),
        grid_spec=grid_spec,
        input_output_aliases={2: 0},
        compiler_params=_cparams(("arbitrary",)),
        name="moe_dispatch",
    )(dest_flat, xn, jnp.zeros((n_rows, D), F32))


def _blk(r, nu):
    return jnp.minimum(r, nu[0] - 1)


def _expert_changed(r, be_ref):
    prev = be_ref[jnp.maximum(r - 1, 0)]
    return (r == 0) | (be_ref[r] != prev)


def _gate_up_kernel(be_ref, nu_ref, x_ref, wg_ref, wu_ref, h_ref, wgb_ref, wub_ref):
    r = pl.program_id(1)

    @pl.when(r < nu_ref[0])
    def _():
        @pl.when(_expert_changed(r, be_ref))
        def _():
            wgb_ref[...] = wg_ref[...].astype(BF16)
            wub_ref[...] = wu_ref[...].astype(BF16)

        x = x_ref[...].astype(BF16)
        g = jnp.dot(x, wgb_ref[...], preferred_element_type=F32)
        u = jnp.dot(x, wub_ref[...], preferred_element_type=F32)
        h_ref[...] = (jax.nn.silu(g) * u).astype(h_ref.dtype)

    @pl.when(r >= nu_ref[0])
    def _():
        h_ref[...] = jnp.zeros(h_ref.shape, h_ref.dtype)


def _down_kernel(be_ref, nu_ref, h_ref, wd_ref, y_ref, wdb_ref):
    r = pl.program_id(1)

    @pl.when(r < nu_ref[0])
    def _():
        @pl.when(_expert_changed(r, be_ref))
        def _():
            wdb_ref[...] = wd_ref[...].astype(BF16)

        y_ref[...] = jnp.dot(h_ref[...], wdb_ref[...], preferred_element_type=F32)

    @pl.when(r >= nu_ref[0])
    def _():
        y_ref[...] = jnp.zeros(y_ref.shape, y_ref.dtype)


def _expert_mlp(xs, blk_expert, n_used, w_gate, w_up, w_down, layer, tf=512, tn=1024):
    n_rows, D = xs.shape
    RB = MOE_ROWS
    n_blk = n_rows // RB
    F = D_EXPERT
    h = pl.pallas_call(
        _gate_up_kernel,
        out_shape=jax.ShapeDtypeStruct((n_rows, F), BF16),
        grid_spec=pltpu.PrefetchScalarGridSpec(
            num_scalar_prefetch=2,
            grid=(F // tf, n_blk),
            in_specs=[
                pl.BlockSpec((RB, D), lambda f, r, be, nu: (_blk(r, nu), 0)),
                pl.BlockSpec((None, None, D, tf), lambda f, r, be, nu: (layer, be[_blk(r, nu)], 0, f)),
                pl.BlockSpec((None, None, D, tf), lambda f, r, be, nu: (layer, be[_blk(r, nu)], 0, f)),
            ],
            out_specs=pl.BlockSpec((RB, tf), lambda f, r, be, nu: (r, f)),
            scratch_shapes=[pltpu.VMEM((D, tf), BF16), pltpu.VMEM((D, tf), BF16)],
        ),
        compiler_params=_cparams(("arbitrary", "arbitrary")),
        name="moe_gate_up",
    )(blk_expert, n_used, xs, w_gate, w_up)
    y = pl.pallas_call(
        _down_kernel,
        out_shape=jax.ShapeDtypeStruct((n_rows, D), F32),
        grid_spec=pltpu.PrefetchScalarGridSpec(
            num_scalar_prefetch=2,
            grid=(D // tn, n_blk),
            in_specs=[
                pl.BlockSpec((RB, F), lambda n, r, be, nu: (_blk(r, nu), 0)),
                pl.BlockSpec((None, None, F, tn), lambda n, r, be, nu: (layer, be[_blk(r, nu)], 0, n)),
            ],
            out_specs=pl.BlockSpec((RB, tn), lambda n, r, be, nu: (r, n)),
            scratch_shapes=[pltpu.VMEM((F, tn), BF16)],
        ),
        compiler_params=_cparams(("arbitrary", "arbitrary")),
        name="moe_down",
    )(blk_expert, n_used, h, w_down)
    return y


def _combine_kernel(dest_ref, resid_ref, wts_ref, y_ref, o_ref, buf_ref, sem, *, tm):
    base = pl.program_id(0) * (2 * tm)

    def row_copy(t, slot):
        d = dest_ref[base + 2 * t + slot]
        return pltpu.make_async_copy(y_ref.at[pl.ds(d, 1)], buf_ref.at[slot, pl.ds(t, 1)], sem)

    def start(t, c):
        row_copy(t, 0).start()
        row_copy(t, 1).start()
        return c

    def wait(t, c):
        row_copy(t, 0).wait()
        row_copy(t, 1).wait()
        return c

    lax.fori_loop(0, tm, start, 0)
    lax.fori_loop(0, tm, wait, 0)
    w = wts_ref[...]
    o_ref[...] = resid_ref[...] + (w[:, 0:1] * buf_ref[0] + w[:, 1:2] * buf_ref[1])


def _combine(resid, wts, y, dest_flat, tm=128):
    T, D = resid.shape
    grid_spec = pltpu.PrefetchScalarGridSpec(
        num_scalar_prefetch=1,
        grid=(T // tm,),
        in_specs=[pl.BlockSpec((tm, D), lambda i, d: (i, 0)),
                  pl.BlockSpec((tm, ROUTER_LANES), lambda i, d: (i, 0)),
                  pl.BlockSpec(memory_space=pl.ANY)],
        out_specs=pl.BlockSpec((tm, D), lambda i, d: (i, 0)),
        scratch_shapes=[pltpu.VMEM((2, tm, D), F32), pltpu.SemaphoreType.DMA],
    )
    return pl.pallas_call(
        functools.partial(_combine_kernel, tm=tm),
        out_shape=jax.ShapeDtypeStruct((T, D), F32),
        grid_spec=grid_spec,
        compiler_params=_cparams(("arbitrary",)),
        name="moe_combine",
    )(dest_flat, resid, wts, y)


def _hier_moe(h, norm_w, w_group, b_group, w_expert, b_expert, w_gate, w_up, w_down, layer):
    T, D = h.shape
    RB = MOE_ROWS
    n_blk = (T * 2) // RB + N_EXPERTS
    n_rows = n_blk * RB
    xn, ids, wts, cnt = _router(h, norm_w, w_group, b_group, w_expert, b_expert)
    counts = cnt[0, N_GROUPS:N_GROUPS + N_EXPERTS].astype(I32)
    padded = (counts + RB - 1) // RB * RB
    end_padded = jnp.cumsum(padded)
    start_padded = end_padded - padded
    dest = (start_padded[ids[:, 0:2]] + ids[:, 2:4]).reshape(-1)
    n_used = (end_padded[-1:] // RB).astype(I32)
    blk_start = jnp.arange(n_blk, dtype=I32) * RB
    blk_expert = jnp.minimum(jnp.sum((end_padded[None, :] <= blk_start[:, None]).astype(I32), axis=1),
                             N_EXPERTS - 1)
    xs = _dispatch(xn, dest, n_rows)
    y = _expert_mlp(xs, blk_expert, n_used, w_gate, w_up, w_down, layer)
    return _combine(h, wts, y, dest)


def _compress_kernel(x_ref, w1_ref, w2_ref, pe_ref, o_ref):
    half = (CMP_LEN // 2) * B_D
    x = x_ref[...]
    w1 = w1_ref[...].astype(BF16)
    a = jnp.dot(x, w1[:half], preferred_element_type=F32)
    b = jnp.dot(x, w1[half:], preferred_element_type=F32)
    pe = jnp.broadcast_to(pe_ref[...], (8, 2 * half)).astype(BF16)
    pt = jnp.dot(pe, w1, preferred_element_type=F32)[0:1]
    hid = a + pltpu.roll(b, b.shape[0] - 1, 0) + pt
    g = jax.nn.gelu(hid)
    o_ref[...] = jnp.dot(g.astype(BF16), w2_ref[...].astype(BF16),
                         preferred_element_type=F32).astype(o_ref.dtype)


def _compress(tb, w1, w2, pe_flat):
    _, B, nch, width = tb.shape
    G, d = B_GROUPS, B_D
    hid = w1.shape[-1]
    return pl.pallas_call(
        _compress_kernel,
        out_shape=jax.ShapeDtypeStruct((2, B, G, nch, d), BF16),
        grid=(2, B, G),
        in_specs=[pl.BlockSpec((None, None, nch, width), lambda kv, b, g: (kv * G + g, b, 0, 0)),
                  pl.BlockSpec((None, 2 * width, hid), lambda kv, b, g: (kv, 0, 0)),
                  pl.BlockSpec((None, hid, d), lambda kv, b, g: (kv, 0, 0)),
                  pl.BlockSpec((None, 1, 2 * width), lambda kv, b, g: (kv, 0, 0))],
        out_specs=pl.BlockSpec((None, None, None, nch, d), lambda kv, b, g: (kv, b, g, 0, 0)),
        compiler_params=_cparams(("arbitrary", "arbitrary", "arbitrary")),
        name="nsa_compress",
    )(tb, w1, w2, pe_flat)


def _cmp_select_kernel(slopes_ref, q_ref, kc_ref, vc_ref, ov_ref, o_ref, sel_ref, *, tq, n_cmp, n_blocks):
    g = pl.program_id(1)
    i = pl.program_id(2)
    d = B_D
    nch = kc_ref.shape[0]
    pos = i * tq + lax.broadcasted_iota(I32, (tq, 1), 0)
    c = lax.broadcasted_iota(I32, (1, nch), 1)
    dist_i = pos - (c * CMP_STRIDE + (CMP_LEN - 1))
    valid = (dist_i >= 0) & (c < n_cmp)
    dist = dist_i.astype(F32)
    any_valid = jnp.where(pos >= CMP_LEN - 1, 1.0, 0.0)
    kc = kc_ref[...]
    vc = vc_ref[...]
    psum = jnp.zeros((tq, nch), F32)
    outs = []
    for hh in range(B_HPG):
        s = _nt_dot(q_ref[:, hh * d:(hh + 1) * d], kc)
        s = jnp.where(valid, s - slopes_ref[g * B_HPG + hh] * dist, NEG)
        e = jnp.exp(s - jnp.max(s, axis=-1, keepdims=True))
        p = (e / jnp.sum(e, axis=-1, keepdims=True)) * any_valid
        outs.append(jnp.dot(p.astype(BF16), vc, preferred_element_type=F32))
        psum = psum + p
    o_ref[...] = jnp.concatenate(outs, axis=1)

    p_hi = psum.astype(BF16)
    p_lo = (psum - p_hi.astype(F32)).astype(BF16)
    ov = ov_ref[...]
    imp = jnp.dot(p_hi, ov, preferred_element_type=F32) + jnp.dot(p_lo, ov, preferred_element_type=F32)

    lane = lax.broadcasted_iota(I32, imp.shape, 1)
    cur = lax.shift_right_logical(pos, 6)
    forced = (lane == 0) | (lane == cur) | (lane == cur - 1)
    causal = lane * SLC_BLOCK <= pos
    score = jnp.where(forced, FORCE, jnp.where(causal, imp, -FORCE))
    score = jnp.where(lane < n_blocks, score, -2.0 * FORCE)
    rank = jnp.zeros(imp.shape, F32)
    for jp in range(n_blocks):
        col = score[:, jp:jp + 1]
        beats = (col > score) | ((col == score) & (lane > jp))
        rank = rank + jnp.where(beats, 1.0, 0.0)
    sel = (rank < float(SLC_TOPN)) & (lane < n_blocks)
    sel_ref[...] = jnp.where(sel, 1.0, 0.0).astype(sel_ref.dtype)


def _cmp_select(projm, cmp_kv, overlap, B, S, tq=256):
    T = B * S
    G, d = B_GROUPS, B_D
    nq = S // tq
    nch = cmp_kv.shape[3]
    n_cmp = (S - CMP_LEN) // CMP_STRIDE + 1
    n_blocks = S // SLC_BLOCK
    assert SLC_TOPN <= n_blocks <= 128
    slopes = _alibi_slopes(B_HEADS)
    grid_spec = pltpu.PrefetchScalarGridSpec(
        num_scalar_prefetch=1,
        grid=(B, G, nq),
        in_specs=[pl.BlockSpec((tq, B_HPG * d), lambda b, g, i, sl: (b * nq + i, g)),
                  pl.BlockSpec((None, None, None, nch, d), lambda b, g, i, sl: (0, b, g, 0, 0)),
                  pl.BlockSpec((None, None, None, nch, d), lambda b, g, i, sl: (1, b, g, 0, 0)),
                  pl.BlockSpec((nch, 128), lambda b, g, i, sl: (0, 0))],
        out_specs=(pl.BlockSpec((tq, B_HPG * d), lambda b, g, i, sl: (b * nq + i, g)),
                   pl.BlockSpec((None, None, tq, 128), lambda b, g, i, sl: (b, g, i, 0))),
    )
    return pl.pallas_call(
        functools.partial(_cmp_select_kernel, tq=tq, n_cmp=n_cmp, n_blocks=n_blocks),
        out_shape=(jax.ShapeDtypeStruct((T, B_HEADS * d), F32),
                   jax.ShapeDtypeStruct((B, G, S, 128), BF16)),
        grid_spec=grid_spec,
        compiler_params=_cparams(("parallel", "parallel", "arbitrary")),
        name="nsa_cmp_select",
    )(slopes, projm, cmp_kv, cmp_kv, overlap)


def _stack_heads(q_ref, q4_ref, tq):
    for hh in range(B_HPG):
        q4_ref[hh * tq:(hh + 1) * tq, :] = q_ref[:, hh * B_D:(hh + 1) * B_D]


def _row_slopes(slopes_ref, g, tq):
    r = lax.broadcasted_iota(I32, (B_HPG * tq, 1), 0)
    out = jnp.zeros((B_HPG * tq, 1), F32)
    for hh in range(B_HPG):
        out = jnp.where((r >= hh * tq) & (r < (hh + 1) * tq), slopes_ref[g * B_HPG + hh], out)
    return out


def _online_softmax_step(s, v, m_ref, l_ref, acc_ref):
    m_old = m_ref[...]
    m_new = jnp.maximum(m_old, jnp.max(s, axis=-1, keepdims=True))
    alpha = jnp.exp(m_old - m_new)
    e = jnp.exp(s - m_new)
    l_ref[...] = alpha * l_ref[...] + jnp.sum(e, axis=-1, keepdims=True)
    acc_ref[...] = alpha * acc_ref[...] + jnp.dot(e.astype(BF16), v, preferred_element_type=F32)
    m_ref[...] = m_new


def _write_heads(o_ref, acc_ref, l_ref, tq):
    o = acc_ref[...] / l_ref[...]
    for hh in range(B_HPG):
        o_ref[:, hh * B_D:(hh + 1) * B_D] = o[hh * tq:(hh + 1) * tq]


def _slc_attn_kernel(qi_ref, kj_ref, slopes_ref, q_ref, k_ref, v_ref, sel_ref, exp_ref, rel_ref,
                     o_ref, q4_ref, m_ref, l_ref, acc_ref, *, tq):
    g = pl.program_id(1)
    p = pl.program_id(2)
    i = qi_ref[p]
    j = kj_ref[p]

    @pl.when(j == 0)
    def _():
        _stack_heads(q_ref, q4_ref, tq)
        m_ref[...] = jnp.full(m_ref.shape, NEG, F32)
        l_ref[...] = jnp.zeros(l_ref.shape, F32)
        acc_ref[...] = jnp.zeros(acc_ref.shape, F32)

    s = _nt_dot(q4_ref[...], k_ref[...])
    chosen = jnp.dot(sel_ref[...], exp_ref[...], preferred_element_type=F32)
    dist = rel_ref[...] + ((i - j) * tq).astype(F32)
    vis = (chosen > 0.5) & (dist >= 0)
    vis4 = jnp.concatenate([vis] * B_HPG, axis=0)
    dist4 = jnp.concatenate([dist] * B_HPG, axis=0)
    s = jnp.where(vis4, s - _row_slopes(slopes_ref, g, tq) * dist4, NEG)
    _online_softmax_step(s, v_ref[...], m_ref, l_ref, acc_ref)

    @pl.when(j == i)
    def _():
        _write_heads(o_ref, acc_ref, l_ref, tq)


def _slc_attention(projm, sel, B, S, tq=256):
    T = B * S
    G, d = B_GROUPS, B_D
    n = S // tq
    qi, kj = _causal_pairs(n)
    slopes = _alibi_slopes(B_HEADS)
    ksb = (B_HEADS * d + 2 * G * d) // d
    vsb = ksb + G
    expand = jnp.asarray((np.arange(128)[:, None] == (np.arange(S)[None, :] // SLC_BLOCK)), BF16)
    grid_spec = pltpu.PrefetchScalarGridSpec(
        num_scalar_prefetch=3,
        grid=(B, G, int(qi.shape[0])),
        in_specs=[
            pl.BlockSpec((tq, B_HPG * d), lambda b, g, p, qi, kj, sl: (b * n + qi[p], g)),
            pl.BlockSpec((tq, d), lambda b, g, p, qi, kj, sl: (b * n + kj[p], ksb + g)),
            pl.BlockSpec((tq, d), lambda b, g, p, qi, kj, sl: (b * n + kj[p], vsb + g)),
            pl.BlockSpec((None, None, tq, 128), lambda b, g, p, qi, kj, sl: (b, g, qi[p], 0)),
            pl.BlockSpec((128, tq), lambda b, g, p, qi, kj, sl: (0, kj[p])),
            pl.BlockSpec((tq, tq), lambda b, g, p, qi, kj, sl: (0, 0)),
        ],
        out_specs=pl.BlockSpec((tq, B_HPG * d), lambda b, g, p, qi, kj, sl: (b * n + qi[p], g)),
        scratch_shapes=[pltpu.VMEM((B_HPG * tq, d), BF16), pltpu.VMEM((B_HPG * tq, 1), F32),
                        pltpu.VMEM((B_HPG * tq, 1), F32), pltpu.VMEM((B_HPG * tq, d), F32)],
    )
    return pl.pallas_call(
        functools.partial(_slc_attn_kernel, tq=tq),
        out_shape=jax.ShapeDtypeStruct((T, B_HEADS * d), F32),
        grid_spec=grid_spec,
        compiler_params=_cparams(("parallel", "parallel", "arbitrary")),
        name="nsa_selected_attention",
    )(qi, kj, slopes, projm, projm, projm, sel, expand, _rel_matrix(tq, tq))


def _win_attn_kernel(slopes_ref, q_ref, k_ref, v_ref, rel_ref, o_ref, q4_ref, m_ref, l_ref, acc_ref,
                     *, tq, n_kv):
    g = pl.program_id(1)
    i = pl.program_id(2)
    jj = pl.program_id(3)
    jt = i - (n_kv - 1) + jj

    @pl.when(jj == 0)
    def _():
        _stack_heads(q_ref, q4_ref, tq)
        m_ref[...] = jnp.full(m_ref.shape, NEG, F32)
        l_ref[...] = jnp.zeros(l_ref.shape, F32)
        acc_ref[...] = jnp.zeros(acc_ref.shape, F32)

    @pl.when(jt >= 0)
    def _():
        s = _nt_dot(q4_ref[...], k_ref[...])
        dist = rel_ref[...] + ((i - jt) * tq).astype(F32)
        vis = (dist >= 0) & (dist < float(WINDOW))
        vis4 = jnp.concatenate([vis] * B_HPG, axis=0)
        dist4 = jnp.concatenate([dist] * B_HPG, axis=0)
        s = jnp.where(vis4, s - _row_slopes(slopes_ref, g, tq) * dist4, NEG)
        _online_softmax_step(s, v_ref[...], m_ref, l_ref, acc_ref)

    @pl.when(jj == n_kv - 1)
    def _():
        _write_heads(o_ref, acc_ref, l_ref, tq)


def _win_attention(projm, B, S, tq=256):
    T = B * S
    G, d = B_GROUPS, B_D
    n = S // tq
    n_kv = WINDOW // tq + 1
    slopes = _alibi_slopes(B_HEADS)
    kwb = (B_HEADS * d + 4 * G * d) // d
    vwb = kwb + G

    def kv_row(b, i, jj):
        return b * n + jnp.maximum(i - (n_kv - 1) + jj, 0)

    grid_spec = pltpu.PrefetchScalarGridSpec(
        num_scalar_prefetch=1,
        grid=(B, G, n, n_kv),
        in_specs=[
            pl.BlockSpec((tq, B_HPG * d), lambda b, g, i, jj, sl: (b * n + i, g)),
            pl.BlockSpec((tq, d), lambda b, g, i, jj, sl: (kv_row(b, i, jj), kwb + g)),
            pl.BlockSpec((tq, d), lambda b, g, i, jj, sl: (kv_row(b, i, jj), vwb + g)),
            pl.BlockSpec((tq, tq), lambda b, g, i, jj, sl: (0, 0)),
        ],
        out_specs=pl.BlockSpec((tq, B_HPG * d), lambda b, g, i, jj, sl: (b * n + i, g)),
        scratch_shapes=[pltpu.VMEM((B_HPG * tq, d), BF16), pltpu.VMEM((B_HPG * tq, 1), F32),
                        pltpu.VMEM((B_HPG * tq, 1), F32), pltpu.VMEM((B_HPG * tq, d), F32)],
    )
    return pl.pallas_call(
        functools.partial(_win_attn_kernel, tq=tq, n_kv=n_kv),
        out_shape=jax.ShapeDtypeStruct((T, B_HEADS * d), F32),
        grid_spec=grid_spec,
        compiler_params=_cparams(("parallel", "parallel", "parallel", "arbitrary")),
        name="nsa_window_attention",
    )(slopes, projm, projm, projm, _rel_matrix(tq, tq))


def _gate_mix_kernel(gl_ref, oc_ref, os_ref, ow_ref, o_ref):
    sig = jax.nn.sigmoid(gl_ref[...])
    d = B_D
    for hd in range(B_HEADS):
        sl = slice(hd * d, (hd + 1) * d)
        o = (sig[:, 3 * hd:3 * hd + 1] * oc_ref[:, sl] + sig[:, 3 * hd + 1:3 * hd + 2] * os_ref[:, sl]
             + sig[:, 3 * hd + 2:3 * hd + 3] * ow_ref[:, sl])
        o_ref[:, sl] = o.astype(o_ref.dtype)


def _gate_mix(gate_logits, o_cmp, o_slc, o_win, tm=256):
    T, W = o_cmp.shape
    spec = pl.BlockSpec((tm, W), lambda i: (i, 0))
    return pl.pallas_call(
        _gate_mix_kernel,
        out_shape=jax.ShapeDtypeStruct((T, W), BF16),
        grid=(T // tm,),
        in_specs=[pl.BlockSpec((tm, 128), lambda i: (i, 0)), spec, spec, spec],
        out_specs=spec,
        compiler_params=_cparams(("parallel",)),
        name="nsa_gate_mix",
    )(gate_logits, o_cmp, o_slc, o_win)


def _overlap_matrix(S, n_rows):
    nc = (S - CMP_LEN) // CMP_STRIDE + 1
    nsb = S // SLC_BLOCK
    cs = np.arange(nc) * CMP_STRIDE
    ss = np.arange(nsb) * SLC_BLOCK
    ov = np.clip(np.minimum(cs[:, None] + CMP_LEN, ss[None, :] + SLC_BLOCK)
                 - np.maximum(cs[:, None], ss[None, :]), 0, None) / CMP_LEN
    out = np.zeros((n_rows, 128), np.float32)
    out[:nc, :nsb] = ov
    return jnp.asarray(out, BF16)


def _nsa_attention(xn, w_in, cmp_pos, cmp_w1, cmp_w2, B, S):
    T = B * S
    H, G, d = B_HEADS, B_GROUPS, B_D
    n_main = H * d + 6 * G * d
    col_scale = jnp.concatenate([jnp.full((H * d,), d ** -0.5, F32), jnp.ones((n_main - H * d,), F32)])
    projm = _matmul(xn, w_in, n_main, BF16, col_scale=col_scale)
    w_gate = jnp.pad(w_in[:, n_main:], ((0, 0), (0, 128 - 3 * H)))
    gate_logits = _matmul(xn, w_gate, 128, F32, tn=128)

    nch = S // CMP_STRIDE
    tb = projm[:, H * d:H * d + 2 * G * d].reshape(B, nch, CMP_STRIDE, 2 * G, d)
    tb = jnp.transpose(tb, (3, 0, 1, 2, 4)).reshape(2 * G, B, nch, CMP_STRIDE * d)
    cmp_kv = _compress(tb, cmp_w1, cmp_w2, cmp_pos.reshape(2, 1, CMP_LEN * d))

    o_cmp, sel = _cmp_select(projm, cmp_kv, _overlap_matrix(S, nch), B, S)
    o_slc = _slc_attention(projm, sel, B, S)
    o_win = _win_attention(projm, B, S)
    return _gate_mix(gate_logits, o_cmp, o_slc, o_win)


def kernel(x, attn_norm_w, ffn_norm_w, final_norm_w, a_w_in, a_lambda, a_subln_w, a_w_out, b_w_in,
           b_cmp_pos, b_cmp_w1, b_cmp_w2, b_w_out, moe_w_group, moe_b_group, moe_w_expert,
           moe_b_expert, moe_w_gate, moe_w_up, moe_w_down):
    B, S, D = x.shape
    T = B * S
    depth = attn_norm_w.shape[0]
    h = x.reshape(T, D)
    for i in range(depth):
        xn = _rmsnorm(h, attn_norm_w[i], BF16)
        j = i // 2
        if i % 2 == 0:
            lambda_init = 0.8 - 0.6 * math.exp(-0.3 * i)
            n_q = A_HEADS * 2 * A_DH
            n_in = 2 * n_q + A_HEADS * A_DV
            col_scale = jnp.concatenate([jnp.full((n_q,), A_DH ** -0.5, F32), jnp.ones((n_in - n_q,), F32)])
            qkv = _matmul(xn, a_w_in[j], n_in, BF16, col_scale=col_scale)
            o = _diff_attention(qkv, a_lambda[j], a_subln_w[j], lambda_init, B, S)
            h = _matmul(o, a_w_out[j], D, F32, resid=h)
        else:
            o = _nsa_attention(xn, b_w_in[j], b_cmp_pos[j], b_cmp_w1[j], b_cmp_w2[j], B, S)
            h = _matmul(o, b_w_out[j], D, F32, resid=h)
        h = _hier_moe(h, ffn_norm_w[i], moe_w_group[i], moe_b_group[i], moe_w_expert[i], moe_b_expert[i],
                      moe_w_gate, moe_w_up, moe_w_down, i)
    return _rmsnorm(h, final_norm_w, F32).reshape(B, S, D)
```

```python
import functools
import math

import numpy as np
import jax
import jax.numpy as jnp
from jax import lax
from jax.experimental import pallas as pl
from jax.experimental.pallas import tpu as pltpu

F32 = jnp.float32
BF16 = jnp.bfloat16
I32 = jnp.int32

EPS = 1e-6
NEG = -1e30
FORCE = 1e4
LOG2E = math.log2(math.e)

A_HEADS = 8
A_DH = 128
A_DV = 256
B_HEADS = 16
B_GROUPS = 4
B_HPG = 4
B_D = 128
CMP_LEN = 32
CMP_STRIDE = 16
SLC_BLOCK = 64
SLC_TOPN = 16
WINDOW = 512
N_GROUPS = 4
EPG = 8
N_EXPERTS = 32
D_EXPERT = 1024
ROUTER_LANES = 128
MOE_ROWS = 256

VMEM_LIMIT = 56 * 1024 * 1024


def _cparams(sem):
    return pltpu.CompilerParams(dimension_semantics=sem, vmem_limit_bytes=VMEM_LIMIT)


def _alibi_slopes(n):
    return jnp.asarray(np.array([2.0 ** (-8.0 * (i + 1) / n) for i in range(n)], dtype=np.float32))


def _nt_dot(a, b):
    return lax.dot_general(a, b, (((1,), (1,)), ((), ())), preferred_element_type=F32)


def _rms_kernel(x_ref, w_ref, o_ref):
    x = x_ref[...]
    y = x * lax.rsqrt(jnp.mean(x * x, axis=-1, keepdims=True) + EPS)
    o_ref[...] = (y * w_ref[...]).astype(o_ref.dtype)


def _rmsnorm(x, w, out_dtype, tm=512):
    T, D = x.shape
    return pl.pallas_call(
        _rms_kernel,
        out_shape=jax.ShapeDtypeStruct((T, D), out_dtype),
        grid=(T // tm,),
        in_specs=[pl.BlockSpec((tm, D), lambda i: (i, 0)),
                  pl.BlockSpec((1, D), lambda i: (0, 0))],
        out_specs=pl.BlockSpec((tm, D), lambda i: (i, 0)),
        compiler_params=_cparams(("parallel",)),
        name="rmsnorm",
    )(x, w.reshape(1, D))


def _mm_kernel(*refs, has_scale, has_resid):
    a_ref, w_ref = refs[0], refs[1]
    k = 2
    scale_ref = resid_ref = None
    if has_scale:
        scale_ref = refs[k]
        k += 1
    if has_resid:
        resid_ref = refs[k]
        k += 1
    o_ref, wbf_ref = refs[k], refs[k + 1]

    @pl.when(pl.program_id(1) == 0)
    def _():
        wbf_ref[...] = w_ref[...].astype(BF16)

    acc = jnp.dot(a_ref[...], wbf_ref[...], preferred_element_type=F32)
    if has_scale:
        acc = acc * scale_ref[...]
    if has_resid:
        acc = acc + resid_ref[...]
    o_ref[...] = acc.astype(o_ref.dtype)


def _matmul(a, w, n_out, out_dtype, *, col_scale=None, resid=None, tm=1024, tn=512):
    M, K = a.shape
    assert M % tm == 0 and n_out % tn == 0 and w.shape[0] == K and w.shape[1] >= n_out
    in_specs = [pl.BlockSpec((tm, K), lambda j, i: (i, 0)),
                pl.BlockSpec((K, tn), lambda j, i: (0, j))]
    args = [a, w]
    if col_scale is not None:
        in_specs.append(pl.BlockSpec((1, tn), lambda j, i: (0, j)))
        args.append(col_scale.reshape(1, n_out))
    if resid is not None:
        in_specs.append(pl.BlockSpec((tm, tn), lambda j, i: (i, j)))
        args.append(resid)
    return pl.pallas_call(
        functools.partial(_mm_kernel, has_scale=col_scale is not None, has_resid=resid is not None),
        out_shape=jax.ShapeDtypeStruct((M, n_out), out_dtype),
        grid=(n_out // tn, M // tm),
        in_specs=in_specs,
        out_specs=pl.BlockSpec((tm, tn), lambda j, i: (i, j)),
        scratch_shapes=[pltpu.VMEM((K, tn), BF16)],
        compiler_params=_cparams(("parallel", "arbitrary")),
        name="matmul",
    )(*args)


FEAT_BLK = 64
FEAT_POS = 67
FEAT_ONE = 70


def _key_features(S):
    assert S // SLC_BLOCK <= FEAT_BLK
    j = np.arange(S)
    f = np.zeros((S, 128), np.float32)
    f[j, j // SLC_BLOCK] = 1.0
    f[:, FEAT_BLK:FEAT_BLK + 3] = (j // SLC_BLOCK)[:, None]
    f[:, FEAT_POS:FEAT_POS + 3] = (j % SLC_BLOCK)[:, None]
    f[:, FEAT_ONE:FEAT_ONE + 3] = 1.0
    return jnp.asarray(f, BF16)


def _alibi_query_features(slope2, q0):
    lane = lax.broadcasted_iota(I32, (1, 128), 1)
    base = jnp.where((lane >= FEAT_BLK) & (lane < FEAT_BLK + 3), slope2 * float(SLC_BLOCK),
                     jnp.where((lane >= FEAT_POS) & (lane < FEAT_POS + 3), slope2,
                               jnp.where((lane >= FEAT_ONE) & (lane < FEAT_ONE + 3),
                                         -slope2 * q0.astype(F32), 0.0)))
    hi = base.astype(BF16).astype(F32)
    r1 = base - hi
    lo = r1.astype(BF16).astype(F32)
    lo2 = (r1 - lo).astype(BF16).astype(F32)
    first = (lane == FEAT_BLK) | (lane == FEAT_POS) | (lane == FEAT_ONE)
    second = (lane == FEAT_BLK + 1) | (lane == FEAT_POS + 1) | (lane == FEAT_ONE + 1)
    return jnp.where(first, hi, jnp.where(second, lo, lo2))


def _flash_init(m_ref, l_ref, acc_ref):
    m_ref[...] = jnp.full(m_ref.shape, NEG, F32)
    l_ref[...] = jnp.zeros(l_ref.shape, F32)
    acc_ref[...] = jnp.zeros(acc_ref.shape, F32)


def _flash_update(sT, v, m_ref, l_ref, acc_ref, u):
    m_old = m_ref[u]
    m_new = jnp.maximum(m_old, jnp.max(sT, axis=0, keepdims=True))
    alpha = jnp.exp2(m_old - m_new)
    p = jnp.exp2(sT - m_new)
    l_ref[u] = alpha * l_ref[u] + jnp.sum(p, axis=0, keepdims=True)
    pv = lax.dot_general(v, p.astype(BF16), (((0,), (0,)), ((), ())), preferred_element_type=F32)
    acc_ref[u] = alpha * acc_ref[u] + pv
    m_ref[u] = m_new


def _tri_mask(tk, tq, keep_upper):
    r = lax.broadcasted_iota(I32, (tk, tq), 0)
    c = lax.broadcasted_iota(I32, (tk, tq), 1)
    keep = (r <= c) if keep_upper else (r > c)
    return jnp.where(keep, 0.0, NEG)


def _diff_attn_kernel(slopes_ref, q1_ref, q2_ref, k1_ref, k2_ref, v_ref, feat_ref, lam_ref, sw_ref, o_ref,
                      qa_ref, m_ref, l_ref, acc_ref, *, tq, lambda_init):
    h = pl.program_id(1)
    i = pl.program_id(2)
    _flash_init(m_ref, l_ref, acc_ref)
    qfeat = jnp.broadcast_to(_alibi_query_features(slopes_ref[h] * LOG2E, i * tq), (tq, 128)).astype(BF16)
    for c, q_ref in enumerate((q1_ref, q2_ref)):
        qa_ref[c] = jnp.concatenate([q_ref[...], qfeat], axis=1)

    def tile(j, mask):
        k0 = pl.multiple_of(j * tq, tq)
        v = v_ref[pl.ds(k0, tq), :]
        feat = feat_ref[pl.ds(k0, tq), :]
        for c, k_ref in enumerate((k1_ref, k2_ref)):
            sT = _nt_dot(jnp.concatenate([k_ref[pl.ds(k0, tq), :], feat], axis=1), qa_ref[c])
            if mask is not None:
                sT = sT + mask
            _flash_update(sT, v, m_ref, l_ref, acc_ref, c)

    def body(j, carry):
        tile(j, None)
        return carry

    lax.fori_loop(0, i, body, 0)
    tile(i, _tri_mask(tq, tq, True))

    lam = lam_ref[...]
    lmbda = (jnp.exp(jnp.sum(lam[0:1] * lam[1:2], axis=-1, keepdims=True))
             - jnp.exp(jnp.sum(lam[2:3] * lam[3:4], axis=-1, keepdims=True)) + lambda_init)
    oT = acc_ref[0] / l_ref[0] - lmbda * (acc_ref[1] / l_ref[1])
    o = oT.T
    y = o * lax.rsqrt(jnp.mean(o * o, axis=-1, keepdims=True) + EPS)
    y = (y * sw_ref[...]) * (1.0 - lambda_init)
    o_ref[...] = y.astype(o_ref.dtype)


def _diff_attention(qkv, lam, subln_w, lambda_init, B, S, tq=512):
    T = B * S
    H, dh, dv = A_HEADS, A_DH, A_DV
    n = S // tq
    slopes = _alibi_slopes(H)
    kb = H * 2
    vb = (2 * H * 2 * dh) // dv
    grid_spec = pltpu.PrefetchScalarGridSpec(
        num_scalar_prefetch=1,
        grid=(B, H, n),
        in_specs=[
            pl.BlockSpec((tq, dh), lambda b, h, i, sl: (b * n + i, 2 * h)),
            pl.BlockSpec((tq, dh), lambda b, h, i, sl: (b * n + i, 2 * h + 1)),
            pl.BlockSpec((S, dh), lambda b, h, i, sl: (b, kb + 2 * h)),
            pl.BlockSpec((S, dh), lambda b, h, i, sl: (b, kb + 2 * h + 1)),
            pl.BlockSpec((S, dv), lambda b, h, i, sl: (b, vb + h)),
            pl.BlockSpec((S, 128), lambda b, h, i, sl: (0, 0)),
            pl.BlockSpec((4, dh), lambda b, h, i, sl: (0, 0)),
            pl.BlockSpec((1, dv), lambda b, h, i, sl: (0, 0)),
        ],
        out_specs=pl.BlockSpec((tq, dv), lambda b, h, i, sl: (b * n + i, h)),
        scratch_shapes=[pltpu.VMEM((2, tq, 2 * dh), BF16), pltpu.VMEM((2, 1, tq), F32),
                        pltpu.VMEM((2, 1, tq), F32), pltpu.VMEM((2, dv, tq), F32)],
    )
    return pl.pallas_call(
        functools.partial(_diff_attn_kernel, tq=tq, lambda_init=lambda_init),
        out_shape=jax.ShapeDtypeStruct((T, H * dv), BF16),
        grid_spec=grid_spec,
        compiler_params=_cparams(("parallel", "parallel", "arbitrary")),
        name="diff_attention",
    )(slopes, qkv, qkv, qkv, qkv, qkv, _key_features(S), lam, subln_w.reshape(1, dv))


def _router_kernel(h_ref, nw_ref, wr_ref, br_ref, tri_ref, xn_ref, ids_ref, wts_ref, cnt_ref, base_ref):
    @pl.when(pl.program_id(0) == 0)
    def _():
        base_ref[...] = jnp.zeros(base_ref.shape, F32)

    x = h_ref[...]
    xn = (x * lax.rsqrt(jnp.mean(x * x, axis=-1, keepdims=True) + EPS)) * nw_ref[...]
    xn_ref[...] = xn
    logits = jnp.dot(xn, wr_ref[...], preferred_element_type=F32,
                     precision=lax.Precision.HIGHEST) + br_ref[...]
    lane = lax.broadcasted_iota(I32, logits.shape, 1)
    big = jnp.int32(1 << 20)
    is_g = lane < N_GROUPS
    is_e = (lane >= N_GROUPS) & (lane < N_GROUPS + N_EXPERTS)

    lg = jnp.where(is_g, logits, NEG)
    mg = jnp.max(lg, axis=-1, keepdims=True)
    g_top = jnp.min(jnp.where(lg == mg, lane, big), axis=-1, keepdims=True)
    wg = 1.0 / jnp.sum(jnp.where(is_g, jnp.exp(lg - mg), 0.0), axis=-1, keepdims=True)

    in_grp = is_e & (lax.shift_right_logical(lane - N_GROUPS, 3) == g_top)
    le = jnp.where(in_grp, logits, NEG)
    me = jnp.max(le, axis=-1, keepdims=True)
    ee = jnp.where(in_grp, jnp.exp(le - me), 0.0)
    pe = ee / jnp.sum(ee, axis=-1, keepdims=True)
    pe = jnp.where(in_grp, pe, -1.0)
    p1 = jnp.max(pe, axis=-1, keepdims=True)
    i1 = jnp.min(jnp.where(pe == p1, lane, big), axis=-1, keepdims=True)
    pe2 = jnp.where(lane == i1, -1.0, pe)
    p2 = jnp.max(pe2, axis=-1, keepdims=True)
    i2 = jnp.min(jnp.where(pe2 == p2, lane, big), axis=-1, keepdims=True)
    den = p1 + p2
    w0 = wg * p1 / den
    w1 = wg * p2 / den

    oh1 = lane == i1
    oh2 = lane == i2
    both = jnp.where(oh1 | oh2, 1.0, 0.0)
    before = jnp.dot(tri_ref[...], both.astype(BF16), preferred_element_type=F32) + base_ref[...]
    pos0 = jnp.sum(jnp.where(oh1, before, 0.0), axis=-1, keepdims=True).astype(I32)
    pos1 = jnp.sum(jnp.where(oh2, before, 0.0), axis=-1, keepdims=True).astype(I32)
    total = base_ref[...] + jnp.sum(both, axis=0, keepdims=True)
    base_ref[...] = total
    cnt_ref[...] = total

    ids_ref[...] = jnp.where(lane == 0, i1 - N_GROUPS,
                             jnp.where(lane == 1, i2 - N_GROUPS,
                                       jnp.where(lane == 2, pos0, jnp.where(lane == 3, pos1, 0))))
    wts_ref[...] = jnp.where(lane == 0, w0, jnp.where(lane == 1, w1, 0.0))


def _router(h, norm_w, w_group, b_group, w_expert, b_expert, tm=512):
    T, D = h.shape
    pad = ROUTER_LANES - N_GROUPS - N_EXPERTS
    wr = jnp.concatenate([w_group, w_expert, jnp.zeros((D, pad), F32)], axis=1)
    br = jnp.concatenate([b_group, b_expert, jnp.zeros((pad,), F32)]).reshape(1, ROUTER_LANES)
    tri = jnp.asarray(np.tril(np.ones((tm, tm), np.float32), -1), BF16)
    return pl.pallas_call(
        _router_kernel,
        out_shape=(jax.ShapeDtypeStruct((T, D), F32),
                   jax.ShapeDtypeStruct((T, ROUTER_LANES), I32),
                   jax.ShapeDtypeStruct((T, ROUTER_LANES), F32),
                   jax.ShapeDtypeStruct((1, ROUTER_LANES), F32)),
        grid=(T // tm,),
        in_specs=[pl.BlockSpec((tm, D), lambda i: (i, 0)),
                  pl.BlockSpec((1, D), lambda i: (0, 0)),
                  pl.BlockSpec((D, ROUTER_LANES), lambda i: (0, 0)),
                  pl.BlockSpec((1, ROUTER_LANES), lambda i: (0, 0)),
                  pl.BlockSpec((tm, tm), lambda i: (0, 0))],
        out_specs=(pl.BlockSpec((tm, D), lambda i: (i, 0)),
                   pl.BlockSpec((tm, ROUTER_LANES), lambda i: (i, 0)),
                   pl.BlockSpec((tm, ROUTER_LANES), lambda i: (i, 0)),
                   pl.BlockSpec((1, ROUTER_LANES), lambda i: (0, 0))),
        scratch_shapes=[pltpu.VMEM((1, ROUTER_LANES), F32)],
        compiler_params=_cparams(("arbitrary",)),
        name="moe_router",
    )(h, norm_w.reshape(1, D), wr, br, tri)


def _dispatch_kernel(dest_ref, xn_ref, xs_in_ref, xs_ref, sem, *, tm):
    del xs_in_ref
    base = pl.program_id(0) * (2 * tm)

    def row_copy(t, slot):
        d = dest_ref[base + 2 * t + slot]
        return pltpu.make_async_copy(xn_ref.at[pl.ds(t, 1)], xs_ref.at[pl.ds(d, 1)], sem)

    def start(t, c):
        row_copy(t, 0).start()
        row_copy(t, 1).start()
        return c

    def wait(t, c):
        row_copy(t, 0).wait()
        row_copy(t, 1).wait()
        return c

    lax.fori_loop(0, tm, start, 0)
    lax.fori_loop(0, tm, wait, 0)


def _dispatch(xn, dest_flat, n_rows, tm=256):
    T, D = xn.shape
    grid_spec = pltpu.PrefetchScalarGridSpec(
        num_scalar_prefetch=1,
        grid=(T // tm,),
        in_specs=[pl.BlockSpec((tm, D), lambda i, d: (i, 0)),
                  pl.BlockSpec(memory_space=pl.ANY)],
        out_specs=pl.BlockSpec(memory_space=pl.ANY),
        scratch_shapes=[pltpu.SemaphoreType.DMA],
    )
    return pl.pallas_call(
        functools.partial(_dispatch_kernel, tm=tm),
        out_shape=jax.ShapeDtypeStruct((n_rows, D), F32),
        grid_spec=grid_spec,
        input_output_aliases={2: 0},
        compiler_params=_cparams(("arbitrary",)),
        name="moe_dispatch",
    )(dest_flat, xn, jnp.zeros((n_rows, D), F32))


def _blk(r, nu):
    return jnp.minimum(r, nu[0] - 1)


def _expert_changed(r, be_ref):
    prev = be_ref[jnp.maximum(r - 1, 0)]
    return (r == 0) | (be_ref[r] != prev)


def _gate_up_kernel(be_ref, nu_ref, x_ref, wg_ref, wu_ref, h_ref, wgb_ref, wub_ref):
    r = pl.program_id(1)

    @pl.when(r < nu_ref[0])
    def _():
        @pl.when(_expert_changed(r, be_ref))
        def _():
            wgb_ref[...] = wg_ref[...].astype(BF16)
            wub_ref[...] = wu_ref[...].astype(BF16)

        x = x_ref[...].astype(BF16)
        g = jnp.dot(x, wgb_ref[...], preferred_element_type=F32)
        u = jnp.dot(x, wub_ref[...], preferred_element_type=F32)
        h_ref[...] = (jax.nn.silu(g) * u).astype(h_ref.dtype)

    @pl.when(r >= nu_ref[0])
    def _():
        h_ref[...] = jnp.zeros(h_ref.shape, h_ref.dtype)


def _down_kernel(be_ref, nu_ref, h_ref, wd_ref, y_ref, wdb_ref):
    r = pl.program_id(1)

    @pl.when(r < nu_ref[0])
    def _():
        @pl.when(_expert_changed(r, be_ref))
        def _():
            wdb_ref[...] = wd_ref[...].astype(BF16)

        y_ref[...] = jnp.dot(h_ref[...], wdb_ref[...], preferred_element_type=F32)

    @pl.when(r >= nu_ref[0])
    def _():
        y_ref[...] = jnp.zeros(y_ref.shape, y_ref.dtype)


def _expert_mlp(xs, blk_expert, n_used, w_gate, w_up, w_down, layer, tf=512, tn=1024):
    n_rows, D = xs.shape
    RB = MOE_ROWS
    n_blk = n_rows // RB
    F = D_EXPERT
    h = pl.pallas_call(
        _gate_up_kernel,
        out_shape=jax.ShapeDtypeStruct((n_rows, F), BF16),
        grid_spec=pltpu.PrefetchScalarGridSpec(
            num_scalar_prefetch=2,
            grid=(F // tf, n_blk),
            in_specs=[
                pl.BlockSpec((RB, D), lambda f, r, be, nu: (_blk(r, nu), 0)),
                pl.BlockSpec((None, None, D, tf), lambda f, r, be, nu: (layer, be[_blk(r, nu)], 0, f)),
                pl.BlockSpec((None, None, D, tf), lambda f, r, be, nu: (layer, be[_blk(r, nu)], 0, f)),
            ],
            out_specs=pl.BlockSpec((RB, tf), lambda f, r, be, nu: (r, f)),
            scratch_shapes=[pltpu.VMEM((D, tf), BF16), pltpu.VMEM((D, tf), BF16)],
        ),
        compiler_params=_cparams(("arbitrary", "arbitrary")),
        name="moe_gate_up",
    )(blk_expert, n_used, xs, w_gate, w_up)
    y = pl.pallas_call(
        _down_kernel,
        out_shape=jax.ShapeDtypeStruct((n_rows, D), F32),
        grid_spec=pltpu.PrefetchScalarGridSpec(
            num_scalar_prefetch=2,
            grid=(D // tn, n_blk),
            in_specs=[
                pl.BlockSpec((RB, F), lambda n, r, be, nu: (_blk(r, nu), 0)),
                pl.BlockSpec((None, None, F, tn), lambda n, r, be, nu: (layer, be[_blk(r, nu)], 0, n)),
            ],
            out_specs=pl.BlockSpec((RB, tn), lambda n, r, be, nu: (r, n)),
            scratch_shapes=[pltpu.VMEM((F, tn), BF16)],
        ),
        compiler_params=_cparams(("arbitrary", "arbitrary")),
        name="moe_down",
    )(blk_expert, n_used, h, w_down)
    return y


def _combine_kernel(dest_ref, resid_ref, wts_ref, y_ref, o_ref, buf_ref, sem, *, tm):
    base = pl.program_id(0) * (2 * tm)

    def row_copy(t, slot):
        d = dest_ref[base + 2 * t + slot]
        return pltpu.make_async_copy(y_ref.at[pl.ds(d, 1)], buf_ref.at[slot, pl.ds(t, 1)], sem)

    def start(t, c):
        row_copy(t, 0).start()
        row_copy(t, 1).start()
        return c

    def wait(t, c):
        row_copy(t, 0).wait()
        row_copy(t, 1).wait()
        return c

    lax.fori_loop(0, tm, start, 0)
    lax.fori_loop(0, tm, wait, 0)
    w = wts_ref[...]
    o_ref[...] = resid_ref[...] + (w[:, 0:1] * buf_ref[0] + w[:, 1:2] * buf_ref[1])


def _combine(resid, wts, y, dest_flat, tm=128):
    T, D = resid.shape
    grid_spec = pltpu.PrefetchScalarGridSpec(
        num_scalar_prefetch=1,
        grid=(T // tm,),
        in_specs=[pl.BlockSpec((tm, D), lambda i, d: (i, 0)),
                  pl.BlockSpec((tm, ROUTER_LANES), lambda i, d: (i, 0)),
                  pl.BlockSpec(memory_space=pl.ANY)],
        out_specs=pl.BlockSpec((tm, D), lambda i, d: (i, 0)),
        scratch_shapes=[pltpu.VMEM((2, tm, D), F32), pltpu.SemaphoreType.DMA],
    )
    return pl.pallas_call(
        functools.partial(_combine_kernel, tm=tm),
        out_shape=jax.ShapeDtypeStruct((T, D), F32),
        grid_spec=grid_spec,
        compiler_params=_cparams(("arbitrary",)),
        name="moe_combine",
    )(dest_flat, resid, wts, y)


def _hier_moe(h, norm_w, w_group, b_group, w_expert, b_expert, w_gate, w_up, w_down, layer):
    T, D = h.shape
    RB = MOE_ROWS
    n_blk = (T * 2) // RB + N_EXPERTS
    n_rows = n_blk * RB
    xn, ids, wts, cnt = _router(h, norm_w, w_group, b_group, w_expert, b_expert)
    counts = cnt[0, N_GROUPS:N_GROUPS + N_EXPERTS].astype(I32)
    padded = (counts + RB - 1) // RB * RB
    end_padded = jnp.cumsum(padded)
    start_padded = end_padded - padded
    dest = (start_padded[ids[:, 0:2]] + ids[:, 2:4]).reshape(-1)
    n_used = (end_padded[-1:] // RB).astype(I32)
    blk_start = jnp.arange(n_blk, dtype=I32) * RB
    blk_expert = jnp.minimum(jnp.sum((end_padded[None, :] <= blk_start[:, None]).astype(I32), axis=1),
                             N_EXPERTS - 1)
    xs = _dispatch(xn, dest, n_rows)
    y = _expert_mlp(xs, blk_expert, n_used, w_gate, w_up, w_down, layer)
    return _combine(h, wts, y, dest)


def _compress_kernel(x_ref, w1_ref, w2_ref, pe_ref, o_ref):
    half = (CMP_LEN // 2) * B_D
    x = x_ref[...]
    w1 = w1_ref[...].astype(BF16)
    a = jnp.dot(x, w1[:half], preferred_element_type=F32)
    b = jnp.dot(x, w1[half:], preferred_element_type=F32)
    pe = jnp.broadcast_to(pe_ref[...], (8, 2 * half)).astype(BF16)
    pt = jnp.dot(pe, w1, preferred_element_type=F32)[0:1]
    hid = a + pltpu.roll(b, b.shape[0] - 1, 0) + pt
    g = jax.nn.gelu(hid)
    o_ref[...] = jnp.dot(g.astype(BF16), w2_ref[...].astype(BF16),
                         preferred_element_type=F32).astype(o_ref.dtype)


def _compress(tb, w1, w2, pe_flat):
    _, B, nch, width = tb.shape
    G, d = B_GROUPS, B_D
    hid = w1.shape[-1]
    return pl.pallas_call(
        _compress_kernel,
        out_shape=jax.ShapeDtypeStruct((2, B, G, nch, d), BF16),
        grid=(2, B, G),
        in_specs=[pl.BlockSpec((None, None, nch, width), lambda kv, b, g: (kv * G + g, b, 0, 0)),
                  pl.BlockSpec((None, 2 * width, hid), lambda kv, b, g: (kv, 0, 0)),
                  pl.BlockSpec((None, hid, d), lambda kv, b, g: (kv, 0, 0)),
                  pl.BlockSpec((None, 1, 2 * width), lambda kv, b, g: (kv, 0, 0))],
        out_specs=pl.BlockSpec((None, None, None, nch, d), lambda kv, b, g: (kv, b, g, 0, 0)),
        compiler_params=_cparams(("arbitrary", "arbitrary", "arbitrary")),
        name="nsa_compress",
    )(tb, w1, w2, pe_flat)


def _cmp_select_kernel(slopes_ref, q_ref, kc_ref, vc_ref, ov_ref, o_ref, sel_ref, *, tq, n_cmp, n_blocks,
                       n_rank):
    g = pl.program_id(1)
    i = pl.program_id(2)
    d = B_D
    nch = kc_ref.shape[0]
    pos = i * tq + lax.broadcasted_iota(I32, (1, tq), 1)
    c = lax.broadcasted_iota(I32, (nch, 1), 0)
    dist_i = pos - (c * CMP_STRIDE + (CMP_LEN - 1))
    valid = (dist_i >= 0) & (c < n_cmp)
    dist = dist_i.astype(F32)
    any_valid = jnp.where(pos >= CMP_LEN - 1, 1.0, 0.0)
    kc = kc_ref[...]
    vc = vc_ref[...]
    psum = jnp.zeros((nch, tq), F32)
    for hh in range(B_HPG):
        s = _nt_dot(kc, q_ref[:, hh * d:(hh + 1) * d])
        s = jnp.where(valid, s - (slopes_ref[g * B_HPG + hh] * LOG2E) * dist, NEG)
        e = jnp.exp2(s - jnp.max(s, axis=0, keepdims=True))
        p = (e / jnp.sum(e, axis=0, keepdims=True)) * any_valid
        oT = lax.dot_general(vc, p.astype(BF16), (((0,), (0,)), ((), ())), preferred_element_type=F32)
        o_ref[:, hh * d:(hh + 1) * d] = oT.T
        psum = psum + p

    p_hi = psum.astype(BF16)
    p_lo = (psum - p_hi.astype(F32)).astype(BF16)
    ov = ov_ref[...]
    imp = jnp.dot(ov, p_hi, preferred_element_type=F32) + jnp.dot(ov, p_lo, preferred_element_type=F32)
    imp = imp[:n_rank]

    blk = lax.broadcasted_iota(I32, (n_rank, 1), 0)
    cur = lax.shift_right_logical(pos, SLC_BLOCK.bit_length() - 1)
    forced = (blk == 0) | (blk == cur) | (blk == cur - 1)
    causal = blk * SLC_BLOCK <= pos
    score = jnp.where(forced, FORCE, jnp.where(causal, imp, -FORCE))
    score = jnp.where(blk < n_blocks, score, -2.0 * FORCE)
    rank = jnp.zeros(score.shape, F32)
    for jp in range(n_blocks):
        row = score[jp:jp + 1, :]
        tie = jnp.where(blk > jp, 1.0, 0.0)
        rank = rank + jnp.where(row > score, 1.0, jnp.where(row == score, tie, 0.0))
    sel_add = jnp.where((rank < float(SLC_TOPN)) & (blk < n_blocks), 0.0, NEG)
    pad = jnp.zeros((128 - n_rank, tq), F32)
    sel_t = jnp.concatenate([jnp.where(blk < n_blocks, sel_add, 0.0), pad], axis=0) if n_rank < 128 else sel_add
    sel_ref[...] = sel_t.T.astype(sel_ref.dtype)


def _cmp_select(projm, cmp_kv, overlap, B, S, tq=256):
    T = B * S
    G, d = B_GROUPS, B_D
    nq = S // tq
    nch = cmp_kv.shape[3]
    n_cmp = (S - CMP_LEN) // CMP_STRIDE + 1
    n_blocks = S // SLC_BLOCK
    assert SLC_TOPN <= n_blocks <= FEAT_BLK
    slopes = _alibi_slopes(B_HEADS)
    grid_spec = pltpu.PrefetchScalarGridSpec(
        num_scalar_prefetch=1,
        grid=(B, G, nq),
        in_specs=[pl.BlockSpec((tq, B_HPG * d), lambda b, g, i, sl: (b * nq + i, g)),
                  pl.BlockSpec((None, None, None, nch, d), lambda b, g, i, sl: (0, b, g, 0, 0)),
                  pl.BlockSpec((None, None, None, nch, d), lambda b, g, i, sl: (1, b, g, 0, 0)),
                  pl.BlockSpec((128, nch), lambda b, g, i, sl: (0, 0))],
        out_specs=(pl.BlockSpec((tq, B_HPG * d), lambda b, g, i, sl: (b * nq + i, g)),
                   pl.BlockSpec((None, None, tq, 128), lambda b, g, i, sl: (b, g, i, 0))),
    )
    return pl.pallas_call(
        functools.partial(_cmp_select_kernel, tq=tq, n_cmp=n_cmp, n_blocks=n_blocks,
                          n_rank=-(-n_blocks // 8) * 8),
        out_shape=(jax.ShapeDtypeStruct((T, B_HEADS * d), F32),
                   jax.ShapeDtypeStruct((B, G, S, 128), BF16)),
        grid_spec=grid_spec,
        compiler_params=_cparams(("parallel", "parallel", "arbitrary")),
        name="nsa_cmp_select",
    )(slopes, projm, cmp_kv, cmp_kv, overlap)


def _write_heads_T(o_ref, acc_ref, l_ref):
    for hh in range(B_HPG):
        o_ref[:, hh * B_D:(hh + 1) * B_D] = (acc_ref[hh] / l_ref[hh]).T


def _stage_queries(q_ref, qa_ref, slopes_ref, g, q0, sel_add):
    tq = q_ref.shape[0]
    lane = lax.broadcasted_iota(I32, (tq, 128), 1)
    for hh in range(B_HPG):
        coef = jnp.broadcast_to(_alibi_query_features(slopes_ref[g * B_HPG + hh] * LOG2E, q0), (tq, 128))
        if sel_add is not None:
            coef = jnp.where(lane < FEAT_BLK, sel_add, coef)
        qa_ref[hh] = jnp.concatenate([q_ref[:, hh * B_D:(hh + 1) * B_D], coef.astype(BF16)], axis=1)


def _slc_attn_kernel(slopes_ref, q_ref, k_ref, v_ref, feat_ref, sel_ref, o_ref, qa_ref, m_ref, l_ref, acc_ref,
                     *, tq):
    g = pl.program_id(1)
    i = pl.program_id(2)
    _flash_init(m_ref, l_ref, acc_ref)
    _stage_queries(q_ref, qa_ref, slopes_ref, g, i * tq, sel_ref[...].astype(F32))

    def tile(j, causal):
        k0 = pl.multiple_of(j * tq, tq)
        ka = jnp.concatenate([k_ref[pl.ds(k0, tq), :], feat_ref[pl.ds(k0, tq), :]], axis=1)
        v = v_ref[pl.ds(k0, tq), :]
        for hh in range(B_HPG):
            sT = _nt_dot(ka, qa_ref[hh])
            if causal is not None:
                sT = sT + causal
            _flash_update(sT, v, m_ref, l_ref, acc_ref, hh)

    def body(j, carry):
        tile(j, None)
        return carry

    lax.fori_loop(0, i, body, 0)
    tile(i, _tri_mask(tq, tq, True))
    _write_heads_T(o_ref, acc_ref, l_ref)


def _slc_attention(projm, sel, B, S, tq=512):
    T = B * S
    G, d = B_GROUPS, B_D
    n = S // tq
    slopes = _alibi_slopes(B_HEADS)
    ksb = (B_HEADS * d + 2 * G * d) // d
    vsb = ksb + G
    grid_spec = pltpu.PrefetchScalarGridSpec(
        num_scalar_prefetch=1,
        grid=(B, G, n),
        in_specs=[
            pl.BlockSpec((tq, B_HPG * d), lambda b, g, i, sl: (b * n + i, g)),
            pl.BlockSpec((S, d), lambda b, g, i, sl: (b, ksb + g)),
            pl.BlockSpec((S, d), lambda b, g, i, sl: (b, vsb + g)),
            pl.BlockSpec((S, 128), lambda b, g, i, sl: (0, 0)),
            pl.BlockSpec((None, None, tq, 128), lambda b, g, i, sl: (b, g, i, 0)),
        ],
        out_specs=pl.BlockSpec((tq, B_HPG * d), lambda b, g, i, sl: (b * n + i, g)),
        scratch_shapes=[pltpu.VMEM((B_HPG, tq, 2 * d), BF16), pltpu.VMEM((B_HPG, 1, tq), F32),
                        pltpu.VMEM((B_HPG, 1, tq), F32), pltpu.VMEM((B_HPG, d, tq), F32)],
    )
    return pl.pallas_call(
        functools.partial(_slc_attn_kernel, tq=tq),
        out_shape=jax.ShapeDtypeStruct((T, B_HEADS * d), F32),
        grid_spec=grid_spec,
        compiler_params=_cparams(("parallel", "parallel", "arbitrary")),
        name="nsa_selected_attention",
    )(slopes, projm, projm, projm, _key_features(S), sel)


def _win_attn_kernel(slopes_ref, q_ref, k_ref, v_ref, feat_ref, o_ref, qa_ref, m_ref, l_ref, acc_ref, *, tq):
    g = pl.program_id(1)
    i = pl.program_id(2)
    _flash_init(m_ref, l_ref, acc_ref)
    _stage_queries(q_ref, qa_ref, slopes_ref, g, i * tq, None)

    def tile(j, mask):
        k0 = pl.multiple_of(j * tq, tq)
        ka = jnp.concatenate([k_ref[pl.ds(k0, tq), :], feat_ref[pl.ds(k0, tq), :]], axis=1)
        v = v_ref[pl.ds(k0, tq), :]
        for hh in range(B_HPG):
            sT = _nt_dot(ka, qa_ref[hh]) + mask
            _flash_update(sT, v, m_ref, l_ref, acc_ref, hh)

    @pl.when(i > 0)
    def _():
        tile(i - 1, _tri_mask(tq, tq, False))

    tile(i, _tri_mask(tq, tq, True))
    _write_heads_T(o_ref, acc_ref, l_ref)


def _win_attention(projm, B, S):
    T = B * S
    G, d = B_GROUPS, B_D
    tq = WINDOW
    n = S // tq
    slopes = _alibi_slopes(B_HEADS)
    kwb = (B_HEADS * d + 4 * G * d) // d
    vwb = kwb + G
    grid_spec = pltpu.PrefetchScalarGridSpec(
        num_scalar_prefetch=1,
        grid=(B, G, n),
        in_specs=[
            pl.BlockSpec((tq, B_HPG * d), lambda b, g, i, sl: (b * n + i, g)),
            pl.BlockSpec((S, d), lambda b, g, i, sl: (b, kwb + g)),
            pl.BlockSpec((S, d), lambda b, g, i, sl: (b, vwb + g)),
            pl.BlockSpec((S, 128), lambda b, g, i, sl: (0, 0)),
        ],
        out_specs=pl.BlockSpec((tq, B_HPG * d), lambda b, g, i, sl: (b * n + i, g)),
        scratch_shapes=[pltpu.VMEM((B_HPG, tq, 2 * d), BF16), pltpu.VMEM((B_HPG, 1, tq), F32),
                        pltpu.VMEM((B_HPG, 1, tq), F32), pltpu.VMEM((B_HPG, d, tq), F32)],
    )
    return pl.pallas_call(
        functools.partial(_win_attn_kernel, tq=tq),
        out_shape=jax.ShapeDtypeStruct((T, B_HEADS * d), F32),
        grid_spec=grid_spec,
        compiler_params=_cparams(("parallel", "parallel", "arbitrary")),
        name="nsa_window_attention",
    )(slopes, projm, projm, projm, _key_features(S))


def _gate_mix_kernel(gl_ref, oc_ref, os_ref, ow_ref, o_ref):
    sig = jax.nn.sigmoid(gl_ref[...])
    d = B_D
    for hd in range(B_HEADS):
        sl = slice(hd * d, (hd + 1) * d)
        o = (sig[:, 3 * hd:3 * hd + 1] * oc_ref[:, sl] + sig[:, 3 * hd + 1:3 * hd + 2] * os_ref[:, sl]
             + sig[:, 3 * hd + 2:3 * hd + 3] * ow_ref[:, sl])
        o_ref[:, sl] = o.astype(o_ref.dtype)


def _gate_mix(gate_logits, o_cmp, o_slc, o_win, tm=256):
    T, W = o_cmp.shape
    spec = pl.BlockSpec((tm, W), lambda i: (i, 0))
    return pl.pallas_call(
        _gate_mix_kernel,
        out_shape=jax.ShapeDtypeStruct((T, W), BF16),
        grid=(T // tm,),
        in_specs=[pl.BlockSpec((tm, 128), lambda i: (i, 0)), spec, spec, spec],
        out_specs=spec,
        compiler_params=_cparams(("parallel",)),
        name="nsa_gate_mix",
    )(gate_logits, o_cmp, o_slc, o_win)


def _overlap_matrix(S, n_rows):
    nc = (S - CMP_LEN) // CMP_STRIDE + 1
    nsb = S // SLC_BLOCK
    cs = np.arange(nc) * CMP_STRIDE
    ss = np.arange(nsb) * SLC_BLOCK
    ov = np.clip(np.minimum(cs[:, None] + CMP_LEN, ss[None, :] + SLC_BLOCK)
                 - np.maximum(cs[:, None], ss[None, :]), 0, None) / CMP_LEN
    out = np.zeros((128, n_rows), np.float32)
    out[:nsb, :nc] = ov.T
    return jnp.asarray(out, BF16)


def _nsa_attention(xn, w_in, cmp_pos, cmp_w1, cmp_w2, B, S):
    T = B * S
    H, G, d = B_HEADS, B_GROUPS, B_D
    n_main = H * d + 6 * G * d
    col_scale = jnp.concatenate([jnp.full((H * d,), d ** -0.5 * LOG2E, F32), jnp.ones((n_main - H * d,), F32)])
    projm = _matmul(xn, w_in, n_main, BF16, col_scale=col_scale)
    w_gate = jnp.pad(w_in[:, n_main:], ((0, 0), (0, 128 - 3 * H)))
    gate_logits = _matmul(xn, w_gate, 128, F32, tn=128)

    nch = S // CMP_STRIDE
    tb = projm[:, H * d:H * d + 2 * G * d].reshape(B, nch, CMP_STRIDE, 2 * G, d)
    tb = jnp.transpose(tb, (3, 0, 1, 2, 4)).reshape(2 * G, B, nch, CMP_STRIDE * d)
    cmp_kv = _compress(tb, cmp_w1, cmp_w2, cmp_pos.reshape(2, 1, CMP_LEN * d))

    o_cmp, sel = _cmp_select(projm, cmp_kv, _overlap_matrix(S, nch), B, S)
    o_slc = _slc_attention(projm, sel, B, S)
    o_win = _win_attention(projm, B, S)
    return _gate_mix(gate_logits, o_cmp, o_slc, o_win)


def kernel(x, attn_norm_w, ffn_norm_w, final_norm_w, a_w_in, a_lambda, a_subln_w, a_w_out, b_w_in,
           b_cmp_pos, b_cmp_w1, b_cmp_w2, b_w_out, moe_w_group, moe_b_group, moe_w_expert,
           moe_b_expert, moe_w_gate, moe_w_up, moe_w_down):
    B, S, D = x.shape
    T = B * S
    depth = attn_norm_w.shape[0]
    h = x.reshape(T, D)
    for i in range(depth):
        xn = _rmsnorm(h, attn_norm_w[i], BF16)
        j = i // 2
        if i % 2 == 0:
            lambda_init = 0.8 - 0.6 * math.exp(-0.3 * i)
            n_q = A_HEADS * 2 * A_DH
            n_in = 2 * n_q + A_HEADS * A_DV
            col_scale = jnp.concatenate([jnp.full((n_q,), A_DH ** -0.5 * LOG2E, F32),
                                         jnp.ones((n_in - n_q,), F32)])
            qkv = _matmul(xn, a_w_in[j], n_in, BF16, col_scale=col_scale)
            o = _diff_attention(qkv, a_lambda[j], a_subln_w[j], lambda_init, B, S)
            h = _matmul(o, a_w_out[j], D, F32, resid=h)
        else:
            o = _nsa_attention(xn, b_w_in[j], b_cmp_pos[j], b_cmp_w1[j], b_cmp_w2[j], B, S)
            h = _matmul(o, b_w_out[j], D, F32, resid=h)
        h = _hier_moe(h, ffn_norm_w[i], moe_w_group[i], moe_b_group[i], moe_w_expert[i], moe_b_expert[i],
                      moe_w_gate, moe_w_up, moe_w_down, i)
    return _rmsnorm(h, final_norm_w, F32).reshape(B, S, D)
```

```python
import functools
import math

import numpy as np
import jax
import jax.numpy as jnp
from jax import lax
from jax.experimental import pallas as pl
from jax.experimental.pallas import tpu as pltpu

F32 = jnp.float32
BF16 = jnp.bfloat16
I32 = jnp.int32

EPS = 1e-6
NEG = -1e30
FORCE = 1e4
LOG2E = math.log2(math.e)

A_HEADS = 8
A_DH = 128
A_DV = 256
B_HEADS = 16
B_GROUPS = 4
B_HPG = 4
B_D = 128
CMP_LEN = 32
CMP_STRIDE = 16
SLC_BLOCK = 64
SLC_TOPN = 16
WINDOW = 512
N_GROUPS = 4
EPG = 8
N_EXPERTS = 32
D_EXPERT = 1024
ROUTER_LANES = 128
MOE_ROWS = 256
DMA_UNROLL = 8

VMEM_LIMIT = 56 * 1024 * 1024


def _cparams(sem):
    return pltpu.CompilerParams(dimension_semantics=sem, vmem_limit_bytes=VMEM_LIMIT)


def _alibi_slopes(n):
    return jnp.asarray(np.array([2.0 ** (-8.0 * (i + 1) / n) for i in range(n)], dtype=np.float32))


def _nt_dot(a, b):
    return lax.dot_general(a, b, (((1,), (1,)), ((), ())), preferred_element_type=F32)


def _rms_kernel(x_ref, w_ref, o_ref):
    x = x_ref[...]
    y = x * lax.rsqrt(jnp.mean(x * x, axis=-1, keepdims=True) + EPS)
    o_ref[...] = (y * w_ref[...]).astype(o_ref.dtype)


def _rmsnorm(x, w, out_dtype, tm=512):
    T, D = x.shape
    return pl.pallas_call(
        _rms_kernel,
        out_shape=jax.ShapeDtypeStruct((T, D), out_dtype),
        grid=(T // tm,),
        in_specs=[pl.BlockSpec((tm, D), lambda i: (i, 0)),
                  pl.BlockSpec((1, D), lambda i: (0, 0))],
        out_specs=pl.BlockSpec((tm, D), lambda i: (i, 0)),
        compiler_params=_cparams(("parallel",)),
        name="rmsnorm",
    )(x, w.reshape(1, D))


def _mm_kernel(*refs, has_scale, has_resid):
    a_ref, w_ref = refs[0], refs[1]
    k = 2
    scale_ref = resid_ref = None
    if has_scale:
        scale_ref = refs[k]
        k += 1
    if has_resid:
        resid_ref = refs[k]
        k += 1
    o_ref, wbf_ref = refs[k], refs[k + 1]

    @pl.when(pl.program_id(1) == 0)
    def _():
        wbf_ref[...] = w_ref[...].astype(BF16)

    acc = jnp.dot(a_ref[...], wbf_ref[...], preferred_element_type=F32)
    if has_scale:
        acc = acc * scale_ref[...]
    if has_resid:
        acc = acc + resid_ref[...]
    o_ref[...] = acc.astype(o_ref.dtype)


def _matmul(a, w, n_out, out_dtype, *, col_scale=None, resid=None, tm=1024, tn=512):
    M, K = a.shape
    assert M % tm == 0 and n_out % tn == 0 and w.shape[0] == K and w.shape[1] >= n_out
    in_specs = [pl.BlockSpec((tm, K), lambda j, i: (i, 0)),
                pl.BlockSpec((K, tn), lambda j, i: (0, j))]
    args = [a, w]
    if col_scale is not None:
        in_specs.append(pl.BlockSpec((1, tn), lambda j, i: (0, j)))
        args.append(col_scale.reshape(1, n_out))
    if resid is not None:
        in_specs.append(pl.BlockSpec((tm, tn), lambda j, i: (i, j)))
        args.append(resid)
    return pl.pallas_call(
        functools.partial(_mm_kernel, has_scale=col_scale is not None, has_resid=resid is not None),
        out_shape=jax.ShapeDtypeStruct((M, n_out), out_dtype),
        grid=(n_out // tn, M // tm),
        in_specs=in_specs,
        out_specs=pl.BlockSpec((tm, tn), lambda j, i: (i, j)),
        scratch_shapes=[pltpu.VMEM((K, tn), BF16)],
        compiler_params=_cparams(("parallel", "arbitrary")),
        name="matmul",
    )(*args)


FEAT_BLK = 64
FEAT_POS = 67
FEAT_ONE = 70


def _key_features(S):
    assert S // SLC_BLOCK <= FEAT_BLK
    j = np.arange(S)
    f = np.zeros((S, 128), np.float32)
    f[j, j // SLC_BLOCK] = 1.0
    f[:, FEAT_BLK:FEAT_BLK + 3] = (j // SLC_BLOCK)[:, None]
    f[:, FEAT_POS:FEAT_POS + 3] = (j % SLC_BLOCK)[:, None]
    f[:, FEAT_ONE:FEAT_ONE + 3] = 1.0
    return jnp.asarray(f, BF16)


def _alibi_query_features(slope2, q0):
    lane = lax.broadcasted_iota(I32, (1, 128), 1)
    base = jnp.where((lane >= FEAT_BLK) & (lane < FEAT_BLK + 3), slope2 * float(SLC_BLOCK),
                     jnp.where((lane >= FEAT_POS) & (lane < FEAT_POS + 3), slope2,
                               jnp.where((lane >= FEAT_ONE) & (lane < FEAT_ONE + 3),
                                         -slope2 * q0.astype(F32), 0.0)))
    hi = base.astype(BF16).astype(F32)
    r1 = base - hi
    lo = r1.astype(BF16).astype(F32)
    lo2 = (r1 - lo).astype(BF16).astype(F32)
    first = (lane == FEAT_BLK) | (lane == FEAT_POS) | (lane == FEAT_ONE)
    second = (lane == FEAT_BLK + 1) | (lane == FEAT_POS + 1) | (lane == FEAT_ONE + 1)
    return jnp.where(first, hi, jnp.where(second, lo, lo2))


def _flash_init(m_ref, l_ref, acc_ref):
    m_ref[...] = jnp.full(m_ref.shape, NEG, F32)
    l_ref[...] = jnp.zeros(l_ref.shape, F32)
    acc_ref[...] = jnp.zeros(acc_ref.shape, F32)


def _flash_update(sT, v, m_ref, l_ref, acc_ref, u):
    m_old = m_ref[u]
    m_new = jnp.maximum(m_old, jnp.max(sT, axis=0, keepdims=True))
    alpha = jnp.exp2(m_old - m_new)
    p = jnp.exp2(sT - m_new)
    l_ref[u] = alpha * l_ref[u] + jnp.sum(p, axis=0, keepdims=True)
    pv = lax.dot_general(v, p.astype(BF16), (((0,), (0,)), ((), ())), preferred_element_type=F32)
    acc_ref[u] = alpha * acc_ref[u] + pv
    m_ref[u] = m_new


def _tri_mask(tk, tq, keep_upper):
    r = lax.broadcasted_iota(I32, (tk, tq), 0)
    c = lax.broadcasted_iota(I32, (tk, tq), 1)
    keep = (r <= c) if keep_upper else (r > c)
    return jnp.where(keep, 0.0, NEG)


def _diff_attn_kernel(slopes_ref, q1_ref, q2_ref, k1_ref, k2_ref, v_ref, feat_ref, lam_ref, sw_ref, o_ref,
                      qa_ref, m_ref, l_ref, acc_ref, *, tq, lambda_init):
    h = pl.program_id(1)
    i = pl.program_id(2)
    _flash_init(m_ref, l_ref, acc_ref)
    qfeat = jnp.broadcast_to(_alibi_query_features(slopes_ref[h] * LOG2E, i * tq), (tq, 128)).astype(BF16)
    for c, q_ref in enumerate((q1_ref, q2_ref)):
        qa_ref[c] = jnp.concatenate([q_ref[...], qfeat], axis=1)

    def tile(j, mask):
        k0 = pl.multiple_of(j * tq, tq)
        v = v_ref[pl.ds(k0, tq), :]
        feat = feat_ref[pl.ds(k0, tq), :]
        for c, k_ref in enumerate((k1_ref, k2_ref)):
            sT = _nt_dot(jnp.concatenate([k_ref[pl.ds(k0, tq), :], feat], axis=1), qa_ref[c])
            if mask is not None:
                sT = sT + mask
            _flash_update(sT, v, m_ref, l_ref, acc_ref, c)

    def body(j, carry):
        tile(j, None)
        return carry

    lax.fori_loop(0, i, body, 0)
    tile(i, _tri_mask(tq, tq, True))

    lam = lam_ref[...]
    lmbda = (jnp.exp(jnp.sum(lam[0:1] * lam[1:2], axis=-1, keepdims=True))
             - jnp.exp(jnp.sum(lam[2:3] * lam[3:4], axis=-1, keepdims=True)) + lambda_init)
    oT = acc_ref[0] / l_ref[0] - lmbda * (acc_ref[1] / l_ref[1])
    o = oT.T
    y = o * lax.rsqrt(jnp.mean(o * o, axis=-1, keepdims=True) + EPS)
    y = (y * sw_ref[...]) * (1.0 - lambda_init)
    o_ref[...] = y.astype(o_ref.dtype)


def _diff_attention(qkv, lam, subln_w, lambda_init, B, S, tq=512):
    T = B * S
    H, dh, dv = A_HEADS, A_DH, A_DV
    n = S // tq
    slopes = _alibi_slopes(H)
    kb = H * 2
    vb = (2 * H * 2 * dh) // dv
    grid_spec = pltpu.PrefetchScalarGridSpec(
        num_scalar_prefetch=1,
        grid=(B, H, n),
        in_specs=[
            pl.BlockSpec((tq, dh), lambda b, h, i, sl: (b * n + i, 2 * h)),
            pl.BlockSpec((tq, dh), lambda b, h, i, sl: (b * n + i, 2 * h + 1)),
            pl.BlockSpec((S, dh), lambda b, h, i, sl: (b, kb + 2 * h)),
            pl.BlockSpec((S, dh), lambda b, h, i, sl: (b, kb + 2 * h + 1)),
            pl.BlockSpec((S, dv), lambda b, h, i, sl: (b, vb + h)),
            pl.BlockSpec((S, 128), lambda b, h, i, sl: (0, 0)),
            pl.BlockSpec((4, dh), lambda b, h, i, sl: (0, 0)),
            pl.BlockSpec((1, dv), lambda b, h, i, sl: (0, 0)),
        ],
        out_specs=pl.BlockSpec((tq, dv), lambda b, h, i, sl: (b * n + i, h)),
        scratch_shapes=[pltpu.VMEM((2, tq, 2 * dh), BF16), pltpu.VMEM((2, 1, tq), F32),
                        pltpu.VMEM((2, 1, tq), F32), pltpu.VMEM((2, dv, tq), F32)],
    )
    return pl.pallas_call(
        functools.partial(_diff_attn_kernel, tq=tq, lambda_init=lambda_init),
        out_shape=jax.ShapeDtypeStruct((T, H * dv), BF16),
        grid_spec=grid_spec,
        compiler_params=_cparams(("parallel", "parallel", "arbitrary")),
        name="diff_attention",
    )(slopes, qkv, qkv, qkv, qkv, qkv, _key_features(S), lam, subln_w.reshape(1, dv))


def _router_kernel(h_ref, nw_ref, wr_ref, br_ref, tri_ref, xn_ref, ids_ref, wts_ref, cnt_ref, base_ref):
    @pl.when(pl.program_id(0) == 0)
    def _():
        base_ref[...] = jnp.zeros(base_ref.shape, F32)

    x = h_ref[...]
    xn = (x * lax.rsqrt(jnp.mean(x * x, axis=-1, keepdims=True) + EPS)) * nw_ref[...]
    xn_ref[...] = xn
    logits = jnp.dot(xn, wr_ref[...], preferred_element_type=F32,
                     precision=lax.Precision.HIGHEST) + br_ref[...]
    lane = lax.broadcasted_iota(I32, logits.shape, 1)
    big = jnp.int32(1 << 20)
    is_g = lane < N_GROUPS
    is_e = (lane >= N_GROUPS) & (lane < N_GROUPS + N_EXPERTS)

    lg = jnp.where(is_g, logits, NEG)
    mg = jnp.max(lg, axis=-1, keepdims=True)
    g_top = jnp.min(jnp.where(lg == mg, lane, big), axis=-1, keepdims=True)
    wg = 1.0 / jnp.sum(jnp.where(is_g, jnp.exp(lg - mg), 0.0), axis=-1, keepdims=True)

    in_grp = is_e & (lax.shift_right_logical(lane - N_GROUPS, 3) == g_top)
    le = jnp.where(in_grp, logits, NEG)
    me = jnp.max(le, axis=-1, keepdims=True)
    ee = jnp.where(in_grp, jnp.exp(le - me), 0.0)
    pe = ee / jnp.sum(ee, axis=-1, keepdims=True)
    pe = jnp.where(in_grp, pe, -1.0)
    p1 = jnp.max(pe, axis=-1, keepdims=True)
    i1 = jnp.min(jnp.where(pe == p1, lane, big), axis=-1, keepdims=True)
    pe2 = jnp.where(lane == i1, -1.0, pe)
    p2 = jnp.max(pe2, axis=-1, keepdims=True)
    i2 = jnp.min(jnp.where(pe2 == p2, lane, big), axis=-1, keepdims=True)
    den = p1 + p2
    w0 = wg * p1 / den
    w1 = wg * p2 / den

    oh1 = lane == i1
    oh2 = lane == i2
    both = jnp.where(oh1 | oh2, 1.0, 0.0)
    before = jnp.dot(tri_ref[...], both.astype(BF16), preferred_element_type=F32) + base_ref[...]
    pos0 = jnp.sum(jnp.where(oh1, before, 0.0), axis=-1, keepdims=True).astype(I32)
    pos1 = jnp.sum(jnp.where(oh2, before, 0.0), axis=-1, keepdims=True).astype(I32)
    total = base_ref[...] + jnp.sum(both, axis=0, keepdims=True)
    base_ref[...] = total
    cnt_ref[...] = total

    ids_ref[...] = jnp.where(lane == 0, i1 - N_GROUPS,
                             jnp.where(lane == 1, i2 - N_GROUPS,
                                       jnp.where(lane == 2, pos0, jnp.where(lane == 3, pos1, 0))))
    wts_ref[...] = jnp.where(lane == 0, w0, jnp.where(lane == 1, w1, 0.0))


def _router(h, norm_w, w_group, b_group, w_expert, b_expert, tm=512):
    T, D = h.shape
    pad = ROUTER_LANES - N_GROUPS - N_EXPERTS
    wr = jnp.concatenate([w_group, w_expert, jnp.zeros((D, pad), F32)], axis=1)
    br = jnp.concatenate([b_group, b_expert, jnp.zeros((pad,), F32)]).reshape(1, ROUTER_LANES)
    tri = jnp.asarray(np.tril(np.ones((tm, tm), np.float32), -1), BF16)
    return pl.pallas_call(
        _router_kernel,
        out_shape=(jax.ShapeDtypeStruct((T, D), F32),
                   jax.ShapeDtypeStruct((T, ROUTER_LANES), I32),
                   jax.ShapeDtypeStruct((T, ROUTER_LANES), F32),
                   jax.ShapeDtypeStruct((1, ROUTER_LANES), F32)),
        grid=(T // tm,),
        in_specs=[pl.BlockSpec((tm, D), lambda i: (i, 0)),
                  pl.BlockSpec((1, D), lambda i: (0, 0)),
                  pl.BlockSpec((D, ROUTER_LANES), lambda i: (0, 0)),
                  pl.BlockSpec((1, ROUTER_LANES), lambda i: (0, 0)),
                  pl.BlockSpec((tm, tm), lambda i: (0, 0))],
        out_specs=(pl.BlockSpec((tm, D), lambda i: (i, 0)),
                   pl.BlockSpec((tm, ROUTER_LANES), lambda i: (i, 0)),
                   pl.BlockSpec((tm, ROUTER_LANES), lambda i: (i, 0)),
                   pl.BlockSpec((1, ROUTER_LANES), lambda i: (0, 0))),
        scratch_shapes=[pltpu.VMEM((1, ROUTER_LANES), F32)],
        compiler_params=_cparams(("arbitrary",)),
        name="moe_router",
    )(h, norm_w.reshape(1, D), wr, br, tri)


def _dispatch_kernel(dest_ref, xn_ref, xs_in_ref, xs_ref, sem, *, tm):
    del xs_in_ref
    base = pl.program_id(0) * (2 * tm)

    def row_copy(t, slot):
        d = dest_ref[base + 2 * t + slot]
        return pltpu.make_async_copy(xn_ref.at[pl.ds(t, 1)], xs_ref.at[pl.ds(d, 1)], sem)

    def start(t, c):
        row_copy(t, 0).start()
        row_copy(t, 1).start()
        return c

    def wait(t, c):
        row_copy(t, 0).wait()
        row_copy(t, 1).wait()
        return c

    lax.fori_loop(0, tm, start, 0, unroll=DMA_UNROLL)
    lax.fori_loop(0, tm, wait, 0, unroll=DMA_UNROLL)


def _dispatch(xn, dest_flat, n_rows, tm=256):
    T, D = xn.shape
    grid_spec = pltpu.PrefetchScalarGridSpec(
        num_scalar_prefetch=1,
        grid=(T // tm,),
        in_specs=[pl.BlockSpec((tm, D), lambda i, d: (i, 0)),
                  pl.BlockSpec(memory_space=pl.ANY)],
        out_specs=pl.BlockSpec(memory_space=pl.ANY),
        scratch_shapes=[pltpu.SemaphoreType.DMA],
    )
    return pl.pallas_call(
        functools.partial(_dispatch_kernel, tm=tm),
        out_shape=jax.ShapeDtypeStruct((n_rows, D), F32),
        grid_spec=grid_spec,
        input_output_aliases={2: 0},
        compiler_params=_cparams(("arbitrary",)),
        name="moe_dispatch",
    )(dest_flat, xn, jnp.zeros((n_rows, D), F32))


def _moe_plan(counts, n_blk):
    RB = MOE_ROWS
    nb_e = (counts + RB - 1) // RB
    end_b = jnp.cumsum(nb_e)
    start_b = end_b - nb_e
    n_used = end_b[-1:].astype(I32)
    r = jnp.arange(n_blk, dtype=I32)
    be = jnp.minimum(jnp.sum((end_b[None, :] <= r[:, None]).astype(I32), axis=1), N_EXPERTS - 1)
    active = nb_e > 0
    ordinal = jnp.cumsum(active.astype(I32)) - 1
    idx = jnp.arange(N_EXPERTS, dtype=I32)
    later = jnp.where(active[None, :] & (idx[None, :] > idx[:, None]), idx[None, :], N_EXPERTS)
    nxt_e = jnp.min(later, axis=1)
    nxt_e = jnp.where(nxt_e == N_EXPERTS, -1, nxt_e)
    first = ((r == start_b[be]) & (r < n_used[0])).astype(I32)
    plan = jnp.stack([be, first, ordinal[be] % 2, nxt_e[be]]).astype(I32)
    return plan, n_used, (start_b * RB).astype(I32)


def _stream_expert_weights(plan_ref, r, w_hbms, wbuf, sem, layer):
    e, slot, nxt = plan_ref[0, r], plan_ref[2, r], plan_ref[3, r]

    def copies(expert, s):
        return [pltpu.make_async_copy(w.at[layer, expert], wbuf.at[s, k], sem.at[s, k])
                for k, w in enumerate(w_hbms)]

    @pl.when(plan_ref[1, r] == 1)
    def _():
        @pl.when(r == 0)
        def _():
            for c in copies(e, slot):
                c.start()

        for c in copies(e, slot):
            c.wait()

        @pl.when(nxt >= 0)
        def _():
            for c in copies(nxt, 1 - slot):
                c.start()

    return slot


def _gate_up_kernel(plan_ref, nu_ref, x_ref, wg_hbm, wu_hbm, h_ref, wbuf, sem, *, layer):
    r = pl.program_id(0)

    @pl.when(r < nu_ref[0])
    def _():
        slot = _stream_expert_weights(plan_ref, r, (wg_hbm, wu_hbm), wbuf, sem, layer)
        x = x_ref[...]
        g = jnp.dot(x, wbuf[slot, 0], preferred_element_type=F32)
        u = jnp.dot(x, wbuf[slot, 1], preferred_element_type=F32)
        h_ref[...] = (jax.nn.silu(g) * u).astype(h_ref.dtype)

    @pl.when(r >= nu_ref[0])
    def _():
        h_ref[...] = jnp.zeros(h_ref.shape, h_ref.dtype)


def _down_kernel(plan_ref, nu_ref, h_ref, wd_hbm, y_ref, wbuf, sem, *, layer):
    r = pl.program_id(0)

    @pl.when(r < nu_ref[0])
    def _():
        slot = _stream_expert_weights(plan_ref, r, (wd_hbm,), wbuf, sem, layer)
        y_ref[...] = jnp.dot(h_ref[...].astype(F32), wbuf[slot, 0], preferred_element_type=F32)

    @pl.when(r >= nu_ref[0])
    def _():
        y_ref[...] = jnp.zeros(y_ref.shape, y_ref.dtype)


def _expert_mlp(xs, plan, n_used, w_gate, w_up, w_down, layer):
    n_rows, D = xs.shape
    RB = MOE_ROWS
    n_blk = n_rows // RB
    F = D_EXPERT

    def used_blk(r, plan, nu):
        return (jnp.minimum(r, nu[0] - 1), 0)

    h = pl.pallas_call(
        functools.partial(_gate_up_kernel, layer=layer),
        out_shape=jax.ShapeDtypeStruct((n_rows, F), BF16),
        grid_spec=pltpu.PrefetchScalarGridSpec(
            num_scalar_prefetch=2,
            grid=(n_blk,),
            in_specs=[pl.BlockSpec((RB, D), used_blk),
                      pl.BlockSpec(memory_space=pl.ANY),
                      pl.BlockSpec(memory_space=pl.ANY)],
            out_specs=pl.BlockSpec((RB, F), lambda r, plan, nu: (r, 0)),
            scratch_shapes=[pltpu.VMEM((2, 2, D, F), F32), pltpu.SemaphoreType.DMA((2, 2))],
        ),
        compiler_params=_cparams(("arbitrary",)),
        name="moe_gate_up",
    )(plan, n_used, xs, w_gate, w_up)
    y = pl.pallas_call(
        functools.partial(_down_kernel, layer=layer),
        out_shape=jax.ShapeDtypeStruct((n_rows, D), F32),
        grid_spec=pltpu.PrefetchScalarGridSpec(
            num_scalar_prefetch=2,
            grid=(n_blk,),
            in_specs=[pl.BlockSpec((RB, F), used_blk),
                      pl.BlockSpec(memory_space=pl.ANY)],
            out_specs=pl.BlockSpec((RB, D), lambda r, plan, nu: (r, 0)),
            scratch_shapes=[pltpu.VMEM((2, 1, F, D), F32), pltpu.SemaphoreType.DMA((2, 1))],
        ),
        compiler_params=_cparams(("arbitrary",)),
        name="moe_down",
    )(plan, n_used, h, w_down)
    return y


def _combine_kernel(dest_ref, resid_ref, wts_ref, y_ref, o_ref, buf_ref, sem, *, tm):
    i = pl.program_id(0)
    n = pl.num_programs(0)

    def row_copy(tile, t, slot):
        d = dest_ref[tile * (2 * tm) + 2 * t + slot]
        par = tile % 2
        return pltpu.make_async_copy(y_ref.at[pl.ds(d, 1)], buf_ref.at[par, slot, pl.ds(t, 1)], sem.at[par])

    def gather(tile):
        def start(t, c):
            row_copy(tile, t, 0).start()
            row_copy(tile, t, 1).start()
            return c
        lax.fori_loop(0, tm, start, 0, unroll=DMA_UNROLL)

    @pl.when(i == 0)
    def _():
        gather(i)

    @pl.when(i + 1 < n)
    def _():
        gather(i + 1)

    def wait(t, c):
        row_copy(i, t, 0).wait()
        row_copy(i, t, 1).wait()
        return c

    lax.fori_loop(0, tm, wait, 0, unroll=DMA_UNROLL)
    par = i % 2
    w = wts_ref[...]
    o_ref[...] = resid_ref[...] + (w[:, 0:1] * buf_ref[par, 0] + w[:, 1:2] * buf_ref[par, 1])


def _combine(resid, wts, y, dest_flat, tm=128):
    T, D = resid.shape
    grid_spec = pltpu.PrefetchScalarGridSpec(
        num_scalar_prefetch=1,
        grid=(T // tm,),
        in_specs=[pl.BlockSpec((tm, D), lambda i, d: (i, 0)),
                  pl.BlockSpec((tm, ROUTER_LANES), lambda i, d: (i, 0)),
                  pl.BlockSpec(memory_space=pl.ANY)],
        out_specs=pl.BlockSpec((tm, D), lambda i, d: (i, 0)),
        scratch_shapes=[pltpu.VMEM((2, 2, tm, D), F32), pltpu.SemaphoreType.DMA((2,))],
    )
    return pl.pallas_call(
        functools.partial(_combine_kernel, tm=tm),
        out_shape=jax.ShapeDtypeStruct((T, D), F32),
        grid_spec=grid_spec,
        compiler_params=_cparams(("arbitrary",)),
        name="moe_combine",
    )(dest_flat, resid, wts, y)


def _hier_moe(h, norm_w, w_group, b_group, w_expert, b_expert, w_gate, w_up, w_down, layer):
    T, D = h.shape
    RB = MOE_ROWS
    n_blk = (T * 2) // RB + N_EXPERTS
    n_rows = n_blk * RB
    xn, ids, wts, cnt = _router(h, norm_w, w_group, b_group, w_expert, b_expert)
    counts = cnt[0, N_GROUPS:N_GROUPS + N_EXPERTS].astype(I32)
    plan, n_used, start_padded = _moe_plan(counts, n_blk)
    dest = (start_padded[ids[:, 0:2]] + ids[:, 2:4]).reshape(-1)
    xs = _dispatch(xn, dest, n_rows)
    y = _expert_mlp(xs, plan, n_used, w_gate, w_up, w_down, layer)
    return _combine(h, wts, y, dest)


def _compress_kernel(x_ref, w1_ref, w2_ref, pe_ref, o_ref):
    half = (CMP_LEN // 2) * B_D
    x = x_ref[...]
    w1 = w1_ref[...].astype(BF16)
    a = jnp.dot(x, w1[:half], preferred_element_type=F32)
    b = jnp.dot(x, w1[half:], preferred_element_type=F32)
    pe = jnp.broadcast_to(pe_ref[...], (8, 2 * half)).astype(BF16)
    pt = jnp.dot(pe, w1, preferred_element_type=F32)[0:1]
    hid = a + pltpu.roll(b, b.shape[0] - 1, 0) + pt
    g = jax.nn.gelu(hid)
    o_ref[...] = jnp.dot(g.astype(BF16), w2_ref[...].astype(BF16),
                         preferred_element_type=F32).astype(o_ref.dtype)


def _compress(tb, w1, w2, pe_flat):
    _, B, nch, width = tb.shape
    G, d = B_GROUPS, B_D
    hid = w1.shape[-1]
    return pl.pallas_call(
        _compress_kernel,
        out_shape=jax.ShapeDtypeStruct((2, B, G, nch, d), BF16),
        grid=(2, B, G),
        in_specs=[pl.BlockSpec((None, None, nch, width), lambda kv, b, g: (kv * G + g, b, 0, 0)),
                  pl.BlockSpec((None, 2 * width, hid), lambda kv, b, g: (kv, 0, 0)),
                  pl.BlockSpec((None, hid, d), lambda kv, b, g: (kv, 0, 0)),
                  pl.BlockSpec((None, 1, 2 * width), lambda kv, b, g: (kv, 0, 0))],
        out_specs=pl.BlockSpec((None, None, None, nch, d), lambda kv, b, g: (kv, b, g, 0, 0)),
        compiler_params=_cparams(("arbitrary", "arbitrary", "arbitrary")),
        name="nsa_compress",
    )(tb, w1, w2, pe_flat)


def _cmp_select_kernel(slopes_ref, q_ref, kc_ref, vc_ref, ov_ref, o_ref, sel_ref, *, tq, n_cmp, n_blocks,
                       n_rank):
    g = pl.program_id(1)
    i = pl.program_id(2)
    d = B_D
    nch = kc_ref.shape[0]
    pos = i * tq + lax.broadcasted_iota(I32, (1, tq), 1)
    c = lax.broadcasted_iota(I32, (nch, 1), 0)
    dist_i = pos - (c * CMP_STRIDE + (CMP_LEN - 1))
    valid = (dist_i >= 0) & (c < n_cmp)
    dist = dist_i.astype(F32)
    any_valid = jnp.where(pos >= CMP_LEN - 1, 1.0, 0.0)
    kc = kc_ref[...]
    vc = vc_ref[...]
    psum = jnp.zeros((nch, tq), F32)
    for hh in range(B_HPG):
        s = _nt_dot(kc, q_ref[:, hh * d:(hh + 1) * d])
        s = jnp.where(valid, s - (slopes_ref[g * B_HPG + hh] * LOG2E) * dist, NEG)
        e = jnp.exp2(s - jnp.max(s, axis=0, keepdims=True))
        p = (e / jnp.sum(e, axis=0, keepdims=True)) * any_valid
        oT = lax.dot_general(vc, p.astype(BF16), (((0,), (0,)), ((), ())), preferred_element_type=F32)
        o_ref[:, hh * d:(hh + 1) * d] = oT.T
        psum = psum + p

    p_hi = psum.astype(BF16)
    p_lo = (psum - p_hi.astype(F32)).astype(BF16)
    ov = ov_ref[...]
    imp = jnp.dot(ov, p_hi, preferred_element_type=F32) + jnp.dot(ov, p_lo, preferred_element_type=F32)
    imp = imp[:n_rank]

    blk = lax.broadcasted_iota(I32, (n_rank, 1), 0)
    cur = lax.shift_right_logical(pos, SLC_BLOCK.bit_length() - 1)
    forced = (blk == 0) | (blk == cur) | (blk == cur - 1)
    causal = blk * SLC_BLOCK <= pos
    score = jnp.where(forced, FORCE, jnp.where(causal, imp, -FORCE))
    score = jnp.where(blk < n_blocks, score, -2.0 * FORCE)
    rank = jnp.zeros(score.shape, F32)
    for jp in range(n_blocks):
        row = score[jp:jp + 1, :]
        tie = jnp.where(blk > jp, 1.0, 0.0)
        rank = rank + jnp.where(row > score, 1.0, jnp.where(row == score, tie, 0.0))
    sel_add = jnp.where((rank < float(SLC_TOPN)) & (blk < n_blocks), 0.0, NEG)
    pad = jnp.zeros((128 - n_rank, tq), F32)
    sel_t = jnp.concatenate([jnp.where(blk < n_blocks, sel_add, 0.0), pad], axis=0) if n_rank < 128 else sel_add
    sel_ref[...] = sel_t.T.astype(sel_ref.dtype)


def _cmp_select(projm, cmp_kv, overlap, B, S, tq=256):
    T = B * S
    G, d = B_GROUPS, B_D
    nq = S // tq
    nch = cmp_kv.shape[3]
    n_cmp = (S - CMP_LEN) // CMP_STRIDE + 1
    n_blocks = S // SLC_BLOCK
    assert SLC_TOPN <= n_blocks <= FEAT_BLK
    slopes = _alibi_slopes(B_HEADS)
    grid_spec = pltpu.PrefetchScalarGridSpec(
        num_scalar_prefetch=1,
        grid=(B, G, nq),
        in_specs=[pl.BlockSpec((tq, B_HPG * d), lambda b, g, i, sl: (b * nq + i, g)),
                  pl.BlockSpec((None, None, None, nch, d), lambda b, g, i, sl: (0, b, g, 0, 0)),
                  pl.BlockSpec((None, None, None, nch, d), lambda b, g, i, sl: (1, b, g, 0, 0)),
                  pl.BlockSpec((128, nch), lambda b, g, i, sl: (0, 0))],
        out_specs=(pl.BlockSpec((tq, B_HPG * d), lambda b, g, i, sl: (b * nq + i, g)),
                   pl.BlockSpec((None, None, tq, 128), lambda b, g, i, sl: (b, g, i, 0))),
    )
    return pl.pallas_call(
        functools.partial(_cmp_select_kernel, tq=tq, n_cmp=n_cmp, n_blocks=n_blocks,
                          n_rank=-(-n_blocks // 8) * 8),
        out_shape=(jax.ShapeDtypeStruct((T, B_HEADS * d), F32),
                   jax.ShapeDtypeStruct((B, G, S, 128), BF16)),
        grid_spec=grid_spec,
        compiler_params=_cparams(("parallel", "parallel", "arbitrary")),
        name="nsa_cmp_select",
    )(slopes, projm, cmp_kv, cmp_kv, overlap)


def _write_heads_T(o_ref, acc_ref, l_ref):
    for hh in range(B_HPG):
        o_ref[:, hh * B_D:(hh + 1) * B_D] = (acc_ref[hh] / l_ref[hh]).T


def _stage_queries(q_ref, qa_ref, slopes_ref, g, q0, sel_add):
    tq = q_ref.shape[0]
    lane = lax.broadcasted_iota(I32, (tq, 128), 1)
    for hh in range(B_HPG):
        coef = jnp.broadcast_to(_alibi_query_features(slopes_ref[g * B_HPG + hh] * LOG2E, q0), (tq, 128))
        if sel_add is not None:
            coef = jnp.where(lane < FEAT_BLK, sel_add, coef)
        qa_ref[hh] = jnp.concatenate([q_ref[:, hh * B_D:(hh + 1) * B_D], coef.astype(BF16)], axis=1)


def _slc_attn_kernel(slopes_ref, q_ref, k_ref, v_ref, feat_ref, sel_ref, o_ref, qa_ref, m_ref, l_ref, acc_ref,
                     *, tq):
    g = pl.program_id(1)
    i = pl.program_id(2)
    _flash_init(m_ref, l_ref, acc_ref)
    _stage_queries(q_ref, qa_ref, slopes_ref, g, i * tq, sel_ref[...].astype(F32))

    def tile(j, causal):
        k0 = pl.multiple_of(j * tq, tq)
        ka = jnp.concatenate([k_ref[pl.ds(k0, tq), :], feat_ref[pl.ds(k0, tq), :]], axis=1)
        v = v_ref[pl.ds(k0, tq), :]
        for hh in range(B_HPG):
            sT = _nt_dot(ka, qa_ref[hh])
            if causal is not None:
                sT = sT + causal
            _flash_update(sT, v, m_ref, l_ref, acc_ref, hh)

    def body(j, carry):
        tile(j, None)
        return carry

    lax.fori_loop(0, i, body, 0)
    tile(i, _tri_mask(tq, tq, True))
    _write_heads_T(o_ref, acc_ref, l_ref)


def _slc_attention(projm, sel, B, S, tq=512):
    T = B * S
    G, d = B_GROUPS, B_D
    n = S // tq
    slopes = _alibi_slopes(B_HEADS)
    ksb = (B_HEADS * d + 2 * G * d) // d
    vsb = ksb + G
    grid_spec = pltpu.PrefetchScalarGridSpec(
        num_scalar_prefetch=1,
        grid=(B, G, n),
        in_specs=[
            pl.BlockSpec((tq, B_HPG * d), lambda b, g, i, sl: (b * n + i, g)),
            pl.BlockSpec((S, d), lambda b, g, i, sl: (b, ksb + g)),
            pl.BlockSpec((S, d), lambda b, g, i, sl: (b, vsb + g)),
            pl.BlockSpec((S, 128), lambda b, g, i, sl: (0, 0)),
            pl.BlockSpec((None, None, tq, 128), lambda b, g, i, sl: (b, g, i, 0)),
        ],
        out_specs=pl.BlockSpec((tq, B_HPG * d), lambda b, g, i, sl: (b * n + i, g)),
        scratch_shapes=[pltpu.VMEM((B_HPG, tq, 2 * d), BF16), pltpu.VMEM((B_HPG, 1, tq), F32),
                        pltpu.VMEM((B_HPG, 1, tq), F32), pltpu.VMEM((B_HPG, d, tq), F32)],
    )
    return pl.pallas_call(
        functools.partial(_slc_attn_kernel, tq=tq),
        out_shape=jax.ShapeDtypeStruct((T, B_HEADS * d), F32),
        grid_spec=grid_spec,
        compiler_params=_cparams(("parallel", "parallel", "arbitrary")),
        name="nsa_selected_attention",
    )(slopes, projm, projm, projm, _key_features(S), sel)


def _win_attn_kernel(slopes_ref, q_ref, k_ref, v_ref, feat_ref, o_ref, qa_ref, m_ref, l_ref, acc_ref, *, tq):
    g = pl.program_id(1)
    i = pl.program_id(2)
    _flash_init(m_ref, l_ref, acc_ref)
    _stage_queries(q_ref, qa_ref, slopes_ref, g, i * tq, None)

    def tile(j, mask):
        k0 = pl.multiple_of(j * tq, tq)
        ka = jnp.concatenate([k_ref[pl.ds(k0, tq), :], feat_ref[pl.ds(k0, tq), :]], axis=1)
        v = v_ref[pl.ds(k0, tq), :]
        for hh in range(B_HPG):
            sT = _nt_dot(ka, qa_ref[hh]) + mask
            _flash_update(sT, v, m_ref, l_ref, acc_ref, hh)

    @pl.when(i > 0)
    def _():
        tile(i - 1, _tri_mask(tq, tq, False))

    tile(i, _tri_mask(tq, tq, True))
    _write_heads_T(o_ref, acc_ref, l_ref)


def _win_attention(projm, B, S):
    T = B * S
    G, d = B_GROUPS, B_D
    tq = WINDOW
    n = S // tq
    slopes = _alibi_slopes(B_HEADS)
    kwb = (B_HEADS * d + 4 * G * d) // d
    vwb = kwb + G
    grid_spec = pltpu.PrefetchScalarGridSpec(
        num_scalar_prefetch=1,
        grid=(B, G, n),
        in_specs=[
            pl.BlockSpec((tq, B_HPG * d), lambda b, g, i, sl: (b * n + i, g)),
            pl.BlockSpec((S, d), lambda b, g, i, sl: (b, kwb + g)),
            pl.BlockSpec((S, d), lambda b, g, i, sl: (b, vwb + g)),
            pl.BlockSpec((S, 128), lambda b, g, i, sl: (0, 0)),
        ],
        out_specs=pl.BlockSpec((tq, B_HPG * d), lambda b, g, i, sl: (b * n + i, g)),
        scratch_shapes=[pltpu.VMEM((B_HPG, tq, 2 * d), BF16), pltpu.VMEM((B_HPG, 1, tq), F32),
                        pltpu.VMEM((B_HPG, 1, tq), F32), pltpu.VMEM((B_HPG, d, tq), F32)],
    )
    return pl.pallas_call(
        functools.partial(_win_attn_kernel, tq=tq),
        out_shape=jax.ShapeDtypeStruct((T, B_HEADS * d), F32),
        grid_spec=grid_spec,
        compiler_params=_cparams(("parallel", "parallel", "arbitrary")),
        name="nsa_window_attention",
    )(slopes, projm, projm, projm, _key_features(S))


def _gate_mix_kernel(gl_ref, oc_ref, os_ref, ow_ref, o_ref):
    sig = jax.nn.sigmoid(gl_ref[...])
    d = B_D
    for hd in range(B_HEADS):
        sl = slice(hd * d, (hd + 1) * d)
        o = (sig[:, 3 * hd:3 * hd + 1] * oc_ref[:, sl] + sig[:, 3 * hd + 1:3 * hd + 2] * os_ref[:, sl]
             + sig[:, 3 * hd + 2:3 * hd + 3] * ow_ref[:, sl])
        o_ref[:, sl] = o.astype(o_ref.dtype)


def _gate_mix(gate_logits, o_cmp, o_slc, o_win, tm=256):
    T, W = o_cmp.shape
    spec = pl.BlockSpec((tm, W), lambda i: (i, 0))
    return pl.pallas_call(
        _gate_mix_kernel,
        out_shape=jax.ShapeDtypeStruct((T, W), BF16),
        grid=(T // tm,),
        in_specs=[pl.BlockSpec((tm, 128), lambda i: (i, 0)), spec, spec, spec],
        out_specs=spec,
        compiler_params=_cparams(("parallel",)),
        name="nsa_gate_mix",
    )(gate_logits, o_cmp, o_slc, o_win)


def _overlap_matrix(S, n_rows):
    nc = (S - CMP_LEN) // CMP_STRIDE + 1
    nsb = S // SLC_BLOCK
    cs = np.arange(nc) * CMP_STRIDE
    ss = np.arange(nsb) * SLC_BLOCK
    ov = np.clip(np.minimum(cs[:, None] + CMP_LEN, ss[None, :] + SLC_BLOCK)
                 - np.maximum(cs[:, None], ss[None, :]), 0, None) / CMP_LEN
    out = np.zeros((128, n_rows), np.float32)
    out[:nsb, :nc] = ov.T
    return jnp.asarray(out, BF16)


def _nsa_attention(xn, w_in, cmp_pos, cmp_w1, cmp_w2, B, S):
    T = B * S
    H, G, d = B_HEADS, B_GROUPS, B_D
    n_main = H * d + 6 * G * d
    col_scale = jnp.concatenate([jnp.full((H * d,), d ** -0.5 * LOG2E, F32), jnp.ones((n_main - H * d,), F32)])
    projm = _matmul(xn, w_in, n_main, BF16, col_scale=col_scale)
    w_gate = jnp.pad(w_in[:, n_main:], ((0, 0), (0, 128 - 3 * H)))
    gate_logits = _matmul(xn, w_gate, 128, F32, tn=128)

    nch = S // CMP_STRIDE
    tb = projm[:, H * d:H * d + 2 * G * d].reshape(B, nch, CMP_STRIDE, 2 * G, d)
    tb = jnp.transpose(tb, (3, 0, 1, 2, 4)).reshape(2 * G, B, nch, CMP_STRIDE * d)
    cmp_kv = _compress(tb, cmp_w1, cmp_w2, cmp_pos.reshape(2, 1, CMP_LEN * d))

    o_cmp, sel = _cmp_select(projm, cmp_kv, _overlap_matrix(S, nch), B, S)
    o_slc = _slc_attention(projm, sel, B, S)
    o_win = _win_attention(projm, B, S)
    return _gate_mix(gate_logits, o_cmp, o_slc, o_win)


def kernel(x, attn_norm_w, ffn_norm_w, final_norm_w, a_w_in, a_lambda, a_subln_w, a_w_out, b_w_in,
           b_cmp_pos, b_cmp_w1, b_cmp_w2, b_w_out, moe_w_group, moe_b_group, moe_w_expert,
           moe_b_expert, moe_w_gate, moe_w_up, moe_w_down):
    B, S, D = x.shape
    T = B * S
    depth = attn_norm_w.shape[0]
    h = x.reshape(T, D)
    for i in range(depth):
        xn = _rmsnorm(h, attn_norm_w[i], BF16)
        j = i // 2
        if i % 2 == 0:
            lambda_init = 0.8 - 0.6 * math.exp(-0.3 * i)
            n_q = A_HEADS * 2 * A_DH
            n_in = 2 * n_q + A_HEADS * A_DV
            col_scale = jnp.concatenate([jnp.full((n_q,), A_DH ** -0.5 * LOG2E, F32),
                                         jnp.ones((n_in - n_q,), F32)])
            qkv = _matmul(xn, a_w_in[j], n_in, BF16, col_scale=col_scale)
            o = _diff_attention(qkv, a_lambda[j], a_subln_w[j], lambda_init, B, S)
            h = _matmul(o, a_w_out[j], D, F32, resid=h)
        else:
            o = _nsa_attention(xn, b_w_in[j], b_cmp_pos[j], b_cmp_w1[j], b_cmp_w2[j], B, S)
            h = _matmul(o, b_w_out[j], D, F32, resid=h)
        h = _hier_moe(h, ffn_norm_w[i], moe_w_group[i], moe_b_group[i], moe_w_expert[i], moe_b_expert[i],
                      moe_w_gate, moe_w_up, moe_w_down, i)
    return _rmsnorm(h, final_norm_w, F32).reshape(B, S, D)
```

```python
import functools
import math

import numpy as np
import jax
import jax.numpy as jnp
from jax import lax
from jax.experimental import pallas as pl
from jax.experimental.pallas import tpu as pltpu

F32 = jnp.float32
BF16 = jnp.bfloat16
I32 = jnp.int32

EPS = 1e-6
NEG = -1e30
FORCE = 1e4
LOG2E = math.log2(math.e)

A_HEADS = 8
A_DH = 128
A_DV = 256
B_HEADS = 16
B_GROUPS = 4
B_HPG = 4
B_D = 128
CMP_LEN = 32
CMP_STRIDE = 16
SLC_BLOCK = 64
SLC_TOPN = 16
WINDOW = 512
N_GROUPS = 4
EPG = 8
N_EXPERTS = 32
D_EXPERT = 1024
ROUTER_LANES = 128
MOE_ROWS = 256
DMA_UNROLL = 8
ATTN_QC = 512

VMEM_LIMIT = 56 * 1024 * 1024


def _cparams(sem, flags=None):
    return pltpu.CompilerParams(dimension_semantics=sem, vmem_limit_bytes=VMEM_LIMIT, flags=flags)


def _alibi_slopes(n):
    return jnp.asarray(np.array([2.0 ** (-8.0 * (i + 1) / n) for i in range(n)], dtype=np.float32))


def _nt_dot(a, b):
    return lax.dot_general(a, b, (((1,), (1,)), ((), ())), preferred_element_type=F32)


def _rms_kernel(x_ref, w_ref, o_ref):
    x = x_ref[...]
    y = x * lax.rsqrt(jnp.mean(x * x, axis=-1, keepdims=True) + EPS)
    o_ref[...] = (y * w_ref[...]).astype(o_ref.dtype)


def _rmsnorm(x, w, out_dtype, tm=512):
    T, D = x.shape
    return pl.pallas_call(
        _rms_kernel,
        out_shape=jax.ShapeDtypeStruct((T, D), out_dtype),
        grid=(T // tm,),
        in_specs=[pl.BlockSpec((tm, D), lambda i: (i, 0)),
                  pl.BlockSpec((1, D), lambda i: (0, 0))],
        out_specs=pl.BlockSpec((tm, D), lambda i: (i, 0)),
        compiler_params=_cparams(("parallel",)),
        name="rmsnorm",
    )(x, w.reshape(1, D))


def _mm_kernel(*refs, has_scale, has_resid):
    a_ref, w_ref = refs[0], refs[1]
    k = 2
    scale_ref = resid_ref = None
    if has_scale:
        scale_ref = refs[k]
        k += 1
    if has_resid:
        resid_ref = refs[k]
        k += 1
    o_ref, wbf_ref = refs[k], refs[k + 1]

    @pl.when(pl.program_id(1) == 0)
    def _():
        wbf_ref[...] = w_ref[...].astype(BF16)

    acc = jnp.dot(a_ref[...], wbf_ref[...], preferred_element_type=F32)
    if has_scale:
        acc = acc * scale_ref[...]
    if has_resid:
        acc = acc + resid_ref[...]
    o_ref[...] = acc.astype(o_ref.dtype)


def _matmul(a, w, n_out, out_dtype, *, col_scale=None, resid=None, tm=1024, tn=512):
    M, K = a.shape
    assert M % tm == 0 and n_out % tn == 0 and w.shape[0] == K and w.shape[1] >= n_out
    in_specs = [pl.BlockSpec((tm, K), lambda j, i: (i, 0)),
                pl.BlockSpec((K, tn), lambda j, i: (0, j))]
    args = [a, w]
    if col_scale is not None:
        in_specs.append(pl.BlockSpec((1, tn), lambda j, i: (0, j)))
        args.append(col_scale.reshape(1, n_out))
    if resid is not None:
        in_specs.append(pl.BlockSpec((tm, tn), lambda j, i: (i, j)))
        args.append(resid)
    return pl.pallas_call(
        functools.partial(_mm_kernel, has_scale=col_scale is not None, has_resid=resid is not None),
        out_shape=jax.ShapeDtypeStruct((M, n_out), out_dtype),
        grid=(n_out // tn, M // tm),
        in_specs=in_specs,
        out_specs=pl.BlockSpec((tm, tn), lambda j, i: (i, j)),
        scratch_shapes=[pltpu.VMEM((K, tn), BF16)],
        compiler_params=_cparams(("parallel", "arbitrary")),
        name="matmul",
    )(*args)


FEAT_BLK = 64
FEAT_POS = 67
FEAT_ONE = 70


def _key_features(S):
    assert S // SLC_BLOCK <= FEAT_BLK
    j = np.arange(S)
    f = np.zeros((S, 128), np.float32)
    f[j, j // SLC_BLOCK] = 1.0
    f[:, FEAT_BLK:FEAT_BLK + 3] = (j // SLC_BLOCK)[:, None]
    f[:, FEAT_POS:FEAT_POS + 3] = (j % SLC_BLOCK)[:, None]
    f[:, FEAT_ONE:FEAT_ONE + 3] = 1.0
    return jnp.asarray(f, BF16)


def _alibi_query_features(slope2, q0):
    lane = lax.broadcasted_iota(I32, (1, 128), 1)
    base = jnp.where((lane >= FEAT_BLK) & (lane < FEAT_BLK + 3), slope2 * float(SLC_BLOCK),
                     jnp.where((lane >= FEAT_POS) & (lane < FEAT_POS + 3), slope2,
                               jnp.where((lane >= FEAT_ONE) & (lane < FEAT_ONE + 3),
                                         -slope2 * q0.astype(F32), 0.0)))
    hi = base.astype(BF16).astype(F32)
    r1 = base - hi
    lo = r1.astype(BF16).astype(F32)
    lo2 = (r1 - lo).astype(BF16).astype(F32)
    first = (lane == FEAT_BLK) | (lane == FEAT_POS) | (lane == FEAT_ONE)
    second = (lane == FEAT_BLK + 1) | (lane == FEAT_POS + 1) | (lane == FEAT_ONE + 1)
    return jnp.where(first, hi, jnp.where(second, lo, lo2))


def _flash_init(m_ref, l_ref, acc_ref):
    m_ref[...] = jnp.full(m_ref.shape, NEG, F32)
    l_ref[...] = jnp.zeros(l_ref.shape, F32)
    acc_ref[...] = jnp.zeros(acc_ref.shape, F32)


def _flash_probs(sT, m_ref, l_ref, u):
    m_old = m_ref[u]
    m_new = jnp.maximum(m_old, jnp.max(sT, axis=0, keepdims=True))
    alpha = jnp.exp2(m_old - m_new)
    p = jnp.exp2(sT - m_new)
    l_ref[u] = alpha * l_ref[u] + jnp.sum(p, axis=0, keepdims=True)
    m_ref[u] = m_new
    return p.astype(BF16), alpha


def _flash_accumulate(p, alpha, v, acc_ref, u):
    pv = lax.dot_general(v, p, (((0,), (0,)), ((), ())), preferred_element_type=F32)
    acc_ref[u] = alpha * acc_ref[u] + pv


def _flash_units(score_fns, v, m_ref, l_ref, acc_ref):
    n = len(score_fns)
    scores, probs = {}, {}
    for s in range(n + 2):
        if s < n:
            scores[s] = score_fns[s]()
        if 1 <= s <= n:
            probs[s - 1] = _flash_probs(scores.pop(s - 1), m_ref, l_ref, s - 1)
        if 2 <= s <= n + 1:
            p, alpha = probs.pop(s - 2)
            _flash_accumulate(p, alpha, v, acc_ref, s - 2)


def _tri_mask(tk, tq, keep_upper):
    r = lax.broadcasted_iota(I32, (tk, tq), 0)
    c = lax.broadcasted_iota(I32, (tk, tq), 1)
    keep = (r <= c) if keep_upper else (r > c)
    return jnp.where(keep, 0.0, NEG)


def _diff_attn_kernel(slopes_ref, q1_ref, q2_ref, k1_ref, k2_ref, v_ref, feat_ref, lam_ref, sw_ref, o_ref,
                      qa_ref, m_ref, l_ref, acc_ref, *, tq, lambda_init):
    h = pl.program_id(1)
    i = pl.program_id(2)
    _flash_init(m_ref, l_ref, acc_ref)
    qfeat = jnp.broadcast_to(_alibi_query_features(slopes_ref[h] * LOG2E, i * tq), (tq, 128)).astype(BF16)
    for c, q_ref in enumerate((q1_ref, q2_ref)):
        qa_ref[c] = jnp.concatenate([q_ref[...], qfeat], axis=1)

    def tile(j, mask):
        k0 = pl.multiple_of(j * tq, tq)
        v = v_ref[pl.ds(k0, tq), :]
        feat = feat_ref[pl.ds(k0, tq), :]
        qc = m_ref.shape[-1]

        def score_fn(c, k_ref, x):
            def fn():
                ka = jnp.concatenate([k_ref[pl.ds(k0, tq), :], feat], axis=1)
                sT = _nt_dot(ka, qa_ref[c, x * qc:(x + 1) * qc, :])
                return sT if mask is None else sT + mask[:, x * qc:(x + 1) * qc]
            return fn

        _flash_units([score_fn(c, k_ref, x) for c, k_ref in enumerate((k1_ref, k2_ref))
                      for x in range(tq // qc)], v, m_ref, l_ref, acc_ref)

    def body(j, carry):
        tile(j, None)
        return carry

    lax.fori_loop(0, i, body, 0)
    tile(i, _tri_mask(tq, tq, True))

    lam = lam_ref[...]
    lmbda = (jnp.exp(jnp.sum(lam[0:1] * lam[1:2], axis=-1, keepdims=True))
             - jnp.exp(jnp.sum(lam[2:3] * lam[3:4], axis=-1, keepdims=True)) + lambda_init)
    qc = m_ref.shape[-1]
    nx = tq // qc
    for x in range(nx):
        oT = acc_ref[x] / l_ref[x] - lmbda * (acc_ref[nx + x] / l_ref[nx + x])
        o = oT.T
        y = o * lax.rsqrt(jnp.mean(o * o, axis=-1, keepdims=True) + EPS)
        y = (y * sw_ref[...]) * (1.0 - lambda_init)
        o_ref[x * qc:(x + 1) * qc, :] = y.astype(o_ref.dtype)


def _diff_attention(qkv, lam, subln_w, lambda_init, B, S, tq=512):
    T = B * S
    H, dh, dv = A_HEADS, A_DH, A_DV
    n = S // tq
    slopes = _alibi_slopes(H)
    kb = H * 2
    vb = (2 * H * 2 * dh) // dv
    grid_spec = pltpu.PrefetchScalarGridSpec(
        num_scalar_prefetch=1,
        grid=(B, H, n),
        in_specs=[
            pl.BlockSpec((tq, dh), lambda b, h, i, sl: (b * n + i, 2 * h)),
            pl.BlockSpec((tq, dh), lambda b, h, i, sl: (b * n + i, 2 * h + 1)),
            pl.BlockSpec((S, dh), lambda b, h, i, sl: (b, kb + 2 * h)),
            pl.BlockSpec((S, dh), lambda b, h, i, sl: (b, kb + 2 * h + 1)),
            pl.BlockSpec((S, dv), lambda b, h, i, sl: (b, vb + h)),
            pl.BlockSpec((S, 128), lambda b, h, i, sl: (0, 0)),
            pl.BlockSpec((4, dh), lambda b, h, i, sl: (0, 0)),
            pl.BlockSpec((1, dv), lambda b, h, i, sl: (0, 0)),
        ],
        out_specs=pl.BlockSpec((tq, dv), lambda b, h, i, sl: (b * n + i, h)),
        scratch_shapes=[pltpu.VMEM((2, tq, 2 * dh), BF16)] + _flash_state(2, tq, dv),
    )
    return pl.pallas_call(
        functools.partial(_diff_attn_kernel, tq=tq, lambda_init=lambda_init),
        out_shape=jax.ShapeDtypeStruct((T, H * dv), BF16),
        grid_spec=grid_spec,
        compiler_params=_cparams(("parallel", "parallel", "arbitrary")),
        name="diff_attention",
    )(slopes, qkv, qkv, qkv, qkv, qkv, _key_features(S), lam, subln_w.reshape(1, dv))


def _router_kernel(h_ref, nw_ref, wr_ref, br_ref, tri_ref, xn_ref, ids_ref, wts_ref, cnt_ref, base_ref):
    @pl.when(pl.program_id(0) == 0)
    def _():
        base_ref[...] = jnp.zeros(base_ref.shape, F32)

    x = h_ref[...]
    xn = (x * lax.rsqrt(jnp.mean(x * x, axis=-1, keepdims=True) + EPS)) * nw_ref[...]
    xn_ref[...] = xn
    logits = jnp.dot(xn, wr_ref[...], preferred_element_type=F32,
                     precision=lax.Precision.HIGHEST) + br_ref[...]
    lane = lax.broadcasted_iota(I32, logits.shape, 1)
    big = jnp.int32(1 << 20)
    is_g = lane < N_GROUPS
    is_e = (lane >= N_GROUPS) & (lane < N_GROUPS + N_EXPERTS)

    lg = jnp.where(is_g, logits, NEG)
    mg = jnp.max(lg, axis=-1, keepdims=True)
    g_top = jnp.min(jnp.where(lg == mg, lane, big), axis=-1, keepdims=True)
    wg = 1.0 / jnp.sum(jnp.where(is_g, jnp.exp(lg - mg), 0.0), axis=-1, keepdims=True)

    in_grp = is_e & (lax.shift_right_logical(lane - N_GROUPS, 3) == g_top)
    le = jnp.where(in_grp, logits, NEG)
    me = jnp.max(le, axis=-1, keepdims=True)
    ee = jnp.where(in_grp, jnp.exp(le - me), 0.0)
    pe = ee / jnp.sum(ee, axis=-1, keepdims=True)
    pe = jnp.where(in_grp, pe, -1.0)
    p1 = jnp.max(pe, axis=-1, keepdims=True)
    i1 = jnp.min(jnp.where(pe == p1, lane, big), axis=-1, keepdims=True)
    pe2 = jnp.where(lane == i1, -1.0, pe)
    p2 = jnp.max(pe2, axis=-1, keepdims=True)
    i2 = jnp.min(jnp.where(pe2 == p2, lane, big), axis=-1, keepdims=True)
    den = p1 + p2
    w0 = wg * p1 / den
    w1 = wg * p2 / den

    oh1 = lane == i1
    oh2 = lane == i2
    both = jnp.where(oh1 | oh2, 1.0, 0.0)
    before = jnp.dot(tri_ref[...], both.astype(BF16), preferred_element_type=F32) + base_ref[...]
    pos0 = jnp.sum(jnp.where(oh1, before, 0.0), axis=-1, keepdims=True).astype(I32)
    pos1 = jnp.sum(jnp.where(oh2, before, 0.0), axis=-1, keepdims=True).astype(I32)
    total = base_ref[...] + jnp.sum(both, axis=0, keepdims=True)
    base_ref[...] = total
    cnt_ref[...] = total

    ids = jnp.where(lane == 0, i1 - N_GROUPS,
                    jnp.where(lane == 1, i2 - N_GROUPS,
                              jnp.where(lane == 2, pos0, jnp.where(lane == 3, pos1, 0))))
    ids_ref[...] = ids.T[:8]
    wts_ref[...] = jnp.where(lane == 0, w0, jnp.where(lane == 1, w1, 0.0))


def _router(h, norm_w, w_group, b_group, w_expert, b_expert, tm=512):
    T, D = h.shape
    pad = ROUTER_LANES - N_GROUPS - N_EXPERTS
    wr = jnp.concatenate([w_group, w_expert, jnp.zeros((D, pad), F32)], axis=1)
    br = jnp.concatenate([b_group, b_expert, jnp.zeros((pad,), F32)]).reshape(1, ROUTER_LANES)
    tri = jnp.asarray(np.tril(np.ones((tm, tm), np.float32), -1), BF16)
    return pl.pallas_call(
        _router_kernel,
        out_shape=(jax.ShapeDtypeStruct((T, D), F32),
                   jax.ShapeDtypeStruct((8, T), I32),
                   jax.ShapeDtypeStruct((T, ROUTER_LANES), F32),
                   jax.ShapeDtypeStruct((1, ROUTER_LANES), F32)),
        grid=(T // tm,),
        in_specs=[pl.BlockSpec((tm, D), lambda i: (i, 0)),
                  pl.BlockSpec((1, D), lambda i: (0, 0)),
                  pl.BlockSpec((D, ROUTER_LANES), lambda i: (0, 0)),
                  pl.BlockSpec((1, ROUTER_LANES), lambda i: (0, 0)),
                  pl.BlockSpec((tm, tm), lambda i: (0, 0))],
        out_specs=(pl.BlockSpec((tm, D), lambda i: (i, 0)),
                   pl.BlockSpec((8, tm), lambda i: (0, i)),
                   pl.BlockSpec((tm, ROUTER_LANES), lambda i: (i, 0)),
                   pl.BlockSpec((1, ROUTER_LANES), lambda i: (0, 0))),
        scratch_shapes=[pltpu.VMEM((1, ROUTER_LANES), F32)],
        compiler_params=_cparams(("arbitrary",)),
        name="moe_router",
    )(h, norm_w.reshape(1, D), wr, br, tri)


def _dispatch_kernel(dest_ref, xn_ref, xs_in_ref, xs_ref, sem, *, tm, n_tok):
    del xs_in_ref
    base = pl.program_id(0) * tm
    for t in range(tm):
        for slot in range(2):
            d = dest_ref[slot * n_tok + base + t]
            pltpu.make_async_copy(xn_ref.at[pl.ds(t, 1)], xs_ref.at[pl.ds(d, 1)], sem).start()
    for slot in range(2):
        pltpu.make_async_copy(xn_ref, xs_ref.at[pl.ds(0, tm)], sem).wait()


def _dispatch(xn, dest_flat, n_rows, tm=256):
    T, D = xn.shape
    grid_spec = pltpu.PrefetchScalarGridSpec(
        num_scalar_prefetch=1,
        grid=(T // tm,),
        in_specs=[pl.BlockSpec((tm, D), lambda i, d: (i, 0)),
                  pl.BlockSpec(memory_space=pl.ANY)],
        out_specs=pl.BlockSpec(memory_space=pl.ANY),
        scratch_shapes=[pltpu.SemaphoreType.DMA],
    )
    return pl.pallas_call(
        functools.partial(_dispatch_kernel, tm=tm, n_tok=T),
        out_shape=jax.ShapeDtypeStruct((n_rows, D), F32),
        grid_spec=grid_spec,
        input_output_aliases={2: 0},
        compiler_params=_cparams(("arbitrary",)),
        name="moe_dispatch",
    )(dest_flat, xn, jnp.zeros((n_rows, D), F32))


def _moe_plan(counts, n_blk):
    RB = MOE_ROWS
    nb_e = (counts + RB - 1) // RB
    end_b = jnp.cumsum(nb_e)
    start_b = end_b - nb_e
    n_used = end_b[-1:].astype(I32)
    r = jnp.arange(n_blk, dtype=I32)
    be = jnp.minimum(jnp.sum((end_b[None, :] <= r[:, None]).astype(I32), axis=1), N_EXPERTS - 1)
    active = nb_e > 0
    ordinal = jnp.cumsum(active.astype(I32)) - 1
    idx = jnp.arange(N_EXPERTS, dtype=I32)
    later = jnp.where(active[None, :] & (idx[None, :] > idx[:, None]), idx[None, :], N_EXPERTS)
    nxt_e = jnp.min(later, axis=1)
    nxt_e = jnp.where(nxt_e == N_EXPERTS, -1, nxt_e)
    first = ((r == start_b[be]) & (r < n_used[0])).astype(I32)
    plan = jnp.stack([be, first, ordinal[be] % 2, nxt_e[be]]).astype(I32)
    return plan, n_used, (start_b * RB).astype(I32)


def _stream_expert_weights(plan_ref, r, w_hbms, wbuf, sem, layer):
    e, slot, nxt = plan_ref[0, r], plan_ref[2, r], plan_ref[3, r]

    def copies(expert, s):
        return [pltpu.make_async_copy(w.at[layer, expert], wbuf.at[s, k], sem.at[s, k])
                for k, w in enumerate(w_hbms)]

    @pl.when(plan_ref[1, r] == 1)
    def _():
        @pl.when(r == 0)
        def _():
            for c in copies(e, slot):
                c.start()

        for c in copies(e, slot):
            c.wait()

        @pl.when(nxt >= 0)
        def _():
            for c in copies(nxt, 1 - slot):
                c.start()

    return slot


def _gate_up_kernel(plan_ref, nu_ref, x_ref, wg_hbm, wu_hbm, h_ref, wbuf, sem, *, layer):
    r = pl.program_id(0)

    @pl.when(r < nu_ref[0])
    def _():
        slot = _stream_expert_weights(plan_ref, r, (wg_hbm, wu_hbm), wbuf, sem, layer)
        x = x_ref[...]
        g = jnp.dot(x, wbuf[slot, 0], preferred_element_type=F32)
        u = jnp.dot(x, wbuf[slot, 1], preferred_element_type=F32)
        h_ref[...] = (jax.nn.silu(g) * u).astype(h_ref.dtype)

    @pl.when(r >= nu_ref[0])
    def _():
        h_ref[...] = jnp.zeros(h_ref.shape, h_ref.dtype)


def _down_kernel(plan_ref, nu_ref, h_ref, wd_hbm, y_ref, wbuf, sem, *, layer):
    r = pl.program_id(0)

    @pl.when(r < nu_ref[0])
    def _():
        slot = _stream_expert_weights(plan_ref, r, (wd_hbm,), wbuf, sem, layer)
        y_ref[...] = jnp.dot(h_ref[...].astype(F32), wbuf[slot, 0], preferred_element_type=F32)

    @pl.when(r >= nu_ref[0])
    def _():
        y_ref[...] = jnp.zeros(y_ref.shape, y_ref.dtype)


def _expert_mlp(xs, plan, n_used, w_gate, w_up, w_down, layer):
    n_rows, D = xs.shape
    RB = MOE_ROWS
    n_blk = n_rows // RB
    F = D_EXPERT

    def used_blk(r, plan, nu):
        return (jnp.minimum(r, nu[0] - 1), 0)

    h = pl.pallas_call(
        functools.partial(_gate_up_kernel, layer=layer),
        out_shape=jax.ShapeDtypeStruct((n_rows, F), BF16),
        grid_spec=pltpu.PrefetchScalarGridSpec(
            num_scalar_prefetch=2,
            grid=(n_blk,),
            in_specs=[pl.BlockSpec((RB, D), used_blk),
                      pl.BlockSpec(memory_space=pl.ANY),
                      pl.BlockSpec(memory_space=pl.ANY)],
            out_specs=pl.BlockSpec((RB, F), lambda r, plan, nu: (r, 0)),
            scratch_shapes=[pltpu.VMEM((2, 2, D, F), F32), pltpu.SemaphoreType.DMA((2, 2))],
        ),
        compiler_params=_cparams(("arbitrary",)),
        name="moe_gate_up",
    )(plan, n_used, xs, w_gate, w_up)
    y = pl.pallas_call(
        functools.partial(_down_kernel, layer=layer),
        out_shape=jax.ShapeDtypeStruct((n_rows, D), F32),
        grid_spec=pltpu.PrefetchScalarGridSpec(
            num_scalar_prefetch=2,
            grid=(n_blk,),
            in_specs=[pl.BlockSpec((RB, F), used_blk),
                      pl.BlockSpec(memory_space=pl.ANY)],
            out_specs=pl.BlockSpec((RB, D), lambda r, plan, nu: (r, 0)),
            scratch_shapes=[pltpu.VMEM((2, 1, F, D), F32), pltpu.SemaphoreType.DMA((2, 1))],
        ),
        compiler_params=_cparams(("arbitrary",)),
        name="moe_down",
    )(plan, n_used, h, w_down)
    return y


def _combine_kernel(dest_ref, resid_ref, wts_ref, y_ref, o_ref, buf_ref, sem, *, tm, n_tok):
    i = pl.program_id(0)
    n = pl.num_programs(0)

    def gather(tile, par):
        for t in range(tm):
            for slot in range(2):
                d = dest_ref[slot * n_tok + tile * tm + t]
                pltpu.make_async_copy(y_ref.at[pl.ds(d, 1)], buf_ref.at[par, slot, pl.ds(t, 1)],
                                      sem.at[par]).start()

    @pl.when(i == 0)
    def _():
        gather(i, 0)

    for par in range(2):
        @pl.when((i + 1 < n) & ((i + 1) % 2 == par))
        def _():
            gather(i + 1, par)

    for par in range(2):
        @pl.when(i % 2 == par)
        def _():
            for slot in range(2):
                pltpu.make_async_copy(y_ref.at[pl.ds(0, tm)], buf_ref.at[par, slot], sem.at[par]).wait()
            w = wts_ref[...]
            o_ref[...] = resid_ref[...] + (w[:, 0:1] * buf_ref[par, 0] + w[:, 1:2] * buf_ref[par, 1])


def _combine(resid, wts, y, dest_flat, tm=128):
    T, D = resid.shape
    grid_spec = pltpu.PrefetchScalarGridSpec(
        num_scalar_prefetch=1,
        grid=(T // tm,),
        in_specs=[pl.BlockSpec((tm, D), lambda i, d: (i, 0)),
                  pl.BlockSpec((tm, ROUTER_LANES), lambda i, d: (i, 0)),
                  pl.BlockSpec(memory_space=pl.ANY)],
        out_specs=pl.BlockSpec((tm, D), lambda i, d: (i, 0)),
        scratch_shapes=[pltpu.VMEM((2, 2, tm, D), F32), pltpu.SemaphoreType.DMA((2,))],
    )
    return pl.pallas_call(
        functools.partial(_combine_kernel, tm=tm, n_tok=T),
        out_shape=jax.ShapeDtypeStruct((T, D), F32),
        grid_spec=grid_spec,
        compiler_params=_cparams(("arbitrary",)),
        name="moe_combine",
    )(dest_flat, resid, wts, y)


def _hier_moe(h, norm_w, w_group, b_group, w_expert, b_expert, w_gate, w_up, w_down, layer):
    T, D = h.shape
    RB = MOE_ROWS
    n_blk = (T * 2) // RB + N_EXPERTS
    n_rows = n_blk * RB
    xn, ids, wts, cnt = _router(h, norm_w, w_group, b_group, w_expert, b_expert)
    counts = cnt[0, N_GROUPS:N_GROUPS + N_EXPERTS].astype(I32)
    plan, n_used, start_padded = _moe_plan(counts, n_blk)
    dest = (start_padded[ids[0:2]] + ids[2:4]).reshape(-1)
    xs = _dispatch(xn, dest, n_rows)
    y = _expert_mlp(xs, plan, n_used, w_gate, w_up, w_down, layer)
    return _combine(h, wts, y, dest)


def _compress_kernel(x_ref, w1_ref, w2_ref, pe_ref, o_ref):
    half = (CMP_LEN // 2) * B_D
    x = x_ref[...]
    w1 = w1_ref[...].astype(BF16)
    a = jnp.dot(x, w1[:half], preferred_element_type=F32)
    b = jnp.dot(x, w1[half:], preferred_element_type=F32)
    pe = jnp.broadcast_to(pe_ref[...], (8, 2 * half)).astype(BF16)
    pt = jnp.dot(pe, w1, preferred_element_type=F32)[0:1]
    hid = a + pltpu.roll(b, b.shape[0] - 1, 0) + pt
    g = jax.nn.gelu(hid)
    o_ref[...] = jnp.dot(g.astype(BF16), w2_ref[...].astype(BF16),
                         preferred_element_type=F32).astype(o_ref.dtype)


def _compress(tb, w1, w2, pe_flat):
    _, B, nch, width = tb.shape
    G, d = B_GROUPS, B_D
    hid = w1.shape[-1]
    return pl.pallas_call(
        _compress_kernel,
        out_shape=jax.ShapeDtypeStruct((2, B, G, nch, d), BF16),
        grid=(2, B, G),
        in_specs=[pl.BlockSpec((None, None, nch, width), lambda kv, b, g: (kv * G + g, b, 0, 0)),
                  pl.BlockSpec((None, 2 * width, hid), lambda kv, b, g: (kv, 0, 0)),
                  pl.BlockSpec((None, hid, d), lambda kv, b, g: (kv, 0, 0)),
                  pl.BlockSpec((None, 1, 2 * width), lambda kv, b, g: (kv, 0, 0))],
        out_specs=pl.BlockSpec((None, None, None, nch, d), lambda kv, b, g: (kv, b, g, 0, 0)),
        compiler_params=_cparams(("arbitrary", "arbitrary", "arbitrary")),
        name="nsa_compress",
    )(tb, w1, w2, pe_flat)


def _cmp_select_kernel(slopes_ref, q_ref, kc_ref, vc_ref, ov_ref, o_ref, sel_ref, *, tq, n_cmp, n_blocks,
                       n_rank):
    g = pl.program_id(1)
    i = pl.program_id(2)
    d = B_D
    nch = kc_ref.shape[0]
    pos = i * tq + lax.broadcasted_iota(I32, (1, tq), 1)
    c = lax.broadcasted_iota(I32, (nch, 1), 0)
    dist_i = pos - (c * CMP_STRIDE + (CMP_LEN - 1))
    valid = (dist_i >= 0) & (c < n_cmp)
    dist = dist_i.astype(F32)
    any_valid = jnp.where(pos >= CMP_LEN - 1, 1.0, 0.0)
    kc = kc_ref[...]
    vc = vc_ref[...]
    psum = jnp.zeros((nch, tq), F32)
    for hh in range(B_HPG):
        s = _nt_dot(kc, q_ref[:, hh * d:(hh + 1) * d])
        s = jnp.where(valid, s - (slopes_ref[g * B_HPG + hh] * LOG2E) * dist, NEG)
        e = jnp.exp2(s - jnp.max(s, axis=0, keepdims=True))
        p = (e / jnp.sum(e, axis=0, keepdims=True)) * any_valid
        oT = lax.dot_general(vc, p.astype(BF16), (((0,), (0,)), ((), ())), preferred_element_type=F32)
        o_ref[:, hh * d:(hh + 1) * d] = oT.T
        psum = psum + p

    p_hi = psum.astype(BF16)
    p_lo = (psum - p_hi.astype(F32)).astype(BF16)
    ov = ov_ref[...]
    imp = jnp.dot(ov, p_hi, preferred_element_type=F32) + jnp.dot(ov, p_lo, preferred_element_type=F32)
    imp = imp[:n_rank]

    blk = lax.broadcasted_iota(I32, (n_rank, 1), 0)
    cur = lax.shift_right_logical(pos, SLC_BLOCK.bit_length() - 1)
    forced = (blk == 0) | (blk == cur) | (blk == cur - 1)
    causal = blk * SLC_BLOCK <= pos
    score = jnp.where(forced, FORCE, jnp.where(causal, imp, -FORCE))
    score = jnp.where(blk < n_blocks, score, -2.0 * FORCE)
    rank = jnp.zeros(score.shape, F32)
    for jp in range(n_blocks):
        row = score[jp:jp + 1, :]
        tie = jnp.where(blk > jp, 1.0, 0.0)
        rank = rank + jnp.where(row > score, 1.0, jnp.where(row == score, tie, 0.0))
    sel_add = jnp.where((rank < float(SLC_TOPN)) & (blk < n_blocks), 0.0, NEG)
    pad = jnp.zeros((128 - n_rank, tq), F32)
    sel_t = jnp.concatenate([jnp.where(blk < n_blocks, sel_add, 0.0), pad], axis=0) if n_rank < 128 else sel_add
    sel_ref[...] = sel_t.T.astype(sel_ref.dtype)


def _cmp_select(projm, cmp_kv, overlap, B, S, tq=256):
    T = B * S
    G, d = B_GROUPS, B_D
    nq = S // tq
    nch = cmp_kv.shape[3]
    n_cmp = (S - CMP_LEN) // CMP_STRIDE + 1
    n_blocks = S // SLC_BLOCK
    assert SLC_TOPN <= n_blocks <= FEAT_BLK
    slopes = _alibi_slopes(B_HEADS)
    grid_spec = pltpu.PrefetchScalarGridSpec(
        num_scalar_prefetch=1,
        grid=(B, G, nq),
        in_specs=[pl.BlockSpec((tq, B_HPG * d), lambda b, g, i, sl: (b * nq + i, g)),
                  pl.BlockSpec((None, None, None, nch, d), lambda b, g, i, sl: (0, b, g, 0, 0)),
                  pl.BlockSpec((None, None, None, nch, d), lambda b, g, i, sl: (1, b, g, 0, 0)),
                  pl.BlockSpec((128, nch), lambda b, g, i, sl: (0, 0))],
        out_specs=(pl.BlockSpec((tq, B_HPG * d), lambda b, g, i, sl: (b * nq + i, g)),
                   pl.BlockSpec((None, None, tq, 128), lambda b, g, i, sl: (b, g, i, 0))),
    )
    return pl.pallas_call(
        functools.partial(_cmp_select_kernel, tq=tq, n_cmp=n_cmp, n_blocks=n_blocks,
                          n_rank=-(-n_blocks // 8) * 8),
        out_shape=(jax.ShapeDtypeStruct((T, B_HEADS * d), F32),
                   jax.ShapeDtypeStruct((B, G, S, 128), BF16)),
        grid_spec=grid_spec,
        compiler_params=_cparams(("parallel", "parallel", "arbitrary")),
        name="nsa_cmp_select",
    )(slopes, projm, cmp_kv, cmp_kv, overlap)


def _flash_tile(ka, v, qa_ref, mask, m_ref, l_ref, acc_ref):
    n_heads, tq, _ = qa_ref.shape
    qc = m_ref.shape[-1]

    def score_fn(hh, c):
        def fn():
            sT = _nt_dot(ka, qa_ref[hh, c * qc:(c + 1) * qc, :])
            return sT if mask is None else sT + mask[:, c * qc:(c + 1) * qc]
        return fn

    _flash_units([score_fn(hh, c) for hh in range(n_heads) for c in range(tq // qc)], v, m_ref, l_ref, acc_ref)


def _write_heads_T(o_ref, acc_ref, l_ref):
    tq = o_ref.shape[0]
    qc = l_ref.shape[-1]
    for hh in range(B_HPG):
        for c in range(tq // qc):
            u = hh * (tq // qc) + c
            o_ref[c * qc:(c + 1) * qc, hh * B_D:(hh + 1) * B_D] = (acc_ref[u] / l_ref[u]).T


def _flash_state(n_heads, tq, dv):
    qc = min(ATTN_QC, tq)
    n_units = n_heads * (tq // qc)
    return [pltpu.VMEM((n_units, 1, qc), F32), pltpu.VMEM((n_units, 1, qc), F32),
            pltpu.VMEM((n_units, dv, qc), F32)]


def _stage_queries(q_ref, qa_ref, slopes_ref, g, q0, sel_add):
    tq = q_ref.shape[0]
    lane = lax.broadcasted_iota(I32, (tq, 128), 1)
    for hh in range(B_HPG):
        coef = jnp.broadcast_to(_alibi_query_features(slopes_ref[g * B_HPG + hh] * LOG2E, q0), (tq, 128))
        if sel_add is not None:
            coef = jnp.where(lane < FEAT_BLK, sel_add, coef)
        qa_ref[hh] = jnp.concatenate([q_ref[:, hh * B_D:(hh + 1) * B_D], coef.astype(BF16)], axis=1)


def _slc_attn_kernel(slopes_ref, q_ref, k_ref, v_ref, feat_ref, sel_ref, o_ref, qa_ref, m_ref, l_ref, acc_ref,
                     *, tq):
    g = pl.program_id(1)
    i = pl.program_id(2)
    _flash_init(m_ref, l_ref, acc_ref)
    _stage_queries(q_ref, qa_ref, slopes_ref, g, i * tq, sel_ref[...].astype(F32))

    def tile(j, causal):
        k0 = pl.multiple_of(j * tq, tq)
        ka = jnp.concatenate([k_ref[pl.ds(k0, tq), :], feat_ref[pl.ds(k0, tq), :]], axis=1)
        v = v_ref[pl.ds(k0, tq), :]
        _flash_tile(ka, v, qa_ref, causal, m_ref, l_ref, acc_ref)

    def body(j, carry):
        tile(j, None)
        return carry

    lax.fori_loop(0, i, body, 0)
    tile(i, _tri_mask(tq, tq, True))
    _write_heads_T(o_ref, acc_ref, l_ref)


def _slc_attention(projm, sel, B, S, tq=512):
    T = B * S
    G, d = B_GROUPS, B_D
    n = S // tq
    slopes = _alibi_slopes(B_HEADS)
    ksb = (B_HEADS * d + 2 * G * d) // d
    vsb = ksb + G
    grid_spec = pltpu.PrefetchScalarGridSpec(
        num_scalar_prefetch=1,
        grid=(B, G, n),
        in_specs=[
            pl.BlockSpec((tq, B_HPG * d), lambda b, g, i, sl: (b * n + i, g)),
            pl.BlockSpec((S, d), lambda b, g, i, sl: (b, ksb + g)),
            pl.BlockSpec((S, d), lambda b, g, i, sl: (b, vsb + g)),
            pl.BlockSpec((S, 128), lambda b, g, i, sl: (0, 0)),
            pl.BlockSpec((None, None, tq, 128), lambda b, g, i, sl: (b, g, i, 0)),
        ],
        out_specs=pl.BlockSpec((tq, B_HPG * d), lambda b, g, i, sl: (b * n + i, g)),
        scratch_shapes=[pltpu.VMEM((B_HPG, tq, 2 * d), BF16)] + _flash_state(B_HPG, tq, d),
    )
    return pl.pallas_call(
        functools.partial(_slc_attn_kernel, tq=tq),
        out_shape=jax.ShapeDtypeStruct((T, B_HEADS * d), F32),
        grid_spec=grid_spec,
        compiler_params=_cparams(("parallel", "parallel", "arbitrary")),
        name="nsa_selected_attention",
    )(slopes, projm, projm, projm, _key_features(S), sel)


def _win_attn_kernel(slopes_ref, q_ref, k_ref, v_ref, feat_ref, o_ref, qa_ref, m_ref, l_ref, acc_ref, *, tq):
    g = pl.program_id(1)
    i = pl.program_id(2)
    _flash_init(m_ref, l_ref, acc_ref)
    _stage_queries(q_ref, qa_ref, slopes_ref, g, i * tq, None)

    def tile(j, mask):
        k0 = pl.multiple_of(j * tq, tq)
        ka = jnp.concatenate([k_ref[pl.ds(k0, tq), :], feat_ref[pl.ds(k0, tq), :]], axis=1)
        v = v_ref[pl.ds(k0, tq), :]
        _flash_tile(ka, v, qa_ref, mask, m_ref, l_ref, acc_ref)

    @pl.when(i > 0)
    def _():
        tile(i - 1, _tri_mask(tq, tq, False))

    tile(i, _tri_mask(tq, tq, True))
    _write_heads_T(o_ref, acc_ref, l_ref)


def _win_attention(projm, B, S):
    T = B * S
    G, d = B_GROUPS, B_D
    tq = WINDOW
    n = S // tq
    slopes = _alibi_slopes(B_HEADS)
    kwb = (B_HEADS * d + 4 * G * d) // d
    vwb = kwb + G
    grid_spec = pltpu.PrefetchScalarGridSpec(
        num_scalar_prefetch=1,
        grid=(B, G, n),
        in_specs=[
            pl.BlockSpec((tq, B_HPG * d), lambda b, g, i, sl: (b * n + i, g)),
            pl.BlockSpec((S, d), lambda b, g, i, sl: (b, kwb + g)),
            pl.BlockSpec((S, d), lambda b, g, i, sl: (b, vwb + g)),
            pl.BlockSpec((S, 128), lambda b, g, i, sl: (0, 0)),
        ],
        out_specs=pl.BlockSpec((tq, B_HPG * d), lambda b, g, i, sl: (b * n + i, g)),
        scratch_shapes=[pltpu.VMEM((B_HPG, tq, 2 * d), BF16)] + _flash_state(B_HPG, tq, d),
    )
    return pl.pallas_call(
        functools.partial(_win_attn_kernel, tq=tq),
        out_shape=jax.ShapeDtypeStruct((T, B_HEADS * d), F32),
        grid_spec=grid_spec,
        compiler_params=_cparams(("parallel", "parallel", "arbitrary")),
        name="nsa_window_attention",
    )(slopes, projm, projm, projm, _key_features(S))


def _gate_mix_kernel(gl_ref, oc_ref, os_ref, ow_ref, o_ref):
    sig = jax.nn.sigmoid(gl_ref[...])
    d = B_D
    for hd in range(B_HEADS):
        sl = slice(hd * d, (hd + 1) * d)
        o = (sig[:, 3 * hd:3 * hd + 1] * oc_ref[:, sl] + sig[:, 3 * hd + 1:3 * hd + 2] * os_ref[:, sl]
             + sig[:, 3 * hd + 2:3 * hd + 3] * ow_ref[:, sl])
        o_ref[:, sl] = o.astype(o_ref.dtype)


def _gate_mix(gate_logits, o_cmp, o_slc, o_win, tm=256):
    T, W = o_cmp.shape
    spec = pl.BlockSpec((tm, W), lambda i: (i, 0))
    return pl.pallas_call(
        _gate_mix_kernel,
        out_shape=jax.ShapeDtypeStruct((T, W), BF16),
        grid=(T // tm,),
        in_specs=[pl.BlockSpec((tm, 128), lambda i: (i, 0)), spec, spec, spec],
        out_specs=spec,
        compiler_params=_cparams(("parallel",)),
        name="nsa_gate_mix",
    )(gate_logits, o_cmp, o_slc, o_win)


def _overlap_matrix(S, n_rows):
    nc = (S - CMP_LEN) // CMP_STRIDE + 1
    nsb = S // SLC_BLOCK
    cs = np.arange(nc) * CMP_STRIDE
    ss = np.arange(nsb) * SLC_BLOCK
    ov = np.clip(np.minimum(cs[:, None] + CMP_LEN, ss[None, :] + SLC_BLOCK)
                 - np.maximum(cs[:, None], ss[None, :]), 0, None) / CMP_LEN
    out = np.zeros((128, n_rows), np.float32)
    out[:nsb, :nc] = ov.T
    return jnp.asarray(out, BF16)


def _nsa_attention(xn, w_in, cmp_pos, cmp_w1, cmp_w2, B, S):
    T = B * S
    H, G, d = B_HEADS, B_GROUPS, B_D
    n_main = H * d + 6 * G * d
    col_scale = jnp.concatenate([jnp.full((H * d,), d ** -0.5 * LOG2E, F32), jnp.ones((n_main - H * d,), F32)])
    projm = _matmul(xn, w_in, n_main, BF16, col_scale=col_scale)
    w_gate = jnp.pad(w_in[:, n_main:], ((0, 0), (0, 128 - 3 * H)))
    gate_logits = _matmul(xn, w_gate, 128, F32, tn=128)

    nch = S // CMP_STRIDE
    tb = projm[:, H * d:H * d + 2 * G * d].reshape(B, nch, CMP_STRIDE, 2 * G, d)
    tb = jnp.transpose(tb, (3, 0, 1, 2, 4)).reshape(2 * G, B, nch, CMP_STRIDE * d)
    cmp_kv = _compress(tb, cmp_w1, cmp_w2, cmp_pos.reshape(2, 1, CMP_LEN * d))

    o_cmp, sel = _cmp_select(projm, cmp_kv, _overlap_matrix(S, nch), B, S)
    o_slc = _slc_attention(projm, sel, B, S)
    o_win = _win_attention(projm, B, S)
    return _gate_mix(gate_logits, o_cmp, o_slc, o_win)


def kernel(x, attn_norm_w, ffn_norm_w, final_norm_w, a_w_in, a_lambda, a_subln_w, a_w_out, b_w_in,
           b_cmp_pos, b_cmp_w1, b_cmp_w2, b_w_out, moe_w_group, moe_b_group, moe_w_expert,
           moe_b_expert, moe_w_gate, moe_w_up, moe_w_down):
    B, S, D = x.shape
    T = B * S
    depth = attn_norm_w.shape[0]
    h = x.reshape(T, D)
    for i in range(depth):
        xn = _rmsnorm(h, attn_norm_w[i], BF16)
        j = i // 2
        if i % 2 == 0:
            lambda_init = 0.8 - 0.6 * math.exp(-0.3 * i)
            n_q = A_HEADS * 2 * A_DH
            n_in = 2 * n_q + A_HEADS * A_DV
            col_scale = jnp.concatenate([jnp.full((n_q,), A_DH ** -0.5 * LOG2E, F32),
                                         jnp.ones((n_in - n_q,), F32)])
            qkv = _matmul(xn, a_w_in[j], n_in, BF16, col_scale=col_scale)
            o = _diff_attention(qkv, a_lambda[j], a_subln_w[j], lambda_init, B, S)
            h = _matmul(o, a_w_out[j], D, F32, resid=h)
        else:
            o = _nsa_attention(xn, b_w_in[j], b_cmp_pos[j], b_cmp_w1[j], b_cmp_w2[j], B, S)
            h = _matmul(o, b_w_out[j], D, F32, resid=h)
        h = _hier_moe(h, ffn_norm_w[i], moe_w_group[i], moe_b_group[i], moe_w_expert[i], moe_b_expert[i],
                      moe_w_gate, moe_w_up, moe_w_down, i)
    return _rmsnorm(h, final_norm_w, F32).reshape(B, S, D)
```

```python
import functools
import math

import numpy as np
import jax
import jax.numpy as jnp
from jax import lax
from jax.experimental import pallas as pl
from jax.experimental.pallas import tpu as pltpu

F32 = jnp.float32
BF16 = jnp.bfloat16
I32 = jnp.int32

EPS = 1e-6
NEG = -1e30
FORCE = 1e4
LOG2E = math.log2(math.e)

A_HEADS = 8
A_DH = 128
A_DV = 256
B_HEADS = 16
B_GROUPS = 4
B_HPG = 4
B_D = 128
CMP_LEN = 32
CMP_STRIDE = 16
SLC_BLOCK = 64
SLC_TOPN = 16
WINDOW = 512
N_GROUPS = 4
EPG = 8
N_EXPERTS = 32
D_EXPERT = 1024
ROUTER_LANES = 128
MOE_ROWS = 256
DMA_UNROLL = 8
ATTN_QC = 512
SLC_KEY_TILE = 512

VMEM_LIMIT = 56 * 1024 * 1024


def _cparams(sem, flags=None):
    return pltpu.CompilerParams(dimension_semantics=sem, vmem_limit_bytes=VMEM_LIMIT, flags=flags)


def _alibi_slopes(n):
    return jnp.asarray(np.array([2.0 ** (-8.0 * (i + 1) / n) for i in range(n)], dtype=np.float32))


def _nt_dot(a, b):
    return lax.dot_general(a, b, (((1,), (1,)), ((), ())), preferred_element_type=F32)


def _rms_kernel(x_ref, w_ref, o_ref):
    x = x_ref[...]
    y = x * lax.rsqrt(jnp.mean(x * x, axis=-1, keepdims=True) + EPS)
    o_ref[...] = (y * w_ref[...]).astype(o_ref.dtype)


def _rmsnorm(x, w, out_dtype, tm=512):
    T, D = x.shape
    return pl.pallas_call(
        _rms_kernel,
        out_shape=jax.ShapeDtypeStruct((T, D), out_dtype),
        grid=(T // tm,),
        in_specs=[pl.BlockSpec((tm, D), lambda i: (i, 0)),
                  pl.BlockSpec((1, D), lambda i: (0, 0))],
        out_specs=pl.BlockSpec((tm, D), lambda i: (i, 0)),
        compiler_params=_cparams(("parallel",)),
        name="rmsnorm",
    )(x, w.reshape(1, D))


def _mm_kernel(*refs, has_scale, has_resid):
    a_ref, w_ref = refs[0], refs[1]
    k = 2
    scale_ref = resid_ref = None
    if has_scale:
        scale_ref = refs[k]
        k += 1
    if has_resid:
        resid_ref = refs[k]
        k += 1
    o_ref, wbf_ref = refs[k], refs[k + 1]

    @pl.when(pl.program_id(1) == 0)
    def _():
        wbf_ref[...] = w_ref[...].astype(BF16)

    acc = jnp.dot(a_ref[...], wbf_ref[...], preferred_element_type=F32)
    if has_scale:
        acc = acc * scale_ref[...]
    if has_resid:
        acc = acc + resid_ref[...]
    o_ref[...] = acc.astype(o_ref.dtype)


def _matmul(a, w, n_out, out_dtype, *, col_scale=None, resid=None, tm=1024, tn=512):
    M, K = a.shape
    assert M % tm == 0 and n_out % tn == 0 and w.shape[0] == K and w.shape[1] >= n_out
    in_specs = [pl.BlockSpec((tm, K), lambda j, i: (i, 0)),
                pl.BlockSpec((K, tn), lambda j, i: (0, j))]
    args = [a, w]
    if col_scale is not None:
        in_specs.append(pl.BlockSpec((1, tn), lambda j, i: (0, j)))
        args.append(col_scale.reshape(1, n_out))
    if resid is not None:
        in_specs.append(pl.BlockSpec((tm, tn), lambda j, i: (i, j)))
        args.append(resid)
    return pl.pallas_call(
        functools.partial(_mm_kernel, has_scale=col_scale is not None, has_resid=resid is not None),
        out_shape=jax.ShapeDtypeStruct((M, n_out), out_dtype),
        grid=(n_out // tn, M // tm),
        in_specs=in_specs,
        out_specs=pl.BlockSpec((tm, tn), lambda j, i: (i, j)),
        scratch_shapes=[pltpu.VMEM((K, tn), BF16)],
        compiler_params=_cparams(("parallel", "arbitrary")),
        name="matmul",
    )(*args)


FEAT_BLK = 64
FEAT_POS = 67
FEAT_ONE = 70


def _key_features(S):
    assert S // SLC_BLOCK <= FEAT_BLK
    j = np.arange(S)
    f = np.zeros((S, 128), np.float32)
    f[j, j // SLC_BLOCK] = 1.0
    f[:, FEAT_BLK:FEAT_BLK + 3] = (j // SLC_BLOCK)[:, None]
    f[:, FEAT_POS:FEAT_POS + 3] = (j % SLC_BLOCK)[:, None]
    f[:, FEAT_ONE:FEAT_ONE + 3] = 1.0
    return jnp.asarray(f, BF16)


def _alibi_query_features(slope2, q0):
    lane = lax.broadcasted_iota(I32, (1, 128), 1)
    base = jnp.where((lane >= FEAT_BLK) & (lane < FEAT_BLK + 3), slope2 * float(SLC_BLOCK),
                     jnp.where((lane >= FEAT_POS) & (lane < FEAT_POS + 3), slope2,
                               jnp.where((lane >= FEAT_ONE) & (lane < FEAT_ONE + 3),
                                         -slope2 * q0.astype(F32), 0.0)))
    hi = base.astype(BF16).astype(F32)
    r1 = base - hi
    lo = r1.astype(BF16).astype(F32)
    lo2 = (r1 - lo).astype(BF16).astype(F32)
    first = (lane == FEAT_BLK) | (lane == FEAT_POS) | (lane == FEAT_ONE)
    second = (lane == FEAT_BLK + 1) | (lane == FEAT_POS + 1) | (lane == FEAT_ONE + 1)
    return jnp.where(first, hi, jnp.where(second, lo, lo2))


def _flash_init(m_ref, l_ref, acc_ref):
    m_ref[...] = jnp.full(m_ref.shape, NEG, F32)
    l_ref[...] = jnp.zeros(l_ref.shape, F32)
    acc_ref[...] = jnp.zeros(acc_ref.shape, F32)


def _flash_probs(sT, m_ref, l_ref, u):
    m_old = m_ref[u]
    m_new = jnp.maximum(m_old, jnp.max(sT, axis=0, keepdims=True))
    alpha = jnp.exp2(m_old - m_new)
    p = jnp.exp2(sT - m_new)
    l_ref[u] = alpha * l_ref[u] + jnp.sum(p, axis=0, keepdims=True)
    m_ref[u] = m_new
    return p.astype(BF16), alpha


def _flash_accumulate(p, alpha, v, acc_ref, u):
    pv = lax.dot_general(v, p, (((0,), (0,)), ((), ())), preferred_element_type=F32)
    acc_ref[u] = alpha * acc_ref[u] + pv


def _flash_units(score_fns, v, m_ref, l_ref, acc_ref):
    n = len(score_fns)
    scores, probs = {}, {}
    for s in range(n + 2):
        if s < n:
            scores[s] = score_fns[s]()
        if 1 <= s <= n:
            probs[s - 1] = _flash_probs(scores.pop(s - 1), m_ref, l_ref, s - 1)
        if 2 <= s <= n + 1:
            p, alpha = probs.pop(s - 2)
            _flash_accumulate(p, alpha, v, acc_ref, s - 2)


def _tri_mask(tk, tq, keep_upper):
    r = lax.broadcasted_iota(I32, (tk, tq), 0)
    c = lax.broadcasted_iota(I32, (tk, tq), 1)
    keep = (r <= c) if keep_upper else (r > c)
    return jnp.where(keep, 0.0, NEG)


def _diff_attn_kernel(slopes_ref, q1_ref, q2_ref, k1_ref, k2_ref, v_ref, feat_ref, lam_ref, sw_ref, o_ref,
                      qa_ref, m_ref, l_ref, acc_ref, *, tq, lambda_init):
    h = pl.program_id(1)
    i = pl.program_id(2)
    _flash_init(m_ref, l_ref, acc_ref)
    qfeat = jnp.broadcast_to(_alibi_query_features(slopes_ref[h] * LOG2E, i * tq), (tq, 128)).astype(BF16)
    for c, q_ref in enumerate((q1_ref, q2_ref)):
        qa_ref[c] = jnp.concatenate([q_ref[...], qfeat], axis=1)

    def tile(j, mask):
        k0 = pl.multiple_of(j * tq, tq)
        v = v_ref[pl.ds(k0, tq), :]
        feat = feat_ref[pl.ds(k0, tq), :]
        qc = m_ref.shape[-1]

        def score_fn(c, k_ref, x):
            def fn():
                ka = jnp.concatenate([k_ref[pl.ds(k0, tq), :], feat], axis=1)
                sT = _nt_dot(ka, qa_ref[c, x * qc:(x + 1) * qc, :])
                return sT if mask is None else sT + mask[:, x * qc:(x + 1) * qc]
            return fn

        _flash_units([score_fn(c, k_ref, x) for c, k_ref in enumerate((k1_ref, k2_ref))
                      for x in range(tq // qc)], v, m_ref, l_ref, acc_ref)

    def body(j, carry):
        tile(j, None)
        return carry

    lax.fori_loop(0, i, body, 0)
    tile(i, _tri_mask(tq, tq, True))

    lam = lam_ref[...]
    lmbda = (jnp.exp(jnp.sum(lam[0:1] * lam[1:2], axis=-1, keepdims=True))
             - jnp.exp(jnp.sum(lam[2:3] * lam[3:4], axis=-1, keepdims=True)) + lambda_init)
    qc = m_ref.shape[-1]
    nx = tq // qc
    for x in range(nx):
        oT = acc_ref[x] / l_ref[x] - lmbda * (acc_ref[nx + x] / l_ref[nx + x])
        o = oT.T
        y = o * lax.rsqrt(jnp.mean(o * o, axis=-1, keepdims=True) + EPS)
        y = (y * sw_ref[...]) * (1.0 - lambda_init)
        o_ref[x * qc:(x + 1) * qc, :] = y.astype(o_ref.dtype)


def _diff_attention(qkv, lam, subln_w, lambda_init, B, S, tq=512):
    T = B * S
    H, dh, dv = A_HEADS, A_DH, A_DV
    n = S // tq
    slopes = _alibi_slopes(H)
    kb = H * 2
    vb = (2 * H * 2 * dh) // dv
    grid_spec = pltpu.PrefetchScalarGridSpec(
        num_scalar_prefetch=1,
        grid=(B, H, n),
        in_specs=[
            pl.BlockSpec((tq, dh), lambda b, h, i, sl: (b * n + i, 2 * h)),
            pl.BlockSpec((tq, dh), lambda b, h, i, sl: (b * n + i, 2 * h + 1)),
            pl.BlockSpec((S, dh), lambda b, h, i, sl: (b, kb + 2 * h)),
            pl.BlockSpec((S, dh), lambda b, h, i, sl: (b, kb + 2 * h + 1)),
            pl.BlockSpec((S, dv), lambda b, h, i, sl: (b, vb + h)),
            pl.BlockSpec((S, 128), lambda b, h, i, sl: (0, 0)),
            pl.BlockSpec((4, dh), lambda b, h, i, sl: (0, 0)),
            pl.BlockSpec((1, dv), lambda b, h, i, sl: (0, 0)),
        ],
        out_specs=pl.BlockSpec((tq, dv), lambda b, h, i, sl: (b * n + i, h)),
        scratch_shapes=[pltpu.VMEM((2, tq, 2 * dh), BF16)] + _flash_state(2, tq, dv),
    )
    return pl.pallas_call(
        functools.partial(_diff_attn_kernel, tq=tq, lambda_init=lambda_init),
        out_shape=jax.ShapeDtypeStruct((T, H * dv), BF16),
        grid_spec=grid_spec,
        compiler_params=_cparams(("parallel", "parallel", "arbitrary")),
        name="diff_attention",
    )(slopes, qkv, qkv, qkv, qkv, qkv, _key_features(S), lam, subln_w.reshape(1, dv))


def _router_kernel(h_ref, nw_ref, wr_ref, br_ref, tri_ref, xn_ref, ids_ref, wts_ref, cnt_ref, base_ref):
    @pl.when(pl.program_id(0) == 0)
    def _():
        base_ref[...] = jnp.zeros(base_ref.shape, F32)

    x = h_ref[...]
    xn = (x * lax.rsqrt(jnp.mean(x * x, axis=-1, keepdims=True) + EPS)) * nw_ref[...]
    xn_ref[...] = xn
    logits = jnp.dot(xn, wr_ref[...], preferred_element_type=F32,
                     precision=lax.Precision.HIGHEST) + br_ref[...]
    lane = lax.broadcasted_iota(I32, logits.shape, 1)
    big = jnp.int32(1 << 20)
    is_g = lane < N_GROUPS
    is_e = (lane >= N_GROUPS) & (lane < N_GROUPS + N_EXPERTS)

    lg = jnp.where(is_g, logits, NEG)
    mg = jnp.max(lg, axis=-1, keepdims=True)
    g_top = jnp.min(jnp.where(lg == mg, lane, big), axis=-1, keepdims=True)
    wg = 1.0 / jnp.sum(jnp.where(is_g, jnp.exp(lg - mg), 0.0), axis=-1, keepdims=True)

    in_grp = is_e & (lax.shift_right_logical(lane - N_GROUPS, 3) == g_top)
    le = jnp.where(in_grp, logits, NEG)
    me = jnp.max(le, axis=-1, keepdims=True)
    ee = jnp.where(in_grp, jnp.exp(le - me), 0.0)
    pe = ee / jnp.sum(ee, axis=-1, keepdims=True)
    pe = jnp.where(in_grp, pe, -1.0)
    p1 = jnp.max(pe, axis=-1, keepdims=True)
    i1 = jnp.min(jnp.where(pe == p1, lane, big), axis=-1, keepdims=True)
    pe2 = jnp.where(lane == i1, -1.0, pe)
    p2 = jnp.max(pe2, axis=-1, keepdims=True)
    i2 = jnp.min(jnp.where(pe2 == p2, lane, big), axis=-1, keepdims=True)
    den = p1 + p2
    w0 = wg * p1 / den
    w1 = wg * p2 / den

    oh1 = lane == i1
    oh2 = lane == i2
    both = jnp.where(oh1 | oh2, 1.0, 0.0)
    before = jnp.dot(tri_ref[...], both.astype(BF16), preferred_element_type=F32) + base_ref[...]
    pos0 = jnp.sum(jnp.where(oh1, before, 0.0), axis=-1, keepdims=True).astype(I32)
    pos1 = jnp.sum(jnp.where(oh2, before, 0.0), axis=-1, keepdims=True).astype(I32)
    total = base_ref[...] + jnp.sum(both, axis=0, keepdims=True)
    base_ref[...] = total
    cnt_ref[...] = total

    ids = jnp.where(lane == 0, i1 - N_GROUPS,
                    jnp.where(lane == 1, i2 - N_GROUPS,
                              jnp.where(lane == 2, pos0, jnp.where(lane == 3, pos1, 0))))
    ids_ref[...] = ids.T[:8]
    wts_ref[...] = jnp.where(lane == 0, w0, jnp.where(lane == 1, w1, 0.0))


def _router(h, norm_w, w_group, b_group, w_expert, b_expert, tm=512):
    T, D = h.shape
    pad = ROUTER_LANES - N_GROUPS - N_EXPERTS
    wr = jnp.concatenate([w_group, w_expert, jnp.zeros((D, pad), F32)], axis=1)
    br = jnp.concatenate([b_group, b_expert, jnp.zeros((pad,), F32)]).reshape(1, ROUTER_LANES)
    tri = jnp.asarray(np.tril(np.ones((tm, tm), np.float32), -1), BF16)
    return pl.pallas_call(
        _router_kernel,
        out_shape=(jax.ShapeDtypeStruct((T, D), F32),
                   jax.ShapeDtypeStruct((8, T), I32),
                   jax.ShapeDtypeStruct((T, ROUTER_LANES), F32),
                   jax.ShapeDtypeStruct((1, ROUTER_LANES), F32)),
        grid=(T // tm,),
        in_specs=[pl.BlockSpec((tm, D), lambda i: (i, 0)),
                  pl.BlockSpec((1, D), lambda i: (0, 0)),
                  pl.BlockSpec((D, ROUTER_LANES), lambda i: (0, 0)),
                  pl.BlockSpec((1, ROUTER_LANES), lambda i: (0, 0)),
                  pl.BlockSpec((tm, tm), lambda i: (0, 0))],
        out_specs=(pl.BlockSpec((tm, D), lambda i: (i, 0)),
                   pl.BlockSpec((8, tm), lambda i: (0, i)),
                   pl.BlockSpec((tm, ROUTER_LANES), lambda i: (i, 0)),
                   pl.BlockSpec((1, ROUTER_LANES), lambda i: (0, 0))),
        scratch_shapes=[pltpu.VMEM((1, ROUTER_LANES), F32)],
        compiler_params=_cparams(("arbitrary",)),
        name="moe_router",
    )(h, norm_w.reshape(1, D), wr, br, tri)


def _dispatch_kernel(dest_ref, xn_ref, xs_in_ref, xs_ref, sem, *, tm, n_tok):
    del xs_in_ref
    base = pl.program_id(0) * tm
    for t in range(tm):
        for slot in range(2):
            d = dest_ref[slot * n_tok + base + t]
            pltpu.make_async_copy(xn_ref.at[pl.ds(t, 1)], xs_ref.at[pl.ds(d, 1)], sem).start()
    for slot in range(2):
        pltpu.make_async_copy(xn_ref, xs_ref.at[pl.ds(0, tm)], sem).wait()


def _dispatch(xn, dest_flat, n_rows, tm=256):
    T, D = xn.shape
    grid_spec = pltpu.PrefetchScalarGridSpec(
        num_scalar_prefetch=1,
        grid=(T // tm,),
        in_specs=[pl.BlockSpec((tm, D), lambda i, d: (i, 0)),
                  pl.BlockSpec(memory_space=pl.ANY)],
        out_specs=pl.BlockSpec(memory_space=pl.ANY),
        scratch_shapes=[pltpu.SemaphoreType.DMA],
    )
    return pl.pallas_call(
        functools.partial(_dispatch_kernel, tm=tm, n_tok=T),
        out_shape=jax.ShapeDtypeStruct((n_rows, D), F32),
        grid_spec=grid_spec,
        input_output_aliases={2: 0},
        compiler_params=_cparams(("arbitrary",)),
        name="moe_dispatch",
    )(dest_flat, xn, jnp.zeros((n_rows, D), F32))


def _moe_plan(counts, n_blk):
    RB = MOE_ROWS
    nb_e = (counts + RB - 1) // RB
    end_b = jnp.cumsum(nb_e)
    start_b = end_b - nb_e
    n_used = end_b[-1:].astype(I32)
    r = jnp.arange(n_blk, dtype=I32)
    be = jnp.minimum(jnp.sum((end_b[None, :] <= r[:, None]).astype(I32), axis=1), N_EXPERTS - 1)
    active = nb_e > 0
    ordinal = jnp.cumsum(active.astype(I32)) - 1
    idx = jnp.arange(N_EXPERTS, dtype=I32)
    later = jnp.where(active[None, :] & (idx[None, :] > idx[:, None]), idx[None, :], N_EXPERTS)
    nxt_e = jnp.min(later, axis=1)
    nxt_e = jnp.where(nxt_e == N_EXPERTS, -1, nxt_e)
    first = ((r == start_b[be]) & (r < n_used[0])).astype(I32)
    plan = jnp.stack([be, first, ordinal[be] % 2, nxt_e[be]]).astype(I32)
    return plan, n_used, (start_b * RB).astype(I32)


def _stream_expert_weights(plan_ref, r, w_hbms, wbuf, sem, layer):
    e, slot, nxt = plan_ref[0, r], plan_ref[2, r], plan_ref[3, r]

    def copies(expert, s):
        return [pltpu.make_async_copy(w.at[layer, expert], wbuf.at[s, k], sem.at[s, k])
                for k, w in enumerate(w_hbms)]

    @pl.when(plan_ref[1, r] == 1)
    def _():
        @pl.when(r == 0)
        def _():
            for c in copies(e, slot):
                c.start()

        for c in copies(e, slot):
            c.wait()

        @pl.when(nxt >= 0)
        def _():
            for c in copies(nxt, 1 - slot):
                c.start()

    return slot


def _gate_up_kernel(plan_ref, nu_ref, x_ref, wg_hbm, wu_hbm, h_ref, wbuf, sem, *, layer):
    r = pl.program_id(0)

    @pl.when(r < nu_ref[0])
    def _():
        slot = _stream_expert_weights(plan_ref, r, (wg_hbm, wu_hbm), wbuf, sem, layer)
        x = x_ref[...]
        g = jnp.dot(x, wbuf[slot, 0], preferred_element_type=F32)
        u = jnp.dot(x, wbuf[slot, 1], preferred_element_type=F32)
        h_ref[...] = (jax.nn.silu(g) * u).astype(h_ref.dtype)

    @pl.when(r >= nu_ref[0])
    def _():
        h_ref[...] = jnp.zeros(h_ref.shape, h_ref.dtype)


def _down_kernel(plan_ref, nu_ref, h_ref, wd_hbm, y_ref, wbuf, sem, *, layer):
    r = pl.program_id(0)

    @pl.when(r < nu_ref[0])
    def _():
        slot = _stream_expert_weights(plan_ref, r, (wd_hbm,), wbuf, sem, layer)
        y_ref[...] = jnp.dot(h_ref[...].astype(F32), wbuf[slot, 0], preferred_element_type=F32)

    @pl.when(r >= nu_ref[0])
    def _():
        y_ref[...] = jnp.zeros(y_ref.shape, y_ref.dtype)


def _expert_mlp(xs, plan, n_used, w_gate, w_up, w_down, layer):
    n_rows, D = xs.shape
    RB = MOE_ROWS
    n_blk = n_rows // RB
    F = D_EXPERT

    def used_blk(r, plan, nu):
        return (jnp.minimum(r, nu[0] - 1), 0)

    h = pl.pallas_call(
        functools.partial(_gate_up_kernel, layer=layer),
        out_shape=jax.ShapeDtypeStruct((n_rows, F), BF16),
        grid_spec=pltpu.PrefetchScalarGridSpec(
            num_scalar_prefetch=2,
            grid=(n_blk,),
            in_specs=[pl.BlockSpec((RB, D), used_blk),
                      pl.BlockSpec(memory_space=pl.ANY),
                      pl.BlockSpec(memory_space=pl.ANY)],
            out_specs=pl.BlockSpec((RB, F), lambda r, plan, nu: (r, 0)),
            scratch_shapes=[pltpu.VMEM((2, 2, D, F), F32), pltpu.SemaphoreType.DMA((2, 2))],
        ),
        compiler_params=_cparams(("arbitrary",)),
        name="moe_gate_up",
    )(plan, n_used, xs, w_gate, w_up)
    y = pl.pallas_call(
        functools.partial(_down_kernel, layer=layer),
        out_shape=jax.ShapeDtypeStruct((n_rows, D), F32),
        grid_spec=pltpu.PrefetchScalarGridSpec(
            num_scalar_prefetch=2,
            grid=(n_blk,),
            in_specs=[pl.BlockSpec((RB, F), used_blk),
                      pl.BlockSpec(memory_space=pl.ANY)],
            out_specs=pl.BlockSpec((RB, D), lambda r, plan, nu: (r, 0)),
            scratch_shapes=[pltpu.VMEM((2, 1, F, D), F32), pltpu.SemaphoreType.DMA((2, 1))],
        ),
        compiler_params=_cparams(("arbitrary",)),
        name="moe_down",
    )(plan, n_used, h, w_down)
    return y


def _combine_kernel(dest_ref, resid_ref, wts_ref, y_ref, o_ref, buf_ref, sem, *, tm, n_tok):
    i = pl.program_id(0)
    n = pl.num_programs(0)

    def gather(tile, par):
        for t in range(tm):
            for slot in range(2):
                d = dest_ref[slot * n_tok + tile * tm + t]
                pltpu.make_async_copy(y_ref.at[pl.ds(d, 1)], buf_ref.at[par, slot, pl.ds(t, 1)],
                                      sem.at[par]).start()

    @pl.when(i == 0)
    def _():
        gather(i, 0)

    for par in range(2):
        @pl.when((i + 1 < n) & ((i + 1) % 2 == par))
        def _():
            gather(i + 1, par)

    for par in range(2):
        @pl.when(i % 2 == par)
        def _():
            for slot in range(2):
                pltpu.make_async_copy(y_ref.at[pl.ds(0, tm)], buf_ref.at[par, slot], sem.at[par]).wait()
            w = wts_ref[...]
            o_ref[...] = resid_ref[...] + (w[:, 0:1] * buf_ref[par, 0] + w[:, 1:2] * buf_ref[par, 1])


def _combine(resid, wts, y, dest_flat, tm=128):
    T, D = resid.shape
    grid_spec = pltpu.PrefetchScalarGridSpec(
        num_scalar_prefetch=1,
        grid=(T // tm,),
        in_specs=[pl.BlockSpec((tm, D), lambda i, d: (i, 0)),
                  pl.BlockSpec((tm, ROUTER_LANES), lambda i, d: (i, 0)),
                  pl.BlockSpec(memory_space=pl.ANY)],
        out_specs=pl.BlockSpec((tm, D), lambda i, d: (i, 0)),
        scratch_shapes=[pltpu.VMEM((2, 2, tm, D), F32), pltpu.SemaphoreType.DMA((2,))],
    )
    return pl.pallas_call(
        functools.partial(_combine_kernel, tm=tm, n_tok=T),
        out_shape=jax.ShapeDtypeStruct((T, D), F32),
        grid_spec=grid_spec,
        compiler_params=_cparams(("arbitrary",)),
        name="moe_combine",
    )(dest_flat, resid, wts, y)


def _hier_moe(h, norm_w, w_group, b_group, w_expert, b_expert, w_gate, w_up, w_down, layer):
    T, D = h.shape
    RB = MOE_ROWS
    n_blk = (T * 2) // RB + N_EXPERTS
    n_rows = n_blk * RB
    xn, ids, wts, cnt = _router(h, norm_w, w_group, b_group, w_expert, b_expert)
    counts = cnt[0, N_GROUPS:N_GROUPS + N_EXPERTS].astype(I32)
    plan, n_used, start_padded = _moe_plan(counts, n_blk)
    is_e = ids[None, 0:2] == jnp.arange(N_EXPERTS, dtype=I32)[:, None, None]
    dest = (jnp.sum(jnp.where(is_e, start_padded[:, None, None], 0), axis=0) + ids[2:4]).reshape(-1)
    xs = _dispatch(xn, dest, n_rows)
    y = _expert_mlp(xs, plan, n_used, w_gate, w_up, w_down, layer)
    return _combine(h, wts, y, dest)


def _compress_kernel(x_ref, w1_ref, w2_ref, pe_ref, o_ref, xf_ref):
    d = B_D
    half = CMP_STRIDE * d
    nch = x_ref.shape[0] // CMP_STRIDE
    xf_ref[...] = x_ref[...].astype(F32)
    w1 = w1_ref[...].astype(BF16)
    a = jnp.zeros((nch, w1.shape[1]), F32)
    b = jnp.zeros((nch, w1.shape[1]), F32)
    for l in range(CMP_STRIDE):
        xl = xf_ref[pl.ds(l, nch, stride=CMP_STRIDE), :].astype(BF16)
        a = a + jnp.dot(xl, w1[l * d:(l + 1) * d], preferred_element_type=F32)
        b = b + jnp.dot(xl, w1[half + l * d:half + (l + 1) * d], preferred_element_type=F32)
    pe = jnp.broadcast_to(pe_ref[...], (8, 2 * half)).astype(BF16)
    pt = jnp.dot(pe, w1, preferred_element_type=F32)[0:1]
    hid = a + pltpu.roll(b, b.shape[0] - 1, 0) + pt
    g = jax.nn.gelu(hid)
    o_ref[...] = jnp.dot(g.astype(BF16), w2_ref[...].astype(BF16),
                         preferred_element_type=F32).astype(o_ref.dtype)


def _compress(projm, w1, w2, pe_flat, B, S):
    G, d = B_GROUPS, B_D
    nch = S // CMP_STRIDE
    width = CMP_STRIDE * d
    hid = w1.shape[-1]
    cb = B_HEADS
    return pl.pallas_call(
        _compress_kernel,
        out_shape=jax.ShapeDtypeStruct((2, B, G, nch, d), BF16),
        grid=(2, B, G),
        in_specs=[pl.BlockSpec((S, d), lambda kv, b, g: (b, cb + kv * G + g)),
                  pl.BlockSpec((None, 2 * width, hid), lambda kv, b, g: (kv, 0, 0)),
                  pl.BlockSpec((None, hid, d), lambda kv, b, g: (kv, 0, 0)),
                  pl.BlockSpec((None, 1, 2 * width), lambda kv, b, g: (kv, 0, 0))],
        out_specs=pl.BlockSpec((None, None, None, nch, d), lambda kv, b, g: (kv, b, g, 0, 0)),
        scratch_shapes=[pltpu.VMEM((S, d), F32)],
        compiler_params=_cparams(("arbitrary", "arbitrary", "arbitrary")),
        name="nsa_compress",
    )(projm, w1, w2, pe_flat)


def _cmp_select_kernel(slopes_ref, q_ref, kc_ref, vc_ref, ov_ref, o_ref, sel_ref, need_ref, *, tq, n_cmp,
                       n_blocks, n_rank):
    g = pl.program_id(1)
    i = pl.program_id(2)
    d = B_D
    nch = kc_ref.shape[0]
    pos = i * tq + lax.broadcasted_iota(I32, (1, tq), 1)
    c = lax.broadcasted_iota(I32, (nch, 1), 0)
    dist_i = pos - (c * CMP_STRIDE + (CMP_LEN - 1))
    valid = (dist_i >= 0) & (c < n_cmp)
    dist = dist_i.astype(F32)
    any_valid = jnp.where(pos >= CMP_LEN - 1, 1.0, 0.0)
    kc = kc_ref[...]
    vc = vc_ref[...]
    psum = jnp.zeros((nch, tq), F32)
    for hh in range(B_HPG):
        s = _nt_dot(kc, q_ref[:, hh * d:(hh + 1) * d])
        s = jnp.where(valid, s - (slopes_ref[g * B_HPG + hh] * LOG2E) * dist, NEG)
        e = jnp.exp2(s - jnp.max(s, axis=0, keepdims=True))
        p = (e / jnp.sum(e, axis=0, keepdims=True)) * any_valid
        oT = lax.dot_general(vc, p.astype(BF16), (((0,), (0,)), ((), ())), preferred_element_type=F32)
        o_ref[:, hh * d:(hh + 1) * d] = oT.T
        psum = psum + p

    p_hi = psum.astype(BF16)
    p_lo = (psum - p_hi.astype(F32)).astype(BF16)
    ov = ov_ref[...]
    imp = jnp.dot(ov, p_hi, preferred_element_type=F32) + jnp.dot(ov, p_lo, preferred_element_type=F32)
    imp = imp[:n_rank]

    blk = lax.broadcasted_iota(I32, (n_rank, 1), 0)
    cur = lax.shift_right_logical(pos, SLC_BLOCK.bit_length() - 1)
    forced = (blk == 0) | (blk == cur) | (blk == cur - 1)
    causal = blk * SLC_BLOCK <= pos
    score = jnp.where(forced, FORCE, jnp.where(causal, imp, -FORCE))
    score = jnp.where(blk < n_blocks, score, -2.0 * FORCE)
    rank = jnp.zeros(score.shape, F32)
    for jp in range(n_blocks):
        row = score[jp:jp + 1, :]
        tie = jnp.where(blk > jp, 1.0, 0.0)
        rank = rank + jnp.where(row > score, 1.0, jnp.where(row == score, tie, 0.0))
    sel_add = jnp.where((rank < float(SLC_TOPN)) & (blk < n_blocks), 0.0, NEG)
    pad = jnp.zeros((128 - n_rank, tq), F32)
    sel_t = jnp.concatenate([jnp.where(blk < n_blocks, sel_add, 0.0), pad], axis=0) if n_rank < 128 else sel_add
    sel_ref[...] = sel_t.T.astype(sel_ref.dtype)
    per_tile = SLC_KEY_TILE // SLC_BLOCK
    blk_any = jnp.max(jnp.where(sel_add == 0.0, 1, 0), axis=1, keepdims=True)
    rows = []
    for j in range(8):
        if (j + 1) * per_tile <= n_rank:
            hit = jnp.max(blk_any[j * per_tile:(j + 1) * per_tile], axis=0, keepdims=True)
            rows.append(jnp.broadcast_to(hit, (1, 128)))
        else:
            rows.append(jnp.zeros((1, 128), I32))
    need_ref[...] = jnp.concatenate(rows, axis=0)


def _cmp_select(projm, cmp_kv, overlap, B, S, tq=256):
    T = B * S
    G, d = B_GROUPS, B_D
    nq = S // tq
    nch = cmp_kv.shape[3]
    n_cmp = (S - CMP_LEN) // CMP_STRIDE + 1
    n_blocks = S // SLC_BLOCK
    assert SLC_TOPN <= n_blocks <= FEAT_BLK
    slopes = _alibi_slopes(B_HEADS)
    grid_spec = pltpu.PrefetchScalarGridSpec(
        num_scalar_prefetch=1,
        grid=(B, G, nq),
        in_specs=[pl.BlockSpec((tq, B_HPG * d), lambda b, g, i, sl: (b * nq + i, g)),
                  pl.BlockSpec((None, None, None, nch, d), lambda b, g, i, sl: (0, b, g, 0, 0)),
                  pl.BlockSpec((None, None, None, nch, d), lambda b, g, i, sl: (1, b, g, 0, 0)),
                  pl.BlockSpec((128, nch), lambda b, g, i, sl: (0, 0))],
        out_specs=(pl.BlockSpec((tq, B_HPG * d), lambda b, g, i, sl: (b * nq + i, g)),
                   pl.BlockSpec((None, None, tq, 128), lambda b, g, i, sl: (b, g, i, 0)),
                   pl.BlockSpec((None, None, None, 8, 128), lambda b, g, i, sl: (b, g, i, 0, 0))),
    )
    assert S // SLC_KEY_TILE <= 8
    return pl.pallas_call(
        functools.partial(_cmp_select_kernel, tq=tq, n_cmp=n_cmp, n_blocks=n_blocks,
                          n_rank=-(-n_blocks // 8) * 8),
        out_shape=(jax.ShapeDtypeStruct((T, B_HEADS * d), F32),
                   jax.ShapeDtypeStruct((B, G, S, 128), BF16),
                   jax.ShapeDtypeStruct((B, G, nq, 8, 128), I32)),
        grid_spec=grid_spec,
        compiler_params=_cparams(("parallel", "parallel", "arbitrary")),
        name="nsa_cmp_select",
    )(slopes, projm, cmp_kv, cmp_kv, overlap)


def _flash_tile(ka, v, qa_ref, mask, m_ref, l_ref, acc_ref):
    n_heads, tq, _ = qa_ref.shape
    qc = m_ref.shape[-1]

    def score_fn(hh, c):
        def fn():
            sT = _nt_dot(ka, qa_ref[hh, c * qc:(c + 1) * qc, :])
            return sT if mask is None else sT + mask[:, c * qc:(c + 1) * qc]
        return fn

    _flash_units([score_fn(hh, c) for hh in range(n_heads) for c in range(tq // qc)], v, m_ref, l_ref, acc_ref)


def _write_heads_T(o_ref, acc_ref, l_ref):
    tq = o_ref.shape[0]
    qc = l_ref.shape[-1]
    for hh in range(B_HPG):
        for c in range(tq // qc):
            u = hh * (tq // qc) + c
            o_ref[c * qc:(c + 1) * qc, hh * B_D:(hh + 1) * B_D] = (acc_ref[u] / l_ref[u]).T


def _flash_state(n_heads, tq, dv):
    qc = min(ATTN_QC, tq)
    n_units = n_heads * (tq // qc)
    return [pltpu.VMEM((n_units, 1, qc), F32), pltpu.VMEM((n_units, 1, qc), F32),
            pltpu.VMEM((n_units, dv, qc), F32)]


def _stage_queries(q_ref, qa_ref, slopes_ref, g, q0, sel_add):
    tq = q_ref.shape[0]
    lane = lax.broadcasted_iota(I32, (tq, 128), 1)
    for hh in range(B_HPG):
        coef = jnp.broadcast_to(_alibi_query_features(slopes_ref[g * B_HPG + hh] * LOG2E, q0), (tq, 128))
        if sel_add is not None:
            coef = jnp.where(lane < FEAT_BLK, sel_add, coef)
        qa_ref[hh] = jnp.concatenate([q_ref[:, hh * B_D:(hh + 1) * B_D], coef.astype(BF16)], axis=1)


def _slc_attn_kernel(slopes_ref, need_ref, q_ref, k_ref, v_ref, feat_ref, sel_ref, o_ref, qa_ref, m_ref, l_ref,
                     acc_ref, *, tq):
    b = pl.program_id(0)
    g = pl.program_id(1)
    i = pl.program_id(2)
    n = pl.num_programs(2)
    _flash_init(m_ref, l_ref, acc_ref)
    _stage_queries(q_ref, qa_ref, slopes_ref, g, i * tq, sel_ref[...].astype(F32))

    def tile(j, causal):
        k0 = pl.multiple_of(j * tq, tq)
        ka = jnp.concatenate([k_ref[pl.ds(k0, tq), :], feat_ref[pl.ds(k0, tq), :]], axis=1)
        v = v_ref[pl.ds(k0, tq), :]
        _flash_tile(ka, v, qa_ref, causal, m_ref, l_ref, acc_ref)

    need_base = ((b * pl.num_programs(1) + g) * n + i) * n

    def body(j, carry):
        @pl.when((j == 0) | (need_ref[need_base + j] > 0))
        def _():
            tile(j, None)
        return carry

    lax.fori_loop(0, i, body, 0)
    tile(i, _tri_mask(tq, tq, True))
    _write_heads_T(o_ref, acc_ref, l_ref)


def _slc_attention(projm, sel, need, B, S):
    T = B * S
    G, d = B_GROUPS, B_D
    tq = SLC_KEY_TILE
    n = S // tq
    slopes = _alibi_slopes(B_HEADS)
    ksb = (B_HEADS * d + 2 * G * d) // d
    vsb = ksb + G
    grid_spec = pltpu.PrefetchScalarGridSpec(
        num_scalar_prefetch=2,
        grid=(B, G, n),
        in_specs=[
            pl.BlockSpec((tq, B_HPG * d), lambda b, g, i, sl, nd: (b * n + i, g)),
            pl.BlockSpec((S, d), lambda b, g, i, sl, nd: (b, ksb + g)),
            pl.BlockSpec((S, d), lambda b, g, i, sl, nd: (b, vsb + g)),
            pl.BlockSpec((S, 128), lambda b, g, i, sl, nd: (0, 0)),
            pl.BlockSpec((None, None, tq, 128), lambda b, g, i, sl, nd: (b, g, i, 0)),
        ],
        out_specs=pl.BlockSpec((tq, B_HPG * d), lambda b, g, i, sl, nd: (b * n + i, g)),
        scratch_shapes=[pltpu.VMEM((B_HPG, tq, 2 * d), BF16)] + _flash_state(B_HPG, tq, d),
    )
    return pl.pallas_call(
        functools.partial(_slc_attn_kernel, tq=tq),
        out_shape=jax.ShapeDtypeStruct((T, B_HEADS * d), F32),
        grid_spec=grid_spec,
        compiler_params=_cparams(("parallel", "parallel", "arbitrary")),
        name="nsa_selected_attention",
    )(slopes, need.reshape(-1), projm, projm, projm, _key_features(S), sel)


def _win_attn_kernel(slopes_ref, q_ref, k_ref, v_ref, feat_ref, o_ref, qa_ref, m_ref, l_ref, acc_ref, *, tq):
    g = pl.program_id(1)
    i = pl.program_id(2)
    _flash_init(m_ref, l_ref, acc_ref)
    _stage_queries(q_ref, qa_ref, slopes_ref, g, i * tq, None)

    def tile(j, mask):
        k0 = pl.multiple_of(j * tq, tq)
        ka = jnp.concatenate([k_ref[pl.ds(k0, tq), :], feat_ref[pl.ds(k0, tq), :]], axis=1)
        v = v_ref[pl.ds(k0, tq), :]
        _flash_tile(ka, v, qa_ref, mask, m_ref, l_ref, acc_ref)

    @pl.when(i > 0)
    def _():
        tile(i - 1, _tri_mask(tq, tq, False))

    tile(i, _tri_mask(tq, tq, True))
    _write_heads_T(o_ref, acc_ref, l_ref)


def _win_attention(projm, B, S):
    T = B * S
    G, d = B_GROUPS, B_D
    tq = WINDOW
    n = S // tq
    slopes = _alibi_slopes(B_HEADS)
    kwb = (B_HEADS * d + 4 * G * d) // d
    vwb = kwb + G
    grid_spec = pltpu.PrefetchScalarGridSpec(
        num_scalar_prefetch=1,
        grid=(B, G, n),
        in_specs=[
            pl.BlockSpec((tq, B_HPG * d), lambda b, g, i, sl: (b * n + i, g)),
            pl.BlockSpec((S, d), lambda b, g, i, sl: (b, kwb + g)),
            pl.BlockSpec((S, d), lambda b, g, i, sl: (b, vwb + g)),
            pl.BlockSpec((S, 128), lambda b, g, i, sl: (0, 0)),
        ],
        out_specs=pl.BlockSpec((tq, B_HPG * d), lambda b, g, i, sl: (b * n + i, g)),
        scratch_shapes=[pltpu.VMEM((B_HPG, tq, 2 * d), BF16)] + _flash_state(B_HPG, tq, d),
    )
    return pl.pallas_call(
        functools.partial(_win_attn_kernel, tq=tq),
        out_shape=jax.ShapeDtypeStruct((T, B_HEADS * d), F32),
        grid_spec=grid_spec,
        compiler_params=_cparams(("parallel", "parallel", "arbitrary")),
        name="nsa_window_attention",
    )(slopes, projm, projm, projm, _key_features(S))


def _gate_mix_kernel(gl_ref, oc_ref, os_ref, ow_ref, o_ref):
    sig = jax.nn.sigmoid(gl_ref[...])
    d = B_D
    for hd in range(B_HEADS):
        sl = slice(hd * d, (hd + 1) * d)
        o = (sig[:, 3 * hd:3 * hd + 1] * oc_ref[:, sl] + sig[:, 3 * hd + 1:3 * hd + 2] * os_ref[:, sl]
             + sig[:, 3 * hd + 2:3 * hd + 3] * ow_ref[:, sl])
        o_ref[:, sl] = o.astype(o_ref.dtype)


def _gate_mix(gate_logits, o_cmp, o_slc, o_win, tm=256):
    T, W = o_cmp.shape
    spec = pl.BlockSpec((tm, W), lambda i: (i, 0))
    return pl.pallas_call(
        _gate_mix_kernel,
        out_shape=jax.ShapeDtypeStruct((T, W), BF16),
        grid=(T // tm,),
        in_specs=[pl.BlockSpec((tm, 128), lambda i: (i, 0)), spec, spec, spec],
        out_specs=spec,
        compiler_params=_cparams(("parallel",)),
        name="nsa_gate_mix",
    )(gate_logits, o_cmp, o_slc, o_win)


def _overlap_matrix(S, n_rows):
    nc = (S - CMP_LEN) // CMP_STRIDE + 1
    nsb = S // SLC_BLOCK
    cs = np.arange(nc) * CMP_STRIDE
    ss = np.arange(nsb) * SLC_BLOCK
    ov = np.clip(np.minimum(cs[:, None] + CMP_LEN, ss[None, :] + SLC_BLOCK)
                 - np.maximum(cs[:, None], ss[None, :]), 0, None) / CMP_LEN
    out = np.zeros((128, n_rows), np.float32)
    out[:nsb, :nc] = ov.T
    return jnp.asarray(out, BF16)


def _nsa_attention(xn, w_in, cmp_pos, cmp_w1, cmp_w2, B, S):
    T = B * S
    H, G, d = B_HEADS, B_GROUPS, B_D
    n_main = H * d + 6 * G * d
    col_scale = jnp.concatenate([jnp.full((H * d,), d ** -0.5 * LOG2E, F32), jnp.ones((n_main - H * d,), F32)])
    projm = _matmul(xn, w_in, n_main, BF16, col_scale=col_scale)
    w_gate = jnp.pad(w_in[:, n_main:], ((0, 0), (0, 128 - 3 * H)))
    gate_logits = _matmul(xn, w_gate, 128, F32, tn=128)

    cmp_kv = _compress(projm, cmp_w1, cmp_w2, cmp_pos.reshape(2, 1, CMP_LEN * d), B, S)
    o_cmp, sel, need = _cmp_select(projm, cmp_kv, _overlap_matrix(S, S // CMP_STRIDE), B, S)
    n_t = S // SLC_KEY_TILE
    need = need[:, :, :, :n_t, 0].reshape(B, G, n_t, -1, n_t).max(axis=3)
    o_slc = _slc_attention(projm, sel, need, B, S)
    o_win = _win_attention(projm, B, S)
    return _gate_mix(gate_logits, o_cmp, o_slc, o_win)


def kernel(x, attn_norm_w, ffn_norm_w, final_norm_w, a_w_in, a_lambda, a_subln_w, a_w_out, b_w_in,
           b_cmp_pos, b_cmp_w1, b_cmp_w2, b_w_out, moe_w_group, moe_b_group, moe_w_expert,
           moe_b_expert, moe_w_gate, moe_w_up, moe_w_down):
    B, S, D = x.shape
    T = B * S
    depth = attn_norm_w.shape[0]
    h = x.reshape(T, D)
    for i in range(depth):
        xn = _rmsnorm(h, attn_norm_w[i], BF16)
        j = i // 2
        if i % 2 == 0:
            lambda_init = 0.8 - 0.6 * math.exp(-0.3 * i)
            n_q = A_HEADS * 2 * A_DH
            n_in = 2 * n_q + A_HEADS * A_DV
            col_scale = jnp.concatenate([jnp.full((n_q,), A_DH ** -0.5 * LOG2E, F32),
                                         jnp.ones((n_in - n_q,), F32)])
            qkv = _matmul(xn, a_w_in[j], n_in, BF16, col_scale=col_scale)
            o = _diff_attention(qkv, a_lambda[j], a_subln_w[j], lambda_init, B, S)
            h = _matmul(o, a_w_out[j], D, F32, resid=h)
        else:
            o = _nsa_attention(xn, b_w_in[j], b_cmp_pos[j], b_cmp_w1[j], b_cmp_w2[j], B, S)
            h = _matmul(o, b_w_out[j], D, F32, resid=h)
        h = _hier_moe(h, ffn_norm_w[i], moe_w_group[i], moe_b_group[i], moe_w_expert[i], moe_b_expert[i],
                      moe_w_gate, moe_w_up, moe_w_down, i)
    return _rmsnorm(h, final_norm_w, F32).reshape(B, S, D)
```

```python
import functools
import math

import numpy as np
import jax
import jax.numpy as jnp
from jax import lax
from jax.experimental import pallas as pl
from jax.experimental.pallas import tpu as pltpu

F32 = jnp.float32
BF16 = jnp.bfloat16
I32 = jnp.int32

EPS = 1e-6
NEG = -1e30
FORCE = 1e4
LOG2E = math.log2(math.e)

A_HEADS = 8
A_DH = 128
A_DV = 256
B_HEADS = 16
B_GROUPS = 4
B_HPG = 4
B_D = 128
CMP_LEN = 32
CMP_STRIDE = 16
SLC_BLOCK = 64
SLC_TOPN = 16
WINDOW = 512
N_GROUPS = 4
EPG = 8
N_EXPERTS = 32
D_EXPERT = 1024
ROUTER_LANES = 128
MOE_ROWS = 256
DMA_UNROLL = 8
ATTN_QC = 512
SLC_KEY_TILE = 512

VMEM_LIMIT = 56 * 1024 * 1024


def _cparams(sem, flags=None):
    return pltpu.CompilerParams(dimension_semantics=sem, vmem_limit_bytes=VMEM_LIMIT, flags=flags)


def _alibi_slopes(n):
    return jnp.asarray(np.array([2.0 ** (-8.0 * (i + 1) / n) for i in range(n)], dtype=np.float32))


def _nt_dot(a, b):
    return lax.dot_general(a, b, (((1,), (1,)), ((), ())), preferred_element_type=F32)


def _rms_kernel(x_ref, w_ref, o_ref):
    x = x_ref[...]
    y = x * lax.rsqrt(jnp.mean(x * x, axis=-1, keepdims=True) + EPS)
    o_ref[...] = (y * w_ref[...]).astype(o_ref.dtype)


def _rmsnorm(x, w, out_dtype, tm=512):
    T, D = x.shape
    return pl.pallas_call(
        _rms_kernel,
        out_shape=jax.ShapeDtypeStruct((T, D), out_dtype),
        grid=(T // tm,),
        in_specs=[pl.BlockSpec((tm, D), lambda i: (i, 0)),
                  pl.BlockSpec((1, D), lambda i: (0, 0))],
        out_specs=pl.BlockSpec((tm, D), lambda i: (i, 0)),
        compiler_params=_cparams(("parallel",)),
        name="rmsnorm",
    )(x, w.reshape(1, D))


def _mm_kernel(*refs, has_scale, has_resid):
    a_ref, w_ref = refs[0], refs[1]
    k = 2
    scale_ref = resid_ref = None
    if has_scale:
        scale_ref = refs[k]
        k += 1
    if has_resid:
        resid_ref = refs[k]
        k += 1
    o_ref, wbf_ref = refs[k], refs[k + 1]

    @pl.when(pl.program_id(1) == 0)
    def _():
        wbf_ref[...] = w_ref[...].astype(BF16)

    acc = jnp.dot(a_ref[...], wbf_ref[...], preferred_element_type=F32)
    if has_scale:
        acc = acc * scale_ref[...]
    if has_resid:
        acc = acc + resid_ref[...]
    o_ref[...] = acc.astype(o_ref.dtype)


def _matmul(a, w, n_out, out_dtype, *, col_scale=None, resid=None, tm=1024, tn=512):
    M, K = a.shape
    assert M % tm == 0 and n_out % tn == 0 and w.shape[0] == K and w.shape[1] >= n_out
    in_specs = [pl.BlockSpec((tm, K), lambda j, i: (i, 0)),
                pl.BlockSpec((K, tn), lambda j, i: (0, j))]
    args = [a, w]
    if col_scale is not None:
        in_specs.append(pl.BlockSpec((1, tn), lambda j, i: (0, j)))
        args.append(col_scale.reshape(1, n_out))
    if resid is not None:
        in_specs.append(pl.BlockSpec((tm, tn), lambda j, i: (i, j)))
        args.append(resid)
    return pl.pallas_call(
        functools.partial(_mm_kernel, has_scale=col_scale is not None, has_resid=resid is not None),
        out_shape=jax.ShapeDtypeStruct((M, n_out), out_dtype),
        grid=(n_out // tn, M // tm),
        in_specs=in_specs,
        out_specs=pl.BlockSpec((tm, tn), lambda j, i: (i, j)),
        scratch_shapes=[pltpu.VMEM((K, tn), BF16)],
        compiler_params=_cparams(("parallel", "arbitrary")),
        name="matmul",
    )(*args)


FEAT_BLK = 64
FEAT_POS = 67
FEAT_ONE = 70


def _key_features(S):
    assert S // SLC_BLOCK <= FEAT_BLK
    j = np.arange(S)
    f = np.zeros((S, 128), np.float32)
    f[j, j // SLC_BLOCK] = 1.0
    f[:, FEAT_BLK:FEAT_BLK + 3] = (j // SLC_BLOCK)[:, None]
    f[:, FEAT_POS:FEAT_POS + 3] = (j % SLC_BLOCK)[:, None]
    f[:, FEAT_ONE:FEAT_ONE + 3] = 1.0
    return jnp.asarray(f, BF16)


def _alibi_query_features(slope2, q0):
    lane = lax.broadcasted_iota(I32, (1, 128), 1)
    base = jnp.where((lane >= FEAT_BLK) & (lane < FEAT_BLK + 3), slope2 * float(SLC_BLOCK),
                     jnp.where((lane >= FEAT_POS) & (lane < FEAT_POS + 3), slope2,
                               jnp.where((lane >= FEAT_ONE) & (lane < FEAT_ONE + 3),
                                         -slope2 * q0.astype(F32), 0.0)))
    hi = base.astype(BF16).astype(F32)
    r1 = base - hi
    lo = r1.astype(BF16).astype(F32)
    lo2 = (r1 - lo).astype(BF16).astype(F32)
    first = (lane == FEAT_BLK) | (lane == FEAT_POS) | (lane == FEAT_ONE)
    second = (lane == FEAT_BLK + 1) | (lane == FEAT_POS + 1) | (lane == FEAT_ONE + 1)
    return jnp.where(first, hi, jnp.where(second, lo, lo2))


def _flash_init(m_ref, l_ref, acc_ref):
    m_ref[...] = jnp.full(m_ref.shape, NEG, F32)
    l_ref[...] = jnp.zeros(l_ref.shape, F32)
    acc_ref[...] = jnp.zeros(acc_ref.shape, F32)


def _flash_probs(sT, m_ref, l_ref, u):
    m_old = m_ref[u]
    m_new = jnp.maximum(m_old, jnp.max(sT, axis=0, keepdims=True))
    alpha = jnp.exp2(m_old - m_new)
    p = jnp.exp2(sT - m_new)
    l_ref[u] = alpha * l_ref[u] + jnp.sum(p, axis=0, keepdims=True)
    m_ref[u] = m_new
    return p.astype(BF16), alpha


def _flash_accumulate(p, alpha, v, acc_ref, u):
    pv = lax.dot_general(v, p, (((0,), (0,)), ((), ())), preferred_element_type=F32)
    acc_ref[u] = alpha * acc_ref[u] + pv


def _flash_units(units, m_ref, l_ref, acc_ref):
    n = len(units)
    scores, probs = {}, {}
    for s in range(n + 2):
        if s < n:
            scores[s] = units[s][0]()
        if 1 <= s <= n:
            probs[s - 1] = _flash_probs(scores.pop(s - 1), m_ref, l_ref, units[s - 1][2])
        if 2 <= s <= n + 1:
            p, alpha = probs.pop(s - 2)
            _flash_accumulate(p, alpha, units[s - 2][1], acc_ref, units[s - 2][2])


def _tri_mask(tk, tq, keep_upper):
    r = lax.broadcasted_iota(I32, (tk, tq), 0)
    c = lax.broadcasted_iota(I32, (tk, tq), 1)
    keep = (r <= c) if keep_upper else (r > c)
    return jnp.where(keep, 0.0, NEG)


def _diff_attn_kernel(slopes_ref, q1_ref, q2_ref, k1_ref, k2_ref, v_ref, feat_ref, lam_ref, sw_ref, o_ref,
                      qa_ref, m_ref, l_ref, acc_ref, *, tq, lambda_init):
    h = pl.program_id(1)
    i = pl.program_id(2)
    _flash_init(m_ref, l_ref, acc_ref)
    qfeat = jnp.broadcast_to(_alibi_query_features(slopes_ref[h] * LOG2E, i * tq), (tq, 128)).astype(BF16)
    for c, q_ref in enumerate((q1_ref, q2_ref)):
        qa_ref[c] = jnp.concatenate([q_ref[...], qfeat], axis=1)

    def tile_units(j, mask):
        k0 = pl.multiple_of(j * tq, tq)
        v = v_ref[pl.ds(k0, tq), :]
        feat = feat_ref[pl.ds(k0, tq), :]
        qc = m_ref.shape[-1]

        def score_fn(c, k_ref, x):
            def fn():
                ka = jnp.concatenate([k_ref[pl.ds(k0, tq), :], feat], axis=1)
                sT = _nt_dot(ka, qa_ref[c, x * qc:(x + 1) * qc, :])
                return sT if mask is None else sT + mask[:, x * qc:(x + 1) * qc]
            return fn

        return [(score_fn(c, k_ref, x), v, c * (tq // qc) + x)
                for c, k_ref in enumerate((k1_ref, k2_ref)) for x in range(tq // qc)]

    def body(p, carry):
        _flash_units(tile_units(2 * p, None) + tile_units(2 * p + 1, None), m_ref, l_ref, acc_ref)
        return carry

    lax.fori_loop(0, lax.shift_right_logical(i, 1), body, 0)

    @pl.when(i % 2 == 1)
    def _():
        _flash_units(tile_units(i - 1, None), m_ref, l_ref, acc_ref)

    _flash_units(tile_units(i, _tri_mask(tq, tq, True)), m_ref, l_ref, acc_ref)

    lam = lam_ref[...]
    lmbda = (jnp.exp(jnp.sum(lam[0:1] * lam[1:2], axis=-1, keepdims=True))
             - jnp.exp(jnp.sum(lam[2:3] * lam[3:4], axis=-1, keepdims=True)) + lambda_init)
    qc = m_ref.shape[-1]
    nx = tq // qc
    for x in range(nx):
        oT = acc_ref[x] / l_ref[x] - lmbda * (acc_ref[nx + x] / l_ref[nx + x])
        o = oT.T
        y = o * lax.rsqrt(jnp.mean(o * o, axis=-1, keepdims=True) + EPS)
        y = (y * sw_ref[...]) * (1.0 - lambda_init)
        o_ref[x * qc:(x + 1) * qc, :] = y.astype(o_ref.dtype)


def _diff_attention(qkv, lam, subln_w, lambda_init, B, S, tq=512):
    T = B * S
    H, dh, dv = A_HEADS, A_DH, A_DV
    n = S // tq
    slopes = _alibi_slopes(H)
    kb = H * 2
    vb = (2 * H * 2 * dh) // dv
    grid_spec = pltpu.PrefetchScalarGridSpec(
        num_scalar_prefetch=1,
        grid=(B, H, n),
        in_specs=[
            pl.BlockSpec((tq, dh), lambda b, h, i, sl: (b * n + i, 2 * h)),
            pl.BlockSpec((tq, dh), lambda b, h, i, sl: (b * n + i, 2 * h + 1)),
            pl.BlockSpec((S, dh), lambda b, h, i, sl: (b, kb + 2 * h)),
            pl.BlockSpec((S, dh), lambda b, h, i, sl: (b, kb + 2 * h + 1)),
            pl.BlockSpec((S, dv), lambda b, h, i, sl: (b, vb + h)),
            pl.BlockSpec((S, 128), lambda b, h, i, sl: (0, 0)),
            pl.BlockSpec((4, dh), lambda b, h, i, sl: (0, 0)),
            pl.BlockSpec((1, dv), lambda b, h, i, sl: (0, 0)),
        ],
        out_specs=pl.BlockSpec((tq, dv), lambda b, h, i, sl: (b * n + i, h)),
        scratch_shapes=[pltpu.VMEM((2, tq, 2 * dh), BF16)] + _flash_state(2, tq, dv),
    )
    return pl.pallas_call(
        functools.partial(_diff_attn_kernel, tq=tq, lambda_init=lambda_init),
        out_shape=jax.ShapeDtypeStruct((T, H * dv), BF16),
        grid_spec=grid_spec,
        compiler_params=_cparams(("parallel", "parallel", "arbitrary")),
        name="diff_attention",
    )(slopes, qkv, qkv, qkv, qkv, qkv, _key_features(S), lam, subln_w.reshape(1, dv))


def _router_kernel(h_ref, nw_ref, wr_ref, br_ref, tri_ref, xn_ref, ids_ref, wts_ref, cnt_ref, base_ref):
    @pl.when(pl.program_id(0) == 0)
    def _():
        base_ref[...] = jnp.zeros(base_ref.shape, F32)

    x = h_ref[...]
    xn = (x * lax.rsqrt(jnp.mean(x * x, axis=-1, keepdims=True) + EPS)) * nw_ref[...]
    xn_ref[...] = xn
    w = wr_ref[...]
    x_hi = xn.astype(BF16)
    x_lo = (xn - x_hi.astype(F32)).astype(BF16)
    w_hi = w.astype(BF16)
    w_lo = (w - w_hi.astype(F32)).astype(BF16)
    logits = (jnp.dot(x_hi, w_hi, preferred_element_type=F32)
              + (jnp.dot(x_hi, w_lo, preferred_element_type=F32)
                 + jnp.dot(x_lo, w_hi, preferred_element_type=F32))) + br_ref[...]
    lane = lax.broadcasted_iota(I32, logits.shape, 1)
    big = jnp.int32(1 << 20)
    is_g = lane < N_GROUPS
    is_e = (lane >= N_GROUPS) & (lane < N_GROUPS + N_EXPERTS)

    lg = jnp.where(is_g, logits, NEG)
    mg = jnp.max(lg, axis=-1, keepdims=True)
    g_top = jnp.min(jnp.where(lg == mg, lane, big), axis=-1, keepdims=True)
    wg = 1.0 / jnp.sum(jnp.where(is_g, jnp.exp(lg - mg), 0.0), axis=-1, keepdims=True)

    in_grp = is_e & (lax.shift_right_logical(lane - N_GROUPS, 3) == g_top)
    le = jnp.where(in_grp, logits, NEG)
    me = jnp.max(le, axis=-1, keepdims=True)
    ee = jnp.where(in_grp, jnp.exp(le - me), 0.0)
    pe = ee / jnp.sum(ee, axis=-1, keepdims=True)
    pe = jnp.where(in_grp, pe, -1.0)
    p1 = jnp.max(pe, axis=-1, keepdims=True)
    i1 = jnp.min(jnp.where(pe == p1, lane, big), axis=-1, keepdims=True)
    pe2 = jnp.where(lane == i1, -1.0, pe)
    p2 = jnp.max(pe2, axis=-1, keepdims=True)
    i2 = jnp.min(jnp.where(pe2 == p2, lane, big), axis=-1, keepdims=True)
    den = p1 + p2
    w0 = wg * p1 / den
    w1 = wg * p2 / den

    oh1 = lane == i1
    oh2 = lane == i2
    both = jnp.where(oh1 | oh2, 1.0, 0.0)
    before = jnp.dot(tri_ref[...], both.astype(BF16), preferred_element_type=F32) + base_ref[...]
    pos0 = jnp.sum(jnp.where(oh1, before, 0.0), axis=-1, keepdims=True).astype(I32)
    pos1 = jnp.sum(jnp.where(oh2, before, 0.0), axis=-1, keepdims=True).astype(I32)
    total = base_ref[...] + jnp.sum(both, axis=0, keepdims=True)
    base_ref[...] = total
    cnt_ref[...] = total

    ids = jnp.where(lane == 0, i1 - N_GROUPS,
                    jnp.where(lane == 1, i2 - N_GROUPS,
                              jnp.where(lane == 2, pos0, jnp.where(lane == 3, pos1, 0))))
    ids_ref[...] = ids.T[:8]
    wts_ref[...] = jnp.where(lane == 0, w0, jnp.where(lane == 1, w1, 0.0))


def _router(h, norm_w, w_group, b_group, w_expert, b_expert, tm=512):
    T, D = h.shape
    pad = ROUTER_LANES - N_GROUPS - N_EXPERTS
    wr = jnp.concatenate([w_group, w_expert, jnp.zeros((D, pad), F32)], axis=1)
    br = jnp.concatenate([b_group, b_expert, jnp.zeros((pad,), F32)]).reshape(1, ROUTER_LANES)
    tri = jnp.asarray(np.tril(np.ones((tm, tm), np.float32), -1), BF16)
    return pl.pallas_call(
        _router_kernel,
        out_shape=(jax.ShapeDtypeStruct((T, D), F32),
                   jax.ShapeDtypeStruct((8, T), I32),
                   jax.ShapeDtypeStruct((T, ROUTER_LANES), F32),
                   jax.ShapeDtypeStruct((1, ROUTER_LANES), F32)),
        grid=(T // tm,),
        in_specs=[pl.BlockSpec((tm, D), lambda i: (i, 0)),
                  pl.BlockSpec((1, D), lambda i: (0, 0)),
                  pl.BlockSpec((D, ROUTER_LANES), lambda i: (0, 0)),
                  pl.BlockSpec((1, ROUTER_LANES), lambda i: (0, 0)),
                  pl.BlockSpec((tm, tm), lambda i: (0, 0))],
        out_specs=(pl.BlockSpec((tm, D), lambda i: (i, 0)),
                   pl.BlockSpec((8, tm), lambda i: (0, i)),
                   pl.BlockSpec((tm, ROUTER_LANES), lambda i: (i, 0)),
                   pl.BlockSpec((1, ROUTER_LANES), lambda i: (0, 0))),
        scratch_shapes=[pltpu.VMEM((1, ROUTER_LANES), F32)],
        compiler_params=_cparams(("arbitrary",)),
        name="moe_router",
    )(h, norm_w.reshape(1, D), wr, br, tri)


def _dispatch_kernel(dest_ref, xn_ref, xs_in_ref, xs_ref, sem, *, tm, n_tok):
    del xs_in_ref
    base = pl.program_id(0) * tm
    for t in range(tm):
        for slot in range(2):
            d = dest_ref[slot * n_tok + base + t]
            pltpu.make_async_copy(xn_ref.at[pl.ds(t, 1)], xs_ref.at[pl.ds(d, 1)], sem).start()
    for slot in range(2):
        pltpu.make_async_copy(xn_ref, xs_ref.at[pl.ds(0, tm)], sem).wait()


def _dispatch(xn, dest_flat, n_rows, tm=256):
    T, D = xn.shape
    grid_spec = pltpu.PrefetchScalarGridSpec(
        num_scalar_prefetch=1,
        grid=(T // tm,),
        in_specs=[pl.BlockSpec((tm, D), lambda i, d: (i, 0)),
                  pl.BlockSpec(memory_space=pl.ANY)],
        out_specs=pl.BlockSpec(memory_space=pl.ANY),
        scratch_shapes=[pltpu.SemaphoreType.DMA],
    )
    return pl.pallas_call(
        functools.partial(_dispatch_kernel, tm=tm, n_tok=T),
        out_shape=jax.ShapeDtypeStruct((n_rows, D), F32),
        grid_spec=grid_spec,
        input_output_aliases={2: 0},
        compiler_params=_cparams(("arbitrary",)),
        name="moe_dispatch",
    )(dest_flat, xn, jnp.zeros((n_rows, D), F32))


def _moe_plan(counts, n_blk):
    RB = MOE_ROWS
    nb_e = (counts + RB - 1) // RB
    end_b = jnp.cumsum(nb_e)
    start_b = end_b - nb_e
    n_used = end_b[-1:].astype(I32)
    r = jnp.arange(n_blk, dtype=I32)
    be = jnp.minimum(jnp.sum((end_b[None, :] <= r[:, None]).astype(I32), axis=1), N_EXPERTS - 1)
    active = nb_e > 0
    ordinal = jnp.cumsum(active.astype(I32)) - 1
    idx = jnp.arange(N_EXPERTS, dtype=I32)
    later = jnp.where(active[None, :] & (idx[None, :] > idx[:, None]), idx[None, :], N_EXPERTS)
    nxt_e = jnp.min(later, axis=1)
    nxt_e = jnp.where(nxt_e == N_EXPERTS, -1, nxt_e)
    first = ((r == start_b[be]) & (r < n_used[0])).astype(I32)
    plan = jnp.stack([be, first, ordinal[be] % 2, nxt_e[be]]).astype(I32)
    return plan, n_used, (start_b * RB).astype(I32)


def _stream_expert_weights(plan_ref, r, w_hbms, wbuf, sem, layer):
    e, slot, nxt = plan_ref[0, r], plan_ref[2, r], plan_ref[3, r]

    def copies(expert, s):
        return [pltpu.make_async_copy(w.at[layer, expert], wbuf.at[s, k], sem.at[s, k])
                for k, w in enumerate(w_hbms)]

    @pl.when(plan_ref[1, r] == 1)
    def _():
        @pl.when(r == 0)
        def _():
            for c in copies(e, slot):
                c.start()

        for c in copies(e, slot):
            c.wait()

        @pl.when(nxt >= 0)
        def _():
            for c in copies(nxt, 1 - slot):
                c.start()

    return slot


def _gate_up_kernel(plan_ref, nu_ref, x_ref, wg_hbm, wu_hbm, h_ref, wbuf, sem, *, layer):
    r = pl.program_id(0)

    @pl.when(r < nu_ref[0])
    def _():
        slot = _stream_expert_weights(plan_ref, r, (wg_hbm, wu_hbm), wbuf, sem, layer)
        x = x_ref[...]
        g = jnp.dot(x, wbuf[slot, 0], preferred_element_type=F32)
        u = jnp.dot(x, wbuf[slot, 1], preferred_element_type=F32)
        h_ref[...] = (jax.nn.silu(g) * u).astype(h_ref.dtype)

    @pl.when(r >= nu_ref[0])
    def _():
        h_ref[...] = jnp.zeros(h_ref.shape, h_ref.dtype)


def _down_kernel(plan_ref, nu_ref, h_ref, wd_hbm, y_ref, wbuf, sem, *, layer):
    r = pl.program_id(0)

    @pl.when(r < nu_ref[0])
    def _():
        slot = _stream_expert_weights(plan_ref, r, (wd_hbm,), wbuf, sem, layer)
        y_ref[...] = jnp.dot(h_ref[...].astype(F32), wbuf[slot, 0], preferred_element_type=F32)

    @pl.when(r >= nu_ref[0])
    def _():
        y_ref[...] = jnp.zeros(y_ref.shape, y_ref.dtype)


def _expert_mlp(xs, plan, n_used, w_gate, w_up, w_down, layer):
    n_rows, D = xs.shape
    RB = MOE_ROWS
    n_blk = n_rows // RB
    F = D_EXPERT

    def used_blk(r, plan, nu):
        return (jnp.minimum(r, nu[0] - 1), 0)

    h = pl.pallas_call(
        functools.partial(_gate_up_kernel, layer=layer),
        out_shape=jax.ShapeDtypeStruct((n_rows, F), BF16),
        grid_spec=pltpu.PrefetchScalarGridSpec(
            num_scalar_prefetch=2,
            grid=(n_blk,),
            in_specs=[pl.BlockSpec((RB, D), used_blk),
                      pl.BlockSpec(memory_space=pl.ANY),
                      pl.BlockSpec(memory_space=pl.ANY)],
            out_specs=pl.BlockSpec((RB, F), lambda r, plan, nu: (r, 0)),
            scratch_shapes=[pltpu.VMEM((2, 2, D, F), F32), pltpu.SemaphoreType.DMA((2, 2))],
        ),
        compiler_params=_cparams(("arbitrary",)),
        name="moe_gate_up",
    )(plan, n_used, xs, w_gate, w_up)
    y = pl.pallas_call(
        functools.partial(_down_kernel, layer=layer),
        out_shape=jax.ShapeDtypeStruct((n_rows, D), F32),
        grid_spec=pltpu.PrefetchScalarGridSpec(
            num_scalar_prefetch=2,
            grid=(n_blk,),
            in_specs=[pl.BlockSpec((RB, F), used_blk),
                      pl.BlockSpec(memory_space=pl.ANY)],
            out_specs=pl.BlockSpec((RB, D), lambda r, plan, nu: (r, 0)),
            scratch_shapes=[pltpu.VMEM((2, 1, F, D), F32), pltpu.SemaphoreType.DMA((2, 1))],
        ),
        compiler_params=_cparams(("arbitrary",)),
        name="moe_down",
    )(plan, n_used, h, w_down)
    return y


def _combine_kernel(dest_ref, resid_ref, wts_ref, nw_ref, y_ref, *rest, tm, n_tok, keep_h):
    if keep_h:
        o_ref, n_ref, buf_ref, sem = rest
    else:
        o_ref = None
        n_ref, buf_ref, sem = rest
    i = pl.program_id(0)
    n = pl.num_programs(0)

    def gather(tile, par):
        for t in range(tm):
            for slot in range(2):
                d = dest_ref[slot * n_tok + tile * tm + t]
                pltpu.make_async_copy(y_ref.at[pl.ds(d, 1)], buf_ref.at[par, slot, pl.ds(t, 1)],
                                      sem.at[par]).start()

    @pl.when(i == 0)
    def _():
        gather(i, 0)

    for par in range(2):
        @pl.when((i + 1 < n) & ((i + 1) % 2 == par))
        def _():
            gather(i + 1, par)

    for par in range(2):
        @pl.when(i % 2 == par)
        def _():
            for slot in range(2):
                pltpu.make_async_copy(y_ref.at[pl.ds(0, tm)], buf_ref.at[par, slot], sem.at[par]).wait()
            w = wts_ref[...]
            hn = resid_ref[...] + (w[:, 0:1] * buf_ref[par, 0] + w[:, 1:2] * buf_ref[par, 1])
            if keep_h:
                o_ref[...] = hn
            yn = hn * lax.rsqrt(jnp.mean(hn * hn, axis=-1, keepdims=True) + EPS)
            n_ref[...] = (yn * nw_ref[...]).astype(n_ref.dtype)


def _combine(resid, wts, y, dest_flat, next_norm_w, norm_dtype, keep_h, tm=128):
    T, D = resid.shape
    row_spec = pl.BlockSpec((tm, D), lambda i, d: (i, 0))
    out_shape = [jax.ShapeDtypeStruct((T, D), norm_dtype)]
    out_specs = [row_spec]
    if keep_h:
        out_shape.insert(0, jax.ShapeDtypeStruct((T, D), F32))
        out_specs.insert(0, row_spec)
    grid_spec = pltpu.PrefetchScalarGridSpec(
        num_scalar_prefetch=1,
        grid=(T // tm,),
        in_specs=[row_spec,
                  pl.BlockSpec((tm, ROUTER_LANES), lambda i, d: (i, 0)),
                  pl.BlockSpec((1, D), lambda i, d: (0, 0)),
                  pl.BlockSpec(memory_space=pl.ANY)],
        out_specs=tuple(out_specs),
        scratch_shapes=[pltpu.VMEM((2, 2, tm, D), F32), pltpu.SemaphoreType.DMA((2,))],
    )
    outs = pl.pallas_call(
        functools.partial(_combine_kernel, tm=tm, n_tok=T, keep_h=keep_h),
        out_shape=tuple(out_shape),
        grid_spec=grid_spec,
        compiler_params=_cparams(("arbitrary",)),
        name="moe_combine",
    )(dest_flat, resid, wts, next_norm_w.reshape(1, D), y)
    return (outs[0], outs[1]) if keep_h else (None, outs[0])


def _hier_moe(h, norm_w, w_group, b_group, w_expert, b_expert, w_gate, w_up, w_down, layer,
              next_norm_w, norm_dtype, keep_h):
    T, D = h.shape
    RB = MOE_ROWS
    n_blk = (T * 2) // RB + N_EXPERTS
    n_rows = n_blk * RB
    xn, ids, wts, cnt = _router(h, norm_w, w_group, b_group, w_expert, b_expert)
    counts = cnt[0, N_GROUPS:N_GROUPS + N_EXPERTS].astype(I32)
    plan, n_used, start_padded = _moe_plan(counts, n_blk)
    is_e = ids[None, 0:2] == jnp.arange(N_EXPERTS, dtype=I32)[:, None, None]
    dest = (jnp.sum(jnp.where(is_e, start_padded[:, None, None], 0), axis=0) + ids[2:4]).reshape(-1)
    xs = _dispatch(xn, dest, n_rows)
    y = _expert_mlp(xs, plan, n_used, w_gate, w_up, w_down, layer)
    return _combine(h, wts, y, dest, next_norm_w, norm_dtype, keep_h)


def _compress_kernel(x_ref, w1_ref, w2_ref, pe_ref, o_ref, xf_ref):
    d = B_D
    half = CMP_STRIDE * d
    nch = x_ref.shape[0] // CMP_STRIDE
    xf_ref[...] = x_ref[...].astype(F32)
    w1 = w1_ref[...].astype(BF16)
    a = jnp.zeros((nch, w1.shape[1]), F32)
    b = jnp.zeros((nch, w1.shape[1]), F32)
    for l in range(CMP_STRIDE):
        xl = xf_ref[pl.ds(l, nch, stride=CMP_STRIDE), :].astype(BF16)
        a = a + jnp.dot(xl, w1[l * d:(l + 1) * d], preferred_element_type=F32)
        b = b + jnp.dot(xl, w1[half + l * d:half + (l + 1) * d], preferred_element_type=F32)
    pe = jnp.broadcast_to(pe_ref[...], (8, 2 * half)).astype(BF16)
    pt = jnp.dot(pe, w1, preferred_element_type=F32)[0:1]
    hid = a + pltpu.roll(b, b.shape[0] - 1, 0) + pt
    g = jax.nn.gelu(hid)
    o_ref[...] = jnp.dot(g.astype(BF16), w2_ref[...].astype(BF16),
                         preferred_element_type=F32).astype(o_ref.dtype)


def _compress(projm, w1, w2, pe_flat, B, S):
    G, d = B_GROUPS, B_D
    nch = S // CMP_STRIDE
    width = CMP_STRIDE * d
    hid = w1.shape[-1]
    cb = B_HEADS
    return pl.pallas_call(
        _compress_kernel,
        out_shape=jax.ShapeDtypeStruct((2, B, G, nch, d), BF16),
        grid=(2, B, G),
        in_specs=[pl.BlockSpec((S, d), lambda kv, b, g: (b, cb + kv * G + g)),
                  pl.BlockSpec((None, 2 * width, hid), lambda kv, b, g: (kv, 0, 0)),
                  pl.BlockSpec((None, hid, d), lambda kv, b, g: (kv, 0, 0)),
                  pl.BlockSpec((None, 1, 2 * width), lambda kv, b, g: (kv, 0, 0))],
        out_specs=pl.BlockSpec((None, None, None, nch, d), lambda kv, b, g: (kv, b, g, 0, 0)),
        scratch_shapes=[pltpu.VMEM((S, d), F32)],
        compiler_params=_cparams(("arbitrary", "arbitrary", "arbitrary")),
        name="nsa_compress",
    )(projm, w1, w2, pe_flat)


def _cmp_select_kernel(slopes_ref, q_ref, kc_ref, vc_ref, ov_ref, o_ref, sel_ref, need_ref, *, tq, n_cmp,
                       n_blocks, n_rank):
    g = pl.program_id(1)
    i = pl.program_id(2)
    d = B_D
    nch = kc_ref.shape[0]
    pos = i * tq + lax.broadcasted_iota(I32, (1, tq), 1)
    c = lax.broadcasted_iota(I32, (nch, 1), 0)
    dist_i = pos - (c * CMP_STRIDE + (CMP_LEN - 1))
    valid = (dist_i >= 0) & (c < n_cmp)
    dist = dist_i.astype(F32)
    any_valid = jnp.where(pos >= CMP_LEN - 1, 1.0, 0.0)
    kc = kc_ref[...]
    vc = vc_ref[...]
    psum = jnp.zeros((nch, tq), F32)
    for hh in range(B_HPG):
        s = _nt_dot(kc, q_ref[:, hh * d:(hh + 1) * d])
        s = jnp.where(valid, s - (slopes_ref[g * B_HPG + hh] * LOG2E) * dist, NEG)
        e = jnp.exp2(s - jnp.max(s, axis=0, keepdims=True))
        p = (e / jnp.sum(e, axis=0, keepdims=True)) * any_valid
        oT = lax.dot_general(vc, p.astype(BF16), (((0,), (0,)), ((), ())), preferred_element_type=F32)
        o_ref[:, hh * d:(hh + 1) * d] = oT.T
        psum = psum + p

    p_hi = psum.astype(BF16)
    p_lo = (psum - p_hi.astype(F32)).astype(BF16)
    ov = ov_ref[...]
    imp = jnp.dot(ov, p_hi, preferred_element_type=F32) + jnp.dot(ov, p_lo, preferred_element_type=F32)
    imp = imp[:n_rank]

    blk = lax.broadcasted_iota(I32, (n_rank, 1), 0)
    cur = lax.shift_right_logical(pos, SLC_BLOCK.bit_length() - 1)
    forced = (blk == 0) | (blk == cur) | (blk == cur - 1)
    causal = blk * SLC_BLOCK <= pos
    score = jnp.where(forced, FORCE, jnp.where(causal, imp, -FORCE))
    score = jnp.where(blk < n_blocks, score, -2.0 * FORCE)
    rank = jnp.zeros(score.shape, F32)
    for jp in range(n_blocks):
        row = score[jp:jp + 1, :]
        tie = jnp.where(blk > jp, 1.0, 0.0)
        rank = rank + jnp.where(row > score, 1.0, jnp.where(row == score, tie, 0.0))
    sel_add = jnp.where((rank < float(SLC_TOPN)) & (blk < n_blocks), 0.0, NEG)
    pad = jnp.zeros((128 - n_rank, tq), F32)
    sel_t = jnp.concatenate([jnp.where(blk < n_blocks, sel_add, 0.0), pad], axis=0) if n_rank < 128 else sel_add
    sel_ref[...] = sel_t.T.astype(sel_ref.dtype)
    per_tile = SLC_KEY_TILE // SLC_BLOCK
    blk_any = jnp.max(jnp.where(sel_add == 0.0, 1, 0), axis=1, keepdims=True)
    rows = []
    for j in range(8):
        if (j + 1) * per_tile <= n_rank:
            hit = jnp.max(blk_any[j * per_tile:(j + 1) * per_tile], axis=0, keepdims=True)
            rows.append(jnp.broadcast_to(hit, (1, 128)))
        else:
            rows.append(jnp.zeros((1, 128), I32))
    need_ref[...] = jnp.concatenate(rows, axis=0)


def _cmp_select(projm, cmp_kv, overlap, B, S, tq=256):
    T = B * S
    G, d = B_GROUPS, B_D
    nq = S // tq
    nch = cmp_kv.shape[3]
    n_cmp = (S - CMP_LEN) // CMP_STRIDE + 1
    n_blocks = S // SLC_BLOCK
    assert SLC_TOPN <= n_blocks <= FEAT_BLK
    slopes = _alibi_slopes(B_HEADS)
    grid_spec = pltpu.PrefetchScalarGridSpec(
        num_scalar_prefetch=1,
        grid=(B, G, nq),
        in_specs=[pl.BlockSpec((tq, B_HPG * d), lambda b, g, i, sl: (b * nq + i, g)),
                  pl.BlockSpec((None, None, None, nch, d), lambda b, g, i, sl: (0, b, g, 0, 0)),
                  pl.BlockSpec((None, None, None, nch, d), lambda b, g, i, sl: (1, b, g, 0, 0)),
                  pl.BlockSpec((128, nch), lambda b, g, i, sl: (0, 0))],
        out_specs=(pl.BlockSpec((tq, B_HPG * d), lambda b, g, i, sl: (b * nq + i, g)),
                   pl.BlockSpec((None, None, tq, 128), lambda b, g, i, sl: (b, g, i, 0)),
                   pl.BlockSpec((None, None, None, 8, 128), lambda b, g, i, sl: (b, g, i, 0, 0))),
    )
    assert S // SLC_KEY_TILE <= 8
    return pl.pallas_call(
        functools.partial(_cmp_select_kernel, tq=tq, n_cmp=n_cmp, n_blocks=n_blocks,
                          n_rank=-(-n_blocks // 8) * 8),
        out_shape=(jax.ShapeDtypeStruct((T, B_HEADS * d), F32),
                   jax.ShapeDtypeStruct((B, G, S, 128), BF16),
                   jax.ShapeDtypeStruct((B, G, nq, 8, 128), I32)),
        grid_spec=grid_spec,
        compiler_params=_cparams(("parallel", "parallel", "arbitrary")),
        name="nsa_cmp_select",
    )(slopes, projm, cmp_kv, cmp_kv, overlap)


def _head_units(k_ref, v_ref, feat_ref, qa_ref, j, mask, qc):
    n_heads, tq, _ = qa_ref.shape
    k0 = pl.multiple_of(j * tq, tq)
    v = v_ref[pl.ds(k0, tq), :]

    def score_fn(hh, c):
        def fn():
            ka = jnp.concatenate([k_ref[pl.ds(k0, tq), :], feat_ref[pl.ds(k0, tq), :]], axis=1)
            sT = _nt_dot(ka, qa_ref[hh, c * qc:(c + 1) * qc, :])
            return sT if mask is None else sT + mask[:, c * qc:(c + 1) * qc]
        return fn

    return [(score_fn(hh, c), v, hh * (tq // qc) + c) for hh in range(n_heads) for c in range(tq // qc)]


def _write_heads_T(o_ref, acc_ref, l_ref):
    tq = o_ref.shape[0]
    qc = l_ref.shape[-1]
    for hh in range(B_HPG):
        for c in range(tq // qc):
            u = hh * (tq // qc) + c
            o_ref[c * qc:(c + 1) * qc, hh * B_D:(hh + 1) * B_D] = (acc_ref[u] / l_ref[u]).T


def _flash_state(n_heads, tq, dv):
    qc = min(ATTN_QC, tq)
    n_units = n_heads * (tq // qc)
    return [pltpu.VMEM((n_units, 1, qc), F32), pltpu.VMEM((n_units, 1, qc), F32),
            pltpu.VMEM((n_units, dv, qc), F32)]


def _stage_queries(q_ref, qa_ref, slopes_ref, g, q0, sel_add):
    tq = q_ref.shape[0]
    lane = lax.broadcasted_iota(I32, (tq, 128), 1)
    for hh in range(B_HPG):
        coef = jnp.broadcast_to(_alibi_query_features(slopes_ref[g * B_HPG + hh] * LOG2E, q0), (tq, 128))
        if sel_add is not None:
            coef = jnp.where(lane < FEAT_BLK, sel_add, coef)
        qa_ref[hh] = jnp.concatenate([q_ref[:, hh * B_D:(hh + 1) * B_D], coef.astype(BF16)], axis=1)


def _slc_attn_kernel(slopes_ref, need_ref, q_ref, k_ref, v_ref, feat_ref, sel_ref, o_ref, qa_ref, m_ref, l_ref,
                     acc_ref, *, tq):
    b = pl.program_id(0)
    g = pl.program_id(1)
    i = pl.program_id(2)
    n = pl.num_programs(2)
    _flash_init(m_ref, l_ref, acc_ref)
    _stage_queries(q_ref, qa_ref, slopes_ref, g, i * tq, sel_ref[...].astype(F32))

    qc = m_ref.shape[-1]

    def units(j, causal=None):
        return _head_units(k_ref, v_ref, feat_ref, qa_ref, j, causal, qc)

    need_base = ((b * pl.num_programs(1) + g) * n + i) * n

    def needed(j):
        return (j == 0) | (need_ref[need_base + jnp.maximum(j, 0)] > 0)

    def body(p, carry):
        j0, j1 = 2 * p, 2 * p + 1
        n0, n1 = needed(j0), needed(j1)

        @pl.when(n0 & n1)
        def _():
            _flash_units(units(j0) + units(j1), m_ref, l_ref, acc_ref)

        @pl.when(n0 & jnp.logical_not(n1))
        def _():
            _flash_units(units(j0), m_ref, l_ref, acc_ref)

        @pl.when(jnp.logical_not(n0) & n1)
        def _():
            _flash_units(units(j1), m_ref, l_ref, acc_ref)

        return carry

    lax.fori_loop(0, lax.shift_right_logical(i, 1), body, 0)

    @pl.when((i % 2 == 1) & needed(i - 1))
    def _():
        _flash_units(units(i - 1), m_ref, l_ref, acc_ref)

    _flash_units(units(i, _tri_mask(tq, tq, True)), m_ref, l_ref, acc_ref)
    _write_heads_T(o_ref, acc_ref, l_ref)


def _slc_attention(projm, sel, need, B, S):
    T = B * S
    G, d = B_GROUPS, B_D
    tq = SLC_KEY_TILE
    n = S // tq
    slopes = _alibi_slopes(B_HEADS)
    ksb = (B_HEADS * d + 2 * G * d) // d
    vsb = ksb + G
    grid_spec = pltpu.PrefetchScalarGridSpec(
        num_scalar_prefetch=2,
        grid=(B, G, n),
        in_specs=[
            pl.BlockSpec((tq, B_HPG * d), lambda b, g, i, sl, nd: (b * n + i, g)),
            pl.BlockSpec((S, d), lambda b, g, i, sl, nd: (b, ksb + g)),
            pl.BlockSpec((S, d), lambda b, g, i, sl, nd: (b, vsb + g)),
            pl.BlockSpec((S, 128), lambda b, g, i, sl, nd: (0, 0)),
            pl.BlockSpec((None, None, tq, 128), lambda b, g, i, sl, nd: (b, g, i, 0)),
        ],
        out_specs=pl.BlockSpec((tq, B_HPG * d), lambda b, g, i, sl, nd: (b * n + i, g)),
        scratch_shapes=[pltpu.VMEM((B_HPG, tq, 2 * d), BF16)] + _flash_state(B_HPG, tq, d),
    )
    return pl.pallas_call(
        functools.partial(_slc_attn_kernel, tq=tq),
        out_shape=jax.ShapeDtypeStruct((T, B_HEADS * d), F32),
        grid_spec=grid_spec,
        compiler_params=_cparams(("parallel", "parallel", "arbitrary")),
        name="nsa_selected_attention",
    )(slopes, need.reshape(-1), projm, projm, projm, _key_features(S), sel)


def _win_attn_kernel(slopes_ref, q_ref, k_ref, v_ref, feat_ref, o_ref, qa_ref, m_ref, l_ref, acc_ref, *, tq):
    g = pl.program_id(1)
    i = pl.program_id(2)
    _flash_init(m_ref, l_ref, acc_ref)
    _stage_queries(q_ref, qa_ref, slopes_ref, g, i * tq, None)

    qc = m_ref.shape[-1]

    def units(j, mask):
        return _head_units(k_ref, v_ref, feat_ref, qa_ref, j, mask, qc)

    @pl.when(i > 0)
    def _():
        _flash_units(units(i - 1, _tri_mask(tq, tq, False)) + units(i, _tri_mask(tq, tq, True)),
                     m_ref, l_ref, acc_ref)

    @pl.when(i == 0)
    def _():
        _flash_units(units(i, _tri_mask(tq, tq, True)), m_ref, l_ref, acc_ref)

    _write_heads_T(o_ref, acc_ref, l_ref)


def _win_attention(projm, B, S):
    T = B * S
    G, d = B_GROUPS, B_D
    tq = WINDOW
    n = S // tq
    slopes = _alibi_slopes(B_HEADS)
    kwb = (B_HEADS * d + 4 * G * d) // d
    vwb = kwb + G
    grid_spec = pltpu.PrefetchScalarGridSpec(
        num_scalar_prefetch=1,
        grid=(B, G, n),
        in_specs=[
            pl.BlockSpec((tq, B_HPG * d), lambda b, g, i, sl: (b * n + i, g)),
            pl.BlockSpec((S, d), lambda b, g, i, sl: (b, kwb + g)),
            pl.BlockSpec((S, d), lambda b, g, i, sl: (b, vwb + g)),
            pl.BlockSpec((S, 128), lambda b, g, i, sl: (0, 0)),
        ],
        out_specs=pl.BlockSpec((tq, B_HPG * d), lambda b, g, i, sl: (b * n + i, g)),
        scratch_shapes=[pltpu.VMEM((B_HPG, tq, 2 * d), BF16)] + _flash_state(B_HPG, tq, d),
    )
    return pl.pallas_call(
        functools.partial(_win_attn_kernel, tq=tq),
        out_shape=jax.ShapeDtypeStruct((T, B_HEADS * d), F32),
        grid_spec=grid_spec,
        compiler_params=_cparams(("parallel", "parallel", "arbitrary")),
        name="nsa_window_attention",
    )(slopes, projm, projm, projm, _key_features(S))


def _gate_mix_kernel(gl_ref, oc_ref, os_ref, ow_ref, o_ref):
    sig = jax.nn.sigmoid(gl_ref[...])
    d = B_D
    for hd in range(B_HEADS):
        sl = slice(hd * d, (hd + 1) * d)
        o = (sig[:, 3 * hd:3 * hd + 1] * oc_ref[:, sl] + sig[:, 3 * hd + 1:3 * hd + 2] * os_ref[:, sl]
             + sig[:, 3 * hd + 2:3 * hd + 3] * ow_ref[:, sl])
        o_ref[:, sl] = o.astype(o_ref.dtype)


def _gate_mix(gate_logits, o_cmp, o_slc, o_win, tm=256):
    T, W = o_cmp.shape
    spec = pl.BlockSpec((tm, W), lambda i: (i, 0))
    return pl.pallas_call(
        _gate_mix_kernel,
        out_shape=jax.ShapeDtypeStruct((T, W), BF16),
        grid=(T // tm,),
        in_specs=[pl.BlockSpec((tm, 128), lambda i: (i, 0)), spec, spec, spec],
        out_specs=spec,
        compiler_params=_cparams(("parallel",)),
        name="nsa_gate_mix",
    )(gate_logits, o_cmp, o_slc, o_win)


def _overlap_matrix(S, n_rows):
    nc = (S - CMP_LEN) // CMP_STRIDE + 1
    nsb = S // SLC_BLOCK
    cs = np.arange(nc) * CMP_STRIDE
    ss = np.arange(nsb) * SLC_BLOCK
    ov = np.clip(np.minimum(cs[:, None] + CMP_LEN, ss[None, :] + SLC_BLOCK)
                 - np.maximum(cs[:, None], ss[None, :]), 0, None) / CMP_LEN
    out = np.zeros((128, n_rows), np.float32)
    out[:nsb, :nc] = ov.T
    return jnp.asarray(out, BF16)


def _nsa_attention(xn, w_in, cmp_pos, cmp_w1, cmp_w2, B, S):
    T = B * S
    H, G, d = B_HEADS, B_GROUPS, B_D
    n_main = H * d + 6 * G * d
    col_scale = jnp.concatenate([jnp.full((H * d,), d ** -0.5 * LOG2E, F32), jnp.ones((n_main - H * d,), F32)])
    projm = _matmul(xn, w_in, n_main, BF16, col_scale=col_scale)
    w_gate = jnp.pad(w_in[:, n_main:], ((0, 0), (0, 128 - 3 * H)))
    gate_logits = _matmul(xn, w_gate, 128, F32, tn=128)

    cmp_kv = _compress(projm, cmp_w1, cmp_w2, cmp_pos.reshape(2, 1, CMP_LEN * d), B, S)
    o_cmp, sel, need = _cmp_select(projm, cmp_kv, _overlap_matrix(S, S // CMP_STRIDE), B, S)
    n_t = S // SLC_KEY_TILE
    need = need[:, :, :, :n_t, 0].reshape(B, G, n_t, -1, n_t).max(axis=3)
    o_slc = _slc_attention(projm, sel, need, B, S)
    o_win = _win_attention(projm, B, S)
    return _gate_mix(gate_logits, o_cmp, o_slc, o_win)


def kernel(x, attn_norm_w, ffn_norm_w, final_norm_w, a_w_in, a_lambda, a_subln_w, a_w_out, b_w_in,
           b_cmp_pos, b_cmp_w1, b_cmp_w2, b_w_out, moe_w_group, moe_b_group, moe_w_expert,
           moe_b_expert, moe_w_gate, moe_w_up, moe_w_down):
    B, S, D = x.shape
    T = B * S
    depth = attn_norm_w.shape[0]
    h = x.reshape(T, D)
    xn = _rmsnorm(h, attn_norm_w[0], BF16)
    for i in range(depth):
        j = i // 2
        if i % 2 == 0:
            lambda_init = 0.8 - 0.6 * math.exp(-0.3 * i)
            n_q = A_HEADS * 2 * A_DH
            n_in = 2 * n_q + A_HEADS * A_DV
            col_scale = jnp.concatenate([jnp.full((n_q,), A_DH ** -0.5 * LOG2E, F32),
                                         jnp.ones((n_in - n_q,), F32)])
            qkv = _matmul(xn, a_w_in[j], n_in, BF16, col_scale=col_scale)
            o = _diff_attention(qkv, a_lambda[j], a_subln_w[j], lambda_init, B, S)
            h = _matmul(o, a_w_out[j], D, F32, resid=h)
        else:
            o = _nsa_attention(xn, b_w_in[j], b_cmp_pos[j], b_cmp_w1[j], b_cmp_w2[j], B, S)
            h = _matmul(o, b_w_out[j], D, F32, resid=h)
        last = i == depth - 1
        h, xn = _hier_moe(h, ffn_norm_w[i], moe_w_group[i], moe_b_group[i], moe_w_expert[i], moe_b_expert[i],
                          moe_w_gate, moe_w_up, moe_w_down, i,
                          final_norm_w if last else attn_norm_w[i + 1], F32 if last else BF16, not last)
    return xn.reshape(B, S, D)
```

```python
import functools
import math

import numpy as np
import jax
import jax.numpy as jnp
from jax import lax
from jax.experimental import pallas as pl
from jax.experimental.pallas import tpu as pltpu

F32 = jnp.float32
BF16 = jnp.bfloat16
I32 = jnp.int32

EPS = 1e-6
NEG = -1e30
FORCE = 1e4
LOG2E = math.log2(math.e)

A_HEADS = 8
A_DH = 128
A_DV = 256
B_HEADS = 16
B_GROUPS = 4
B_HPG = 4
B_D = 128
CMP_LEN = 32
CMP_STRIDE = 16
SLC_BLOCK = 64
SLC_TOPN = 16
WINDOW = 512
N_GROUPS = 4
EPG = 8
N_EXPERTS = 32
D_EXPERT = 1024
ROUTER_LANES = 128
MOE_ROWS = 256
DMA_UNROLL = 8
ATTN_QC = 512
SLC_KEY_TILE = 512

VMEM_LIMIT = 56 * 1024 * 1024


def _cparams(sem, flags=None):
    return pltpu.CompilerParams(dimension_semantics=sem, vmem_limit_bytes=VMEM_LIMIT, flags=flags)


def _alibi_slopes(n):
    return jnp.asarray(np.array([2.0 ** (-8.0 * (i + 1) / n) for i in range(n)], dtype=np.float32))


def _nt_dot(a, b):
    return lax.dot_general(a, b, (((1,), (1,)), ((), ())), preferred_element_type=F32)


def _rms_kernel(x_ref, w_ref, o_ref):
    x = x_ref[...]
    y = x * lax.rsqrt(jnp.mean(x * x, axis=-1, keepdims=True) + EPS)
    o_ref[...] = (y * w_ref[...]).astype(o_ref.dtype)


def _rmsnorm(x, w, out_dtype, tm=512):
    T, D = x.shape
    return pl.pallas_call(
        _rms_kernel,
        out_shape=jax.ShapeDtypeStruct((T, D), out_dtype),
        grid=(T // tm,),
        in_specs=[pl.BlockSpec((tm, D), lambda i: (i, 0)),
                  pl.BlockSpec((1, D), lambda i: (0, 0))],
        out_specs=pl.BlockSpec((tm, D), lambda i: (i, 0)),
        compiler_params=_cparams(("parallel",)),
        name="rmsnorm",
    )(x, w.reshape(1, D))


def _mm_kernel(*refs, has_scale, has_resid):
    a_ref, w_ref = refs[0], refs[1]
    k = 2
    scale_ref = resid_ref = None
    if has_scale:
        scale_ref = refs[k]
        k += 1
    if has_resid:
        resid_ref = refs[k]
        k += 1
    o_ref, wbf_ref = refs[k], refs[k + 1]

    @pl.when(pl.program_id(1) == 0)
    def _():
        wbf_ref[...] = w_ref[...].astype(BF16)

    acc = jnp.dot(a_ref[...], wbf_ref[...], preferred_element_type=F32)
    if has_scale:
        acc = acc * scale_ref[...]
    if has_resid:
        acc = acc + resid_ref[...]
    o_ref[...] = acc.astype(o_ref.dtype)


def _matmul(a, w, n_out, out_dtype, *, col_scale=None, resid=None, tm=1024, tn=512):
    M, K = a.shape
    assert M % tm == 0 and n_out % tn == 0 and w.shape[0] == K and w.shape[1] >= n_out
    in_specs = [pl.BlockSpec((tm, K), lambda j, i: (i, 0)),
                pl.BlockSpec((K, tn), lambda j, i: (0, j))]
    args = [a, w]
    if col_scale is not None:
        in_specs.append(pl.BlockSpec((1, tn), lambda j, i: (0, j)))
        args.append(col_scale.reshape(1, n_out))
    if resid is not None:
        in_specs.append(pl.BlockSpec((tm, tn), lambda j, i: (i, j)))
        args.append(resid)
    return pl.pallas_call(
        functools.partial(_mm_kernel, has_scale=col_scale is not None, has_resid=resid is not None),
        out_shape=jax.ShapeDtypeStruct((M, n_out), out_dtype),
        grid=(n_out // tn, M // tm),
        in_specs=in_specs,
        out_specs=pl.BlockSpec((tm, tn), lambda j, i: (i, j)),
        scratch_shapes=[pltpu.VMEM((K, tn), BF16)],
        compiler_params=_cparams(("parallel", "arbitrary")),
        name="matmul",
    )(*args)


FEAT_BLK = 64
FEAT_POS = 67
FEAT_ONE = 70


def _key_features(S):
    assert S // SLC_BLOCK <= FEAT_BLK
    j = np.arange(S)
    f = np.zeros((S, 128), np.float32)
    f[j, j // SLC_BLOCK] = 1.0
    f[:, FEAT_BLK:FEAT_BLK + 3] = (j // SLC_BLOCK)[:, None]
    f[:, FEAT_POS:FEAT_POS + 3] = (j % SLC_BLOCK)[:, None]
    f[:, FEAT_ONE:FEAT_ONE + 3] = 1.0
    return jnp.asarray(f, BF16)


def _alibi_query_features(slope2, q0):
    lane = lax.broadcasted_iota(I32, (1, 128), 1)
    base = jnp.where((lane >= FEAT_BLK) & (lane < FEAT_BLK + 3), slope2 * float(SLC_BLOCK),
                     jnp.where((lane >= FEAT_POS) & (lane < FEAT_POS + 3), slope2,
                               jnp.where((lane >= FEAT_ONE) & (lane < FEAT_ONE + 3),
                                         -slope2 * q0.astype(F32), 0.0)))
    hi = base.astype(BF16).astype(F32)
    r1 = base - hi
    lo = r1.astype(BF16).astype(F32)
    lo2 = (r1 - lo).astype(BF16).astype(F32)
    first = (lane == FEAT_BLK) | (lane == FEAT_POS) | (lane == FEAT_ONE)
    second = (lane == FEAT_BLK + 1) | (lane == FEAT_POS + 1) | (lane == FEAT_ONE + 1)
    return jnp.where(first, hi, jnp.where(second, lo, lo2))


def _flash_init(m_ref, l_ref, acc_ref):
    m_ref[...] = jnp.full(m_ref.shape, NEG, F32)
    l_ref[...] = jnp.zeros(l_ref.shape, F32)
    acc_ref[...] = jnp.zeros(acc_ref.shape, F32)


def _flash_probs(sT, m_ref, l_ref, u):
    m_old = m_ref[u]
    m_new = jnp.maximum(m_old, jnp.max(sT, axis=0, keepdims=True))
    alpha = jnp.exp2(m_old - m_new)
    p = jnp.exp2(sT - m_new)
    l_ref[u] = alpha * l_ref[u] + jnp.sum(p, axis=0, keepdims=True)
    m_ref[u] = m_new
    return p.astype(BF16), alpha


def _flash_accumulate(p, alpha, v, acc_ref, u):
    pv = lax.dot_general(v, p, (((0,), (0,)), ((), ())), preferred_element_type=F32)
    acc_ref[u] = alpha * acc_ref[u] + pv


def _flash_units(units, m_ref, l_ref, acc_ref):
    n = len(units)
    scores, probs = {}, {}
    for s in range(n + 2):
        if s < n:
            scores[s] = units[s][0]()
        if 1 <= s <= n:
            probs[s - 1] = _flash_probs(scores.pop(s - 1), m_ref, l_ref, units[s - 1][2])
        if 2 <= s <= n + 1:
            p, alpha = probs.pop(s - 2)
            _flash_accumulate(p, alpha, units[s - 2][1], acc_ref, units[s - 2][2])


def _tri_mask(tk, tq, keep_upper):
    r = lax.broadcasted_iota(I32, (tk, tq), 0)
    c = lax.broadcasted_iota(I32, (tk, tq), 1)
    keep = (r <= c) if keep_upper else (r > c)
    return jnp.where(keep, 0.0, NEG)


def _diff_attn_kernel(slopes_ref, q1_ref, q2_ref, k1_ref, k2_ref, v_ref, feat_ref, lam_ref, sw_ref, o_ref,
                      qa_ref, m_ref, l_ref, acc_ref, *, tq, lambda_init):
    h = pl.program_id(1)
    i = pl.program_id(2)
    _flash_init(m_ref, l_ref, acc_ref)
    qfeat = jnp.broadcast_to(_alibi_query_features(slopes_ref[h] * LOG2E, i * tq), (tq, 128)).astype(BF16)
    for c, q_ref in enumerate((q1_ref, q2_ref)):
        qa_ref[c] = jnp.concatenate([q_ref[...], qfeat], axis=1)

    def tile_units(j, mask):
        k0 = pl.multiple_of(j * tq, tq)
        v = v_ref[pl.ds(k0, tq), :]
        feat = feat_ref[pl.ds(k0, tq), :]
        qc = m_ref.shape[-1]

        def score_fn(c, k_ref, x):
            def fn():
                ka = jnp.concatenate([k_ref[pl.ds(k0, tq), :], feat], axis=1)
                sT = _nt_dot(ka, qa_ref[c, x * qc:(x + 1) * qc, :])
                return sT if mask is None else sT + mask[:, x * qc:(x + 1) * qc]
            return fn

        return [(score_fn(c, k_ref, x), v, c * (tq // qc) + x)
                for c, k_ref in enumerate((k1_ref, k2_ref)) for x in range(tq // qc)]

    def body(p, carry):
        _flash_units(tile_units(2 * p, None) + tile_units(2 * p + 1, None), m_ref, l_ref, acc_ref)
        return carry

    lax.fori_loop(0, lax.shift_right_logical(i, 1), body, 0)

    @pl.when(i % 2 == 1)
    def _():
        _flash_units(tile_units(i - 1, None), m_ref, l_ref, acc_ref)

    _flash_units(tile_units(i, _tri_mask(tq, tq, True)), m_ref, l_ref, acc_ref)

    lam = lam_ref[...]
    lmbda = (jnp.exp(jnp.sum(lam[0:1] * lam[1:2], axis=-1, keepdims=True))
             - jnp.exp(jnp.sum(lam[2:3] * lam[3:4], axis=-1, keepdims=True)) + lambda_init)
    qc = m_ref.shape[-1]
    nx = tq // qc
    for x in range(nx):
        oT = acc_ref[x] / l_ref[x] - lmbda * (acc_ref[nx + x] / l_ref[nx + x])
        o = oT.T
        y = o * lax.rsqrt(jnp.mean(o * o, axis=-1, keepdims=True) + EPS)
        y = (y * sw_ref[...]) * (1.0 - lambda_init)
        o_ref[x * qc:(x + 1) * qc, :] = y.astype(o_ref.dtype)


def _diff_attention(qkv, lam, subln_w, lambda_init, B, S, tq=512):
    T = B * S
    H, dh, dv = A_HEADS, A_DH, A_DV
    n = S // tq
    slopes = _alibi_slopes(H)
    kb = H * 2
    vb = (2 * H * 2 * dh) // dv
    grid_spec = pltpu.PrefetchScalarGridSpec(
        num_scalar_prefetch=1,
        grid=(B, H, n),
        in_specs=[
            pl.BlockSpec((tq, dh), lambda b, h, i, sl: (b * n + i, 2 * h)),
            pl.BlockSpec((tq, dh), lambda b, h, i, sl: (b * n + i, 2 * h + 1)),
            pl.BlockSpec((S, dh), lambda b, h, i, sl: (b, kb + 2 * h)),
            pl.BlockSpec((S, dh), lambda b, h, i, sl: (b, kb + 2 * h + 1)),
            pl.BlockSpec((S, dv), lambda b, h, i, sl: (b, vb + h)),
            pl.BlockSpec((S, 128), lambda b, h, i, sl: (0, 0)),
            pl.BlockSpec((4, dh), lambda b, h, i, sl: (0, 0)),
            pl.BlockSpec((1, dv), lambda b, h, i, sl: (0, 0)),
        ],
        out_specs=pl.BlockSpec((tq, dv), lambda b, h, i, sl: (b * n + i, h)),
        scratch_shapes=[pltpu.VMEM((2, tq, 2 * dh), BF16)] + _flash_state(2, tq, dv),
    )
    return pl.pallas_call(
        functools.partial(_diff_attn_kernel, tq=tq, lambda_init=lambda_init),
        out_shape=jax.ShapeDtypeStruct((T, H * dv), BF16),
        grid_spec=grid_spec,
        compiler_params=_cparams(("parallel", "parallel", "arbitrary")),
        name="diff_attention",
    )(slopes, qkv, qkv, qkv, qkv, qkv, _key_features(S), lam, subln_w.reshape(1, dv))


def _router_kernel(h_ref, nw_ref, wr_ref, br_ref, tri_ref, xn_ref, ids_ref, wts_ref, cnt_ref, base_ref):
    @pl.when(pl.program_id(0) == 0)
    def _():
        base_ref[...] = jnp.zeros(base_ref.shape, F32)

    x = h_ref[...]
    xn = (x * lax.rsqrt(jnp.mean(x * x, axis=-1, keepdims=True) + EPS)) * nw_ref[...]
    xn_ref[...] = xn
    w = wr_ref[...]
    x_hi = xn.astype(BF16)
    x_lo = (xn - x_hi.astype(F32)).astype(BF16)
    w_hi = w.astype(BF16)
    w_lo = (w - w_hi.astype(F32)).astype(BF16)
    logits = (jnp.dot(x_hi, w_hi, preferred_element_type=F32)
              + (jnp.dot(x_hi, w_lo, preferred_element_type=F32)
                 + jnp.dot(x_lo, w_hi, preferred_element_type=F32))) + br_ref[...]
    lane = lax.broadcasted_iota(I32, logits.shape, 1)
    big = jnp.int32(1 << 20)
    is_g = lane < N_GROUPS
    is_e = (lane >= N_GROUPS) & (lane < N_GROUPS + N_EXPERTS)

    lg = jnp.where(is_g, logits, NEG)
    mg = jnp.max(lg, axis=-1, keepdims=True)
    g_top = jnp.min(jnp.where(lg == mg, lane, big), axis=-1, keepdims=True)
    wg = 1.0 / jnp.sum(jnp.where(is_g, jnp.exp(lg - mg), 0.0), axis=-1, keepdims=True)

    in_grp = is_e & (lax.shift_right_logical(lane - N_GROUPS, 3) == g_top)
    le = jnp.where(in_grp, logits, NEG)
    me = jnp.max(le, axis=-1, keepdims=True)
    ee = jnp.where(in_grp, jnp.exp(le - me), 0.0)
    pe = ee / jnp.sum(ee, axis=-1, keepdims=True)
    pe = jnp.where(in_grp, pe, -1.0)
    p1 = jnp.max(pe, axis=-1, keepdims=True)
    i1 = jnp.min(jnp.where(pe == p1, lane, big), axis=-1, keepdims=True)
    pe2 = jnp.where(lane == i1, -1.0, pe)
    p2 = jnp.max(pe2, axis=-1, keepdims=True)
    i2 = jnp.min(jnp.where(pe2 == p2, lane, big), axis=-1, keepdims=True)
    den = p1 + p2
    w0 = wg * p1 / den
    w1 = wg * p2 / den

    oh1 = lane == i1
    oh2 = lane == i2
    both = jnp.where(oh1 | oh2, 1.0, 0.0)
    before = jnp.dot(tri_ref[...], both.astype(BF16), preferred_element_type=F32) + base_ref[...]
    pos0 = jnp.sum(jnp.where(oh1, before, 0.0), axis=-1, keepdims=True).astype(I32)
    pos1 = jnp.sum(jnp.where(oh2, before, 0.0), axis=-1, keepdims=True).astype(I32)
    total = base_ref[...] + jnp.sum(both, axis=0, keepdims=True)
    base_ref[...] = total
    cnt_ref[...] = total

    ids = jnp.where(lane == 0, i1 - N_GROUPS,
                    jnp.where(lane == 1, i2 - N_GROUPS,
                              jnp.where(lane == 2, pos0, jnp.where(lane == 3, pos1, 0))))
    ids_ref[...] = ids.T[:8]
    wts_ref[...] = jnp.where(lane == 0, w0, jnp.where(lane == 1, w1, 0.0))


def _router(h, norm_w, w_group, b_group, w_expert, b_expert, tm=512):
    T, D = h.shape
    pad = ROUTER_LANES - N_GROUPS - N_EXPERTS
    wr = jnp.concatenate([w_group, w_expert, jnp.zeros((D, pad), F32)], axis=1)
    br = jnp.concatenate([b_group, b_expert, jnp.zeros((pad,), F32)]).reshape(1, ROUTER_LANES)
    tri = jnp.asarray(np.tril(np.ones((tm, tm), np.float32), -1), BF16)
    return pl.pallas_call(
        _router_kernel,
        out_shape=(jax.ShapeDtypeStruct((T, D), F32),
                   jax.ShapeDtypeStruct((8, T), I32),
                   jax.ShapeDtypeStruct((T, ROUTER_LANES), F32),
                   jax.ShapeDtypeStruct((1, ROUTER_LANES), F32)),
        grid=(T // tm,),
        in_specs=[pl.BlockSpec((tm, D), lambda i: (i, 0)),
                  pl.BlockSpec((1, D), lambda i: (0, 0)),
                  pl.BlockSpec((D, ROUTER_LANES), lambda i: (0, 0)),
                  pl.BlockSpec((1, ROUTER_LANES), lambda i: (0, 0)),
                  pl.BlockSpec((tm, tm), lambda i: (0, 0))],
        out_specs=(pl.BlockSpec((tm, D), lambda i: (i, 0)),
                   pl.BlockSpec((8, tm), lambda i: (0, i)),
                   pl.BlockSpec((tm, ROUTER_LANES), lambda i: (i, 0)),
                   pl.BlockSpec((1, ROUTER_LANES), lambda i: (0, 0))),
        scratch_shapes=[pltpu.VMEM((1, ROUTER_LANES), F32)],
        compiler_params=_cparams(("arbitrary",)),
        name="moe_router",
    )(h, norm_w.reshape(1, D), wr, br, tri)


def _dispatch_kernel(dest_ref, zero_ref, xn_ref, xs_ref, zbuf_ref, sem, zsem, *, tm, n_tok):
    @pl.when(pl.program_id(0) == 0)
    def _():
        zbuf_ref[...] = jnp.zeros(zbuf_ref.shape, zbuf_ref.dtype)
        rb = zbuf_ref.shape[0]

        def fill(r):
            return pltpu.make_async_copy(zbuf_ref, xs_ref.at[pl.ds(r * rb, rb)], zsem)

        for r in range(zero_ref.shape[0]):
            @pl.when(zero_ref[r] == 1)
            def _():
                fill(r).start()
        for r in range(zero_ref.shape[0]):
            @pl.when(zero_ref[r] == 1)
            def _():
                fill(r).wait()

    base = pl.program_id(0) * tm
    for t in range(tm):
        for slot in range(2):
            d = dest_ref[slot * n_tok + base + t]
            pltpu.make_async_copy(xn_ref.at[pl.ds(t, 1)], xs_ref.at[pl.ds(d, 1)], sem).start()
    for slot in range(2):
        pltpu.make_async_copy(xn_ref, xs_ref.at[pl.ds(0, tm)], sem).wait()


def _dispatch(xn, dest_flat, zero_blk, n_rows, tm=256):
    T, D = xn.shape
    grid_spec = pltpu.PrefetchScalarGridSpec(
        num_scalar_prefetch=2,
        grid=(T // tm,),
        in_specs=[pl.BlockSpec((tm, D), lambda i, d, z: (i, 0))],
        out_specs=pl.BlockSpec(memory_space=pl.ANY),
        scratch_shapes=[pltpu.VMEM((MOE_ROWS, D), F32), pltpu.SemaphoreType.DMA, pltpu.SemaphoreType.DMA],
    )
    return pl.pallas_call(
        functools.partial(_dispatch_kernel, tm=tm, n_tok=T),
        out_shape=jax.ShapeDtypeStruct((n_rows, D), F32),
        grid_spec=grid_spec,
        compiler_params=_cparams(("arbitrary",)),
        name="moe_dispatch",
    )(dest_flat, zero_blk, xn)


def _moe_plan(counts, n_blk):
    RB = MOE_ROWS
    nb_e = (counts + RB - 1) // RB
    end_b = jnp.cumsum(nb_e)
    start_b = end_b - nb_e
    n_used = end_b[-1:].astype(I32)
    r = jnp.arange(n_blk, dtype=I32)
    be = jnp.minimum(jnp.sum((end_b[None, :] <= r[:, None]).astype(I32), axis=1), N_EXPERTS - 1)
    active = nb_e > 0
    ordinal = jnp.cumsum(active.astype(I32)) - 1
    idx = jnp.arange(N_EXPERTS, dtype=I32)
    later = jnp.where(active[None, :] & (idx[None, :] > idx[:, None]), idx[None, :], N_EXPERTS)
    nxt_e = jnp.min(later, axis=1)
    nxt_e = jnp.where(nxt_e == N_EXPERTS, -1, nxt_e)
    first = ((r == start_b[be]) & (r < n_used[0])).astype(I32)
    plan = jnp.stack([be, first, ordinal[be] % 2, nxt_e[be]]).astype(I32)
    zero_blk = ((r == end_b[be] - 1) | (r >= n_used[0])).astype(I32)
    return plan, n_used, (start_b * RB).astype(I32), zero_blk


def _stream_expert_weights(plan_ref, r, w_hbms, wbuf, sem, layer):
    e, slot, nxt = plan_ref[0, r], plan_ref[2, r], plan_ref[3, r]

    def copies(expert, s):
        return [pltpu.make_async_copy(w.at[layer, expert], wbuf.at[s, k], sem.at[s, k])
                for k, w in enumerate(w_hbms)]

    @pl.when(plan_ref[1, r] == 1)
    def _():
        @pl.when(r == 0)
        def _():
            for c in copies(e, slot):
                c.start()

        for c in copies(e, slot):
            c.wait()

        @pl.when(nxt >= 0)
        def _():
            for c in copies(nxt, 1 - slot):
                c.start()

    return slot


def _gate_up_kernel(plan_ref, nu_ref, x_ref, wg_hbm, wu_hbm, h_ref, wbuf, sem, *, layer):
    r = pl.program_id(0)

    @pl.when(r < nu_ref[0])
    def _():
        slot = _stream_expert_weights(plan_ref, r, (wg_hbm, wu_hbm), wbuf, sem, layer)
        x = x_ref[...]
        g = jnp.dot(x, wbuf[slot, 0], preferred_element_type=F32)
        u = jnp.dot(x, wbuf[slot, 1], preferred_element_type=F32)
        h_ref[...] = (jax.nn.silu(g) * u).astype(h_ref.dtype)

    @pl.when(r >= nu_ref[0])
    def _():
        h_ref[...] = jnp.zeros(h_ref.shape, h_ref.dtype)


def _down_kernel(plan_ref, nu_ref, h_ref, wd_hbm, y_ref, wbuf, sem, *, layer):
    r = pl.program_id(0)

    @pl.when(r < nu_ref[0])
    def _():
        slot = _stream_expert_weights(plan_ref, r, (wd_hbm,), wbuf, sem, layer)
        y_ref[...] = jnp.dot(h_ref[...].astype(F32), wbuf[slot, 0], preferred_element_type=F32)

    @pl.when(r >= nu_ref[0])
    def _():
        y_ref[...] = jnp.zeros(y_ref.shape, y_ref.dtype)


def _expert_mlp(xs, plan, n_used, w_gate, w_up, w_down, layer):
    n_rows, D = xs.shape
    RB = MOE_ROWS
    n_blk = n_rows // RB
    F = D_EXPERT

    def used_blk(r, plan, nu):
        return (jnp.minimum(r, nu[0] - 1), 0)

    h = pl.pallas_call(
        functools.partial(_gate_up_kernel, layer=layer),
        out_shape=jax.ShapeDtypeStruct((n_rows, F), BF16),
        grid_spec=pltpu.PrefetchScalarGridSpec(
            num_scalar_prefetch=2,
            grid=(n_blk,),
            in_specs=[pl.BlockSpec((RB, D), used_blk),
                      pl.BlockSpec(memory_space=pl.ANY),
                      pl.BlockSpec(memory_space=pl.ANY)],
            out_specs=pl.BlockSpec((RB, F), lambda r, plan, nu: (r, 0)),
            scratch_shapes=[pltpu.VMEM((2, 2, D, F), F32), pltpu.SemaphoreType.DMA((2, 2))],
        ),
        compiler_params=_cparams(("arbitrary",)),
        name="moe_gate_up",
    )(plan, n_used, xs, w_gate, w_up)
    y = pl.pallas_call(
        functools.partial(_down_kernel, layer=layer),
        out_shape=jax.ShapeDtypeStruct((n_rows, D), F32),
        grid_spec=pltpu.PrefetchScalarGridSpec(
            num_scalar_prefetch=2,
            grid=(n_blk,),
            in_specs=[pl.BlockSpec((RB, F), used_blk),
                      pl.BlockSpec(memory_space=pl.ANY)],
            out_specs=pl.BlockSpec((RB, D), lambda r, plan, nu: (r, 0)),
            scratch_shapes=[pltpu.VMEM((2, 1, F, D), F32), pltpu.SemaphoreType.DMA((2, 1))],
        ),
        compiler_params=_cparams(("arbitrary",)),
        name="moe_down",
    )(plan, n_used, h, w_down)
    return y


def _combine_kernel(dest_ref, resid_ref, wts_ref, nw_ref, y_ref, *rest, tm, n_tok, keep_h):
    if keep_h:
        o_ref, n_ref, buf_ref, sem = rest
    else:
        o_ref = None
        n_ref, buf_ref, sem = rest
    i = pl.program_id(0)
    n = pl.num_programs(0)

    def gather(tile, par):
        for t in range(tm):
            for slot in range(2):
                d = dest_ref[slot * n_tok + tile * tm + t]
                pltpu.make_async_copy(y_ref.at[pl.ds(d, 1)], buf_ref.at[par, slot, pl.ds(t, 1)],
                                      sem.at[par]).start()

    @pl.when(i == 0)
    def _():
        gather(i, 0)

    for par in range(2):
        @pl.when((i + 1 < n) & ((i + 1) % 2 == par))
        def _():
            gather(i + 1, par)

    for par in range(2):
        @pl.when(i % 2 == par)
        def _():
            for slot in range(2):
                pltpu.make_async_copy(y_ref.at[pl.ds(0, tm)], buf_ref.at[par, slot], sem.at[par]).wait()
            w = wts_ref[...]
            hn = resid_ref[...] + (w[:, 0:1] * buf_ref[par, 0] + w[:, 1:2] * buf_ref[par, 1])
            if keep_h:
                o_ref[...] = hn
            yn = hn * lax.rsqrt(jnp.mean(hn * hn, axis=-1, keepdims=True) + EPS)
            n_ref[...] = (yn * nw_ref[...]).astype(n_ref.dtype)


def _combine(resid, wts, y, dest_flat, next_norm_w, norm_dtype, keep_h, tm=128):
    T, D = resid.shape
    row_spec = pl.BlockSpec((tm, D), lambda i, d: (i, 0))
    out_shape = [jax.ShapeDtypeStruct((T, D), norm_dtype)]
    out_specs = [row_spec]
    if keep_h:
        out_shape.insert(0, jax.ShapeDtypeStruct((T, D), F32))
        out_specs.insert(0, row_spec)
    grid_spec = pltpu.PrefetchScalarGridSpec(
        num_scalar_prefetch=1,
        grid=(T // tm,),
        in_specs=[row_spec,
                  pl.BlockSpec((tm, ROUTER_LANES), lambda i, d: (i, 0)),
                  pl.BlockSpec((1, D), lambda i, d: (0, 0)),
                  pl.BlockSpec(memory_space=pl.ANY)],
        out_specs=tuple(out_specs),
        scratch_shapes=[pltpu.VMEM((2, 2, tm, D), F32), pltpu.SemaphoreType.DMA((2,))],
    )
    outs = pl.pallas_call(
        functools.partial(_combine_kernel, tm=tm, n_tok=T, keep_h=keep_h),
        out_shape=tuple(out_shape),
        grid_spec=grid_spec,
        compiler_params=_cparams(("arbitrary",)),
        name="moe_combine",
    )(dest_flat, resid, wts, next_norm_w.reshape(1, D), y)
    return (outs[0], outs[1]) if keep_h else (None, outs[0])


def _hier_moe(h, norm_w, w_group, b_group, w_expert, b_expert, w_gate, w_up, w_down, layer,
              next_norm_w, norm_dtype, keep_h):
    T, D = h.shape
    RB = MOE_ROWS
    n_blk = (T * 2) // RB + N_EXPERTS
    n_rows = n_blk * RB
    xn, ids, wts, cnt = _router(h, norm_w, w_group, b_group, w_expert, b_expert)
    counts = cnt[0, N_GROUPS:N_GROUPS + N_EXPERTS].astype(I32)
    plan, n_used, start_padded, zero_blk = _moe_plan(counts, n_blk)
    is_e = ids[None, 0:2] == jnp.arange(N_EXPERTS, dtype=I32)[:, None, None]
    dest = (jnp.sum(jnp.where(is_e, start_padded[:, None, None], 0), axis=0) + ids[2:4]).reshape(-1)
    xs = _dispatch(xn, dest, zero_blk, n_rows)
    y = _expert_mlp(xs, plan, n_used, w_gate, w_up, w_down, layer)
    return _combine(h, wts, y, dest, next_norm_w, norm_dtype, keep_h)


def _compress_kernel(x_ref, w1_ref, w2_ref, pe_ref, o_ref, xf_ref):
    d = B_D
    half = CMP_STRIDE * d
    nch = x_ref.shape[0] // CMP_STRIDE
    xf_ref[...] = x_ref[...].astype(F32)
    w1 = w1_ref[...].astype(BF16)
    a = jnp.zeros((nch, w1.shape[1]), F32)
    b = jnp.zeros((nch, w1.shape[1]), F32)
    for l in range(CMP_STRIDE):
        xl = xf_ref[pl.ds(l, nch, stride=CMP_STRIDE), :].astype(BF16)
        a = a + jnp.dot(xl, w1[l * d:(l + 1) * d], preferred_element_type=F32)
        b = b + jnp.dot(xl, w1[half + l * d:half + (l + 1) * d], preferred_element_type=F32)
    pe = jnp.broadcast_to(pe_ref[...], (8, 2 * half)).astype(BF16)
    pt = jnp.dot(pe, w1, preferred_element_type=F32)[0:1]
    hid = a + pltpu.roll(b, b.shape[0] - 1, 0) + pt
    g = jax.nn.gelu(hid)
    o_ref[...] = jnp.dot(g.astype(BF16), w2_ref[...].astype(BF16),
                         preferred_element_type=F32).astype(o_ref.dtype)


def _compress(projm, w1, w2, pe_flat, B, S):
    G, d = B_GROUPS, B_D
    nch = S // CMP_STRIDE
    width = CMP_STRIDE * d
    hid = w1.shape[-1]
    cb = B_HEADS
    return pl.pallas_call(
        _compress_kernel,
        out_shape=jax.ShapeDtypeStruct((2, B, G, nch, d), BF16),
        grid=(2, B, G),
        in_specs=[pl.BlockSpec((S, d), lambda kv, b, g: (b, cb + kv * G + g)),
                  pl.BlockSpec((None, 2 * width, hid), lambda kv, b, g: (kv, 0, 0)),
                  pl.BlockSpec((None, hid, d), lambda kv, b, g: (kv, 0, 0)),
                  pl.BlockSpec((None, 1, 2 * width), lambda kv, b, g: (kv, 0, 0))],
        out_specs=pl.BlockSpec((None, None, None, nch, d), lambda kv, b, g: (kv, b, g, 0, 0)),
        scratch_shapes=[pltpu.VMEM((S, d), F32)],
        compiler_params=_cparams(("arbitrary", "arbitrary", "arbitrary")),
        name="nsa_compress",
    )(projm, w1, w2, pe_flat)


def _cmp_select_kernel(slopes_ref, q_ref, kc_ref, vc_ref, ov_ref, o_ref, sel_ref, need_ref, *, tq, n_cmp,
                       n_blocks, n_rank):
    g = pl.program_id(1)
    i = pl.program_id(2)
    d = B_D
    nch = kc_ref.shape[0]
    pos = i * tq + lax.broadcasted_iota(I32, (1, tq), 1)
    c = lax.broadcasted_iota(I32, (nch, 1), 0)
    dist_i = pos - (c * CMP_STRIDE + (CMP_LEN - 1))
    valid = (dist_i >= 0) & (c < n_cmp)
    dist = dist_i.astype(F32)
    any_valid = jnp.where(pos >= CMP_LEN - 1, 1.0, 0.0)
    kc = kc_ref[...]
    vc = vc_ref[...]
    psum = jnp.zeros((nch, tq), F32)
    for hh in range(B_HPG):
        s = _nt_dot(kc, q_ref[:, hh * d:(hh + 1) * d])
        s = jnp.where(valid, s - (slopes_ref[g * B_HPG + hh] * LOG2E) * dist, NEG)
        e = jnp.exp2(s - jnp.max(s, axis=0, keepdims=True))
        p = (e / jnp.sum(e, axis=0, keepdims=True)) * any_valid
        oT = lax.dot_general(vc, p.astype(BF16), (((0,), (0,)), ((), ())), preferred_element_type=F32)
        o_ref[:, hh * d:(hh + 1) * d] = oT.T
        psum = psum + p

    p_hi = psum.astype(BF16)
    p_lo = (psum - p_hi.astype(F32)).astype(BF16)
    ov = ov_ref[...]
    imp = jnp.dot(ov, p_hi, preferred_element_type=F32) + jnp.dot(ov, p_lo, preferred_element_type=F32)
    imp = imp[:n_rank]

    blk = lax.broadcasted_iota(I32, (n_rank, 1), 0)
    cur = lax.shift_right_logical(pos, SLC_BLOCK.bit_length() - 1)
    forced = (blk == 0) | (blk == cur) | (blk == cur - 1)
    causal = blk * SLC_BLOCK <= pos
    score = jnp.where(forced, FORCE, jnp.where(causal, imp, -FORCE))
    score = jnp.where(blk < n_blocks, score, -2.0 * FORCE)
    blk_f = blk.astype(F32)
    remaining = score
    sel_add = jnp.full(score.shape, NEG, F32)
    for _ in range(SLC_TOPN):
        top = jnp.max(remaining, axis=0, keepdims=True)
        first = jnp.min(jnp.where(remaining == top, blk_f, float(n_rank)), axis=0, keepdims=True)
        pick = blk_f == first
        sel_add = jnp.where(pick, 0.0, sel_add)
        remaining = jnp.where(pick, -4.0 * FORCE, remaining)
    pad = jnp.zeros((128 - n_rank, tq), F32)
    sel_t = jnp.concatenate([jnp.where(blk < n_blocks, sel_add, 0.0), pad], axis=0) if n_rank < 128 else sel_add
    sel_ref[...] = sel_t.T.astype(sel_ref.dtype)
    per_tile = SLC_KEY_TILE // SLC_BLOCK
    blk_any = jnp.max(jnp.where(sel_add == 0.0, 1, 0), axis=1, keepdims=True)
    rows = []
    for j in range(8):
        if (j + 1) * per_tile <= n_rank:
            hit = jnp.max(blk_any[j * per_tile:(j + 1) * per_tile], axis=0, keepdims=True)
            rows.append(jnp.broadcast_to(hit, (1, 128)))
        else:
            rows.append(jnp.zeros((1, 128), I32))
    need_ref[...] = jnp.concatenate(rows, axis=0)


def _cmp_select(projm, cmp_kv, overlap, B, S, tq=512):
    T = B * S
    G, d = B_GROUPS, B_D
    nq = S // tq
    nch = cmp_kv.shape[3]
    n_cmp = (S - CMP_LEN) // CMP_STRIDE + 1
    n_blocks = S // SLC_BLOCK
    assert SLC_TOPN <= n_blocks <= FEAT_BLK
    slopes = _alibi_slopes(B_HEADS)
    grid_spec = pltpu.PrefetchScalarGridSpec(
        num_scalar_prefetch=1,
        grid=(B, G, nq),
        in_specs=[pl.BlockSpec((tq, B_HPG * d), lambda b, g, i, sl: (b * nq + i, g)),
                  pl.BlockSpec((None, None, None, nch, d), lambda b, g, i, sl: (0, b, g, 0, 0)),
                  pl.BlockSpec((None, None, None, nch, d), lambda b, g, i, sl: (1, b, g, 0, 0)),
                  pl.BlockSpec((128, nch), lambda b, g, i, sl: (0, 0))],
        out_specs=(pl.BlockSpec((tq, B_HPG * d), lambda b, g, i, sl: (b * nq + i, g)),
                   pl.BlockSpec((None, None, tq, 128), lambda b, g, i, sl: (b, g, i, 0)),
                   pl.BlockSpec((None, None, None, 8, 128), lambda b, g, i, sl: (b, g, i, 0, 0))),
    )
    assert S // SLC_KEY_TILE <= 8
    return pl.pallas_call(
        functools.partial(_cmp_select_kernel, tq=tq, n_cmp=n_cmp, n_blocks=n_blocks,
                          n_rank=-(-n_blocks // 8) * 8),
        out_shape=(jax.ShapeDtypeStruct((T, B_HEADS * d), F32),
                   jax.ShapeDtypeStruct((B, G, S, 128), BF16),
                   jax.ShapeDtypeStruct((B, G, nq, 8, 128), I32)),
        grid_spec=grid_spec,
        compiler_params=_cparams(("parallel", "parallel", "arbitrary")),
        name="nsa_cmp_select",
    )(slopes, projm, cmp_kv, cmp_kv, overlap)


def _head_units(k_ref, v_ref, feat_ref, qa_ref, j, mask, qc):
    n_heads, tq, _ = qa_ref.shape
    k0 = pl.multiple_of(j * tq, tq)
    v = v_ref[pl.ds(k0, tq), :]

    def score_fn(hh, c):
        def fn():
            ka = jnp.concatenate([k_ref[pl.ds(k0, tq), :], feat_ref[pl.ds(k0, tq), :]], axis=1)
            sT = _nt_dot(ka, qa_ref[hh, c * qc:(c + 1) * qc, :])
            return sT if mask is None else sT + mask[:, c * qc:(c + 1) * qc]
        return fn

    return [(score_fn(hh, c), v, hh * (tq // qc) + c) for hh in range(n_heads) for c in range(tq // qc)]


def _write_heads_T(o_ref, acc_ref, l_ref):
    tq = o_ref.shape[0]
    qc = l_ref.shape[-1]
    for hh in range(B_HPG):
        for c in range(tq // qc):
            u = hh * (tq // qc) + c
            o_ref[c * qc:(c + 1) * qc, hh * B_D:(hh + 1) * B_D] = (acc_ref[u] / l_ref[u]).T


def _flash_state(n_heads, tq, dv):
    qc = min(ATTN_QC, tq)
    n_units = n_heads * (tq // qc)
    return [pltpu.VMEM((n_units, 1, qc), F32), pltpu.VMEM((n_units, 1, qc), F32),
            pltpu.VMEM((n_units, dv, qc), F32)]


def _stage_queries(q_ref, qa_ref, slopes_ref, g, q0, sel_add):
    tq = q_ref.shape[0]
    lane = lax.broadcasted_iota(I32, (tq, 128), 1)
    for hh in range(B_HPG):
        coef = jnp.broadcast_to(_alibi_query_features(slopes_ref[g * B_HPG + hh] * LOG2E, q0), (tq, 128))
        if sel_add is not None:
            coef = jnp.where(lane < FEAT_BLK, sel_add, coef)
        qa_ref[hh] = jnp.concatenate([q_ref[:, hh * B_D:(hh + 1) * B_D], coef.astype(BF16)], axis=1)


def _slc_attn_kernel(slopes_ref, need_ref, q_ref, k_ref, v_ref, feat_ref, sel_ref, o_ref, qa_ref, m_ref, l_ref,
                     acc_ref, *, tq):
    b = pl.program_id(0)
    g = pl.program_id(1)
    i = pl.program_id(2)
    n = pl.num_programs(2)
    _flash_init(m_ref, l_ref, acc_ref)
    _stage_queries(q_ref, qa_ref, slopes_ref, g, i * tq, sel_ref[...].astype(F32))

    qc = m_ref.shape[-1]

    def units(j, causal=None):
        return _head_units(k_ref, v_ref, feat_ref, qa_ref, j, causal, qc)

    need_base = ((b * pl.num_programs(1) + g) * n + i) * n

    def needed(j):
        return (j == 0) | (need_ref[need_base + jnp.maximum(j, 0)] > 0)

    def body(p, carry):
        j0, j1 = 2 * p, 2 * p + 1
        n0, n1 = needed(j0), needed(j1)

        @pl.when(n0 & n1)
        def _():
            _flash_units(units(j0) + units(j1), m_ref, l_ref, acc_ref)

        @pl.when(n0 & jnp.logical_not(n1))
        def _():
            _flash_units(units(j0), m_ref, l_ref, acc_ref)

        @pl.when(jnp.logical_not(n0) & n1)
        def _():
            _flash_units(units(j1), m_ref, l_ref, acc_ref)

        return carry

    lax.fori_loop(0, lax.shift_right_logical(i, 1), body, 0)

    @pl.when((i % 2 == 1) & needed(i - 1))
    def _():
        _flash_units(units(i - 1), m_ref, l_ref, acc_ref)

    _flash_units(units(i, _tri_mask(tq, tq, True)), m_ref, l_ref, acc_ref)
    _write_heads_T(o_ref, acc_ref, l_ref)


def _slc_attention(projm, sel, need, B, S):
    T = B * S
    G, d = B_GROUPS, B_D
    tq = SLC_KEY_TILE
    n = S // tq
    slopes = _alibi_slopes(B_HEADS)
    ksb = (B_HEADS * d + 2 * G * d) // d
    vsb = ksb + G
    grid_spec = pltpu.PrefetchScalarGridSpec(
        num_scalar_prefetch=2,
        grid=(B, G, n),
        in_specs=[
            pl.BlockSpec((tq, B_HPG * d), lambda b, g, i, sl, nd: (b * n + i, g)),
            pl.BlockSpec((S, d), lambda b, g, i, sl, nd: (b, ksb + g)),
            pl.BlockSpec((S, d), lambda b, g, i, sl, nd: (b, vsb + g)),
            pl.BlockSpec((S, 128), lambda b, g, i, sl, nd: (0, 0)),
            pl.BlockSpec((None, None, tq, 128), lambda b, g, i, sl, nd: (b, g, i, 0)),
        ],
        out_specs=pl.BlockSpec((tq, B_HPG * d), lambda b, g, i, sl, nd: (b * n + i, g)),
        scratch_shapes=[pltpu.VMEM((B_HPG, tq, 2 * d), BF16)] + _flash_state(B_HPG, tq, d),
    )
    return pl.pallas_call(
        functools.partial(_slc_attn_kernel, tq=tq),
        out_shape=jax.ShapeDtypeStruct((T, B_HEADS * d), F32),
        grid_spec=grid_spec,
        compiler_params=_cparams(("parallel", "parallel", "arbitrary")),
        name="nsa_selected_attention",
    )(slopes, need.reshape(-1), projm, projm, projm, _key_features(S), sel)


def _win_attn_kernel(slopes_ref, q_ref, k_ref, v_ref, feat_ref, o_ref, qa_ref, m_ref, l_ref, acc_ref, *, tq):
    g = pl.program_id(1)
    i = pl.program_id(2)
    _flash_init(m_ref, l_ref, acc_ref)
    _stage_queries(q_ref, qa_ref, slopes_ref, g, i * tq, None)

    qc = m_ref.shape[-1]

    def units(j, mask):
        return _head_units(k_ref, v_ref, feat_ref, qa_ref, j, mask, qc)

    @pl.when(i > 0)
    def _():
        _flash_units(units(i - 1, _tri_mask(tq, tq, False)) + units(i, _tri_mask(tq, tq, True)),
                     m_ref, l_ref, acc_ref)

    @pl.when(i == 0)
    def _():
        _flash_units(units(i, _tri_mask(tq, tq, True)), m_ref, l_ref, acc_ref)

    _write_heads_T(o_ref, acc_ref, l_ref)


def _win_attention(projm, B, S):
    T = B * S
    G, d = B_GROUPS, B_D
    tq = WINDOW
    n = S // tq
    slopes = _alibi_slopes(B_HEADS)
    kwb = (B_HEADS * d + 4 * G * d) // d
    vwb = kwb + G
    grid_spec = pltpu.PrefetchScalarGridSpec(
        num_scalar_prefetch=1,
        grid=(B, G, n),
        in_specs=[
            pl.BlockSpec((tq, B_HPG * d), lambda b, g, i, sl: (b * n + i, g)),
            pl.BlockSpec((S, d), lambda b, g, i, sl: (b, kwb + g)),
            pl.BlockSpec((S, d), lambda b, g, i, sl: (b, vwb + g)),
            pl.BlockSpec((S, 128), lambda b, g, i, sl: (0, 0)),
        ],
        out_specs=pl.BlockSpec((tq, B_HPG * d), lambda b, g, i, sl: (b * n + i, g)),
        scratch_shapes=[pltpu.VMEM((B_HPG, tq, 2 * d), BF16)] + _flash_state(B_HPG, tq, d),
    )
    return pl.pallas_call(
        functools.partial(_win_attn_kernel, tq=tq),
        out_shape=jax.ShapeDtypeStruct((T, B_HEADS * d), F32),
        grid_spec=grid_spec,
        compiler_params=_cparams(("parallel", "parallel", "arbitrary")),
        name="nsa_window_attention",
    )(slopes, projm, projm, projm, _key_features(S))


def _gate_mix_kernel(gl_ref, oc_ref, os_ref, ow_ref, o_ref):
    sig = jax.nn.sigmoid(gl_ref[...])
    d = B_D
    for hd in range(B_HEADS):
        sl = slice(hd * d, (hd + 1) * d)
        o = (sig[:, 3 * hd:3 * hd + 1] * oc_ref[:, sl] + sig[:, 3 * hd + 1:3 * hd + 2] * os_ref[:, sl]
             + sig[:, 3 * hd + 2:3 * hd + 3] * ow_ref[:, sl])
        o_ref[:, sl] = o.astype(o_ref.dtype)


def _gate_mix(gate_logits, o_cmp, o_slc, o_win, tm=256):
    T, W = o_cmp.shape
    spec = pl.BlockSpec((tm, W), lambda i: (i, 0))
    return pl.pallas_call(
        _gate_mix_kernel,
        out_shape=jax.ShapeDtypeStruct((T, W), BF16),
        grid=(T // tm,),
        in_specs=[pl.BlockSpec((tm, 128), lambda i: (i, 0)), spec, spec, spec],
        out_specs=spec,
        compiler_params=_cparams(("parallel",)),
        name="nsa_gate_mix",
    )(gate_logits, o_cmp, o_slc, o_win)


def _overlap_matrix(S, n_rows):
    nc = (S - CMP_LEN) // CMP_STRIDE + 1
    nsb = S // SLC_BLOCK
    cs = np.arange(nc) * CMP_STRIDE
    ss = np.arange(nsb) * SLC_BLOCK
    ov = np.clip(np.minimum(cs[:, None] + CMP_LEN, ss[None, :] + SLC_BLOCK)
                 - np.maximum(cs[:, None], ss[None, :]), 0, None) / CMP_LEN
    out = np.zeros((128, n_rows), np.float32)
    out[:nsb, :nc] = ov.T
    return jnp.asarray(out, BF16)


def _nsa_attention(xn, w_in, cmp_pos, cmp_w1, cmp_w2, B, S):
    T = B * S
    H, G, d = B_HEADS, B_GROUPS, B_D
    n_main = H * d + 6 * G * d
    col_scale = jnp.concatenate([jnp.full((H * d,), d ** -0.5 * LOG2E, F32), jnp.ones((n_main - H * d,), F32)])
    projm = _matmul(xn, w_in, n_main, BF16, col_scale=col_scale)
    w_gate = jnp.pad(w_in[:, n_main:], ((0, 0), (0, 128 - 3 * H)))
    gate_logits = _matmul(xn, w_gate, 128, F32, tn=128)

    cmp_kv = _compress(projm, cmp_w1, cmp_w2, cmp_pos.reshape(2, 1, CMP_LEN * d), B, S)
    o_cmp, sel, need = _cmp_select(projm, cmp_kv, _overlap_matrix(S, S // CMP_STRIDE), B, S)
    n_t = S // SLC_KEY_TILE
    need = need[:, :, :, :n_t, 0].reshape(B, G, n_t, -1, n_t).max(axis=3)
    o_slc = _slc_attention(projm, sel, need, B, S)
    o_win = _win_attention(projm, B, S)
    return _gate_mix(gate_logits, o_cmp, o_slc, o_win)


def kernel(x, attn_norm_w, ffn_norm_w, final_norm_w, a_w_in, a_lambda, a_subln_w, a_w_out, b_w_in,
           b_cmp_pos, b_cmp_w1, b_cmp_w2, b_w_out, moe_w_group, moe_b_group, moe_w_expert,
           moe_b_expert, moe_w_gate, moe_w_up, moe_w_down):
    B, S, D = x.shape
    T = B * S
    depth = attn_norm_w.shape[0]
    h = x.reshape(T, D)
    xn = _rmsnorm(h, attn_norm_w[0], BF16)
    for i in range(depth):
        j = i // 2
        if i % 2 == 0:
            lambda_init = 0.8 - 0.6 * math.exp(-0.3 * i)
            n_q = A_HEADS * 2 * A_DH
            n_in = 2 * n_q + A_HEADS * A_DV
            col_scale = jnp.concatenate([jnp.full((n_q,), A_DH ** -0.5 * LOG2E, F32),
                                         jnp.ones((n_in - n_q,), F32)])
            qkv = _matmul(xn, a_w_in[j], n_in, BF16, col_scale=col_scale)
            o = _diff_attention(qkv, a_lambda[j], a_subln_w[j], lambda_init, B, S)
            h = _matmul(o, a_w_out[j], D, F32, resid=h)
        else:
            o = _nsa_attention(xn, b_w_in[j], b_cmp_pos[j], b_cmp_w1[j], b_cmp_w2[j], B, S)
            h = _matmul(o, b_w_out[j], D, F32, resid=h)
        last = i == depth - 1
        h, xn = _hier_moe(h, ffn_norm_w[i], moe_w_group[i], moe_b_group[i], moe_w_expert[i], moe_b_expert[i],
                          moe_w_gate, moe_w_up, moe_w_down, i,
                          final_norm_w if last else attn_norm_w[i + 1], F32 if last else BF16, not last)
    return xn.reshape(B, S, D)
```

```python
import functools
import math

import numpy as np
import jax
import jax.numpy as jnp
from jax import lax
from jax.experimental import pallas as pl
from jax.experimental.pallas import tpu as pltpu

F32 = jnp.float32
BF16 = jnp.bfloat16
I32 = jnp.int32

EPS = 1e-6
NEG = -1e30
FORCE = 1e4
LOG2E = math.log2(math.e)

A_HEADS = 8
A_DH = 128
A_DV = 256
B_HEADS = 16
B_GROUPS = 4
B_HPG = 4
B_D = 128
CMP_LEN = 32
CMP_STRIDE = 16
SLC_BLOCK = 64
SLC_TOPN = 16
WINDOW = 512
N_GROUPS = 4
EPG = 8
N_EXPERTS = 32
D_EXPERT = 1024
ROUTER_LANES = 128
MOE_ROWS = 256
DMA_UNROLL = 8
ATTN_QC = 512
SLC_KEY_TILE = 512

VMEM_LIMIT = 56 * 1024 * 1024


def _cparams(sem, flags=None):
    return pltpu.CompilerParams(dimension_semantics=sem, vmem_limit_bytes=VMEM_LIMIT, flags=flags)


def _alibi_slopes(n):
    return jnp.asarray(np.array([2.0 ** (-8.0 * (i + 1) / n) for i in range(n)], dtype=np.float32))


def _nt_dot(a, b):
    return lax.dot_general(a, b, (((1,), (1,)), ((), ())), preferred_element_type=F32)


def _rms_kernel(x_ref, w_ref, o_ref):
    x = x_ref[...]
    y = x * lax.rsqrt(jnp.mean(x * x, axis=-1, keepdims=True) + EPS)
    o_ref[...] = (y * w_ref[...]).astype(o_ref.dtype)


def _rmsnorm(x, w, out_dtype, tm=512):
    T, D = x.shape
    return pl.pallas_call(
        _rms_kernel,
        out_shape=jax.ShapeDtypeStruct((T, D), out_dtype),
        grid=(T // tm,),
        in_specs=[pl.BlockSpec((tm, D), lambda i: (i, 0)),
                  pl.BlockSpec((1, D), lambda i: (0, 0))],
        out_specs=pl.BlockSpec((tm, D), lambda i: (i, 0)),
        compiler_params=_cparams(("parallel",)),
        name="rmsnorm",
    )(x, w.reshape(1, D))


def _mm_kernel(*refs, has_scale, has_resid):
    a_ref, w_ref = refs[0], refs[1]
    k = 2
    scale_ref = resid_ref = None
    if has_scale:
        scale_ref = refs[k]
        k += 1
    if has_resid:
        resid_ref = refs[k]
        k += 1
    o_ref, wbf_ref = refs[k], refs[k + 1]

    @pl.when(pl.program_id(1) == 0)
    def _():
        wbf_ref[...] = w_ref[...].astype(BF16)

    acc = jnp.dot(a_ref[...], wbf_ref[...], preferred_element_type=F32)
    if has_scale:
        acc = acc * scale_ref[...]
    if has_resid:
        acc = acc + resid_ref[...]
    o_ref[...] = acc.astype(o_ref.dtype)


def _matmul(a, w, n_out, out_dtype, *, col_scale=None, resid=None, tm=1024, tn=512):
    M, K = a.shape
    assert M % tm == 0 and n_out % tn == 0 and w.shape[0] == K and w.shape[1] >= n_out
    in_specs = [pl.BlockSpec((tm, K), lambda j, i: (i, 0)),
                pl.BlockSpec((K, tn), lambda j, i: (0, j))]
    args = [a, w]
    if col_scale is not None:
        in_specs.append(pl.BlockSpec((1, tn), lambda j, i: (0, j)))
        args.append(col_scale.reshape(1, n_out))
    if resid is not None:
        in_specs.append(pl.BlockSpec((tm, tn), lambda j, i: (i, j)))
        args.append(resid)
    return pl.pallas_call(
        functools.partial(_mm_kernel, has_scale=col_scale is not None, has_resid=resid is not None),
        out_shape=jax.ShapeDtypeStruct((M, n_out), out_dtype),
        grid=(n_out // tn, M // tm),
        in_specs=in_specs,
        out_specs=pl.BlockSpec((tm, tn), lambda j, i: (i, j)),
        scratch_shapes=[pltpu.VMEM((K, tn), BF16)],
        compiler_params=_cparams(("parallel", "arbitrary")),
        name="matmul",
    )(*args)


FEAT_BLK = 64
FEAT_POS = 67
FEAT_ONE = 70


def _key_features(S):
    assert S // SLC_BLOCK <= FEAT_BLK
    j = np.arange(S)
    f = np.zeros((S, 128), np.float32)
    f[j, j // SLC_BLOCK] = 1.0
    f[:, FEAT_BLK:FEAT_BLK + 3] = (j // SLC_BLOCK)[:, None]
    f[:, FEAT_POS:FEAT_POS + 3] = (j % SLC_BLOCK)[:, None]
    f[:, FEAT_ONE:FEAT_ONE + 3] = 1.0
    return jnp.asarray(f, BF16)


def _alibi_query_features(slope2, q0):
    lane = lax.broadcasted_iota(I32, (1, 128), 1)
    base = jnp.where((lane >= FEAT_BLK) & (lane < FEAT_BLK + 3), slope2 * float(SLC_BLOCK),
                     jnp.where((lane >= FEAT_POS) & (lane < FEAT_POS + 3), slope2,
                               jnp.where((lane >= FEAT_ONE) & (lane < FEAT_ONE + 3),
                                         -slope2 * q0.astype(F32), 0.0)))
    hi = base.astype(BF16).astype(F32)
    r1 = base - hi
    lo = r1.astype(BF16).astype(F32)
    lo2 = (r1 - lo).astype(BF16).astype(F32)
    first = (lane == FEAT_BLK) | (lane == FEAT_POS) | (lane == FEAT_ONE)
    second = (lane == FEAT_BLK + 1) | (lane == FEAT_POS + 1) | (lane == FEAT_ONE + 1)
    return jnp.where(first, hi, jnp.where(second, lo, lo2))


def _flash_init(m_ref, l_ref, acc_ref):
    m_ref[...] = jnp.full(m_ref.shape, NEG, F32)
    l_ref[...] = jnp.zeros(l_ref.shape, F32)
    acc_ref[...] = jnp.zeros(acc_ref.shape, F32)


def _flash_probs(sT, m_ref, l_ref, u):
    m_old = m_ref[u]
    m_new = jnp.maximum(m_old, jnp.max(sT, axis=0, keepdims=True))
    alpha = jnp.exp2(m_old - m_new)
    p = jnp.exp2(sT - m_new)
    l_ref[u] = alpha * l_ref[u] + jnp.sum(p, axis=0, keepdims=True)
    m_ref[u] = m_new
    return p.astype(BF16), alpha


def _flash_accumulate(p, alpha, v, acc_ref, u):
    pv = lax.dot_general(v, p, (((0,), (0,)), ((), ())), preferred_element_type=F32)
    acc_ref[u] = alpha * acc_ref[u] + pv


def _flash_units(units, m_ref, l_ref, acc_ref):
    n = len(units)
    scores, probs = {}, {}
    for s in range(n + 2):
        if s < n:
            scores[s] = units[s][0]()
        if 1 <= s <= n:
            probs[s - 1] = _flash_probs(scores.pop(s - 1), m_ref, l_ref, units[s - 1][2])
        if 2 <= s <= n + 1:
            p, alpha = probs.pop(s - 2)
            _flash_accumulate(p, alpha, units[s - 2][1], acc_ref, units[s - 2][2])


def _tri_mask(tk, tq, keep_upper):
    r = lax.broadcasted_iota(I32, (tk, tq), 0)
    c = lax.broadcasted_iota(I32, (tk, tq), 1)
    keep = (r <= c) if keep_upper else (r > c)
    return jnp.where(keep, 0.0, NEG)


def _diff_attn_kernel(slopes_ref, q1_ref, q2_ref, k1_ref, k2_ref, v_ref, feat_ref, lam_ref, sw_ref, o_ref,
                      qa_ref, m_ref, l_ref, acc_ref, *, tq, lambda_init):
    h = pl.program_id(1)
    i = pl.program_id(2)
    _flash_init(m_ref, l_ref, acc_ref)
    qfeat = jnp.broadcast_to(_alibi_query_features(slopes_ref[h] * LOG2E, i * tq), (tq, 128)).astype(BF16)
    for c, q_ref in enumerate((q1_ref, q2_ref)):
        qa_ref[c] = jnp.concatenate([q_ref[...], qfeat], axis=1)

    def tile_units(j, mask):
        k0 = pl.multiple_of(j * tq, tq)
        v = v_ref[pl.ds(k0, tq), :]
        feat = feat_ref[pl.ds(k0, tq), :]
        qc = m_ref.shape[-1]

        def score_fn(c, k_ref, x):
            def fn():
                ka = jnp.concatenate([k_ref[pl.ds(k0, tq), :], feat], axis=1)
                sT = _nt_dot(ka, qa_ref[c, x * qc:(x + 1) * qc, :])
                return sT if mask is None else sT + mask[:, x * qc:(x + 1) * qc]
            return fn

        return [(score_fn(c, k_ref, x), v, c * (tq // qc) + x)
                for c, k_ref in enumerate((k1_ref, k2_ref)) for x in range(tq // qc)]

    def body(p, carry):
        _flash_units(tile_units(2 * p, None) + tile_units(2 * p + 1, None), m_ref, l_ref, acc_ref)
        return carry

    lax.fori_loop(0, lax.shift_right_logical(i, 1), body, 0)

    @pl.when(i % 2 == 1)
    def _():
        _flash_units(tile_units(i - 1, None), m_ref, l_ref, acc_ref)

    _flash_units(tile_units(i, _tri_mask(tq, tq, True)), m_ref, l_ref, acc_ref)

    lam = lam_ref[...]
    lmbda = (jnp.exp(jnp.sum(lam[0:1] * lam[1:2], axis=-1, keepdims=True))
             - jnp.exp(jnp.sum(lam[2:3] * lam[3:4], axis=-1, keepdims=True)) + lambda_init)
    qc = m_ref.shape[-1]
    nx = tq // qc
    for x in range(nx):
        oT = acc_ref[x] / l_ref[x] - lmbda * (acc_ref[nx + x] / l_ref[nx + x])
        o = oT.T
        y = o * lax.rsqrt(jnp.mean(o * o, axis=-1, keepdims=True) + EPS)
        y = (y * sw_ref[...]) * (1.0 - lambda_init)
        o_ref[x * qc:(x + 1) * qc, :] = y.astype(o_ref.dtype)


def _diff_attention(qkv, lam, subln_w, lambda_init, B, S, tq=512):
    T = B * S
    H, dh, dv = A_HEADS, A_DH, A_DV
    n = S // tq
    slopes = _alibi_slopes(H)
    kb = H * 2
    vb = (2 * H * 2 * dh) // dv
    grid_spec = pltpu.PrefetchScalarGridSpec(
        num_scalar_prefetch=1,
        grid=(B, H, n),
        in_specs=[
            pl.BlockSpec((tq, dh), lambda b, h, i, sl: (b * n + i, 2 * h)),
            pl.BlockSpec((tq, dh), lambda b, h, i, sl: (b * n + i, 2 * h + 1)),
            pl.BlockSpec((S, dh), lambda b, h, i, sl: (b, kb + 2 * h)),
            pl.BlockSpec((S, dh), lambda b, h, i, sl: (b, kb + 2 * h + 1)),
            pl.BlockSpec((S, dv), lambda b, h, i, sl: (b, vb + h)),
            pl.BlockSpec((S, 128), lambda b, h, i, sl: (0, 0)),
            pl.BlockSpec((4, dh), lambda b, h, i, sl: (0, 0)),
            pl.BlockSpec((1, dv), lambda b, h, i, sl: (0, 0)),
        ],
        out_specs=pl.BlockSpec((tq, dv), lambda b, h, i, sl: (b * n + i, h)),
        scratch_shapes=[pltpu.VMEM((2, tq, 2 * dh), BF16)] + _flash_state(2, tq, dv),
    )
    return pl.pallas_call(
        functools.partial(_diff_attn_kernel, tq=tq, lambda_init=lambda_init),
        out_shape=jax.ShapeDtypeStruct((T, H * dv), BF16),
        grid_spec=grid_spec,
        compiler_params=_cparams(("parallel", "parallel", "arbitrary")),
        name="diff_attention",
    )(slopes, qkv, qkv, qkv, qkv, qkv, _key_features(S), lam, subln_w.reshape(1, dv))


def _router_kernel(h_ref, nw_ref, wr_ref, br_ref, tri_ref, xn_ref, ids_ref, wts_ref, cnt_ref, base_ref):
    @pl.when(pl.program_id(0) == 0)
    def _():
        base_ref[...] = jnp.zeros(base_ref.shape, F32)

    x = h_ref[...]
    xn = (x * lax.rsqrt(jnp.mean(x * x, axis=-1, keepdims=True) + EPS)) * nw_ref[...]
    xn_ref[...] = xn
    w = wr_ref[...]
    x_hi = xn.astype(BF16)
    x_lo = (xn - x_hi.astype(F32)).astype(BF16)
    w_hi = w.astype(BF16)
    w_lo = (w - w_hi.astype(F32)).astype(BF16)
    logits = (jnp.dot(x_hi, w_hi, preferred_element_type=F32)
              + (jnp.dot(x_hi, w_lo, preferred_element_type=F32)
                 + jnp.dot(x_lo, w_hi, preferred_element_type=F32))) + br_ref[...]
    lane = lax.broadcasted_iota(I32, logits.shape, 1)
    big = jnp.int32(1 << 20)
    is_g = lane < N_GROUPS
    is_e = (lane >= N_GROUPS) & (lane < N_GROUPS + N_EXPERTS)

    lg = jnp.where(is_g, logits, NEG)
    mg = jnp.max(lg, axis=-1, keepdims=True)
    g_top = jnp.min(jnp.where(lg == mg, lane, big), axis=-1, keepdims=True)
    wg = 1.0 / jnp.sum(jnp.where(is_g, jnp.exp(lg - mg), 0.0), axis=-1, keepdims=True)

    in_grp = is_e & (lax.shift_right_logical(lane - N_GROUPS, 3) == g_top)
    le = jnp.where(in_grp, logits, NEG)
    me = jnp.max(le, axis=-1, keepdims=True)
    ee = jnp.where(in_grp, jnp.exp(le - me), 0.0)
    pe = ee / jnp.sum(ee, axis=-1, keepdims=True)
    pe = jnp.where(in_grp, pe, -1.0)
    p1 = jnp.max(pe, axis=-1, keepdims=True)
    i1 = jnp.min(jnp.where(pe == p1, lane, big), axis=-1, keepdims=True)
    pe2 = jnp.where(lane == i1, -1.0, pe)
    p2 = jnp.max(pe2, axis=-1, keepdims=True)
    i2 = jnp.min(jnp.where(pe2 == p2, lane, big), axis=-1, keepdims=True)
    den = p1 + p2
    w0 = wg * p1 / den
    w1 = wg * p2 / den

    oh1 = lane == i1
    oh2 = lane == i2
    both = jnp.where(oh1 | oh2, 1.0, 0.0)
    before = jnp.dot(tri_ref[...], both.astype(BF16), preferred_element_type=F32) + base_ref[...]
    pos0 = jnp.sum(jnp.where(oh1, before, 0.0), axis=-1, keepdims=True).astype(I32)
    pos1 = jnp.sum(jnp.where(oh2, before, 0.0), axis=-1, keepdims=True).astype(I32)
    total = base_ref[...] + jnp.sum(both, axis=0, keepdims=True)
    base_ref[...] = total
    cnt_ref[...] = total

    ids = jnp.where(lane == 0, i1 - N_GROUPS,
                    jnp.where(lane == 1, i2 - N_GROUPS,
                              jnp.where(lane == 2, pos0, jnp.where(lane == 3, pos1, 0))))
    ids_ref[...] = ids.T[:8]
    wts_ref[...] = jnp.where(lane == 0, w0, jnp.where(lane == 1, w1, 0.0))


def _router(h, norm_w, w_group, b_group, w_expert, b_expert, tm=512):
    T, D = h.shape
    pad = ROUTER_LANES - N_GROUPS - N_EXPERTS
    wr = jnp.concatenate([w_group, w_expert, jnp.zeros((D, pad), F32)], axis=1)
    br = jnp.concatenate([b_group, b_expert, jnp.zeros((pad,), F32)]).reshape(1, ROUTER_LANES)
    tri = jnp.asarray(np.tril(np.ones((tm, tm), np.float32), -1), BF16)
    return pl.pallas_call(
        _router_kernel,
        out_shape=(jax.ShapeDtypeStruct((T, D), F32),
                   jax.ShapeDtypeStruct((8, T), I32),
                   jax.ShapeDtypeStruct((T, ROUTER_LANES), F32),
                   jax.ShapeDtypeStruct((1, ROUTER_LANES), F32)),
        grid=(T // tm,),
        in_specs=[pl.BlockSpec((tm, D), lambda i: (i, 0)),
                  pl.BlockSpec((1, D), lambda i: (0, 0)),
                  pl.BlockSpec((D, ROUTER_LANES), lambda i: (0, 0)),
                  pl.BlockSpec((1, ROUTER_LANES), lambda i: (0, 0)),
                  pl.BlockSpec((tm, tm), lambda i: (0, 0))],
        out_specs=(pl.BlockSpec((tm, D), lambda i: (i, 0)),
                   pl.BlockSpec((8, tm), lambda i: (0, i)),
                   pl.BlockSpec((tm, ROUTER_LANES), lambda i: (i, 0)),
                   pl.BlockSpec((1, ROUTER_LANES), lambda i: (0, 0))),
        scratch_shapes=[pltpu.VMEM((1, ROUTER_LANES), F32)],
        compiler_params=_cparams(("arbitrary",)),
        name="moe_router",
    )(h, norm_w.reshape(1, D), wr, br, tri)


def _dispatch_kernel(dest_ref, zero_ref, xn_ref, xs_ref, zbuf_ref, sem, zsem, *, tm, n_tok):
    @pl.when(pl.program_id(0) == 0)
    def _():
        zbuf_ref[...] = jnp.zeros(zbuf_ref.shape, zbuf_ref.dtype)
        rb = zbuf_ref.shape[0]

        def fill(r):
            return pltpu.make_async_copy(zbuf_ref, xs_ref.at[pl.ds(r * rb, rb)], zsem)

        for r in range(zero_ref.shape[0]):
            @pl.when(zero_ref[r] == 1)
            def _():
                fill(r).start()
        for r in range(zero_ref.shape[0]):
            @pl.when(zero_ref[r] == 1)
            def _():
                fill(r).wait()

    base = pl.program_id(0) * tm
    for t in range(tm):
        for slot in range(2):
            d = dest_ref[slot * n_tok + base + t]
            pltpu.make_async_copy(xn_ref.at[pl.ds(t, 1)], xs_ref.at[pl.ds(d, 1)], sem).start(priority=slot)
    for slot in range(2):
        pltpu.make_async_copy(xn_ref, xs_ref.at[pl.ds(0, tm)], sem).wait()


def _dispatch(xn, dest_flat, zero_blk, n_rows, tm=256):
    T, D = xn.shape
    grid_spec = pltpu.PrefetchScalarGridSpec(
        num_scalar_prefetch=2,
        grid=(T // tm,),
        in_specs=[pl.BlockSpec((tm, D), lambda i, d, z: (i, 0))],
        out_specs=pl.BlockSpec(memory_space=pl.ANY),
        scratch_shapes=[pltpu.VMEM((MOE_ROWS, D), F32), pltpu.SemaphoreType.DMA, pltpu.SemaphoreType.DMA],
    )
    return pl.pallas_call(
        functools.partial(_dispatch_kernel, tm=tm, n_tok=T),
        out_shape=jax.ShapeDtypeStruct((n_rows, D), F32),
        grid_spec=grid_spec,
        compiler_params=_cparams(("arbitrary",)),
        name="moe_dispatch",
    )(dest_flat, zero_blk, xn)


def _moe_plan(counts, n_blk):
    RB = MOE_ROWS
    nb_e = (counts + RB - 1) // RB
    end_b = jnp.cumsum(nb_e)
    start_b = end_b - nb_e
    n_used = end_b[-1:].astype(I32)
    r = jnp.arange(n_blk, dtype=I32)
    be = jnp.minimum(jnp.sum((end_b[None, :] <= r[:, None]).astype(I32), axis=1), N_EXPERTS - 1)
    active = nb_e > 0
    ordinal = jnp.cumsum(active.astype(I32)) - 1
    idx = jnp.arange(N_EXPERTS, dtype=I32)
    later = jnp.where(active[None, :] & (idx[None, :] > idx[:, None]), idx[None, :], N_EXPERTS)
    nxt_e = jnp.min(later, axis=1)
    nxt_e = jnp.where(nxt_e == N_EXPERTS, -1, nxt_e)
    first = ((r == start_b[be]) & (r < n_used[0])).astype(I32)
    plan = jnp.stack([be, first, ordinal[be] % 2, nxt_e[be]]).astype(I32)
    zero_blk = ((r == end_b[be] - 1) | (r >= n_used[0])).astype(I32)
    return plan, n_used, (start_b * RB).astype(I32), zero_blk


def _stream_expert_weights(plan_ref, r, w_hbms, wbuf, sem, layer):
    e, slot, nxt = plan_ref[0, r], plan_ref[2, r], plan_ref[3, r]

    def copies(expert, s):
        rows = wbuf.shape[2] // 2
        return [(pltpu.make_async_copy(w.at[layer, expert, pl.ds(half * rows, rows)],
                                       wbuf.at[s, k, pl.ds(half * rows, rows)], sem.at[s, k, half]), half)
                for k, w in enumerate(w_hbms) for half in range(2)]

    @pl.when(plan_ref[1, r] == 1)
    def _():
        @pl.when(r == 0)
        def _():
            for c, half in copies(e, slot):
                c.start(priority=half)

        for c, _ in copies(e, slot):
            c.wait()

        @pl.when(nxt >= 0)
        def _():
            for c, half in copies(nxt, 1 - slot):
                c.start(priority=half)

    return slot


def _gate_up_kernel(plan_ref, nu_ref, x_ref, wg_hbm, wu_hbm, h_ref, wbuf, sem, *, layer):
    r = pl.program_id(0)

    @pl.when(r < nu_ref[0])
    def _():
        slot = _stream_expert_weights(plan_ref, r, (wg_hbm, wu_hbm), wbuf, sem, layer)
        x = x_ref[...]
        g = jnp.dot(x, wbuf[slot, 0], preferred_element_type=F32)
        u = jnp.dot(x, wbuf[slot, 1], preferred_element_type=F32)
        h_ref[...] = (jax.nn.silu(g) * u).astype(h_ref.dtype)

    @pl.when(r >= nu_ref[0])
    def _():
        h_ref[...] = jnp.zeros(h_ref.shape, h_ref.dtype)


def _down_kernel(plan_ref, nu_ref, h_ref, wd_hbm, y_ref, wbuf, sem, *, layer):
    r = pl.program_id(0)

    @pl.when(r < nu_ref[0])
    def _():
        slot = _stream_expert_weights(plan_ref, r, (wd_hbm,), wbuf, sem, layer)
        y_ref[...] = jnp.dot(h_ref[...].astype(F32), wbuf[slot, 0], preferred_element_type=F32)

    @pl.when(r >= nu_ref[0])
    def _():
        y_ref[...] = jnp.zeros(y_ref.shape, y_ref.dtype)


def _expert_mlp(xs, plan, n_used, w_gate, w_up, w_down, layer):
    n_rows, D = xs.shape
    RB = MOE_ROWS
    n_blk = n_rows // RB
    F = D_EXPERT

    def used_blk(r, plan, nu):
        return (jnp.minimum(r, nu[0] - 1), 0)

    h = pl.pallas_call(
        functools.partial(_gate_up_kernel, layer=layer),
        out_shape=jax.ShapeDtypeStruct((n_rows, F), BF16),
        grid_spec=pltpu.PrefetchScalarGridSpec(
            num_scalar_prefetch=2,
            grid=(n_blk,),
            in_specs=[pl.BlockSpec((RB, D), used_blk),
                      pl.BlockSpec(memory_space=pl.ANY),
                      pl.BlockSpec(memory_space=pl.ANY)],
            out_specs=pl.BlockSpec((RB, F), lambda r, plan, nu: (r, 0)),
            scratch_shapes=[pltpu.VMEM((2, 2, D, F), F32), pltpu.SemaphoreType.DMA((2, 2, 2))],
        ),
        compiler_params=_cparams(("arbitrary",)),
        name="moe_gate_up",
    )(plan, n_used, xs, w_gate, w_up)
    y = pl.pallas_call(
        functools.partial(_down_kernel, layer=layer),
        out_shape=jax.ShapeDtypeStruct((n_rows, D), F32),
        grid_spec=pltpu.PrefetchScalarGridSpec(
            num_scalar_prefetch=2,
            grid=(n_blk,),
            in_specs=[pl.BlockSpec((RB, F), used_blk),
                      pl.BlockSpec(memory_space=pl.ANY)],
            out_specs=pl.BlockSpec((RB, D), lambda r, plan, nu: (r, 0)),
            scratch_shapes=[pltpu.VMEM((2, 1, F, D), F32), pltpu.SemaphoreType.DMA((2, 1, 2))],
        ),
        compiler_params=_cparams(("arbitrary",)),
        name="moe_down",
    )(plan, n_used, h, w_down)
    return y


def _combine_kernel(dest_ref, resid_ref, wts_ref, nw_ref, y_ref, *rest, tm, n_tok, keep_h):
    if keep_h:
        o_ref, n_ref, buf_ref, sem = rest
    else:
        o_ref = None
        n_ref, buf_ref, sem = rest
    i = pl.program_id(0)
    n = pl.num_programs(0)

    def gather(tile, par):
        for t in range(tm):
            for slot in range(2):
                d = dest_ref[slot * n_tok + tile * tm + t]
                pltpu.make_async_copy(y_ref.at[pl.ds(d, 1)], buf_ref.at[par, slot, pl.ds(t, 1)],
                                      sem.at[par]).start(priority=slot)

    @pl.when(i == 0)
    def _():
        gather(i, 0)

    for par in range(2):
        @pl.when((i + 1 < n) & ((i + 1) % 2 == par))
        def _():
            gather(i + 1, par)

    for par in range(2):
        @pl.when(i % 2 == par)
        def _():
            for slot in range(2):
                pltpu.make_async_copy(y_ref.at[pl.ds(0, tm)], buf_ref.at[par, slot], sem.at[par]).wait()
            w = wts_ref[...]
            hn = resid_ref[...] + (w[:, 0:1] * buf_ref[par, 0] + w[:, 1:2] * buf_ref[par, 1])
            if keep_h:
                o_ref[...] = hn
            yn = hn * lax.rsqrt(jnp.mean(hn * hn, axis=-1, keepdims=True) + EPS)
            n_ref[...] = (yn * nw_ref[...]).astype(n_ref.dtype)


def _combine(resid, wts, y, dest_flat, next_norm_w, norm_dtype, keep_h, tm=128):
    T, D = resid.shape
    row_spec = pl.BlockSpec((tm, D), lambda i, d: (i, 0))
    out_shape = [jax.ShapeDtypeStruct((T, D), norm_dtype)]
    out_specs = [row_spec]
    if keep_h:
        out_shape.insert(0, jax.ShapeDtypeStruct((T, D), F32))
        out_specs.insert(0, row_spec)
    grid_spec = pltpu.PrefetchScalarGridSpec(
        num_scalar_prefetch=1,
        grid=(T // tm,),
        in_specs=[row_spec,
                  pl.BlockSpec((tm, ROUTER_LANES), lambda i, d: (i, 0)),
                  pl.BlockSpec((1, D), lambda i, d: (0, 0)),
                  pl.BlockSpec(memory_space=pl.ANY)],
        out_specs=tuple(out_specs),
        scratch_shapes=[pltpu.VMEM((2, 2, tm, D), F32), pltpu.SemaphoreType.DMA((2,))],
    )
    outs = pl.pallas_call(
        functools.partial(_combine_kernel, tm=tm, n_tok=T, keep_h=keep_h),
        out_shape=tuple(out_shape),
        grid_spec=grid_spec,
        compiler_params=_cparams(("arbitrary",)),
        name="moe_combine",
    )(dest_flat, resid, wts, next_norm_w.reshape(1, D), y)
    return (outs[0], outs[1]) if keep_h else (None, outs[0])


def _hier_moe(h, norm_w, w_group, b_group, w_expert, b_expert, w_gate, w_up, w_down, layer,
              next_norm_w, norm_dtype, keep_h):
    T, D = h.shape
    RB = MOE_ROWS
    n_blk = (T * 2) // RB + N_EXPERTS
    n_rows = n_blk * RB
    xn, ids, wts, cnt = _router(h, norm_w, w_group, b_group, w_expert, b_expert)
    counts = cnt[0, N_GROUPS:N_GROUPS + N_EXPERTS].astype(I32)
    plan, n_used, start_padded, zero_blk = _moe_plan(counts, n_blk)
    is_e = ids[None, 0:2] == jnp.arange(N_EXPERTS, dtype=I32)[:, None, None]
    dest = (jnp.sum(jnp.where(is_e, start_padded[:, None, None], 0), axis=0) + ids[2:4]).reshape(-1)
    xs = _dispatch(xn, dest, zero_blk, n_rows)
    y = _expert_mlp(xs, plan, n_used, w_gate, w_up, w_down, layer)
    return _combine(h, wts, y, dest, next_norm_w, norm_dtype, keep_h)


def _compress_kernel(x_ref, w1_ref, w2_ref, pe_ref, o_ref, xf_ref):
    d = B_D
    half = CMP_STRIDE * d
    nch = x_ref.shape[0] // CMP_STRIDE
    xf_ref[...] = x_ref[...].astype(F32)
    w1 = w1_ref[...].astype(BF16)
    a = jnp.zeros((nch, w1.shape[1]), F32)
    b = jnp.zeros((nch, w1.shape[1]), F32)
    for l in range(CMP_STRIDE):
        xl = xf_ref[pl.ds(l, nch, stride=CMP_STRIDE), :].astype(BF16)
        a = a + jnp.dot(xl, w1[l * d:(l + 1) * d], preferred_element_type=F32)
        b = b + jnp.dot(xl, w1[half + l * d:half + (l + 1) * d], preferred_element_type=F32)
    pe = jnp.broadcast_to(pe_ref[...], (8, 2 * half)).astype(BF16)
    pt = jnp.dot(pe, w1, preferred_element_type=F32)[0:1]
    hid = a + pltpu.roll(b, b.shape[0] - 1, 0) + pt
    g = jax.nn.gelu(hid)
    o_ref[...] = jnp.dot(g.astype(BF16), w2_ref[...].astype(BF16),
                         preferred_element_type=F32).astype(o_ref.dtype)


def _compress(projm, w1, w2, pe_flat, B, S):
    G, d = B_GROUPS, B_D
    nch = S // CMP_STRIDE
    width = CMP_STRIDE * d
    hid = w1.shape[-1]
    cb = B_HEADS
    return pl.pallas_call(
        _compress_kernel,
        out_shape=jax.ShapeDtypeStruct((2, B, G, nch, d), BF16),
        grid=(2, B, G),
        in_specs=[pl.BlockSpec((S, d), lambda kv, b, g: (b, cb + kv * G + g)),
                  pl.BlockSpec((None, 2 * width, hid), lambda kv, b, g: (kv, 0, 0)),
                  pl.BlockSpec((None, hid, d), lambda kv, b, g: (kv, 0, 0)),
                  pl.BlockSpec((None, 1, 2 * width), lambda kv, b, g: (kv, 0, 0))],
        out_specs=pl.BlockSpec((None, None, None, nch, d), lambda kv, b, g: (kv, b, g, 0, 0)),
        scratch_shapes=[pltpu.VMEM((S, d), F32)],
        compiler_params=_cparams(("arbitrary", "arbitrary", "arbitrary")),
        name="nsa_compress",
    )(projm, w1, w2, pe_flat)


def _cmp_select_kernel(slopes_ref, q_ref, kc_ref, vc_ref, ov_ref, o_ref, sel_ref, need_ref, *, tq, n_cmp,
                       n_blocks, n_rank):
    g = pl.program_id(1)
    i = pl.program_id(2)
    d = B_D
    nch = kc_ref.shape[0]
    pos = i * tq + lax.broadcasted_iota(I32, (1, tq), 1)
    c = lax.broadcasted_iota(I32, (nch, 1), 0)
    dist_i = pos - (c * CMP_STRIDE + (CMP_LEN - 1))
    valid = (dist_i >= 0) & (c < n_cmp)
    dist = dist_i.astype(F32)
    any_valid = jnp.where(pos >= CMP_LEN - 1, 1.0, 0.0)
    kc = kc_ref[...]
    vc = vc_ref[...]
    psum = jnp.zeros((nch, tq), F32)
    for hh in range(B_HPG):
        s = _nt_dot(kc, q_ref[:, hh * d:(hh + 1) * d])
        s = jnp.where(valid, s - (slopes_ref[g * B_HPG + hh] * LOG2E) * dist, NEG)
        e = jnp.exp2(s - jnp.max(s, axis=0, keepdims=True))
        p = (e / jnp.sum(e, axis=0, keepdims=True)) * any_valid
        oT = lax.dot_general(vc, p.astype(BF16), (((0,), (0,)), ((), ())), preferred_element_type=F32)
        o_ref[:, hh * d:(hh + 1) * d] = oT.T.astype(o_ref.dtype)
        psum = psum + p

    p_hi = psum.astype(BF16)
    p_lo = (psum - p_hi.astype(F32)).astype(BF16)
    ov = ov_ref[...]
    imp = jnp.dot(ov, p_hi, preferred_element_type=F32) + jnp.dot(ov, p_lo, preferred_element_type=F32)
    imp = imp[:n_rank]

    blk = lax.broadcasted_iota(I32, (n_rank, 1), 0)
    cur = lax.shift_right_logical(pos, SLC_BLOCK.bit_length() - 1)
    forced = (blk == 0) | (blk == cur) | (blk == cur - 1)
    causal = blk * SLC_BLOCK <= pos
    score = jnp.where(forced, FORCE, jnp.where(causal, imp, -FORCE))
    score = jnp.where(blk < n_blocks, score, -2.0 * FORCE)
    blk_f = blk.astype(F32)
    remaining = score
    sel_add = jnp.full(score.shape, NEG, F32)
    for _ in range(SLC_TOPN):
        top = jnp.max(remaining, axis=0, keepdims=True)
        first = jnp.min(jnp.where(remaining == top, blk_f, float(n_rank)), axis=0, keepdims=True)
        pick = blk_f == first
        sel_add = jnp.where(pick, 0.0, sel_add)
        remaining = jnp.where(pick, -4.0 * FORCE, remaining)
    pad = jnp.zeros((128 - n_rank, tq), F32)
    sel_t = jnp.concatenate([jnp.where(blk < n_blocks, sel_add, 0.0), pad], axis=0) if n_rank < 128 else sel_add
    sel_ref[...] = sel_t.T.astype(sel_ref.dtype)
    per_tile = SLC_KEY_TILE // SLC_BLOCK
    blk_any = jnp.max(jnp.where(sel_add == 0.0, 1, 0), axis=1, keepdims=True)
    rows = []
    for j in range(8):
        if (j + 1) * per_tile <= n_rank:
            hit = jnp.max(blk_any[j * per_tile:(j + 1) * per_tile], axis=0, keepdims=True)
            rows.append(jnp.broadcast_to(hit, (1, 128)))
        else:
            rows.append(jnp.zeros((1, 128), I32))
    need_ref[...] = jnp.concatenate(rows, axis=0)


def _cmp_select(projm, cmp_kv, overlap, B, S, tq=512):
    T = B * S
    G, d = B_GROUPS, B_D
    nq = S // tq
    nch = cmp_kv.shape[3]
    n_cmp = (S - CMP_LEN) // CMP_STRIDE + 1
    n_blocks = S // SLC_BLOCK
    assert SLC_TOPN <= n_blocks <= FEAT_BLK
    slopes = _alibi_slopes(B_HEADS)
    grid_spec = pltpu.PrefetchScalarGridSpec(
        num_scalar_prefetch=1,
        grid=(B, G, nq),
        in_specs=[pl.BlockSpec((tq, B_HPG * d), lambda b, g, i, sl: (b * nq + i, g)),
                  pl.BlockSpec((None, None, None, nch, d), lambda b, g, i, sl: (0, b, g, 0, 0)),
                  pl.BlockSpec((None, None, None, nch, d), lambda b, g, i, sl: (1, b, g, 0, 0)),
                  pl.BlockSpec((128, nch), lambda b, g, i, sl: (0, 0))],
        out_specs=(pl.BlockSpec((tq, B_HPG * d), lambda b, g, i, sl: (b * nq + i, g)),
                   pl.BlockSpec((None, None, tq, 128), lambda b, g, i, sl: (b, g, i, 0)),
                   pl.BlockSpec((None, None, None, 8, 128), lambda b, g, i, sl: (b, g, i, 0, 0))),
    )
    assert S // SLC_KEY_TILE <= 8
    return pl.pallas_call(
        functools.partial(_cmp_select_kernel, tq=tq, n_cmp=n_cmp, n_blocks=n_blocks,
                          n_rank=-(-n_blocks // 8) * 8),
        out_shape=(jax.ShapeDtypeStruct((T, B_HEADS * d), BF16),
                   jax.ShapeDtypeStruct((B, G, S, 128), BF16),
                   jax.ShapeDtypeStruct((B, G, nq, 8, 128), I32)),
        grid_spec=grid_spec,
        compiler_params=_cparams(("parallel", "parallel", "arbitrary")),
        name="nsa_cmp_select",
    )(slopes, projm, cmp_kv, cmp_kv, overlap)


def _head_units(k_ref, v_ref, feat_ref, qa_ref, j, mask, qc):
    n_heads, tq, _ = qa_ref.shape
    k0 = pl.multiple_of(j * tq, tq)
    v = v_ref[pl.ds(k0, tq), :]

    def score_fn(hh, c):
        def fn():
            ka = jnp.concatenate([k_ref[pl.ds(k0, tq), :], feat_ref[pl.ds(k0, tq), :]], axis=1)
            sT = _nt_dot(ka, qa_ref[hh, c * qc:(c + 1) * qc, :])
            return sT if mask is None else sT + mask[:, c * qc:(c + 1) * qc]
        return fn

    return [(score_fn(hh, c), v, hh * (tq // qc) + c) for hh in range(n_heads) for c in range(tq // qc)]


def _write_heads_T(o_ref, acc_ref, l_ref):
    tq = o_ref.shape[0]
    qc = l_ref.shape[-1]
    for hh in range(B_HPG):
        for c in range(tq // qc):
            u = hh * (tq // qc) + c
            o_ref[c * qc:(c + 1) * qc, hh * B_D:(hh + 1) * B_D] = (acc_ref[u] / l_ref[u]).T.astype(o_ref.dtype)


def _flash_state(n_heads, tq, dv):
    qc = min(ATTN_QC, tq)
    n_units = n_heads * (tq // qc)
    return [pltpu.VMEM((n_units, 1, qc), F32), pltpu.VMEM((n_units, 1, qc), F32),
            pltpu.VMEM((n_units, dv, qc), F32)]


def _stage_queries(q_ref, qa_ref, slopes_ref, g, q0, sel_add):
    tq = q_ref.shape[0]
    lane = lax.broadcasted_iota(I32, (tq, 128), 1)
    for hh in range(B_HPG):
        coef = jnp.broadcast_to(_alibi_query_features(slopes_ref[g * B_HPG + hh] * LOG2E, q0), (tq, 128))
        if sel_add is not None:
            coef = jnp.where(lane < FEAT_BLK, sel_add, coef)
        qa_ref[hh] = jnp.concatenate([q_ref[:, hh * B_D:(hh + 1) * B_D], coef.astype(BF16)], axis=1)


def _slc_attn_kernel(slopes_ref, need_ref, q_ref, k_ref, v_ref, feat_ref, sel_ref, o_ref, qa_ref, m_ref, l_ref,
                     acc_ref, *, tq):
    b = pl.program_id(0)
    g = pl.program_id(1)
    i = pl.program_id(2)
    n = pl.num_programs(2)
    _flash_init(m_ref, l_ref, acc_ref)
    _stage_queries(q_ref, qa_ref, slopes_ref, g, i * tq, sel_ref[...].astype(F32))

    qc = m_ref.shape[-1]

    def units(j, causal=None):
        return _head_units(k_ref, v_ref, feat_ref, qa_ref, j, causal, qc)

    need_base = ((b * pl.num_programs(1) + g) * n + i) * n

    def needed(j):
        return (j == 0) | (need_ref[need_base + jnp.maximum(j, 0)] > 0)

    def body(p, carry):
        j0, j1 = 2 * p, 2 * p + 1
        n0, n1 = needed(j0), needed(j1)

        @pl.when(n0 & n1)
        def _():
            _flash_units(units(j0) + units(j1), m_ref, l_ref, acc_ref)

        @pl.when(n0 & jnp.logical_not(n1))
        def _():
            _flash_units(units(j0), m_ref, l_ref, acc_ref)

        @pl.when(jnp.logical_not(n0) & n1)
        def _():
            _flash_units(units(j1), m_ref, l_ref, acc_ref)

        return carry

    lax.fori_loop(0, lax.shift_right_logical(i, 1), body, 0)

    @pl.when((i % 2 == 1) & needed(i - 1))
    def _():
        _flash_units(units(i - 1), m_ref, l_ref, acc_ref)

    _flash_units(units(i, _tri_mask(tq, tq, True)), m_ref, l_ref, acc_ref)
    _write_heads_T(o_ref, acc_ref, l_ref)


def _slc_attention(projm, sel, need, B, S):
    T = B * S
    G, d = B_GROUPS, B_D
    tq = SLC_KEY_TILE
    n = S // tq
    slopes = _alibi_slopes(B_HEADS)
    ksb = (B_HEADS * d + 2 * G * d) // d
    vsb = ksb + G
    grid_spec = pltpu.PrefetchScalarGridSpec(
        num_scalar_prefetch=2,
        grid=(B, G, n),
        in_specs=[
            pl.BlockSpec((tq, B_HPG * d), lambda b, g, i, sl, nd: (b * n + i, g)),
            pl.BlockSpec((S, d), lambda b, g, i, sl, nd: (b, ksb + g)),
            pl.BlockSpec((S, d), lambda b, g, i, sl, nd: (b, vsb + g)),
            pl.BlockSpec((S, 128), lambda b, g, i, sl, nd: (0, 0)),
            pl.BlockSpec((None, None, tq, 128), lambda b, g, i, sl, nd: (b, g, i, 0)),
        ],
        out_specs=pl.BlockSpec((tq, B_HPG * d), lambda b, g, i, sl, nd: (b * n + i, g)),
        scratch_shapes=[pltpu.VMEM((B_HPG, tq, 2 * d), BF16)] + _flash_state(B_HPG, tq, d),
    )
    return pl.pallas_call(
        functools.partial(_slc_attn_kernel, tq=tq),
        out_shape=jax.ShapeDtypeStruct((T, B_HEADS * d), BF16),
        grid_spec=grid_spec,
        compiler_params=_cparams(("parallel", "parallel", "arbitrary")),
        name="nsa_selected_attention",
    )(slopes, need.reshape(-1), projm, projm, projm, _key_features(S), sel)


def _win_attn_kernel(slopes_ref, q_ref, k_ref, v_ref, feat_ref, o_ref, qa_ref, m_ref, l_ref, acc_ref, *, tq):
    g = pl.program_id(1)
    i = pl.program_id(2)
    _flash_init(m_ref, l_ref, acc_ref)
    _stage_queries(q_ref, qa_ref, slopes_ref, g, i * tq, None)

    qc = m_ref.shape[-1]

    def units(j, mask):
        return _head_units(k_ref, v_ref, feat_ref, qa_ref, j, mask, qc)

    @pl.when(i > 0)
    def _():
        _flash_units(units(i - 1, _tri_mask(tq, tq, False)) + units(i, _tri_mask(tq, tq, True)),
                     m_ref, l_ref, acc_ref)

    @pl.when(i == 0)
    def _():
        _flash_units(units(i, _tri_mask(tq, tq, True)), m_ref, l_ref, acc_ref)

    _write_heads_T(o_ref, acc_ref, l_ref)


def _win_attention(projm, B, S):
    T = B * S
    G, d = B_GROUPS, B_D
    tq = WINDOW
    n = S // tq
    slopes = _alibi_slopes(B_HEADS)
    kwb = (B_HEADS * d + 4 * G * d) // d
    vwb = kwb + G
    grid_spec = pltpu.PrefetchScalarGridSpec(
        num_scalar_prefetch=1,
        grid=(B, G, n),
        in_specs=[
            pl.BlockSpec((tq, B_HPG * d), lambda b, g, i, sl: (b * n + i, g)),
            pl.BlockSpec((S, d), lambda b, g, i, sl: (b, kwb + g)),
            pl.BlockSpec((S, d), lambda b, g, i, sl: (b, vwb + g)),
            pl.BlockSpec((S, 128), lambda b, g, i, sl: (0, 0)),
        ],
        out_specs=pl.BlockSpec((tq, B_HPG * d), lambda b, g, i, sl: (b * n + i, g)),
        scratch_shapes=[pltpu.VMEM((B_HPG, tq, 2 * d), BF16)] + _flash_state(B_HPG, tq, d),
    )
    return pl.pallas_call(
        functools.partial(_win_attn_kernel, tq=tq),
        out_shape=jax.ShapeDtypeStruct((T, B_HEADS * d), BF16),
        grid_spec=grid_spec,
        compiler_params=_cparams(("parallel", "parallel", "arbitrary")),
        name="nsa_window_attention",
    )(slopes, projm, projm, projm, _key_features(S))


def _gate_mix_kernel(gl_ref, oc_ref, os_ref, ow_ref, o_ref):
    sig = jax.nn.sigmoid(gl_ref[...])
    d = B_D
    for hd in range(B_HEADS):
        sl = slice(hd * d, (hd + 1) * d)
        o = (sig[:, 3 * hd:3 * hd + 1] * oc_ref[:, sl] + sig[:, 3 * hd + 1:3 * hd + 2] * os_ref[:, sl]
             + sig[:, 3 * hd + 2:3 * hd + 3] * ow_ref[:, sl])
        o_ref[:, sl] = o.astype(o_ref.dtype)


def _gate_mix(gate_logits, o_cmp, o_slc, o_win, tm=256):
    T, W = o_cmp.shape
    spec = pl.BlockSpec((tm, W), lambda i: (i, 0))
    return pl.pallas_call(
        _gate_mix_kernel,
        out_shape=jax.ShapeDtypeStruct((T, W), BF16),
        grid=(T // tm,),
        in_specs=[pl.BlockSpec((tm, 128), lambda i: (i, 0)), spec, spec, spec],
        out_specs=spec,
        compiler_params=_cparams(("parallel",)),
        name="nsa_gate_mix",
    )(gate_logits, o_cmp, o_slc, o_win)


def _overlap_matrix(S, n_rows):
    nc = (S - CMP_LEN) // CMP_STRIDE + 1
    nsb = S // SLC_BLOCK
    cs = np.arange(nc) * CMP_STRIDE
    ss = np.arange(nsb) * SLC_BLOCK
    ov = np.clip(np.minimum(cs[:, None] + CMP_LEN, ss[None, :] + SLC_BLOCK)
                 - np.maximum(cs[:, None], ss[None, :]), 0, None) / CMP_LEN
    out = np.zeros((128, n_rows), np.float32)
    out[:nsb, :nc] = ov.T
    return jnp.asarray(out, BF16)


def _nsa_attention(xn, w_in, cmp_pos, cmp_w1, cmp_w2, B, S):
    T = B * S
    H, G, d = B_HEADS, B_GROUPS, B_D
    n_main = H * d + 6 * G * d
    col_scale = jnp.concatenate([jnp.full((H * d,), d ** -0.5 * LOG2E, F32), jnp.ones((n_main - H * d,), F32)])
    projm = _matmul(xn, w_in, n_main, BF16, col_scale=col_scale)
    w_gate = jnp.pad(w_in[:, n_main:], ((0, 0), (0, 128 - 3 * H)))
    gate_logits = _matmul(xn, w_gate, 128, F32, tn=128)

    cmp_kv = _compress(projm, cmp_w1, cmp_w2, cmp_pos.reshape(2, 1, CMP_LEN * d), B, S)
    o_cmp, sel, need = _cmp_select(projm, cmp_kv, _overlap_matrix(S, S // CMP_STRIDE), B, S)
    n_t = S // SLC_KEY_TILE
    need = need[:, :, :, :n_t, 0].reshape(B, G, n_t, -1, n_t).max(axis=3)
    o_slc = _slc_attention(projm, sel, need, B, S)
    o_win = _win_attention(projm, B, S)
    return _gate_mix(gate_logits, o_cmp, o_slc, o_win)


def kernel(x, attn_norm_w, ffn_norm_w, final_norm_w, a_w_in, a_lambda, a_subln_w, a_w_out, b_w_in,
           b_cmp_pos, b_cmp_w1, b_cmp_w2, b_w_out, moe_w_group, moe_b_group, moe_w_expert,
           moe_b_expert, moe_w_gate, moe_w_up, moe_w_down):
    B, S, D = x.shape
    T = B * S
    depth = attn_norm_w.shape[0]
    h = x.reshape(T, D)
    xn = _rmsnorm(h, attn_norm_w[0], BF16)
    for i in range(depth):
        j = i // 2
        if i % 2 == 0:
            lambda_init = 0.8 - 0.6 * math.exp(-0.3 * i)
            n_q = A_HEADS * 2 * A_DH
            n_in = 2 * n_q + A_HEADS * A_DV
            col_scale = jnp.concatenate([jnp.full((n_q,), A_DH ** -0.5 * LOG2E, F32),
                                         jnp.ones((n_in - n_q,), F32)])
            qkv = _matmul(xn, a_w_in[j], n_in, BF16, col_scale=col_scale)
            o = _diff_attention(qkv, a_lambda[j], a_subln_w[j], lambda_init, B, S)
            h = _matmul(o, a_w_out[j], D, F32, resid=h)
        else:
            o = _nsa_attention(xn, b_w_in[j], b_cmp_pos[j], b_cmp_w1[j], b_cmp_w2[j], B, S)
            h = _matmul(o, b_w_out[j], D, F32, resid=h)
        last = i == depth - 1
        h, xn = _hier_moe(h, ffn_norm_w[i], moe_w_group[i], moe_b_group[i], moe_w_expert[i], moe_b_expert[i],
                          moe_w_gate, moe_w_up, moe_w_down, i,
                          final_norm_w if last else attn_norm_w[i + 1], F32 if last else BF16, not last)
    return xn.reshape(B, S, D)
```

```python
import functools
import math

import numpy as np
import jax
import jax.numpy as jnp
from jax import lax
from jax.experimental import pallas as pl
from jax.experimental.pallas import tpu as pltpu

F32 = jnp.float32
BF16 = jnp.bfloat16
I32 = jnp.int32

EPS = 1e-6
NEG = -1e30
FORCE = 1e4
LOG2E = math.log2(math.e)

A_HEADS = 8
A_DH = 128
A_DV = 256
B_HEADS = 16
B_GROUPS = 4
B_HPG = 4
B_D = 128
CMP_LEN = 32
CMP_STRIDE = 16
SLC_BLOCK = 64
SLC_TOPN = 16
WINDOW = 512
N_GROUPS = 4
EPG = 8
N_EXPERTS = 32
D_EXPERT = 1024
ROUTER_LANES = 128
MOE_ROWS = 256
DMA_UNROLL = 8
ATTN_QC = 512
SLC_KEY_TILE = 512

VMEM_LIMIT = 56 * 1024 * 1024


def _cparams(sem, flags=None):
    return pltpu.CompilerParams(dimension_semantics=sem, vmem_limit_bytes=VMEM_LIMIT, flags=flags)


def _alibi_slopes(n):
    return jnp.asarray(np.array([2.0 ** (-8.0 * (i + 1) / n) for i in range(n)], dtype=np.float32))


def _nt_dot(a, b):
    return lax.dot_general(a, b, (((1,), (1,)), ((), ())), preferred_element_type=F32)


def _rms_kernel(x_ref, w_ref, o_ref):
    x = x_ref[...]
    y = x * lax.rsqrt(jnp.mean(x * x, axis=-1, keepdims=True) + EPS)
    o_ref[...] = (y * w_ref[...]).astype(o_ref.dtype)


def _rows_spec(x, tm, tn, col_arg):
    if x.ndim == 2:
        return pl.BlockSpec((tm, tn), lambda *g: (g[-1], col_arg(g)))
    per_seq = x.shape[1] // tm
    return pl.BlockSpec((None, tm, tn), lambda *g: (g[-1] // per_seq, g[-1] % per_seq, col_arg(g)))


def _rmsnorm(x, w, out_dtype, tm=512):
    D = x.shape[-1]
    T = x.size // D
    return pl.pallas_call(
        _rms_kernel,
        out_shape=jax.ShapeDtypeStruct((T, D), out_dtype),
        grid=(T // tm,),
        in_specs=[_rows_spec(x, tm, D, lambda g: 0),
                  pl.BlockSpec((1, D), lambda i: (0, 0))],
        out_specs=pl.BlockSpec((tm, D), lambda i: (i, 0)),
        compiler_params=_cparams(("parallel",)),
        name="rmsnorm",
    )(x, w.reshape(1, D))


def _mm_kernel(*refs, has_scale, has_resid):
    a_ref, w_ref = refs[0], refs[1]
    k = 2
    scale_ref = resid_ref = None
    if has_scale:
        scale_ref = refs[k]
        k += 1
    if has_resid:
        resid_ref = refs[k]
        k += 1
    o_ref, wbf_ref = refs[k], refs[k + 1]

    @pl.when(pl.program_id(1) == 0)
    def _():
        wbf_ref[...] = w_ref[...].astype(BF16)

    acc = jnp.dot(a_ref[...], wbf_ref[...], preferred_element_type=F32)
    if has_scale:
        acc = acc * scale_ref[...]
    if has_resid:
        acc = acc + resid_ref[...]
    o_ref[...] = acc.astype(o_ref.dtype)


def _matmul(a, w, n_out, out_dtype, *, col_scale=None, resid=None, tm=1024, tn=512):
    M, K = a.shape
    assert M % tm == 0 and n_out % tn == 0 and w.shape[0] == K and w.shape[1] >= n_out
    in_specs = [pl.BlockSpec((tm, K), lambda j, i: (i, 0)),
                pl.BlockSpec((K, tn), lambda j, i: (0, j))]
    args = [a, w]
    if col_scale is not None:
        in_specs.append(pl.BlockSpec((1, tn), lambda j, i: (0, j)))
        args.append(col_scale.reshape(1, n_out))
    if resid is not None:
        in_specs.append(_rows_spec(resid, tm, tn, lambda g: g[0]))
        args.append(resid)
    return pl.pallas_call(
        functools.partial(_mm_kernel, has_scale=col_scale is not None, has_resid=resid is not None),
        out_shape=jax.ShapeDtypeStruct((M, n_out), out_dtype),
        grid=(n_out // tn, M // tm),
        in_specs=in_specs,
        out_specs=pl.BlockSpec((tm, tn), lambda j, i: (i, j)),
        scratch_shapes=[pltpu.VMEM((K, tn), BF16)],
        compiler_params=_cparams(("parallel", "arbitrary")),
        name="matmul",
    )(*args)


FEAT_BLK = 64
FEAT_POS = 67
FEAT_ONE = 70


def _key_features(S):
    assert S // SLC_BLOCK <= FEAT_BLK
    j = np.arange(S)
    f = np.zeros((S, 128), np.float32)
    f[j, j // SLC_BLOCK] = 1.0
    f[:, FEAT_BLK:FEAT_BLK + 3] = (j // SLC_BLOCK)[:, None]
    f[:, FEAT_POS:FEAT_POS + 3] = (j % SLC_BLOCK)[:, None]
    f[:, FEAT_ONE:FEAT_ONE + 3] = 1.0
    return jnp.asarray(f, BF16)


def _alibi_query_features(slope2, q0):
    lane = lax.broadcasted_iota(I32, (1, 128), 1)
    base = jnp.where((lane >= FEAT_BLK) & (lane < FEAT_BLK + 3), slope2 * float(SLC_BLOCK),
                     jnp.where((lane >= FEAT_POS) & (lane < FEAT_POS + 3), slope2,
                               jnp.where((lane >= FEAT_ONE) & (lane < FEAT_ONE + 3),
                                         -slope2 * q0.astype(F32), 0.0)))
    hi = base.astype(BF16).astype(F32)
    r1 = base - hi
    lo = r1.astype(BF16).astype(F32)
    lo2 = (r1 - lo).astype(BF16).astype(F32)
    first = (lane == FEAT_BLK) | (lane == FEAT_POS) | (lane == FEAT_ONE)
    second = (lane == FEAT_BLK + 1) | (lane == FEAT_POS + 1) | (lane == FEAT_ONE + 1)
    return jnp.where(first, hi, jnp.where(second, lo, lo2))


def _flash_init(m_ref, l_ref, acc_ref):
    m_ref[...] = jnp.full(m_ref.shape, NEG, F32)
    l_ref[...] = jnp.zeros(l_ref.shape, F32)
    acc_ref[...] = jnp.zeros(acc_ref.shape, F32)


def _flash_probs(sT, m_ref, l_ref, u):
    m_old = m_ref[u]
    m_new = jnp.maximum(m_old, jnp.max(sT, axis=0, keepdims=True))
    alpha = jnp.exp2(m_old - m_new)
    p = jnp.exp2(sT - m_new)
    l_ref[u] = alpha * l_ref[u] + jnp.sum(p, axis=0, keepdims=True)
    m_ref[u] = m_new
    return p.astype(BF16), alpha


def _flash_accumulate(p, alpha, v, acc_ref, u):
    pv = lax.dot_general(v, p, (((0,), (0,)), ((), ())), preferred_element_type=F32)
    acc_ref[u] = alpha * acc_ref[u] + pv


def _flash_units(units, m_ref, l_ref, acc_ref):
    n = len(units)
    scores, probs = {}, {}
    for s in range(n + 2):
        if s < n:
            scores[s] = units[s][0]()
        if 1 <= s <= n:
            probs[s - 1] = _flash_probs(scores.pop(s - 1), m_ref, l_ref, units[s - 1][2])
        if 2 <= s <= n + 1:
            p, alpha = probs.pop(s - 2)
            _flash_accumulate(p, alpha, units[s - 2][1], acc_ref, units[s - 2][2])


def _tri_mask(tk, tq, keep_upper):
    r = lax.broadcasted_iota(I32, (tk, tq), 0)
    c = lax.broadcasted_iota(I32, (tk, tq), 1)
    keep = (r <= c) if keep_upper else (r > c)
    return jnp.where(keep, 0.0, NEG)


def _diff_attn_kernel(slopes_ref, q1_ref, q2_ref, k1_ref, k2_ref, v_ref, feat_ref, lam_ref, sw_ref, o_ref,
                      qa_ref, m_ref, l_ref, acc_ref, *, tq, lambda_init):
    h = pl.program_id(1)
    i = pl.program_id(2)
    _flash_init(m_ref, l_ref, acc_ref)
    qfeat = jnp.broadcast_to(_alibi_query_features(slopes_ref[h] * LOG2E, i * tq), (tq, 128)).astype(BF16)
    for c, q_ref in enumerate((q1_ref, q2_ref)):
        qa_ref[c] = jnp.concatenate([q_ref[...], qfeat], axis=1)

    def tile_units(j, mask):
        k0 = pl.multiple_of(j * tq, tq)
        v = v_ref[pl.ds(k0, tq), :]
        feat = feat_ref[pl.ds(k0, tq), :]
        qc = m_ref.shape[-1]

        def score_fn(c, k_ref, x):
            def fn():
                ka = jnp.concatenate([k_ref[pl.ds(k0, tq), :], feat], axis=1)
                sT = _nt_dot(ka, qa_ref[c, x * qc:(x + 1) * qc, :])
                return sT if mask is None else sT + mask[:, x * qc:(x + 1) * qc]
            return fn

        return [(score_fn(c, k_ref, x), v, c * (tq // qc) + x)
                for c, k_ref in enumerate((k1_ref, k2_ref)) for x in range(tq // qc)]

    def body(p, carry):
        _flash_units(tile_units(2 * p, None) + tile_units(2 * p + 1, None), m_ref, l_ref, acc_ref)
        return carry

    lax.fori_loop(0, lax.shift_right_logical(i, 1), body, 0)

    @pl.when(i % 2 == 1)
    def _():
        _flash_units(tile_units(i - 1, None), m_ref, l_ref, acc_ref)

    _flash_units(tile_units(i, _tri_mask(tq, tq, True)), m_ref, l_ref, acc_ref)

    lam = lam_ref[...]
    lmbda = (jnp.exp(jnp.sum(lam[0:1] * lam[1:2], axis=-1, keepdims=True))
             - jnp.exp(jnp.sum(lam[2:3] * lam[3:4], axis=-1, keepdims=True)) + lambda_init)
    qc = m_ref.shape[-1]
    nx = tq // qc
    for x in range(nx):
        oT = acc_ref[x] / l_ref[x] - lmbda * (acc_ref[nx + x] / l_ref[nx + x])
        o = oT.T
        y = o * lax.rsqrt(jnp.mean(o * o, axis=-1, keepdims=True) + EPS)
        y = (y * sw_ref[...]) * (1.0 - lambda_init)
        o_ref[x * qc:(x + 1) * qc, :] = y.astype(o_ref.dtype)


def _diff_attention(qkv, lam, subln_w, lambda_init, B, S, tq=512):
    T = B * S
    H, dh, dv = A_HEADS, A_DH, A_DV
    n = S // tq
    slopes = _alibi_slopes(H)
    kb = H * 2
    vb = (2 * H * 2 * dh) // dv
    grid_spec = pltpu.PrefetchScalarGridSpec(
        num_scalar_prefetch=1,
        grid=(B, H, n),
        in_specs=[
            pl.BlockSpec((tq, dh), lambda b, h, i, sl: (b * n + i, 2 * h)),
            pl.BlockSpec((tq, dh), lambda b, h, i, sl: (b * n + i, 2 * h + 1)),
            pl.BlockSpec((S, dh), lambda b, h, i, sl: (b, kb + 2 * h)),
            pl.BlockSpec((S, dh), lambda b, h, i, sl: (b, kb + 2 * h + 1)),
            pl.BlockSpec((S, dv), lambda b, h, i, sl: (b, vb + h)),
            pl.BlockSpec((S, 128), lambda b, h, i, sl: (0, 0)),
            pl.BlockSpec((4, dh), lambda b, h, i, sl: (0, 0)),
            pl.BlockSpec((1, dv), lambda b, h, i, sl: (0, 0)),
        ],
        out_specs=pl.BlockSpec((tq, dv), lambda b, h, i, sl: (b * n + i, h)),
        scratch_shapes=[pltpu.VMEM((2, tq, 2 * dh), BF16)] + _flash_state(2, tq, dv),
    )
    return pl.pallas_call(
        functools.partial(_diff_attn_kernel, tq=tq, lambda_init=lambda_init),
        out_shape=jax.ShapeDtypeStruct((T, H * dv), BF16),
        grid_spec=grid_spec,
        compiler_params=_cparams(("parallel", "parallel", "arbitrary")),
        name="diff_attention",
    )(slopes, qkv, qkv, qkv, qkv, qkv, _key_features(S), lam, subln_w.reshape(1, dv))


def _ffn_norm(h_ref, nw_ref):
    x = h_ref[...]
    return (x * lax.rsqrt(jnp.mean(x * x, axis=-1, keepdims=True) + EPS)) * nw_ref[...]


def _router_kernel(h_ref, nw_ref, wr_ref, br_ref, tri_ref, ids_ref, wts_ref, cnt_ref, base_ref):
    @pl.when(pl.program_id(0) == 0)
    def _():
        base_ref[...] = jnp.zeros(base_ref.shape, F32)

    xn = _ffn_norm(h_ref, nw_ref)
    w = wr_ref[...]
    x_hi = xn.astype(BF16)
    x_lo = (xn - x_hi.astype(F32)).astype(BF16)
    w_hi = w.astype(BF16)
    w_lo = (w - w_hi.astype(F32)).astype(BF16)
    logits = (jnp.dot(x_hi, w_hi, preferred_element_type=F32)
              + (jnp.dot(x_hi, w_lo, preferred_element_type=F32)
                 + jnp.dot(x_lo, w_hi, preferred_element_type=F32))) + br_ref[...]
    lane = lax.broadcasted_iota(I32, logits.shape, 1)
    big = jnp.int32(1 << 20)
    is_g = lane < N_GROUPS
    is_e = (lane >= N_GROUPS) & (lane < N_GROUPS + N_EXPERTS)

    lg = jnp.where(is_g, logits, NEG)
    mg = jnp.max(lg, axis=-1, keepdims=True)
    g_top = jnp.min(jnp.where(lg == mg, lane, big), axis=-1, keepdims=True)
    wg = 1.0 / jnp.sum(jnp.where(is_g, jnp.exp(lg - mg), 0.0), axis=-1, keepdims=True)

    in_grp = is_e & (lax.shift_right_logical(lane - N_GROUPS, 3) == g_top)
    le = jnp.where(in_grp, logits, NEG)
    me = jnp.max(le, axis=-1, keepdims=True)
    ee = jnp.where(in_grp, jnp.exp(le - me), 0.0)
    pe = ee / jnp.sum(ee, axis=-1, keepdims=True)
    pe = jnp.where(in_grp, pe, -1.0)
    p1 = jnp.max(pe, axis=-1, keepdims=True)
    i1 = jnp.min(jnp.where(pe == p1, lane, big), axis=-1, keepdims=True)
    pe2 = jnp.where(lane == i1, -1.0, pe)
    p2 = jnp.max(pe2, axis=-1, keepdims=True)
    i2 = jnp.min(jnp.where(pe2 == p2, lane, big), axis=-1, keepdims=True)
    den = p1 + p2
    w0 = wg * p1 / den
    w1 = wg * p2 / den

    oh1 = lane == i1
    oh2 = lane == i2
    both = jnp.where(oh1 | oh2, 1.0, 0.0)
    before = jnp.dot(tri_ref[...], both.astype(BF16), preferred_element_type=F32) + base_ref[...]
    pos0 = jnp.sum(jnp.where(oh1, before, 0.0), axis=-1, keepdims=True).astype(I32)
    pos1 = jnp.sum(jnp.where(oh2, before, 0.0), axis=-1, keepdims=True).astype(I32)
    total = base_ref[...] + jnp.sum(both, axis=0, keepdims=True)
    base_ref[...] = total
    cnt_ref[...] = total

    ids = jnp.where(lane == 0, i1 - N_GROUPS,
                    jnp.where(lane == 1, i2 - N_GROUPS,
                              jnp.where(lane == 2, pos0, jnp.where(lane == 3, pos1, 0))))
    ids_ref[...] = ids.T[:8]
    wts_ref[...] = jnp.where(lane == 0, w0, jnp.where(lane == 1, w1, 0.0))


def _router(h, norm_w, w_group, b_group, w_expert, b_expert, tm=512):
    T, D = h.shape
    pad = ROUTER_LANES - N_GROUPS - N_EXPERTS
    wr = jnp.concatenate([w_group, w_expert, jnp.zeros((D, pad), F32)], axis=1)
    br = jnp.concatenate([b_group, b_expert, jnp.zeros((pad,), F32)]).reshape(1, ROUTER_LANES)
    tri = jnp.asarray(np.tril(np.ones((tm, tm), np.float32), -1), BF16)
    return pl.pallas_call(
        _router_kernel,
        out_shape=(jax.ShapeDtypeStruct((8, T), I32),
                   jax.ShapeDtypeStruct((T, ROUTER_LANES), F32),
                   jax.ShapeDtypeStruct((1, ROUTER_LANES), F32)),
        grid=(T // tm,),
        in_specs=[pl.BlockSpec((tm, D), lambda i: (i, 0)),
                  pl.BlockSpec((1, D), lambda i: (0, 0)),
                  pl.BlockSpec((D, ROUTER_LANES), lambda i: (0, 0)),
                  pl.BlockSpec((1, ROUTER_LANES), lambda i: (0, 0)),
                  pl.BlockSpec((tm, tm), lambda i: (0, 0))],
        out_specs=(pl.BlockSpec((8, tm), lambda i: (0, i)),
                   pl.BlockSpec((tm, ROUTER_LANES), lambda i: (i, 0)),
                   pl.BlockSpec((1, ROUTER_LANES), lambda i: (0, 0))),
        scratch_shapes=[pltpu.VMEM((1, ROUTER_LANES), F32)],
        compiler_params=_cparams(("arbitrary",)),
        name="moe_router",
    )(h, norm_w.reshape(1, D), wr, br, tri)


def _dispatch_kernel(dest_ref, zero_ref, h_ref, nw_ref, xs_ref, xn_ref, zbuf_ref, sem, zsem, *, tm, n_tok):
    xn_ref[...] = _ffn_norm(h_ref, nw_ref)

    @pl.when(pl.program_id(0) == 0)
    def _():
        zbuf_ref[...] = jnp.zeros(zbuf_ref.shape, zbuf_ref.dtype)
        rb = zbuf_ref.shape[0]

        def fill(r):
            return pltpu.make_async_copy(zbuf_ref, xs_ref.at[pl.ds(r * rb, rb)], zsem)

        for r in range(zero_ref.shape[0]):
            @pl.when(zero_ref[r] == 1)
            def _():
                fill(r).start()
        for r in range(zero_ref.shape[0]):
            @pl.when(zero_ref[r] == 1)
            def _():
                fill(r).wait()

    base = pl.program_id(0) * tm
    for t in range(tm):
        for slot in range(2):
            d = dest_ref[slot * n_tok + base + t]
            pltpu.make_async_copy(xn_ref.at[pl.ds(t, 1)], xs_ref.at[pl.ds(d, 1)], sem).start(priority=slot)
    for slot in range(2):
        pltpu.make_async_copy(xn_ref, xs_ref.at[pl.ds(0, tm)], sem).wait()


def _dispatch(h, norm_w, dest_flat, zero_blk, n_rows, tm=256):
    T, D = h.shape
    grid_spec = pltpu.PrefetchScalarGridSpec(
        num_scalar_prefetch=2,
        grid=(T // tm,),
        in_specs=[pl.BlockSpec((tm, D), lambda i, d, z: (i, 0)),
                  pl.BlockSpec((1, D), lambda i, d, z: (0, 0))],
        out_specs=pl.BlockSpec(memory_space=pl.ANY),
        scratch_shapes=[pltpu.VMEM((tm, D), F32), pltpu.VMEM((MOE_ROWS, D), F32),
                        pltpu.SemaphoreType.DMA, pltpu.SemaphoreType.DMA],
    )
    return pl.pallas_call(
        functools.partial(_dispatch_kernel, tm=tm, n_tok=T),
        out_shape=jax.ShapeDtypeStruct((n_rows, D), F32),
        grid_spec=grid_spec,
        compiler_params=_cparams(("arbitrary",)),
        name="moe_dispatch",
    )(dest_flat, zero_blk, h, norm_w.reshape(1, D))


def _moe_plan(counts, n_blk):
    RB = MOE_ROWS
    nb_e = (counts + RB - 1) // RB
    end_b = jnp.cumsum(nb_e)
    start_b = end_b - nb_e
    n_used = end_b[-1:].astype(I32)
    r = jnp.arange(n_blk, dtype=I32)
    be = jnp.minimum(jnp.sum((end_b[None, :] <= r[:, None]).astype(I32), axis=1), N_EXPERTS - 1)
    active = nb_e > 0
    ordinal = jnp.cumsum(active.astype(I32)) - 1
    idx = jnp.arange(N_EXPERTS, dtype=I32)
    later = jnp.where(active[None, :] & (idx[None, :] > idx[:, None]), idx[None, :], N_EXPERTS)
    nxt_e = jnp.min(later, axis=1)
    nxt_e = jnp.where(nxt_e == N_EXPERTS, -1, nxt_e)
    first = ((r == start_b[be]) & (r < n_used[0])).astype(I32)
    plan = jnp.stack([be, first, ordinal[be] % 2, nxt_e[be]]).astype(I32)
    zero_blk = ((r == end_b[be] - 1) | (r >= n_used[0])).astype(I32)
    return plan, n_used, (start_b * RB).astype(I32), zero_blk


def _stream_expert_weights(plan_ref, r, w_hbms, wbuf, sem, layer):
    e, slot, nxt = plan_ref[0, r], plan_ref[2, r], plan_ref[3, r]

    def copies(expert, s):
        rows = wbuf.shape[2] // 2
        return [(pltpu.make_async_copy(w.at[layer, expert, pl.ds(half * rows, rows)],
                                       wbuf.at[s, k, pl.ds(half * rows, rows)], sem.at[s, k, half]), half)
                for k, w in enumerate(w_hbms) for half in range(2)]

    @pl.when(plan_ref[1, r] == 1)
    def _():
        @pl.when(r == 0)
        def _():
            for c, half in copies(e, slot):
                c.start(priority=half)

        for c, _ in copies(e, slot):
            c.wait()

        @pl.when(nxt >= 0)
        def _():
            for c, half in copies(nxt, 1 - slot):
                c.start(priority=half)

    return slot


def _gate_up_kernel(plan_ref, nu_ref, x_ref, wg_hbm, wu_hbm, h_ref, wbuf, sem, *, layer):
    r = pl.program_id(0)

    @pl.when(r < nu_ref[0])
    def _():
        slot = _stream_expert_weights(plan_ref, r, (wg_hbm, wu_hbm), wbuf, sem, layer)
        x = x_ref[...]
        g = jnp.dot(x, wbuf[slot, 0], preferred_element_type=F32)
        u = jnp.dot(x, wbuf[slot, 1], preferred_element_type=F32)
        h_ref[...] = (jax.nn.silu(g) * u).astype(h_ref.dtype)

    @pl.when(r >= nu_ref[0])
    def _():
        h_ref[...] = jnp.zeros(h_ref.shape, h_ref.dtype)


def _down_kernel(plan_ref, nu_ref, h_ref, wd_hbm, y_ref, wbuf, sem, *, layer):
    r = pl.program_id(0)

    @pl.when(r < nu_ref[0])
    def _():
        slot = _stream_expert_weights(plan_ref, r, (wd_hbm,), wbuf, sem, layer)
        y_ref[...] = jnp.dot(h_ref[...].astype(F32), wbuf[slot, 0], preferred_element_type=F32)

    @pl.when(r >= nu_ref[0])
    def _():
        y_ref[...] = jnp.zeros(y_ref.shape, y_ref.dtype)


def _expert_mlp(xs, plan, n_used, w_gate, w_up, w_down, layer):
    n_rows, D = xs.shape
    RB = MOE_ROWS
    n_blk = n_rows // RB
    F = D_EXPERT

    def used_blk(r, plan, nu):
        return (jnp.minimum(r, nu[0] - 1), 0)

    h = pl.pallas_call(
        functools.partial(_gate_up_kernel, layer=layer),
        out_shape=jax.ShapeDtypeStruct((n_rows, F), BF16),
        grid_spec=pltpu.PrefetchScalarGridSpec(
            num_scalar_prefetch=2,
            grid=(n_blk,),
            in_specs=[pl.BlockSpec((RB, D), used_blk),
                      pl.BlockSpec(memory_space=pl.ANY),
                      pl.BlockSpec(memory_space=pl.ANY)],
            out_specs=pl.BlockSpec((RB, F), lambda r, plan, nu: (r, 0)),
            scratch_shapes=[pltpu.VMEM((2, 2, D, F), F32), pltpu.SemaphoreType.DMA((2, 2, 2))],
        ),
        compiler_params=_cparams(("arbitrary",)),
        name="moe_gate_up",
    )(plan, n_used, xs, w_gate, w_up)
    y = pl.pallas_call(
        functools.partial(_down_kernel, layer=layer),
        out_shape=jax.ShapeDtypeStruct((n_rows, D), F32),
        grid_spec=pltpu.PrefetchScalarGridSpec(
            num_scalar_prefetch=2,
            grid=(n_blk,),
            in_specs=[pl.BlockSpec((RB, F), used_blk),
                      pl.BlockSpec(memory_space=pl.ANY)],
            out_specs=pl.BlockSpec((RB, D), lambda r, plan, nu: (r, 0)),
            scratch_shapes=[pltpu.VMEM((2, 1, F, D), F32), pltpu.SemaphoreType.DMA((2, 1, 2))],
        ),
        compiler_params=_cparams(("arbitrary",)),
        name="moe_down",
    )(plan, n_used, h, w_down)
    return y


def _combine_kernel(dest_ref, resid_ref, wts_ref, nw_ref, y_ref, *rest, tm, n_tok, keep_h):
    if keep_h:
        o_ref, n_ref, buf_ref, sem = rest
    else:
        o_ref = None
        n_ref, buf_ref, sem = rest
    i = pl.program_id(0)
    n = pl.num_programs(0)

    def gather(tile, par):
        for t in range(tm):
            for slot in range(2):
                d = dest_ref[slot * n_tok + tile * tm + t]
                pltpu.make_async_copy(y_ref.at[pl.ds(d, 1)], buf_ref.at[par, slot, pl.ds(t, 1)],
                                      sem.at[par]).start(priority=slot)

    @pl.when(i == 0)
    def _():
        gather(i, 0)

    for par in range(2):
        @pl.when((i + 1 < n) & ((i + 1) % 2 == par))
        def _():
            gather(i + 1, par)

    for par in range(2):
        @pl.when(i % 2 == par)
        def _():
            for slot in range(2):
                pltpu.make_async_copy(y_ref.at[pl.ds(0, tm)], buf_ref.at[par, slot], sem.at[par]).wait()
            w = wts_ref[...]
            hn = resid_ref[...] + (w[:, 0:1] * buf_ref[par, 0] + w[:, 1:2] * buf_ref[par, 1])
            if keep_h:
                o_ref[...] = hn
            yn = hn * lax.rsqrt(jnp.mean(hn * hn, axis=-1, keepdims=True) + EPS)
            n_ref[...] = (yn * nw_ref[...]).astype(n_ref.dtype)


def _combine(resid, wts, y, dest_flat, next_norm_w, norm_dtype, keep_h, norm_shape=None, tm=128):
    T, D = resid.shape
    row_spec = pl.BlockSpec((tm, D), lambda i, d: (i, 0))
    if norm_shape is None:
        out_shape = [jax.ShapeDtypeStruct((T, D), norm_dtype)]
        out_specs = [row_spec]
    else:
        per_seq = norm_shape[1] // tm
        out_shape = [jax.ShapeDtypeStruct(norm_shape, norm_dtype)]
        out_specs = [pl.BlockSpec((None, tm, D), lambda i, d: (i // per_seq, i % per_seq, 0))]
    if keep_h:
        out_shape.insert(0, jax.ShapeDtypeStruct((T, D), F32))
        out_specs.insert(0, row_spec)
    grid_spec = pltpu.PrefetchScalarGridSpec(
        num_scalar_prefetch=1,
        grid=(T // tm,),
        in_specs=[row_spec,
                  pl.BlockSpec((tm, ROUTER_LANES), lambda i, d: (i, 0)),
                  pl.BlockSpec((1, D), lambda i, d: (0, 0)),
                  pl.BlockSpec(memory_space=pl.ANY)],
        out_specs=tuple(out_specs),
        scratch_shapes=[pltpu.VMEM((2, 2, tm, D), F32), pltpu.SemaphoreType.DMA((2,))],
    )
    outs = pl.pallas_call(
        functools.partial(_combine_kernel, tm=tm, n_tok=T, keep_h=keep_h),
        out_shape=tuple(out_shape),
        grid_spec=grid_spec,
        compiler_params=_cparams(("arbitrary",)),
        name="moe_combine",
    )(dest_flat, resid, wts, next_norm_w.reshape(1, D), y)
    return (outs[0], outs[1]) if keep_h else (None, outs[0])


def _hier_moe(h, norm_w, w_group, b_group, w_expert, b_expert, w_gate, w_up, w_down, layer,
              next_norm_w, norm_dtype, keep_h, norm_shape=None):
    T, D = h.shape
    RB = MOE_ROWS
    n_blk = (T * 2) // RB + N_EXPERTS
    n_rows = n_blk * RB
    ids, wts, cnt = _router(h, norm_w, w_group, b_group, w_expert, b_expert)
    counts = cnt[0, N_GROUPS:N_GROUPS + N_EXPERTS].astype(I32)
    plan, n_used, start_padded, zero_blk = _moe_plan(counts, n_blk)
    is_e = ids[None, 0:2] == jnp.arange(N_EXPERTS, dtype=I32)[:, None, None]
    dest = (jnp.sum(jnp.where(is_e, start_padded[:, None, None], 0), axis=0) + ids[2:4]).reshape(-1)
    xs = _dispatch(h, norm_w, dest, zero_blk, n_rows)
    y = _expert_mlp(xs, plan, n_used, w_gate, w_up, w_down, layer)
    return _combine(h, wts, y, dest, next_norm_w, norm_dtype, keep_h, norm_shape)


def _compress_kernel(x_ref, w1_ref, w2_ref, pe_ref, o_ref, xf_ref):
    d = B_D
    half = CMP_STRIDE * d
    nch = x_ref.shape[0] // CMP_STRIDE
    xf_ref[...] = x_ref[...].astype(F32)
    w1 = w1_ref[...].astype(BF16)
    a = jnp.zeros((nch, w1.shape[1]), F32)
    b = jnp.zeros((nch, w1.shape[1]), F32)
    for l in range(CMP_STRIDE):
        xl = xf_ref[pl.ds(l, nch, stride=CMP_STRIDE), :].astype(BF16)
        a = a + jnp.dot(xl, w1[l * d:(l + 1) * d], preferred_element_type=F32)
        b = b + jnp.dot(xl, w1[half + l * d:half + (l + 1) * d], preferred_element_type=F32)
    pe = jnp.broadcast_to(pe_ref[...], (8, 2 * half)).astype(BF16)
    pt = jnp.dot(pe, w1, preferred_element_type=F32)[0:1]
    hid = a + pltpu.roll(b, b.shape[0] - 1, 0) + pt
    g = jax.nn.gelu(hid)
    o_ref[...] = jnp.dot(g.astype(BF16), w2_ref[...].astype(BF16),
                         preferred_element_type=F32).astype(o_ref.dtype)


def _compress(projm, w1, w2, pe_flat, B, S):
    G, d = B_GROUPS, B_D
    nch = S // CMP_STRIDE
    width = CMP_STRIDE * d
    hid = w1.shape[-1]
    cb = B_HEADS
    return pl.pallas_call(
        _compress_kernel,
        out_shape=jax.ShapeDtypeStruct((2, B, G, nch, d), BF16),
        grid=(2, B, G),
        in_specs=[pl.BlockSpec((S, d), lambda kv, b, g: (b, cb + kv * G + g)),
                  pl.BlockSpec((None, 2 * width, hid), lambda kv, b, g: (kv, 0, 0)),
                  pl.BlockSpec((None, hid, d), lambda kv, b, g: (kv, 0, 0)),
                  pl.BlockSpec((None, 1, 2 * width), lambda kv, b, g: (kv, 0, 0))],
        out_specs=pl.BlockSpec((None, None, None, nch, d), lambda kv, b, g: (kv, b, g, 0, 0)),
        scratch_shapes=[pltpu.VMEM((S, d), F32)],
        compiler_params=_cparams(("arbitrary", "arbitrary", "arbitrary")),
        name="nsa_compress",
    )(projm, w1, w2, pe_flat)


def _cmp_select_kernel(slopes_ref, q_ref, kc_ref, vc_ref, ov_ref, o_ref, sel_ref, need_ref, *, tq, n_cmp,
                       n_blocks, n_rank):
    g = pl.program_id(1)
    i = pl.program_id(2)
    d = B_D
    nch = kc_ref.shape[0]
    pos = i * tq + lax.broadcasted_iota(I32, (1, tq), 1)
    c = lax.broadcasted_iota(I32, (nch, 1), 0)
    dist_i = pos - (c * CMP_STRIDE + (CMP_LEN - 1))
    valid = (dist_i >= 0) & (c < n_cmp)
    dist = dist_i.astype(F32)
    any_valid = jnp.where(pos >= CMP_LEN - 1, 1.0, 0.0)
    kc = kc_ref[...]
    vc = vc_ref[...]
    psum = jnp.zeros((nch, tq), F32)
    for hh in range(B_HPG):
        s = _nt_dot(kc, q_ref[:, hh * d:(hh + 1) * d])
        s = jnp.where(valid, s - (slopes_ref[g * B_HPG + hh] * LOG2E) * dist, NEG)
        e = jnp.exp2(s - jnp.max(s, axis=0, keepdims=True))
        p = (e / jnp.sum(e, axis=0, keepdims=True)) * any_valid
        oT = lax.dot_general(vc, p.astype(BF16), (((0,), (0,)), ((), ())), preferred_element_type=F32)
        o_ref[:, hh * d:(hh + 1) * d] = oT.T.astype(o_ref.dtype)
        psum = psum + p

    p_hi = psum.astype(BF16)
    p_lo = (psum - p_hi.astype(F32)).astype(BF16)
    ov = ov_ref[...]
    imp = jnp.dot(ov, p_hi, preferred_element_type=F32) + jnp.dot(ov, p_lo, preferred_element_type=F32)
    imp = imp[:n_rank]

    blk = lax.broadcasted_iota(I32, (n_rank, 1), 0)
    cur = lax.shift_right_logical(pos, SLC_BLOCK.bit_length() - 1)
    forced = (blk == 0) | (blk == cur) | (blk == cur - 1)
    causal = blk * SLC_BLOCK <= pos
    score = jnp.where(forced, FORCE, jnp.where(causal, imp, -FORCE))
    score = jnp.where(blk < n_blocks, score, -2.0 * FORCE)
    blk_f = blk.astype(F32)
    remaining = score
    sel_add = jnp.full(score.shape, NEG, F32)
    for _ in range(SLC_TOPN):
        top = jnp.max(remaining, axis=0, keepdims=True)
        first = jnp.min(jnp.where(remaining == top, blk_f, float(n_rank)), axis=0, keepdims=True)
        pick = blk_f == first
        sel_add = jnp.where(pick, 0.0, sel_add)
        remaining = jnp.where(pick, -4.0 * FORCE, remaining)
    pad = jnp.zeros((128 - n_rank, tq), F32)
    sel_t = jnp.concatenate([jnp.where(blk < n_blocks, sel_add, 0.0), pad], axis=0) if n_rank < 128 else sel_add
    sel_ref[...] = sel_t.T.astype(sel_ref.dtype)
    per_tile = SLC_KEY_TILE // SLC_BLOCK
    blk_any = jnp.max(jnp.where(sel_add == 0.0, 1, 0), axis=1, keepdims=True)
    rows = []
    for j in range(8):
        if (j + 1) * per_tile <= n_rank:
            hit = jnp.max(blk_any[j * per_tile:(j + 1) * per_tile], axis=0, keepdims=True)
            rows.append(jnp.broadcast_to(hit, (1, 128)))
        else:
            rows.append(jnp.zeros((1, 128), I32))
    need_ref[...] = jnp.concatenate(rows, axis=0)


def _cmp_select(projm, cmp_kv, overlap, B, S, tq=512):
    T = B * S
    G, d = B_GROUPS, B_D
    nq = S // tq
    nch = cmp_kv.shape[3]
    n_cmp = (S - CMP_LEN) // CMP_STRIDE + 1
    n_blocks = S // SLC_BLOCK
    assert SLC_TOPN <= n_blocks <= FEAT_BLK
    slopes = _alibi_slopes(B_HEADS)
    grid_spec = pltpu.PrefetchScalarGridSpec(
        num_scalar_prefetch=1,
        grid=(B, G, nq),
        in_specs=[pl.BlockSpec((tq, B_HPG * d), lambda b, g, i, sl: (b * nq + i, g)),
                  pl.BlockSpec((None, None, None, nch, d), lambda b, g, i, sl: (0, b, g, 0, 0)),
                  pl.BlockSpec((None, None, None, nch, d), lambda b, g, i, sl: (1, b, g, 0, 0)),
                  pl.BlockSpec((128, nch), lambda b, g, i, sl: (0, 0))],
        out_specs=(pl.BlockSpec((tq, B_HPG * d), lambda b, g, i, sl: (b * nq + i, g)),
                   pl.BlockSpec((None, None, tq, 128), lambda b, g, i, sl: (b, g, i, 0)),
                   pl.BlockSpec((None, None, None, 8, 128), lambda b, g, i, sl: (b, g, i, 0, 0))),
    )
    assert S // SLC_KEY_TILE <= 8
    return pl.pallas_call(
        functools.partial(_cmp_select_kernel, tq=tq, n_cmp=n_cmp, n_blocks=n_blocks,
                          n_rank=-(-n_blocks // 8) * 8),
        out_shape=(jax.ShapeDtypeStruct((T, B_HEADS * d), BF16),
                   jax.ShapeDtypeStruct((B, G, S, 128), BF16),
                   jax.ShapeDtypeStruct((B, G, nq, 8, 128), I32)),
        grid_spec=grid_spec,
        compiler_params=_cparams(("parallel", "parallel", "arbitrary")),
        name="nsa_cmp_select",
    )(slopes, projm, cmp_kv, cmp_kv, overlap)


def _head_units(k_ref, v_ref, feat_ref, qa_ref, j, mask, qc):
    n_heads, tq, _ = qa_ref.shape
    k0 = pl.multiple_of(j * tq, tq)
    v = v_ref[pl.ds(k0, tq), :]

    def score_fn(hh, c):
        def fn():
            ka = jnp.concatenate([k_ref[pl.ds(k0, tq), :], feat_ref[pl.ds(k0, tq), :]], axis=1)
            sT = _nt_dot(ka, qa_ref[hh, c * qc:(c + 1) * qc, :])
            return sT if mask is None else sT + mask[:, c * qc:(c + 1) * qc]
        return fn

    return [(score_fn(hh, c), v, hh * (tq // qc) + c) for hh in range(n_heads) for c in range(tq // qc)]


def _write_heads_T(o_ref, acc_ref, l_ref):
    tq = o_ref.shape[0]
    qc = l_ref.shape[-1]
    for hh in range(B_HPG):
        for c in range(tq // qc):
            u = hh * (tq // qc) + c
            o_ref[c * qc:(c + 1) * qc, hh * B_D:(hh + 1) * B_D] = (acc_ref[u] / l_ref[u]).T.astype(o_ref.dtype)


def _flash_state(n_heads, tq, dv):
    qc = min(ATTN_QC, tq)
    n_units = n_heads * (tq // qc)
    return [pltpu.VMEM((n_units, 1, qc), F32), pltpu.VMEM((n_units, 1, qc), F32),
            pltpu.VMEM((n_units, dv, qc), F32)]


def _stage_queries(q_ref, qa_ref, slopes_ref, g, q0, sel_add):
    tq = q_ref.shape[0]
    lane = lax.broadcasted_iota(I32, (tq, 128), 1)
    for hh in range(B_HPG):
        coef = jnp.broadcast_to(_alibi_query_features(slopes_ref[g * B_HPG + hh] * LOG2E, q0), (tq, 128))
        if sel_add is not None:
            coef = jnp.where(lane < FEAT_BLK, sel_add, coef)
        qa_ref[hh] = jnp.concatenate([q_ref[:, hh * B_D:(hh + 1) * B_D], coef.astype(BF16)], axis=1)


def _slc_attn_kernel(slopes_ref, need_ref, q_ref, k_ref, v_ref, feat_ref, sel_ref, o_ref, qa_ref, m_ref, l_ref,
                     acc_ref, *, tq):
    b = pl.program_id(0)
    g = pl.program_id(1)
    i = pl.program_id(2)
    n = pl.num_programs(2)
    _flash_init(m_ref, l_ref, acc_ref)
    _stage_queries(q_ref, qa_ref, slopes_ref, g, i * tq, sel_ref[...].astype(F32))

    qc = m_ref.shape[-1]

    def units(j, causal=None):
        return _head_units(k_ref, v_ref, feat_ref, qa_ref, j, causal, qc)

    need_base = ((b * pl.num_programs(1) + g) * n + i) * n

    def needed(j):
        return (j == 0) | (need_ref[need_base + jnp.maximum(j, 0)] > 0)

    def body(p, carry):
        j0, j1 = 2 * p, 2 * p + 1
        n0, n1 = needed(j0), needed(j1)

        @pl.when(n0 & n1)
        def _():
            _flash_units(units(j0) + units(j1), m_ref, l_ref, acc_ref)

        @pl.when(n0 & jnp.logical_not(n1))
        def _():
            _flash_units(units(j0), m_ref, l_ref, acc_ref)

        @pl.when(jnp.logical_not(n0) & n1)
        def _():
            _flash_units(units(j1), m_ref, l_ref, acc_ref)

        return carry

    lax.fori_loop(0, lax.shift_right_logical(i, 1), body, 0)

    @pl.when((i % 2 == 1) & needed(i - 1))
    def _():
        _flash_units(units(i - 1), m_ref, l_ref, acc_ref)

    _flash_units(units(i, _tri_mask(tq, tq, True)), m_ref, l_ref, acc_ref)
    _write_heads_T(o_ref, acc_ref, l_ref)


def _slc_attention(projm, sel, need, B, S):
    T = B * S
    G, d = B_GROUPS, B_D
    tq = SLC_KEY_TILE
    n = S // tq
    slopes = _alibi_slopes(B_HEADS)
    ksb = (B_HEADS * d + 2 * G * d) // d
    vsb = ksb + G
    grid_spec = pltpu.PrefetchScalarGridSpec(
        num_scalar_prefetch=2,
        grid=(B, G, n),
        in_specs=[
            pl.BlockSpec((tq, B_HPG * d), lambda b, g, i, sl, nd: (b * n + i, g)),
            pl.BlockSpec((S, d), lambda b, g, i, sl, nd: (b, ksb + g)),
            pl.BlockSpec((S, d), lambda b, g, i, sl, nd: (b, vsb + g)),
            pl.BlockSpec((S, 128), lambda b, g, i, sl, nd: (0, 0)),
            pl.BlockSpec((None, None, tq, 128), lambda b, g, i, sl, nd: (b, g, i, 0)),
        ],
        out_specs=pl.BlockSpec((tq, B_HPG * d), lambda b, g, i, sl, nd: (b * n + i, g)),
        scratch_shapes=[pltpu.VMEM((B_HPG, tq, 2 * d), BF16)] + _flash_state(B_HPG, tq, d),
    )
    return pl.pallas_call(
        functools.partial(_slc_attn_kernel, tq=tq),
        out_shape=jax.ShapeDtypeStruct((T, B_HEADS * d), BF16),
        grid_spec=grid_spec,
        compiler_params=_cparams(("parallel", "parallel", "arbitrary")),
        name="nsa_selected_attention",
    )(slopes, need.reshape(-1), projm, projm, projm, _key_features(S), sel)


def _win_attn_kernel(slopes_ref, q_ref, k_ref, v_ref, feat_ref, o_ref, qa_ref, m_ref, l_ref, acc_ref, *, tq):
    g = pl.program_id(1)
    i = pl.program_id(2)
    _flash_init(m_ref, l_ref, acc_ref)
    _stage_queries(q_ref, qa_ref, slopes_ref, g, i * tq, None)

    qc = m_ref.shape[-1]

    def units(j, mask):
        return _head_units(k_ref, v_ref, feat_ref, qa_ref, j, mask, qc)

    @pl.when(i > 0)
    def _():
        _flash_units(units(i - 1, _tri_mask(tq, tq, False)) + units(i, _tri_mask(tq, tq, True)),
                     m_ref, l_ref, acc_ref)

    @pl.when(i == 0)
    def _():
        _flash_units(units(i, _tri_mask(tq, tq, True)), m_ref, l_ref, acc_ref)

    _write_heads_T(o_ref, acc_ref, l_ref)


def _win_attention(projm, B, S):
    T = B * S
    G, d = B_GROUPS, B_D
    tq = WINDOW
    n = S // tq
    slopes = _alibi_slopes(B_HEADS)
    kwb = (B_HEADS * d + 4 * G * d) // d
    vwb = kwb + G
    grid_spec = pltpu.PrefetchScalarGridSpec(
        num_scalar_prefetch=1,
        grid=(B, G, n),
        in_specs=[
            pl.BlockSpec((tq, B_HPG * d), lambda b, g, i, sl: (b * n + i, g)),
            pl.BlockSpec((S, d), lambda b, g, i, sl: (b, kwb + g)),
            pl.BlockSpec((S, d), lambda b, g, i, sl: (b, vwb + g)),
            pl.BlockSpec((S, 128), lambda b, g, i, sl: (0, 0)),
        ],
        out_specs=pl.BlockSpec((tq, B_HPG * d), lambda b, g, i, sl: (b * n + i, g)),
        scratch_shapes=[pltpu.VMEM((B_HPG, tq, 2 * d), BF16)] + _flash_state(B_HPG, tq, d),
    )
    return pl.pallas_call(
        functools.partial(_win_attn_kernel, tq=tq),
        out_shape=jax.ShapeDtypeStruct((T, B_HEADS * d), BF16),
        grid_spec=grid_spec,
        compiler_params=_cparams(("parallel", "parallel", "arbitrary")),
        name="nsa_window_attention",
    )(slopes, projm, projm, projm, _key_features(S))


def _gate_mix_kernel(gl_ref, oc_ref, os_ref, ow_ref, o_ref):
    sig = jax.nn.sigmoid(gl_ref[...])
    d = B_D
    for hd in range(B_HEADS):
        sl = slice(hd * d, (hd + 1) * d)
        o = (sig[:, 3 * hd:3 * hd + 1] * oc_ref[:, sl] + sig[:, 3 * hd + 1:3 * hd + 2] * os_ref[:, sl]
             + sig[:, 3 * hd + 2:3 * hd + 3] * ow_ref[:, sl])
        o_ref[:, sl] = o.astype(o_ref.dtype)


def _gate_mix(gate_logits, o_cmp, o_slc, o_win, tm=256):
    T, W = o_cmp.shape
    spec = pl.BlockSpec((tm, W), lambda i: (i, 0))
    return pl.pallas_call(
        _gate_mix_kernel,
        out_shape=jax.ShapeDtypeStruct((T, W), BF16),
        grid=(T // tm,),
        in_specs=[pl.BlockSpec((tm, 128), lambda i: (i, 0)), spec, spec, spec],
        out_specs=spec,
        compiler_params=_cparams(("parallel",)),
        name="nsa_gate_mix",
    )(gate_logits, o_cmp, o_slc, o_win)


def _overlap_matrix(S, n_rows):
    nc = (S - CMP_LEN) // CMP_STRIDE + 1
    nsb = S // SLC_BLOCK
    cs = np.arange(nc) * CMP_STRIDE
    ss = np.arange(nsb) * SLC_BLOCK
    ov = np.clip(np.minimum(cs[:, None] + CMP_LEN, ss[None, :] + SLC_BLOCK)
                 - np.maximum(cs[:, None], ss[None, :]), 0, None) / CMP_LEN
    out = np.zeros((128, n_rows), np.float32)
    out[:nsb, :nc] = ov.T
    return jnp.asarray(out, BF16)


def _nsa_attention(xn, w_in, cmp_pos, cmp_w1, cmp_w2, B, S):
    T = B * S
    H, G, d = B_HEADS, B_GROUPS, B_D
    n_main = H * d + 6 * G * d
    col_scale = jnp.concatenate([jnp.full((H * d,), d ** -0.5 * LOG2E, F32), jnp.ones((n_main - H * d,), F32)])
    projm = _matmul(xn, w_in, n_main, BF16, col_scale=col_scale)
    w_gate = jnp.pad(w_in[:, n_main:], ((0, 0), (0, 128 - 3 * H)))
    gate_logits = _matmul(xn, w_gate, 128, F32, tn=128)

    cmp_kv = _compress(projm, cmp_w1, cmp_w2, cmp_pos.reshape(2, 1, CMP_LEN * d), B, S)
    o_cmp, sel, need = _cmp_select(projm, cmp_kv, _overlap_matrix(S, S // CMP_STRIDE), B, S)
    n_t = S // SLC_KEY_TILE
    need = need[:, :, :, :n_t, 0].reshape(B, G, n_t, -1, n_t).max(axis=3)
    o_slc = _slc_attention(projm, sel, need, B, S)
    o_win = _win_attention(projm, B, S)
    return _gate_mix(gate_logits, o_cmp, o_slc, o_win)


def kernel(x, attn_norm_w, ffn_norm_w, final_norm_w, a_w_in, a_lambda, a_subln_w, a_w_out, b_w_in,
           b_cmp_pos, b_cmp_w1, b_cmp_w2, b_w_out, moe_w_group, moe_b_group, moe_w_expert,
           moe_b_expert, moe_w_gate, moe_w_up, moe_w_down):
    B, S, D = x.shape
    T = B * S
    depth = attn_norm_w.shape[0]
    h = x
    xn = _rmsnorm(h, attn_norm_w[0], BF16)
    for i in range(depth):
        j = i // 2
        if i % 2 == 0:
            lambda_init = 0.8 - 0.6 * math.exp(-0.3 * i)
            n_q = A_HEADS * 2 * A_DH
            n_in = 2 * n_q + A_HEADS * A_DV
            col_scale = jnp.concatenate([jnp.full((n_q,), A_DH ** -0.5 * LOG2E, F32),
                                         jnp.ones((n_in - n_q,), F32)])
            qkv = _matmul(xn, a_w_in[j], n_in, BF16, col_scale=col_scale)
            o = _diff_attention(qkv, a_lambda[j], a_subln_w[j], lambda_init, B, S)
            h = _matmul(o, a_w_out[j], D, F32, resid=h)
        else:
            o = _nsa_attention(xn, b_w_in[j], b_cmp_pos[j], b_cmp_w1[j], b_cmp_w2[j], B, S)
            h = _matmul(o, b_w_out[j], D, F32, resid=h)
        last = i == depth - 1
        h, xn = _hier_moe(h, ffn_norm_w[i], moe_w_group[i], moe_b_group[i], moe_w_expert[i], moe_b_expert[i],
                          moe_w_gate, moe_w_up, moe_w_down, i,
                          final_norm_w if last else attn_norm_w[i + 1], F32 if last else BF16, not last,
                          (B, S, D) if last else None)
    return xn
```

```python
import functools
import math

import numpy as np
import jax
import jax.numpy as jnp
from jax import lax
from jax.experimental import pallas as pl
from jax.experimental.pallas import tpu as pltpu

F32 = jnp.float32
BF16 = jnp.bfloat16
I32 = jnp.int32

EPS = 1e-6
NEG = -1e30
FORCE = 1e4
LOG2E = math.log2(math.e)

A_HEADS = 8
A_DH = 128
A_DV = 256
B_HEADS = 16
B_GROUPS = 4
B_HPG = 4
B_D = 128
CMP_LEN = 32
CMP_STRIDE = 16
SLC_BLOCK = 64
SLC_TOPN = 16
WINDOW = 512
N_GROUPS = 4
EPG = 8
N_EXPERTS = 32
D_EXPERT = 1024
ROUTER_LANES = 128
MOE_ROWS = 256
DMA_UNROLL = 8
ATTN_QC = 512
SLC_KEY_TILE = 512

VMEM_LIMIT = 56 * 1024 * 1024


def _cparams(sem, flags=None):
    return pltpu.CompilerParams(dimension_semantics=sem, vmem_limit_bytes=VMEM_LIMIT, flags=flags)


def _alibi_slopes(n):
    return jnp.asarray(np.array([2.0 ** (-8.0 * (i + 1) / n) for i in range(n)], dtype=np.float32))


def _nt_dot(a, b):
    return lax.dot_general(a, b, (((1,), (1,)), ((), ())), preferred_element_type=F32)


def _rms_kernel(x_ref, w_ref, o_ref):
    x = x_ref[...]
    y = x * lax.rsqrt(jnp.mean(x * x, axis=-1, keepdims=True) + EPS)
    o_ref[...] = (y * w_ref[...]).astype(o_ref.dtype)


def _rows_spec(x, tm, tn, col_arg):
    if x.ndim == 2:
        return pl.BlockSpec((tm, tn), lambda *g: (g[-1], col_arg(g)))
    per_seq = x.shape[1] // tm
    return pl.BlockSpec((None, tm, tn), lambda *g: (g[-1] // per_seq, g[-1] % per_seq, col_arg(g)))


def _rmsnorm(x, w, out_dtype, tm=512):
    D = x.shape[-1]
    T = x.size // D
    return pl.pallas_call(
        _rms_kernel,
        out_shape=jax.ShapeDtypeStruct((T, D), out_dtype),
        grid=(T // tm,),
        in_specs=[_rows_spec(x, tm, D, lambda g: 0),
                  pl.BlockSpec((1, D), lambda i: (0, 0))],
        out_specs=pl.BlockSpec((tm, D), lambda i: (i, 0)),
        compiler_params=_cparams(("parallel",)),
        name="rmsnorm",
    )(x, w.reshape(1, D))


def _mm_kernel(*refs, has_scale, has_resid, w_is_transposed):
    a_ref, w_ref = refs[0], refs[1]
    k = 2
    scale_ref = resid_ref = None
    if has_scale:
        scale_ref = refs[k]
        k += 1
    if has_resid:
        resid_ref = refs[k]
        k += 1
    o_ref, wbf_ref = refs[k], refs[k + 1]

    @pl.when(pl.program_id(1) == 0)
    def _():
        wbf_ref[...] = w_ref[...].astype(BF16)

    if w_is_transposed:
        acc = _nt_dot(a_ref[...], wbf_ref[...])
    else:
        acc = jnp.dot(a_ref[...], wbf_ref[...], preferred_element_type=F32)
    if has_scale:
        acc = acc * scale_ref[...]
    if has_resid:
        acc = acc + resid_ref[...]
    o_ref[...] = acc.astype(o_ref.dtype)


def _matmul(a, w, n_out, out_dtype, *, col_scale=None, resid=None, tm=1024, tn=512, w_is_transposed=False):
    M, K = a.shape
    if w_is_transposed:
        assert M % tm == 0 and n_out % tn == 0 and w.shape[1] == K and w.shape[0] >= n_out
        w_spec, w_block = pl.BlockSpec((tn, K), lambda j, i: (j, 0)), (tn, K)
    else:
        assert M % tm == 0 and n_out % tn == 0 and w.shape[0] == K and w.shape[1] >= n_out
        w_spec, w_block = pl.BlockSpec((K, tn), lambda j, i: (0, j)), (K, tn)
    in_specs = [pl.BlockSpec((tm, K), lambda j, i: (i, 0)), w_spec]
    args = [a, w]
    if col_scale is not None:
        in_specs.append(pl.BlockSpec((1, tn), lambda j, i: (0, j)))
        args.append(col_scale.reshape(1, n_out))
    if resid is not None:
        in_specs.append(_rows_spec(resid, tm, tn, lambda g: g[0]))
        args.append(resid)
    return pl.pallas_call(
        functools.partial(_mm_kernel, has_scale=col_scale is not None, has_resid=resid is not None,
                          w_is_transposed=w_is_transposed),
        out_shape=jax.ShapeDtypeStruct((M, n_out), out_dtype),
        grid=(n_out // tn, M // tm),
        in_specs=in_specs,
        out_specs=pl.BlockSpec((tm, tn), lambda j, i: (i, j)),
        scratch_shapes=[pltpu.VMEM(w_block, BF16)],
        compiler_params=_cparams(("parallel", "arbitrary")),
        name="matmul",
    )(*args)


FEAT_BLK = 64
FEAT_POS = 67
FEAT_ONE = 70


def _key_features(S):
    assert S // SLC_BLOCK <= FEAT_BLK
    j = np.arange(S)
    f = np.zeros((S, 128), np.float32)
    f[j, j // SLC_BLOCK] = 1.0
    f[:, FEAT_BLK:FEAT_BLK + 3] = (j // SLC_BLOCK)[:, None]
    f[:, FEAT_POS:FEAT_POS + 3] = (j % SLC_BLOCK)[:, None]
    f[:, FEAT_ONE:FEAT_ONE + 3] = 1.0
    return jnp.asarray(f, BF16)


def _alibi_query_features(slope2, q0):
    lane = lax.broadcasted_iota(I32, (1, 128), 1)
    base = jnp.where((lane >= FEAT_BLK) & (lane < FEAT_BLK + 3), slope2 * float(SLC_BLOCK),
                     jnp.where((lane >= FEAT_POS) & (lane < FEAT_POS + 3), slope2,
                               jnp.where((lane >= FEAT_ONE) & (lane < FEAT_ONE + 3),
                                         -slope2 * q0.astype(F32), 0.0)))
    hi = base.astype(BF16).astype(F32)
    r1 = base - hi
    lo = r1.astype(BF16).astype(F32)
    lo2 = (r1 - lo).astype(BF16).astype(F32)
    first = (lane == FEAT_BLK) | (lane == FEAT_POS) | (lane == FEAT_ONE)
    second = (lane == FEAT_BLK + 1) | (lane == FEAT_POS + 1) | (lane == FEAT_ONE + 1)
    return jnp.where(first, hi, jnp.where(second, lo, lo2))


def _flash_init(m_ref, l_ref, acc_ref):
    m_ref[...] = jnp.full(m_ref.shape, NEG, F32)
    l_ref[...] = jnp.zeros(l_ref.shape, F32)
    acc_ref[...] = jnp.zeros(acc_ref.shape, F32)


def _flash_probs(sT, m_ref, l_ref, u):
    m_old = m_ref[u]
    m_new = jnp.maximum(m_old, jnp.max(sT, axis=0, keepdims=True))
    alpha = jnp.exp2(m_old - m_new)
    p = jnp.exp2(sT - m_new)
    l_ref[u] = alpha * l_ref[u] + jnp.sum(p, axis=0, keepdims=True)
    m_ref[u] = m_new
    return p.astype(BF16), alpha


def _flash_accumulate(p, alpha, v, acc_ref, u):
    pv = lax.dot_general(v, p, (((0,), (0,)), ((), ())), preferred_element_type=F32)
    acc_ref[u] = alpha * acc_ref[u] + pv


def _flash_units(units, m_ref, l_ref, acc_ref):
    n = len(units)
    scores, probs = {}, {}
    for s in range(n + 2):
        if s < n:
            scores[s] = units[s][0]()
        if 1 <= s <= n:
            probs[s - 1] = _flash_probs(scores.pop(s - 1), m_ref, l_ref, units[s - 1][2])
        if 2 <= s <= n + 1:
            p, alpha = probs.pop(s - 2)
            _flash_accumulate(p, alpha, units[s - 2][1], acc_ref, units[s - 2][2])


def _tri_mask(tk, tq, keep_upper):
    r = lax.broadcasted_iota(I32, (tk, tq), 0)
    c = lax.broadcasted_iota(I32, (tk, tq), 1)
    keep = (r <= c) if keep_upper else (r > c)
    return jnp.where(keep, 0.0, NEG)


def _diff_attn_kernel(slopes_ref, q1_ref, q2_ref, k1_ref, k2_ref, v_ref, feat_ref, lam_ref, sw_ref, o_ref,
                      qa_ref, m_ref, l_ref, acc_ref, *, tq, lambda_init):
    h = pl.program_id(1)
    i = pl.program_id(2)
    _flash_init(m_ref, l_ref, acc_ref)
    qfeat = jnp.broadcast_to(_alibi_query_features(slopes_ref[h] * LOG2E, i * tq), (tq, 128)).astype(BF16)
    for c, q_ref in enumerate((q1_ref, q2_ref)):
        qa_ref[c] = jnp.concatenate([q_ref[...], qfeat], axis=1)

    def tile_units(j, mask):
        k0 = pl.multiple_of(j * tq, tq)
        v = v_ref[pl.ds(k0, tq), :]
        feat = feat_ref[pl.ds(k0, tq), :]
        qc = m_ref.shape[-1]

        def score_fn(c, k_ref, x):
            def fn():
                ka = jnp.concatenate([k_ref[pl.ds(k0, tq), :], feat], axis=1)
                sT = _nt_dot(ka, qa_ref[c, x * qc:(x + 1) * qc, :])
                return sT if mask is None else sT + mask[:, x * qc:(x + 1) * qc]
            return fn

        return [(score_fn(c, k_ref, x), v, c * (tq // qc) + x)
                for c, k_ref in enumerate((k1_ref, k2_ref)) for x in range(tq // qc)]

    def body(p, carry):
        _flash_units(tile_units(2 * p, None) + tile_units(2 * p + 1, None), m_ref, l_ref, acc_ref)
        return carry

    lax.fori_loop(0, lax.shift_right_logical(i, 1), body, 0)

    @pl.when(i % 2 == 1)
    def _():
        _flash_units(tile_units(i - 1, None), m_ref, l_ref, acc_ref)

    _flash_units(tile_units(i, _tri_mask(tq, tq, True)), m_ref, l_ref, acc_ref)

    lam = lam_ref[...]
    lmbda = (jnp.exp(jnp.sum(lam[0:1] * lam[1:2], axis=-1, keepdims=True))
             - jnp.exp(jnp.sum(lam[2:3] * lam[3:4], axis=-1, keepdims=True)) + lambda_init)
    qc = m_ref.shape[-1]
    nx = tq // qc
    for x in range(nx):
        oT = acc_ref[x] / l_ref[x] - lmbda * (acc_ref[nx + x] / l_ref[nx + x])
        o = oT.T
        y = o * lax.rsqrt(jnp.mean(o * o, axis=-1, keepdims=True) + EPS)
        y = (y * sw_ref[...]) * (1.0 - lambda_init)
        o_ref[x * qc:(x + 1) * qc, :] = y.astype(o_ref.dtype)


def _diff_attention(qkv, lam, subln_w, lambda_init, B, S, tq=512):
    T = B * S
    H, dh, dv = A_HEADS, A_DH, A_DV
    n = S // tq
    slopes = _alibi_slopes(H)
    kb = H * 2
    vb = (2 * H * 2 * dh) // dv
    grid_spec = pltpu.PrefetchScalarGridSpec(
        num_scalar_prefetch=1,
        grid=(B, H, n),
        in_specs=[
            pl.BlockSpec((tq, dh), lambda b, h, i, sl: (b * n + i, 2 * h)),
            pl.BlockSpec((tq, dh), lambda b, h, i, sl: (b * n + i, 2 * h + 1)),
            pl.BlockSpec((S, dh), lambda b, h, i, sl: (b, kb + 2 * h)),
            pl.BlockSpec((S, dh), lambda b, h, i, sl: (b, kb + 2 * h + 1)),
            pl.BlockSpec((S, dv), lambda b, h, i, sl: (b, vb + h)),
            pl.BlockSpec((S, 128), lambda b, h, i, sl: (0, 0)),
            pl.BlockSpec((4, dh), lambda b, h, i, sl: (0, 0)),
            pl.BlockSpec((1, dv), lambda b, h, i, sl: (0, 0)),
        ],
        out_specs=pl.BlockSpec((tq, dv), lambda b, h, i, sl: (b * n + i, h)),
        scratch_shapes=[pltpu.VMEM((2, tq, 2 * dh), BF16)] + _flash_state(2, tq, dv),
    )
    return pl.pallas_call(
        functools.partial(_diff_attn_kernel, tq=tq, lambda_init=lambda_init),
        out_shape=jax.ShapeDtypeStruct((T, H * dv), BF16),
        grid_spec=grid_spec,
        compiler_params=_cparams(("parallel", "parallel", "arbitrary")),
        name="diff_attention",
    )(slopes, qkv, qkv, qkv, qkv, qkv, _key_features(S), lam, subln_w.reshape(1, dv))


def _ffn_norm(h_ref, nw_ref):
    x = h_ref[...]
    return (x * lax.rsqrt(jnp.mean(x * x, axis=-1, keepdims=True) + EPS)) * nw_ref[...]


def _router_kernel(h_ref, nw_ref, wr_ref, br_ref, tri_ref, ids_ref, wts_ref, cnt_ref, base_ref):
    @pl.when(pl.program_id(0) == 0)
    def _():
        base_ref[...] = jnp.zeros(base_ref.shape, F32)

    xn = _ffn_norm(h_ref, nw_ref)
    w = wr_ref[...]
    x_hi = xn.astype(BF16)
    x_lo = (xn - x_hi.astype(F32)).astype(BF16)
    w_hi = w.astype(BF16)
    w_lo = (w - w_hi.astype(F32)).astype(BF16)
    logits = (jnp.dot(x_hi, w_hi, preferred_element_type=F32)
              + (jnp.dot(x_hi, w_lo, preferred_element_type=F32)
                 + jnp.dot(x_lo, w_hi, preferred_element_type=F32))) + br_ref[...]
    lane = lax.broadcasted_iota(I32, logits.shape, 1)
    big = jnp.int32(1 << 20)
    is_g = lane < N_GROUPS
    is_e = (lane >= N_GROUPS) & (lane < N_GROUPS + N_EXPERTS)

    lg = jnp.where(is_g, logits, NEG)
    mg = jnp.max(lg, axis=-1, keepdims=True)
    g_top = jnp.min(jnp.where(lg == mg, lane, big), axis=-1, keepdims=True)
    wg = 1.0 / jnp.sum(jnp.where(is_g, jnp.exp(lg - mg), 0.0), axis=-1, keepdims=True)

    in_grp = is_e & (lax.shift_right_logical(lane - N_GROUPS, 3) == g_top)
    le = jnp.where(in_grp, logits, NEG)
    me = jnp.max(le, axis=-1, keepdims=True)
    ee = jnp.where(in_grp, jnp.exp(le - me), 0.0)
    pe = ee / jnp.sum(ee, axis=-1, keepdims=True)
    pe = jnp.where(in_grp, pe, -1.0)
    p1 = jnp.max(pe, axis=-1, keepdims=True)
    i1 = jnp.min(jnp.where(pe == p1, lane, big), axis=-1, keepdims=True)
    pe2 = jnp.where(lane == i1, -1.0, pe)
    p2 = jnp.max(pe2, axis=-1, keepdims=True)
    i2 = jnp.min(jnp.where(pe2 == p2, lane, big), axis=-1, keepdims=True)
    den = p1 + p2
    w0 = wg * p1 / den
    w1 = wg * p2 / den

    oh1 = lane == i1
    oh2 = lane == i2
    both = jnp.where(oh1 | oh2, 1.0, 0.0)
    before = jnp.dot(tri_ref[...], both.astype(BF16), preferred_element_type=F32) + base_ref[...]
    pos0 = jnp.sum(jnp.where(oh1, before, 0.0), axis=-1, keepdims=True).astype(I32)
    pos1 = jnp.sum(jnp.where(oh2, before, 0.0), axis=-1, keepdims=True).astype(I32)
    total = base_ref[...] + jnp.sum(both, axis=0, keepdims=True)
    base_ref[...] = total
    cnt_ref[...] = total

    ids = jnp.where(lane == 0, i1 - N_GROUPS,
                    jnp.where(lane == 1, i2 - N_GROUPS,
                              jnp.where(lane == 2, pos0, jnp.where(lane == 3, pos1, 0))))
    ids_ref[...] = ids.T[:8]
    wts_ref[...] = jnp.where(lane == 0, w0, jnp.where(lane == 1, w1, 0.0))


def _router(h, norm_w, w_group, b_group, w_expert, b_expert, tm=512):
    T, D = h.shape
    pad = ROUTER_LANES - N_GROUPS - N_EXPERTS
    wr = jnp.concatenate([w_group, w_expert, jnp.zeros((D, pad), F32)], axis=1)
    br = jnp.concatenate([b_group, b_expert, jnp.zeros((pad,), F32)]).reshape(1, ROUTER_LANES)
    tri = jnp.asarray(np.tril(np.ones((tm, tm), np.float32), -1), BF16)
    return pl.pallas_call(
        _router_kernel,
        out_shape=(jax.ShapeDtypeStruct((8, T), I32),
                   jax.ShapeDtypeStruct((T, ROUTER_LANES), F32),
                   jax.ShapeDtypeStruct((1, ROUTER_LANES), F32)),
        grid=(T // tm,),
        in_specs=[pl.BlockSpec((tm, D), lambda i: (i, 0)),
                  pl.BlockSpec((1, D), lambda i: (0, 0)),
                  pl.BlockSpec((D, ROUTER_LANES), lambda i: (0, 0)),
                  pl.BlockSpec((1, ROUTER_LANES), lambda i: (0, 0)),
                  pl.BlockSpec((tm, tm), lambda i: (0, 0))],
        out_specs=(pl.BlockSpec((8, tm), lambda i: (0, i)),
                   pl.BlockSpec((tm, ROUTER_LANES), lambda i: (i, 0)),
                   pl.BlockSpec((1, ROUTER_LANES), lambda i: (0, 0))),
        scratch_shapes=[pltpu.VMEM((1, ROUTER_LANES), F32)],
        compiler_params=_cparams(("arbitrary",)),
        name="moe_router",
    )(h, norm_w.reshape(1, D), wr, br, tri)


def _dispatch_kernel(dest_ref, zero_ref, h_ref, nw_ref, xs_ref, xn_ref, zbuf_ref, sem, zsem, *, tm, n_tok):
    xn_ref[...] = _ffn_norm(h_ref, nw_ref)

    @pl.when(pl.program_id(0) == 0)
    def _():
        zbuf_ref[...] = jnp.zeros(zbuf_ref.shape, zbuf_ref.dtype)
        rb = zbuf_ref.shape[0]

        def fill(r):
            return pltpu.make_async_copy(zbuf_ref, xs_ref.at[pl.ds(r * rb, rb)], zsem)

        for r in range(zero_ref.shape[0]):
            @pl.when(zero_ref[r] == 1)
            def _():
                fill(r).start()
        for r in range(zero_ref.shape[0]):
            @pl.when(zero_ref[r] == 1)
            def _():
                fill(r).wait()

    base = pl.program_id(0) * tm
    for t in range(tm):
        for slot in range(2):
            d = dest_ref[slot * n_tok + base + t]
            pltpu.make_async_copy(xn_ref.at[pl.ds(t, 1)], xs_ref.at[pl.ds(d, 1)], sem).start(priority=slot)
    for slot in range(2):
        pltpu.make_async_copy(xn_ref, xs_ref.at[pl.ds(0, tm)], sem).wait()


def _dispatch(h, norm_w, dest_flat, zero_blk, n_rows, tm=256):
    T, D = h.shape
    grid_spec = pltpu.PrefetchScalarGridSpec(
        num_scalar_prefetch=2,
        grid=(T // tm,),
        in_specs=[pl.BlockSpec((tm, D), lambda i, d, z: (i, 0)),
                  pl.BlockSpec((1, D), lambda i, d, z: (0, 0))],
        out_specs=pl.BlockSpec(memory_space=pl.ANY),
        scratch_shapes=[pltpu.VMEM((tm, D), F32), pltpu.VMEM((MOE_ROWS, D), F32),
                        pltpu.SemaphoreType.DMA, pltpu.SemaphoreType.DMA],
    )
    return pl.pallas_call(
        functools.partial(_dispatch_kernel, tm=tm, n_tok=T),
        out_shape=jax.ShapeDtypeStruct((n_rows, D), F32),
        grid_spec=grid_spec,
        compiler_params=_cparams(("arbitrary",)),
        name="moe_dispatch",
    )(dest_flat, zero_blk, h, norm_w.reshape(1, D))


def _moe_plan(counts, n_blk):
    RB = MOE_ROWS
    nb_e = (counts + RB - 1) // RB
    end_b = jnp.cumsum(nb_e)
    start_b = end_b - nb_e
    n_used = end_b[-1:].astype(I32)
    r = jnp.arange(n_blk, dtype=I32)
    be = jnp.minimum(jnp.sum((end_b[None, :] <= r[:, None]).astype(I32), axis=1), N_EXPERTS - 1)
    active = nb_e > 0
    ordinal = jnp.cumsum(active.astype(I32)) - 1
    idx = jnp.arange(N_EXPERTS, dtype=I32)
    later = jnp.where(active[None, :] & (idx[None, :] > idx[:, None]), idx[None, :], N_EXPERTS)
    nxt_e = jnp.min(later, axis=1)
    nxt_e = jnp.where(nxt_e == N_EXPERTS, -1, nxt_e)
    first = ((r == start_b[be]) & (r < n_used[0])).astype(I32)
    plan = jnp.stack([be, first, ordinal[be] % 2, nxt_e[be]]).astype(I32)
    zero_blk = ((r == end_b[be] - 1) | (r >= n_used[0])).astype(I32)
    return plan, n_used, (start_b * RB).astype(I32), zero_blk


def _stream_expert_weights(plan_ref, r, w_hbms, wbuf, sem, layer):
    e, slot, nxt = plan_ref[0, r], plan_ref[2, r], plan_ref[3, r]

    def copies(expert, s):
        rows = wbuf.shape[2] // 2
        return [(pltpu.make_async_copy(w.at[layer, expert, pl.ds(half * rows, rows)],
                                       wbuf.at[s, k, pl.ds(half * rows, rows)], sem.at[s, k, half]), half)
                for k, w in enumerate(w_hbms) for half in range(2)]

    @pl.when(plan_ref[1, r] == 1)
    def _():
        @pl.when(r == 0)
        def _():
            for c, half in copies(e, slot):
                c.start(priority=half)

        for c, _ in copies(e, slot):
            c.wait()

        @pl.when(nxt >= 0)
        def _():
            for c, half in copies(nxt, 1 - slot):
                c.start(priority=half)

    return slot


def _gate_up_kernel(plan_ref, nu_ref, x_ref, wg_hbm, wu_hbm, h_ref, wbuf, sem, *, layer):
    r = pl.program_id(0)

    @pl.when(r < nu_ref[0])
    def _():
        slot = _stream_expert_weights(plan_ref, r, (wg_hbm, wu_hbm), wbuf, sem, layer)
        x = x_ref[...]
        g = jnp.dot(x, wbuf[slot, 0], preferred_element_type=F32)
        u = jnp.dot(x, wbuf[slot, 1], preferred_element_type=F32)
        h_ref[...] = (jax.nn.silu(g) * u).astype(h_ref.dtype)

    @pl.when(r >= nu_ref[0])
    def _():
        h_ref[...] = jnp.zeros(h_ref.shape, h_ref.dtype)


def _down_kernel(plan_ref, nu_ref, h_ref, wd_hbm, y_ref, wbuf, sem, *, layer):
    r = pl.program_id(0)

    @pl.when(r < nu_ref[0])
    def _():
        slot = _stream_expert_weights(plan_ref, r, (wd_hbm,), wbuf, sem, layer)
        y_ref[...] = jnp.dot(h_ref[...].astype(F32), wbuf[slot, 0], preferred_element_type=F32)

    @pl.when(r >= nu_ref[0])
    def _():
        y_ref[...] = jnp.zeros(y_ref.shape, y_ref.dtype)


def _expert_mlp(xs, plan, n_used, w_gate, w_up, w_down, layer):
    n_rows, D = xs.shape
    RB = MOE_ROWS
    n_blk = n_rows // RB
    F = D_EXPERT

    def used_blk(r, plan, nu):
        return (jnp.minimum(r, nu[0] - 1), 0)

    h = pl.pallas_call(
        functools.partial(_gate_up_kernel, layer=layer),
        out_shape=jax.ShapeDtypeStruct((n_rows, F), BF16),
        grid_spec=pltpu.PrefetchScalarGridSpec(
            num_scalar_prefetch=2,
            grid=(n_blk,),
            in_specs=[pl.BlockSpec((RB, D), used_blk),
                      pl.BlockSpec(memory_space=pl.ANY),
                      pl.BlockSpec(memory_space=pl.ANY)],
            out_specs=pl.BlockSpec((RB, F), lambda r, plan, nu: (r, 0)),
            scratch_shapes=[pltpu.VMEM((2, 2, D, F), F32), pltpu.SemaphoreType.DMA((2, 2, 2))],
        ),
        compiler_params=_cparams(("arbitrary",)),
        name="moe_gate_up",
    )(plan, n_used, xs, w_gate, w_up)
    y = pl.pallas_call(
        functools.partial(_down_kernel, layer=layer),
        out_shape=jax.ShapeDtypeStruct((n_rows, D), F32),
        grid_spec=pltpu.PrefetchScalarGridSpec(
            num_scalar_prefetch=2,
            grid=(n_blk,),
            in_specs=[pl.BlockSpec((RB, F), used_blk),
                      pl.BlockSpec(memory_space=pl.ANY)],
            out_specs=pl.BlockSpec((RB, D), lambda r, plan, nu: (r, 0)),
            scratch_shapes=[pltpu.VMEM((2, 1, F, D), F32), pltpu.SemaphoreType.DMA((2, 1, 2))],
        ),
        compiler_params=_cparams(("arbitrary",)),
        name="moe_down",
    )(plan, n_used, h, w_down)
    return y


def _combine_kernel(dest_ref, resid_ref, wts_ref, nw_ref, y_ref, *rest, tm, n_tok, keep_h):
    if keep_h:
        o_ref, n_ref, buf_ref, sem = rest
    else:
        o_ref = None
        n_ref, buf_ref, sem = rest
    i = pl.program_id(0)
    n = pl.num_programs(0)

    def gather(tile, par):
        for t in range(tm):
            for slot in range(2):
                d = dest_ref[slot * n_tok + tile * tm + t]
                pltpu.make_async_copy(y_ref.at[pl.ds(d, 1)], buf_ref.at[par, slot, pl.ds(t, 1)],
                                      sem.at[par]).start(priority=slot)

    @pl.when(i == 0)
    def _():
        gather(i, 0)

    for par in range(2):
        @pl.when((i + 1 < n) & ((i + 1) % 2 == par))
        def _():
            gather(i + 1, par)

    for par in range(2):
        @pl.when(i % 2 == par)
        def _():
            for slot in range(2):
                pltpu.make_async_copy(y_ref.at[pl.ds(0, tm)], buf_ref.at[par, slot], sem.at[par]).wait()
            w = wts_ref[...]
            hn = resid_ref[...] + (w[:, 0:1] * buf_ref[par, 0] + w[:, 1:2] * buf_ref[par, 1])
            if keep_h:
                o_ref[...] = hn
            yn = hn * lax.rsqrt(jnp.mean(hn * hn, axis=-1, keepdims=True) + EPS)
            n_ref[...] = (yn * nw_ref[...]).astype(n_ref.dtype)


def _combine(resid, wts, y, dest_flat, next_norm_w, norm_dtype, keep_h, norm_shape=None, tm=128):
    T, D = resid.shape
    row_spec = pl.BlockSpec((tm, D), lambda i, d: (i, 0))
    if norm_shape is None:
        out_shape = [jax.ShapeDtypeStruct((T, D), norm_dtype)]
        out_specs = [row_spec]
    else:
        per_seq = norm_shape[1] // tm
        out_shape = [jax.ShapeDtypeStruct(norm_shape, norm_dtype)]
        out_specs = [pl.BlockSpec((None, tm, D), lambda i, d: (i // per_seq, i % per_seq, 0))]
    if keep_h:
        out_shape.insert(0, jax.ShapeDtypeStruct((T, D), F32))
        out_specs.insert(0, row_spec)
    grid_spec = pltpu.PrefetchScalarGridSpec(
        num_scalar_prefetch=1,
        grid=(T // tm,),
        in_specs=[row_spec,
                  pl.BlockSpec((tm, ROUTER_LANES), lambda i, d: (i, 0)),
                  pl.BlockSpec((1, D), lambda i, d: (0, 0)),
                  pl.BlockSpec(memory_space=pl.ANY)],
        out_specs=tuple(out_specs),
        scratch_shapes=[pltpu.VMEM((2, 2, tm, D), F32), pltpu.SemaphoreType.DMA((2,))],
    )
    outs = pl.pallas_call(
        functools.partial(_combine_kernel, tm=tm, n_tok=T, keep_h=keep_h),
        out_shape=tuple(out_shape),
        grid_spec=grid_spec,
        compiler_params=_cparams(("arbitrary",)),
        name="moe_combine",
    )(dest_flat, resid, wts, next_norm_w.reshape(1, D), y)
    return (outs[0], outs[1]) if keep_h else (None, outs[0])


def _hier_moe(h, norm_w, w_group, b_group, w_expert, b_expert, w_gate, w_up, w_down, layer,
              next_norm_w, norm_dtype, keep_h, norm_shape=None):
    T, D = h.shape
    RB = MOE_ROWS
    n_blk = (T * 2) // RB + N_EXPERTS
    n_rows = n_blk * RB
    ids, wts, cnt = _router(h, norm_w, w_group, b_group, w_expert, b_expert)
    counts = cnt[0, N_GROUPS:N_GROUPS + N_EXPERTS].astype(I32)
    plan, n_used, start_padded, zero_blk = _moe_plan(counts, n_blk)
    is_e = ids[None, 0:2] == jnp.arange(N_EXPERTS, dtype=I32)[:, None, None]
    dest = (jnp.sum(jnp.where(is_e, start_padded[:, None, None], 0), axis=0) + ids[2:4]).reshape(-1)
    xs = _dispatch(h, norm_w, dest, zero_blk, n_rows)
    y = _expert_mlp(xs, plan, n_used, w_gate, w_up, w_down, layer)
    return _combine(h, wts, y, dest, next_norm_w, norm_dtype, keep_h, norm_shape)


def _compress_kernel(x_ref, w1_ref, w2_ref, pe_ref, o_ref, xf_ref):
    d = B_D
    half = CMP_STRIDE * d
    nch = x_ref.shape[0] // CMP_STRIDE
    xf_ref[...] = x_ref[...].astype(F32)
    w1 = w1_ref[...].astype(BF16)
    a = jnp.zeros((nch, w1.shape[1]), F32)
    b = jnp.zeros((nch, w1.shape[1]), F32)
    for l in range(CMP_STRIDE):
        xl = xf_ref[pl.ds(l, nch, stride=CMP_STRIDE), :].astype(BF16)
        a = a + jnp.dot(xl, w1[l * d:(l + 1) * d], preferred_element_type=F32)
        b = b + jnp.dot(xl, w1[half + l * d:half + (l + 1) * d], preferred_element_type=F32)
    pe = jnp.broadcast_to(pe_ref[...], (8, 2 * half)).astype(BF16)
    pt = jnp.dot(pe, w1, preferred_element_type=F32)[0:1]
    hid = a + pltpu.roll(b, b.shape[0] - 1, 0) + pt
    g = jax.nn.gelu(hid)
    o_ref[...] = jnp.dot(g.astype(BF16), w2_ref[...].astype(BF16),
                         preferred_element_type=F32).astype(o_ref.dtype)


def _compress(projm, w1, w2, pe_flat, B, S):
    G, d = B_GROUPS, B_D
    nch = S // CMP_STRIDE
    width = CMP_STRIDE * d
    hid = w1.shape[-1]
    cb = B_HEADS
    return pl.pallas_call(
        _compress_kernel,
        out_shape=jax.ShapeDtypeStruct((2, B, G, nch, d), BF16),
        grid=(2, B, G),
        in_specs=[pl.BlockSpec((S, d), lambda kv, b, g: (b, cb + kv * G + g)),
                  pl.BlockSpec((None, 2 * width, hid), lambda kv, b, g: (kv, 0, 0)),
                  pl.BlockSpec((None, hid, d), lambda kv, b, g: (kv, 0, 0)),
                  pl.BlockSpec((None, 1, 2 * width), lambda kv, b, g: (kv, 0, 0))],
        out_specs=pl.BlockSpec((None, None, None, nch, d), lambda kv, b, g: (kv, b, g, 0, 0)),
        scratch_shapes=[pltpu.VMEM((S, d), F32)],
        compiler_params=_cparams(("arbitrary", "arbitrary", "arbitrary")),
        name="nsa_compress",
    )(projm, w1, w2, pe_flat)


def _cmp_select_kernel(slopes_ref, q_ref, kc_ref, vc_ref, ov_ref, o_ref, sel_ref, need_ref, *, tq, n_cmp,
                       n_blocks, n_rank):
    g = pl.program_id(1)
    i = pl.program_id(2)
    d = B_D
    nch = kc_ref.shape[0]
    pos = i * tq + lax.broadcasted_iota(I32, (1, tq), 1)
    c = lax.broadcasted_iota(I32, (nch, 1), 0)
    dist_i = pos - (c * CMP_STRIDE + (CMP_LEN - 1))
    valid = (dist_i >= 0) & (c < n_cmp)
    dist = dist_i.astype(F32)
    any_valid = jnp.where(pos >= CMP_LEN - 1, 1.0, 0.0)
    kc = kc_ref[...]
    vc = vc_ref[...]
    psum = jnp.zeros((nch, tq), F32)
    for hh in range(B_HPG):
        s = _nt_dot(kc, q_ref[:, hh * d:(hh + 1) * d])
        s = jnp.where(valid, s - (slopes_ref[g * B_HPG + hh] * LOG2E) * dist, NEG)
        e = jnp.exp2(s - jnp.max(s, axis=0, keepdims=True))
        p = (e / jnp.sum(e, axis=0, keepdims=True)) * any_valid
        oT = lax.dot_general(vc, p.astype(BF16), (((0,), (0,)), ((), ())), preferred_element_type=F32)
        o_ref[:, hh * d:(hh + 1) * d] = oT.T.astype(o_ref.dtype)
        psum = psum + p

    p_hi = psum.astype(BF16)
    p_lo = (psum - p_hi.astype(F32)).astype(BF16)
    ov = ov_ref[...]
    imp = jnp.dot(ov, p_hi, preferred_element_type=F32) + jnp.dot(ov, p_lo, preferred_element_type=F32)
    imp = imp[:n_rank]

    blk = lax.broadcasted_iota(I32, (n_rank, 1), 0)
    cur = lax.shift_right_logical(pos, SLC_BLOCK.bit_length() - 1)
    forced = (blk == 0) | (blk == cur) | (blk == cur - 1)
    causal = blk * SLC_BLOCK <= pos
    score = jnp.where(forced, FORCE, jnp.where(causal, imp, -FORCE))
    score = jnp.where(blk < n_blocks, score, -2.0 * FORCE)
    blk_f = blk.astype(F32)
    remaining = score
    sel_add = jnp.full(score.shape, NEG, F32)
    for _ in range(SLC_TOPN):
        top = jnp.max(remaining, axis=0, keepdims=True)
        first = jnp.min(jnp.where(remaining == top, blk_f, float(n_rank)), axis=0, keepdims=True)
        pick = blk_f == first
        sel_add = jnp.where(pick, 0.0, sel_add)
        remaining = jnp.where(pick, -4.0 * FORCE, remaining)
    pad = jnp.zeros((128 - n_rank, tq), F32)
    sel_t = jnp.concatenate([jnp.where(blk < n_blocks, sel_add, 0.0), pad], axis=0) if n_rank < 128 else sel_add
    sel_ref[...] = sel_t.T.astype(sel_ref.dtype)
    per_tile = SLC_KEY_TILE // SLC_BLOCK
    blk_any = jnp.max(jnp.where(sel_add == 0.0, 1, 0), axis=1, keepdims=True)
    rows = []
    for j in range(8):
        if (j + 1) * per_tile <= n_rank:
            hit = jnp.max(blk_any[j * per_tile:(j + 1) * per_tile], axis=0, keepdims=True)
            rows.append(jnp.broadcast_to(hit, (1, 128)))
        else:
            rows.append(jnp.zeros((1, 128), I32))
    need_ref[...] = jnp.concatenate(rows, axis=0)


def _cmp_select(projm, cmp_kv, overlap, B, S, tq=512):
    T = B * S
    G, d = B_GROUPS, B_D
    nq = S // tq
    nch = cmp_kv.shape[3]
    n_cmp = (S - CMP_LEN) // CMP_STRIDE + 1
    n_blocks = S // SLC_BLOCK
    assert SLC_TOPN <= n_blocks <= FEAT_BLK
    slopes = _alibi_slopes(B_HEADS)
    grid_spec = pltpu.PrefetchScalarGridSpec(
        num_scalar_prefetch=1,
        grid=(B, G, nq),
        in_specs=[pl.BlockSpec((tq, B_HPG * d), lambda b, g, i, sl: (b * nq + i, g)),
                  pl.BlockSpec((None, None, None, nch, d), lambda b, g, i, sl: (0, b, g, 0, 0)),
                  pl.BlockSpec((None, None, None, nch, d), lambda b, g, i, sl: (1, b, g, 0, 0)),
                  pl.BlockSpec((128, nch), lambda b, g, i, sl: (0, 0))],
        out_specs=(pl.BlockSpec((tq, B_HPG * d), lambda b, g, i, sl: (b * nq + i, g)),
                   pl.BlockSpec((None, None, tq, 128), lambda b, g, i, sl: (b, g, i, 0)),
                   pl.BlockSpec((None, None, None, 8, 128), lambda b, g, i, sl: (b, g, i, 0, 0))),
    )
    assert S // SLC_KEY_TILE <= 8
    return pl.pallas_call(
        functools.partial(_cmp_select_kernel, tq=tq, n_cmp=n_cmp, n_blocks=n_blocks,
                          n_rank=-(-n_blocks // 8) * 8),
        out_shape=(jax.ShapeDtypeStruct((T, B_HEADS * d), BF16),
                   jax.ShapeDtypeStruct((B, G, S, 128), BF16),
                   jax.ShapeDtypeStruct((B, G, nq, 8, 128), I32)),
        grid_spec=grid_spec,
        compiler_params=_cparams(("parallel", "parallel", "arbitrary")),
        name="nsa_cmp_select",
    )(slopes, projm, cmp_kv, cmp_kv, overlap)


def _head_units(k_ref, v_ref, feat_ref, qa_ref, j, mask, qc):
    n_heads, tq, _ = qa_ref.shape
    k0 = pl.multiple_of(j * tq, tq)
    v = v_ref[pl.ds(k0, tq), :]

    def score_fn(hh, c):
        def fn():
            ka = jnp.concatenate([k_ref[pl.ds(k0, tq), :], feat_ref[pl.ds(k0, tq), :]], axis=1)
            sT = _nt_dot(ka, qa_ref[hh, c * qc:(c + 1) * qc, :])
            return sT if mask is None else sT + mask[:, c * qc:(c + 1) * qc]
        return fn

    return [(score_fn(hh, c), v, hh * (tq // qc) + c) for hh in range(n_heads) for c in range(tq // qc)]


def _write_heads_T(o_ref, acc_ref, l_ref):
    tq = o_ref.shape[0]
    qc = l_ref.shape[-1]
    for hh in range(B_HPG):
        for c in range(tq // qc):
            u = hh * (tq // qc) + c
            o_ref[c * qc:(c + 1) * qc, hh * B_D:(hh + 1) * B_D] = (acc_ref[u] / l_ref[u]).T.astype(o_ref.dtype)


def _flash_state(n_heads, tq, dv):
    qc = min(ATTN_QC, tq)
    n_units = n_heads * (tq // qc)
    return [pltpu.VMEM((n_units, 1, qc), F32), pltpu.VMEM((n_units, 1, qc), F32),
            pltpu.VMEM((n_units, dv, qc), F32)]


def _stage_queries(q_ref, qa_ref, slopes_ref, g, q0, sel_add):
    tq = q_ref.shape[0]
    lane = lax.broadcasted_iota(I32, (tq, 128), 1)
    for hh in range(B_HPG):
        coef = jnp.broadcast_to(_alibi_query_features(slopes_ref[g * B_HPG + hh] * LOG2E, q0), (tq, 128))
        if sel_add is not None:
            coef = jnp.where(lane < FEAT_BLK, sel_add, coef)
        qa_ref[hh] = jnp.concatenate([q_ref[:, hh * B_D:(hh + 1) * B_D], coef.astype(BF16)], axis=1)


def _slc_attn_kernel(slopes_ref, need_ref, q_ref, k_ref, v_ref, feat_ref, sel_ref, o_ref, qa_ref, m_ref, l_ref,
                     acc_ref, *, tq):
    b = pl.program_id(0)
    g = pl.program_id(1)
    i = pl.program_id(2)
    n = pl.num_programs(2)
    _flash_init(m_ref, l_ref, acc_ref)
    _stage_queries(q_ref, qa_ref, slopes_ref, g, i * tq, sel_ref[...].astype(F32))

    qc = m_ref.shape[-1]

    def units(j, causal=None):
        return _head_units(k_ref, v_ref, feat_ref, qa_ref, j, causal, qc)

    need_base = ((b * pl.num_programs(1) + g) * n + i) * n

    def needed(j):
        return (j == 0) | (need_ref[need_base + jnp.maximum(j, 0)] > 0)

    def body(p, carry):
        j0, j1 = 2 * p, 2 * p + 1
        n0, n1 = needed(j0), needed(j1)

        @pl.when(n0 & n1)
        def _():
            _flash_units(units(j0) + units(j1), m_ref, l_ref, acc_ref)

        @pl.when(n0 & jnp.logical_not(n1))
        def _():
            _flash_units(units(j0), m_ref, l_ref, acc_ref)

        @pl.when(jnp.logical_not(n0) & n1)
        def _():
            _flash_units(units(j1), m_ref, l_ref, acc_ref)

        return carry

    lax.fori_loop(0, lax.shift_right_logical(i, 1), body, 0)

    @pl.when((i % 2 == 1) & needed(i - 1))
    def _():
        _flash_units(units(i - 1), m_ref, l_ref, acc_ref)

    _flash_units(units(i, _tri_mask(tq, tq, True)), m_ref, l_ref, acc_ref)
    _write_heads_T(o_ref, acc_ref, l_ref)


def _slc_attention(projm, sel, need, B, S):
    T = B * S
    G, d = B_GROUPS, B_D
    tq = SLC_KEY_TILE
    n = S // tq
    slopes = _alibi_slopes(B_HEADS)
    ksb = (B_HEADS * d + 2 * G * d) // d
    vsb = ksb + G
    grid_spec = pltpu.PrefetchScalarGridSpec(
        num_scalar_prefetch=2,
        grid=(B, G, n),
        in_specs=[
            pl.BlockSpec((tq, B_HPG * d), lambda b, g, i, sl, nd: (b * n + i, g)),
            pl.BlockSpec((S, d), lambda b, g, i, sl, nd: (b, ksb + g)),
            pl.BlockSpec((S, d), lambda b, g, i, sl, nd: (b, vsb + g)),
            pl.BlockSpec((S, 128), lambda b, g, i, sl, nd: (0, 0)),
            pl.BlockSpec((None, None, tq, 128), lambda b, g, i, sl, nd: (b, g, i, 0)),
        ],
        out_specs=pl.BlockSpec((tq, B_HPG * d), lambda b, g, i, sl, nd: (b * n + i, g)),
        scratch_shapes=[pltpu.VMEM((B_HPG, tq, 2 * d), BF16)] + _flash_state(B_HPG, tq, d),
    )
    return pl.pallas_call(
        functools.partial(_slc_attn_kernel, tq=tq),
        out_shape=jax.ShapeDtypeStruct((T, B_HEADS * d), BF16),
        grid_spec=grid_spec,
        compiler_params=_cparams(("parallel", "parallel", "arbitrary")),
        name="nsa_selected_attention",
    )(slopes, need.reshape(-1), projm, projm, projm, _key_features(S), sel)


def _win_attn_kernel(slopes_ref, q_ref, k_ref, v_ref, feat_ref, gl_ref, oc_ref, os_ref, o_ref, qa_ref, m_ref,
                     l_ref, acc_ref, *, tq):
    g = pl.program_id(1)
    i = pl.program_id(2)
    _flash_init(m_ref, l_ref, acc_ref)
    _stage_queries(q_ref, qa_ref, slopes_ref, g, i * tq, None)

    qc = m_ref.shape[-1]

    def units(j, mask):
        return _head_units(k_ref, v_ref, feat_ref, qa_ref, j, mask, qc)

    @pl.when(i > 0)
    def _():
        _flash_units(units(i - 1, _tri_mask(tq, tq, False)) + units(i, _tri_mask(tq, tq, True)),
                     m_ref, l_ref, acc_ref)

    @pl.when(i == 0)
    def _():
        _flash_units(units(i, _tri_mask(tq, tq, True)), m_ref, l_ref, acc_ref)

    n_chunks = tq // qc
    for hh in range(B_HPG):
        row = (g * B_HPG + hh) * 3
        for c in range(n_chunks):
            qs = slice(c * qc, (c + 1) * qc)
            cols = slice(hh * B_D, (hh + 1) * B_D)
            u = hh * n_chunks + c
            gates = [jax.nn.sigmoid(gl_ref[pl.ds(row + br, 1), qs]) for br in range(3)]
            mix = (gates[0] * oc_ref[qs, cols].astype(F32).T + gates[1] * os_ref[qs, cols].astype(F32).T
                   + gates[2] * (acc_ref[u] / l_ref[u]))
            o_ref[qs, cols] = mix.T.astype(o_ref.dtype)


def _win_attention(projm, gate_logits_t, o_cmp, o_slc, B, S):
    T = B * S
    G, d = B_GROUPS, B_D
    tq = WINDOW
    n = S // tq
    slopes = _alibi_slopes(B_HEADS)
    kwb = (B_HEADS * d + 4 * G * d) // d
    vwb = kwb + G
    grid_spec = pltpu.PrefetchScalarGridSpec(
        num_scalar_prefetch=1,
        grid=(B, G, n),
        in_specs=[
            pl.BlockSpec((tq, B_HPG * d), lambda b, g, i, sl: (b * n + i, g)),
            pl.BlockSpec((S, d), lambda b, g, i, sl: (b, kwb + g)),
            pl.BlockSpec((S, d), lambda b, g, i, sl: (b, vwb + g)),
            pl.BlockSpec((S, 128), lambda b, g, i, sl: (0, 0)),
            pl.BlockSpec((128, tq), lambda b, g, i, sl: (0, b * n + i)),
            pl.BlockSpec((tq, B_HPG * d), lambda b, g, i, sl: (b * n + i, g)),
            pl.BlockSpec((tq, B_HPG * d), lambda b, g, i, sl: (b * n + i, g)),
        ],
        out_specs=pl.BlockSpec((tq, B_HPG * d), lambda b, g, i, sl: (b * n + i, g)),
        scratch_shapes=[pltpu.VMEM((B_HPG, tq, 2 * d), BF16)] + _flash_state(B_HPG, tq, d),
    )
    return pl.pallas_call(
        functools.partial(_win_attn_kernel, tq=tq),
        out_shape=jax.ShapeDtypeStruct((T, B_HEADS * d), BF16),
        grid_spec=grid_spec,
        compiler_params=_cparams(("parallel", "parallel", "arbitrary")),
        name="nsa_window_attention",
    )(slopes, projm, projm, projm, _key_features(S), gate_logits_t, o_cmp, o_slc)


def _overlap_matrix(S, n_rows):
    nc = (S - CMP_LEN) // CMP_STRIDE + 1
    nsb = S // SLC_BLOCK
    cs = np.arange(nc) * CMP_STRIDE
    ss = np.arange(nsb) * SLC_BLOCK
    ov = np.clip(np.minimum(cs[:, None] + CMP_LEN, ss[None, :] + SLC_BLOCK)
                 - np.maximum(cs[:, None], ss[None, :]), 0, None) / CMP_LEN
    out = np.zeros((128, n_rows), np.float32)
    out[:nsb, :nc] = ov.T
    return jnp.asarray(out, BF16)


def _nsa_attention(xn, w_in, cmp_pos, cmp_w1, cmp_w2, B, S):
    T = B * S
    H, G, d = B_HEADS, B_GROUPS, B_D
    n_main = H * d + 6 * G * d
    col_scale = jnp.concatenate([jnp.full((H * d,), d ** -0.5 * LOG2E, F32), jnp.ones((n_main - H * d,), F32)])
    w_t = jnp.swapaxes(w_in, 0, 1)
    projm = _matmul(xn, w_t, n_main, BF16, col_scale=col_scale, w_is_transposed=True)
    w_gate_t = jnp.pad(w_t[n_main:], ((0, 128 - 3 * H), (0, 0)))
    gate_logits = _matmul(xn, w_gate_t, 128, F32, tn=128, w_is_transposed=True)

    cmp_kv = _compress(projm, cmp_w1, cmp_w2, cmp_pos.reshape(2, 1, CMP_LEN * d), B, S)
    o_cmp, sel, need = _cmp_select(projm, cmp_kv, _overlap_matrix(S, S // CMP_STRIDE), B, S)
    n_t = S // SLC_KEY_TILE
    need = need[:, :, :, :n_t, 0].reshape(B, G, n_t, -1, n_t).max(axis=3)
    o_slc = _slc_attention(projm, sel, need, B, S)
    return _win_attention(projm, gate_logits.T, o_cmp, o_slc, B, S)


def kernel(x, attn_norm_w, ffn_norm_w, final_norm_w, a_w_in, a_lambda, a_subln_w, a_w_out, b_w_in,
           b_cmp_pos, b_cmp_w1, b_cmp_w2, b_w_out, moe_w_group, moe_b_group, moe_w_expert,
           moe_b_expert, moe_w_gate, moe_w_up, moe_w_down):
    B, S, D = x.shape
    T = B * S
    depth = attn_norm_w.shape[0]
    h = x
    xn = _rmsnorm(h, attn_norm_w[0], BF16)
    for i in range(depth):
        j = i // 2
        if i % 2 == 0:
            lambda_init = 0.8 - 0.6 * math.exp(-0.3 * i)
            n_q = A_HEADS * 2 * A_DH
            n_in = 2 * n_q + A_HEADS * A_DV
            col_scale = jnp.concatenate([jnp.full((n_q,), A_DH ** -0.5 * LOG2E, F32),
                                         jnp.ones((n_in - n_q,), F32)])
            qkv = _matmul(xn, a_w_in[j], n_in, BF16, col_scale=col_scale)
            o = _diff_attention(qkv, a_lambda[j], a_subln_w[j], lambda_init, B, S)
            h = _matmul(o, a_w_out[j], D, F32, resid=h)
        else:
            o = _nsa_attention(xn, b_w_in[j], b_cmp_pos[j], b_cmp_w1[j], b_cmp_w2[j], B, S)
            h = _matmul(o, b_w_out[j], D, F32, resid=h)
        last = i == depth - 1
        h, xn = _hier_moe(h, ffn_norm_w[i], moe_w_group[i], moe_b_group[i], moe_w_expert[i], moe_b_expert[i],
                          moe_w_gate, moe_w_up, moe_w_down, i,
                          final_norm_w if last else attn_norm_w[i + 1], F32 if last else BF16, not last,
                          (B, S, D) if last else None)
    return xn
```

```python
import functools
import math

import numpy as np
import jax
import jax.numpy as jnp
from jax import lax
from jax.experimental import pallas as pl
from jax.experimental.pallas import tpu as pltpu

F32 = jnp.float32
BF16 = jnp.bfloat16
I32 = jnp.int32

EPS = 1e-6
NEG = -1e30
FORCE = 1e4
LOG2E = math.log2(math.e)

A_HEADS = 8
A_DH = 128
A_DV = 256
B_HEADS = 16
B_GROUPS = 4
B_HPG = 4
B_D = 128
CMP_LEN = 32
CMP_STRIDE = 16
SLC_BLOCK = 64
SLC_TOPN = 16
WINDOW = 512
N_GROUPS = 4
EPG = 8
N_EXPERTS = 32
D_EXPERT = 1024
ROUTER_LANES = 128
MOE_ROWS = 256
ATTN_QC = 512
SLC_KEY_TILE = 512

VMEM_LIMIT = 56 * 1024 * 1024


def _cparams(sem, flags=None):
    return pltpu.CompilerParams(dimension_semantics=sem, vmem_limit_bytes=VMEM_LIMIT, flags=flags)


def _alibi_slopes(n):
    return jnp.asarray(np.array([2.0 ** (-8.0 * (i + 1) / n) for i in range(n)], dtype=np.float32))


def _nt_dot(a, b):
    return lax.dot_general(a, b, (((1,), (1,)), ((), ())), preferred_element_type=F32)


def _rms_kernel(x_ref, w_ref, o_ref):
    x = x_ref[...]
    y = x * lax.rsqrt(jnp.mean(x * x, axis=-1, keepdims=True) + EPS)
    o_ref[...] = (y * w_ref[...]).astype(o_ref.dtype)


def _rows_spec(x, tm, tn, col_arg):
    if x.ndim == 2:
        return pl.BlockSpec((tm, tn), lambda *g: (g[-1], col_arg(g)))
    per_seq = x.shape[1] // tm
    return pl.BlockSpec((None, tm, tn), lambda *g: (g[-1] // per_seq, g[-1] % per_seq, col_arg(g)))


def _rmsnorm(x, w, out_dtype, tm=512):
    D = x.shape[-1]
    T = x.size // D
    return pl.pallas_call(
        _rms_kernel,
        out_shape=jax.ShapeDtypeStruct((T, D), out_dtype),
        grid=(T // tm,),
        in_specs=[_rows_spec(x, tm, D, lambda g: 0),
                  pl.BlockSpec((1, D), lambda i: (0, 0))],
        out_specs=pl.BlockSpec((tm, D), lambda i: (i, 0)),
        compiler_params=_cparams(("parallel",)),
        name="rmsnorm",
    )(x, w.reshape(1, D))


def _mm_kernel(*refs, has_scale, has_resid, w_is_transposed):
    a_ref, w_ref = refs[0], refs[1]
    k = 2
    scale_ref = resid_ref = None
    if has_scale:
        scale_ref = refs[k]
        k += 1
    if has_resid:
        resid_ref = refs[k]
        k += 1
    o_ref, wbf_ref = refs[k], refs[k + 1]

    @pl.when(pl.program_id(1) == 0)
    def _():
        wbf_ref[...] = w_ref[...].astype(BF16)

    if w_is_transposed:
        acc = _nt_dot(a_ref[...], wbf_ref[...])
    else:
        acc = jnp.dot(a_ref[...], wbf_ref[...], preferred_element_type=F32)
    if has_scale:
        acc = acc * scale_ref[...]
    if has_resid:
        acc = acc + resid_ref[...]
    o_ref[...] = acc.astype(o_ref.dtype)


def _matmul(a, w, n_out, out_dtype, *, col_scale=None, resid=None, tm=1024, tn=1024, w_is_transposed=False):
    M, K = a.shape
    if w_is_transposed:
        assert M % tm == 0 and n_out % tn == 0 and w.shape[1] == K and w.shape[0] >= n_out
        w_spec, w_block = pl.BlockSpec((tn, K), lambda j, i: (j, 0)), (tn, K)
    else:
        assert M % tm == 0 and n_out % tn == 0 and w.shape[0] == K and w.shape[1] >= n_out
        w_spec, w_block = pl.BlockSpec((K, tn), lambda j, i: (0, j)), (K, tn)
    in_specs = [pl.BlockSpec((tm, K), lambda j, i: (i, 0)), w_spec]
    args = [a, w]
    if col_scale is not None:
        in_specs.append(pl.BlockSpec((1, tn), lambda j, i: (0, j)))
        args.append(col_scale.reshape(1, n_out))
    if resid is not None:
        in_specs.append(_rows_spec(resid, tm, tn, lambda g: g[0]))
        args.append(resid)
    return pl.pallas_call(
        functools.partial(_mm_kernel, has_scale=col_scale is not None, has_resid=resid is not None,
                          w_is_transposed=w_is_transposed),
        out_shape=jax.ShapeDtypeStruct((M, n_out), out_dtype),
        grid=(n_out // tn, M // tm),
        in_specs=in_specs,
        out_specs=pl.BlockSpec((tm, tn), lambda j, i: (i, j)),
        scratch_shapes=[pltpu.VMEM(w_block, BF16)],
        compiler_params=_cparams(("parallel", "arbitrary")),
        name="matmul",
    )(*args)


FEAT_BLK = 64
FEAT_POS = 67
FEAT_ONE = 70


def _key_features(S):
    assert S // SLC_BLOCK <= FEAT_BLK
    j = np.arange(S)
    f = np.zeros((S, 128), np.float32)
    f[j, j // SLC_BLOCK] = 1.0
    f[:, FEAT_BLK:FEAT_BLK + 3] = (j // SLC_BLOCK)[:, None]
    f[:, FEAT_POS:FEAT_POS + 3] = (j % SLC_BLOCK)[:, None]
    f[:, FEAT_ONE:FEAT_ONE + 3] = 1.0
    return jnp.asarray(f, BF16)


def _alibi_query_features(slope2, q0):
    lane = lax.broadcasted_iota(I32, (1, 128), 1)
    base = jnp.where((lane >= FEAT_BLK) & (lane < FEAT_BLK + 3), slope2 * float(SLC_BLOCK),
                     jnp.where((lane >= FEAT_POS) & (lane < FEAT_POS + 3), slope2,
                               jnp.where((lane >= FEAT_ONE) & (lane < FEAT_ONE + 3),
                                         -slope2 * q0.astype(F32), 0.0)))
    hi = base.astype(BF16).astype(F32)
    r1 = base - hi
    lo = r1.astype(BF16).astype(F32)
    lo2 = (r1 - lo).astype(BF16).astype(F32)
    first = (lane == FEAT_BLK) | (lane == FEAT_POS) | (lane == FEAT_ONE)
    second = (lane == FEAT_BLK + 1) | (lane == FEAT_POS + 1) | (lane == FEAT_ONE + 1)
    return jnp.where(first, hi, jnp.where(second, lo, lo2))


def _flash_init(m_ref, l_ref, acc_ref):
    m_ref[...] = jnp.full(m_ref.shape, NEG, F32)
    l_ref[...] = jnp.zeros(l_ref.shape, F32)
    acc_ref[...] = jnp.zeros(acc_ref.shape, F32)


def _flash_probs(sT, m_ref, l_ref, u):
    m_old = m_ref[u]
    m_new = jnp.maximum(m_old, jnp.max(sT, axis=0, keepdims=True))
    alpha = jnp.exp2(m_old - m_new)
    p = jnp.exp2(sT - m_new)
    l_ref[u] = alpha * l_ref[u] + jnp.sum(p, axis=0, keepdims=True)
    m_ref[u] = m_new
    return p.astype(BF16), alpha


def _flash_accumulate(p, alpha, v, acc_ref, u):
    pv = lax.dot_general(v, p, (((0,), (0,)), ((), ())), preferred_element_type=F32)
    acc_ref[u] = alpha * acc_ref[u] + pv


def _flash_units(units, m_ref, l_ref, acc_ref):
    n = len(units)
    scores, probs = {}, {}
    for s in range(n + 2):
        if s < n:
            scores[s] = units[s][0]()
        if 1 <= s <= n:
            probs[s - 1] = _flash_probs(scores.pop(s - 1), m_ref, l_ref, units[s - 1][2])
        if 2 <= s <= n + 1:
            p, alpha = probs.pop(s - 2)
            _flash_accumulate(p, alpha, units[s - 2][1], acc_ref, units[s - 2][2])


def _tri_mask(tk, tq, keep_upper):
    r = lax.broadcasted_iota(I32, (tk, tq), 0)
    c = lax.broadcasted_iota(I32, (tk, tq), 1)
    keep = (r <= c) if keep_upper else (r > c)
    return jnp.where(keep, 0.0, NEG)


def _diff_attn_kernel(slopes_ref, q1_ref, q2_ref, k1_ref, k2_ref, v_ref, feat_ref, lam_ref, sw_ref, o_ref,
                      qa_ref, m_ref, l_ref, acc_ref, *, tq, lambda_init):
    h = pl.program_id(1)
    i = pl.program_id(2)
    _flash_init(m_ref, l_ref, acc_ref)
    qfeat = jnp.broadcast_to(_alibi_query_features(slopes_ref[h] * LOG2E, i * tq), (tq, 128)).astype(BF16)
    for c, q_ref in enumerate((q1_ref, q2_ref)):
        qa_ref[c] = jnp.concatenate([q_ref[...], qfeat], axis=1)

    def tile_units(j, mask):
        k0 = pl.multiple_of(j * tq, tq)
        v = v_ref[pl.ds(k0, tq), :]
        feat = feat_ref[pl.ds(k0, tq), :]
        qc = m_ref.shape[-1]

        def score_fn(c, k_ref, x):
            def fn():
                ka = jnp.concatenate([k_ref[pl.ds(k0, tq), :], feat], axis=1)
                sT = _nt_dot(ka, qa_ref[c, x * qc:(x + 1) * qc, :])
                return sT if mask is None else sT + mask[:, x * qc:(x + 1) * qc]
            return fn

        return [(score_fn(c, k_ref, x), v, c * (tq // qc) + x)
                for c, k_ref in enumerate((k1_ref, k2_ref)) for x in range(tq // qc)]

    def body(p, carry):
        _flash_units(tile_units(2 * p, None) + tile_units(2 * p + 1, None), m_ref, l_ref, acc_ref)
        return carry

    lax.fori_loop(0, lax.shift_right_logical(i, 1), body, 0)

    @pl.when(i % 2 == 1)
    def _():
        _flash_units(tile_units(i - 1, None), m_ref, l_ref, acc_ref)

    _flash_units(tile_units(i, _tri_mask(tq, tq, True)), m_ref, l_ref, acc_ref)

    lam = lam_ref[...]
    lmbda = (jnp.exp(jnp.sum(lam[0:1] * lam[1:2], axis=-1, keepdims=True))
             - jnp.exp(jnp.sum(lam[2:3] * lam[3:4], axis=-1, keepdims=True)) + lambda_init)
    qc = m_ref.shape[-1]
    nx = tq // qc
    for x in range(nx):
        oT = acc_ref[x] / l_ref[x] - lmbda * (acc_ref[nx + x] / l_ref[nx + x])
        o = oT.T
        y = o * lax.rsqrt(jnp.mean(o * o, axis=-1, keepdims=True) + EPS)
        y = (y * sw_ref[...]) * (1.0 - lambda_init)
        o_ref[x * qc:(x + 1) * qc, :] = y.astype(o_ref.dtype)


def _diff_attention(qkv, lam, subln_w, lambda_init, B, S, tq=512):
    T = B * S
    H, dh, dv = A_HEADS, A_DH, A_DV
    n = S // tq
    slopes = _alibi_slopes(H)
    kb = H * 2
    vb = (2 * H * 2 * dh) // dv
    grid_spec = pltpu.PrefetchScalarGridSpec(
        num_scalar_prefetch=1,
        grid=(B, H, n),
        in_specs=[
            pl.BlockSpec((tq, dh), lambda b, h, i, sl: (b * n + i, 2 * h)),
            pl.BlockSpec((tq, dh), lambda b, h, i, sl: (b * n + i, 2 * h + 1)),
            pl.BlockSpec((S, dh), lambda b, h, i, sl: (b, kb + 2 * h)),
            pl.BlockSpec((S, dh), lambda b, h, i, sl: (b, kb + 2 * h + 1)),
            pl.BlockSpec((S, dv), lambda b, h, i, sl: (b, vb + h)),
            pl.BlockSpec((S, 128), lambda b, h, i, sl: (0, 0)),
            pl.BlockSpec((4, dh), lambda b, h, i, sl: (0, 0)),
            pl.BlockSpec((1, dv), lambda b, h, i, sl: (0, 0)),
        ],
        out_specs=pl.BlockSpec((tq, dv), lambda b, h, i, sl: (b * n + i, h)),
        scratch_shapes=[pltpu.VMEM((2, tq, 2 * dh), BF16)] + _flash_state(2, tq, dv),
    )
    return pl.pallas_call(
        functools.partial(_diff_attn_kernel, tq=tq, lambda_init=lambda_init),
        out_shape=jax.ShapeDtypeStruct((T, H * dv), BF16),
        grid_spec=grid_spec,
        compiler_params=_cparams(("parallel", "parallel", "arbitrary")),
        name="diff_attention",
    )(slopes, qkv, qkv, qkv, qkv, qkv, _key_features(S), lam, subln_w.reshape(1, dv))


def _ffn_norm(h_ref, nw_ref):
    x = h_ref[...]
    return (x * lax.rsqrt(jnp.mean(x * x, axis=-1, keepdims=True) + EPS)) * nw_ref[...]


def _router_kernel(h_ref, nw_ref, wr_ref, br_ref, tri_ref, ids_ref, wts_ref, cnt_ref, base_ref):
    @pl.when(pl.program_id(0) == 0)
    def _():
        base_ref[...] = jnp.zeros(base_ref.shape, F32)

    xn = _ffn_norm(h_ref, nw_ref)
    w = wr_ref[...]
    x_hi = xn.astype(BF16)
    x_lo = (xn - x_hi.astype(F32)).astype(BF16)
    w_hi = w.astype(BF16)
    w_lo = (w - w_hi.astype(F32)).astype(BF16)
    logits = (jnp.dot(x_hi, w_hi, preferred_element_type=F32)
              + (jnp.dot(x_hi, w_lo, preferred_element_type=F32)
                 + jnp.dot(x_lo, w_hi, preferred_element_type=F32))) + br_ref[...]
    lane = lax.broadcasted_iota(I32, logits.shape, 1)
    big = jnp.int32(1 << 20)
    is_g = lane < N_GROUPS
    is_e = (lane >= N_GROUPS) & (lane < N_GROUPS + N_EXPERTS)

    lg = jnp.where(is_g, logits, NEG)
    mg = jnp.max(lg, axis=-1, keepdims=True)
    g_top = jnp.min(jnp.where(lg == mg, lane, big), axis=-1, keepdims=True)
    wg = 1.0 / jnp.sum(jnp.where(is_g, jnp.exp(lg - mg), 0.0), axis=-1, keepdims=True)

    in_grp = is_e & (lax.shift_right_logical(lane - N_GROUPS, 3) == g_top)
    le = jnp.where(in_grp, logits, NEG)
    me = jnp.max(le, axis=-1, keepdims=True)
    ee = jnp.where(in_grp, jnp.exp(le - me), 0.0)
    pe = ee / jnp.sum(ee, axis=-1, keepdims=True)
    pe = jnp.where(in_grp, pe, -1.0)
    p1 = jnp.max(pe, axis=-1, keepdims=True)
    i1 = jnp.min(jnp.where(pe == p1, lane, big), axis=-1, keepdims=True)
    pe2 = jnp.where(lane == i1, -1.0, pe)
    p2 = jnp.max(pe2, axis=-1, keepdims=True)
    i2 = jnp.min(jnp.where(pe2 == p2, lane, big), axis=-1, keepdims=True)
    den = p1 + p2
    w0 = wg * p1 / den
    w1 = wg * p2 / den

    oh1 = lane == i1
    oh2 = lane == i2
    both = jnp.where(oh1 | oh2, 1.0, 0.0)
    before = jnp.dot(tri_ref[...], both.astype(BF16), preferred_element_type=F32) + base_ref[...]
    pos0 = jnp.sum(jnp.where(oh1, before, 0.0), axis=-1, keepdims=True).astype(I32)
    pos1 = jnp.sum(jnp.where(oh2, before, 0.0), axis=-1, keepdims=True).astype(I32)
    total = base_ref[...] + jnp.sum(both, axis=0, keepdims=True)
    base_ref[...] = total
    cnt_ref[...] = total

    ids = jnp.where(lane == 0, i1 - N_GROUPS,
                    jnp.where(lane == 1, i2 - N_GROUPS,
                              jnp.where(lane == 2, pos0, jnp.where(lane == 3, pos1, 0))))
    ids_ref[...] = ids.T[:8]
    wts_ref[...] = jnp.where(lane == 0, w0, jnp.where(lane == 1, w1, 0.0))


def _router(h, norm_w, w_group, b_group, w_expert, b_expert, tm=512):
    T, D = h.shape
    pad = ROUTER_LANES - N_GROUPS - N_EXPERTS
    wr = jnp.concatenate([w_group, w_expert, jnp.zeros((D, pad), F32)], axis=1)
    br = jnp.concatenate([b_group, b_expert, jnp.zeros((pad,), F32)]).reshape(1, ROUTER_LANES)
    tri = jnp.asarray(np.tril(np.ones((tm, tm), np.float32), -1), BF16)
    return pl.pallas_call(
        _router_kernel,
        out_shape=(jax.ShapeDtypeStruct((8, T), I32),
                   jax.ShapeDtypeStruct((T, ROUTER_LANES), F32),
                   jax.ShapeDtypeStruct((1, ROUTER_LANES), F32)),
        grid=(T // tm,),
        in_specs=[pl.BlockSpec((tm, D), lambda i: (i, 0)),
                  pl.BlockSpec((1, D), lambda i: (0, 0)),
                  pl.BlockSpec((D, ROUTER_LANES), lambda i: (0, 0)),
                  pl.BlockSpec((1, ROUTER_LANES), lambda i: (0, 0)),
                  pl.BlockSpec((tm, tm), lambda i: (0, 0))],
        out_specs=(pl.BlockSpec((8, tm), lambda i: (0, i)),
                   pl.BlockSpec((tm, ROUTER_LANES), lambda i: (i, 0)),
                   pl.BlockSpec((1, ROUTER_LANES), lambda i: (0, 0))),
        scratch_shapes=[pltpu.VMEM((1, ROUTER_LANES), F32)],
        compiler_params=_cparams(("arbitrary",)),
        name="moe_router",
    )(h, norm_w.reshape(1, D), wr, br, tri)


def _dispatch_kernel(dest_ref, zero_ref, h_ref, nw_ref, xs_ref, xn_ref, zbuf_ref, sem, zsem, *, tm, n_tok):
    xn_ref[...] = _ffn_norm(h_ref, nw_ref)

    @pl.when(pl.program_id(0) == 0)
    def _():
        zbuf_ref[...] = jnp.zeros(zbuf_ref.shape, zbuf_ref.dtype)
        rb = zbuf_ref.shape[0]

        def fill(r):
            return pltpu.make_async_copy(zbuf_ref, xs_ref.at[pl.ds(r * rb, rb)], zsem)

        for r in range(zero_ref.shape[0]):
            @pl.when(zero_ref[r] == 1)
            def _():
                fill(r).start()
        for r in range(zero_ref.shape[0]):
            @pl.when(zero_ref[r] == 1)
            def _():
                fill(r).wait()

    base = pl.program_id(0) * tm
    for t in range(tm):
        for slot in range(2):
            d = dest_ref[slot * n_tok + base + t]
            pltpu.make_async_copy(xn_ref.at[pl.ds(t, 1)], xs_ref.at[pl.ds(d, 1)], sem).start(priority=slot)
    for slot in range(2):
        pltpu.make_async_copy(xn_ref, xs_ref.at[pl.ds(0, tm)], sem).wait()


def _dispatch(h, norm_w, dest_flat, zero_blk, n_rows, tm=256):
    T, D = h.shape
    grid_spec = pltpu.PrefetchScalarGridSpec(
        num_scalar_prefetch=2,
        grid=(T // tm,),
        in_specs=[pl.BlockSpec((tm, D), lambda i, d, z: (i, 0)),
                  pl.BlockSpec((1, D), lambda i, d, z: (0, 0))],
        out_specs=pl.BlockSpec(memory_space=pl.ANY),
        scratch_shapes=[pltpu.VMEM((tm, D), F32), pltpu.VMEM((MOE_ROWS, D), F32),
                        pltpu.SemaphoreType.DMA, pltpu.SemaphoreType.DMA],
    )
    return pl.pallas_call(
        functools.partial(_dispatch_kernel, tm=tm, n_tok=T),
        out_shape=jax.ShapeDtypeStruct((n_rows, D), F32),
        grid_spec=grid_spec,
        compiler_params=_cparams(("arbitrary",)),
        name="moe_dispatch",
    )(dest_flat, zero_blk, h, norm_w.reshape(1, D))


def _moe_plan(counts, n_blk):
    RB = MOE_ROWS
    nb_e = (counts + RB - 1) // RB
    end_b = jnp.cumsum(nb_e)
    start_b = end_b - nb_e
    n_used = end_b[-1:].astype(I32)
    r = jnp.arange(n_blk, dtype=I32)
    be = jnp.minimum(jnp.sum((end_b[None, :] <= r[:, None]).astype(I32), axis=1), N_EXPERTS - 1)
    active = nb_e > 0
    ordinal = jnp.cumsum(active.astype(I32)) - 1
    idx = jnp.arange(N_EXPERTS, dtype=I32)
    later = jnp.where(active[None, :] & (idx[None, :] > idx[:, None]), idx[None, :], N_EXPERTS)
    nxt_e = jnp.min(later, axis=1)
    nxt_e = jnp.where(nxt_e == N_EXPERTS, -1, nxt_e)
    first = ((r == start_b[be]) & (r < n_used[0])).astype(I32)
    plan = jnp.stack([be, first, ordinal[be] % 2, nxt_e[be]]).astype(I32)
    zero_blk = ((r == end_b[be] - 1) | (r >= n_used[0])).astype(I32)
    return plan, n_used, (start_b * RB).astype(I32), zero_blk


def _stream_expert_weights(plan_ref, r, w_hbms, wbuf, sem, layer):
    e, slot, nxt = plan_ref[0, r], plan_ref[2, r], plan_ref[3, r]

    def copies(expert, s):
        rows = wbuf.shape[2] // 2
        return [(pltpu.make_async_copy(w.at[layer, expert, pl.ds(half * rows, rows)],
                                       wbuf.at[s, k, pl.ds(half * rows, rows)], sem.at[s, k, half]), half)
                for k, w in enumerate(w_hbms) for half in range(2)]

    @pl.when(plan_ref[1, r] == 1)
    def _():
        @pl.when(r == 0)
        def _():
            for c, half in copies(e, slot):
                c.start(priority=half)

        for c, _ in copies(e, slot):
            c.wait()

        @pl.when(nxt >= 0)
        def _():
            for c, half in copies(nxt, 1 - slot):
                c.start(priority=half)

    return slot


def _gate_up_kernel(plan_ref, nu_ref, x_ref, wg_hbm, wu_hbm, h_ref, wbuf, sem, *, layer):
    r = pl.program_id(0)

    @pl.when(r < nu_ref[0])
    def _():
        slot = _stream_expert_weights(plan_ref, r, (wg_hbm, wu_hbm), wbuf, sem, layer)
        x = x_ref[...]
        g = jnp.dot(x, wbuf[slot, 0], preferred_element_type=F32)
        u = jnp.dot(x, wbuf[slot, 1], preferred_element_type=F32)
        h_ref[...] = (jax.nn.silu(g) * u).astype(h_ref.dtype)

    @pl.when(r >= nu_ref[0])
    def _():
        h_ref[...] = jnp.zeros(h_ref.shape, h_ref.dtype)


def _down_kernel(plan_ref, nu_ref, h_ref, wd_hbm, y_ref, wbuf, sem, *, layer):
    r = pl.program_id(0)

    @pl.when(r < nu_ref[0])
    def _():
        slot = _stream_expert_weights(plan_ref, r, (wd_hbm,), wbuf, sem, layer)
        y_ref[...] = jnp.dot(h_ref[...].astype(F32), wbuf[slot, 0], preferred_element_type=F32)

    @pl.when(r >= nu_ref[0])
    def _():
        y_ref[...] = jnp.zeros(y_ref.shape, y_ref.dtype)


def _expert_mlp(xs, plan, n_used, w_gate, w_up, w_down, layer):
    n_rows, D = xs.shape
    RB = MOE_ROWS
    n_blk = n_rows // RB
    F = D_EXPERT

    def used_blk(r, plan, nu):
        return (jnp.minimum(r, nu[0] - 1), 0)

    h = pl.pallas_call(
        functools.partial(_gate_up_kernel, layer=layer),
        out_shape=jax.ShapeDtypeStruct((n_rows, F), BF16),
        grid_spec=pltpu.PrefetchScalarGridSpec(
            num_scalar_prefetch=2,
            grid=(n_blk,),
            in_specs=[pl.BlockSpec((RB, D), used_blk),
                      pl.BlockSpec(memory_space=pl.ANY),
                      pl.BlockSpec(memory_space=pl.ANY)],
            out_specs=pl.BlockSpec((RB, F), lambda r, plan, nu: (r, 0)),
            scratch_shapes=[pltpu.VMEM((2, 2, D, F), F32), pltpu.SemaphoreType.DMA((2, 2, 2))],
        ),
        compiler_params=_cparams(("arbitrary",)),
        name="moe_gate_up",
    )(plan, n_used, xs, w_gate, w_up)
    y = pl.pallas_call(
        functools.partial(_down_kernel, layer=layer),
        out_shape=jax.ShapeDtypeStruct((n_rows, D), F32),
        grid_spec=pltpu.PrefetchScalarGridSpec(
            num_scalar_prefetch=2,
            grid=(n_blk,),
            in_specs=[pl.BlockSpec((RB, F), used_blk),
                      pl.BlockSpec(memory_space=pl.ANY)],
            out_specs=pl.BlockSpec((RB, D), lambda r, plan, nu: (r, 0)),
            scratch_shapes=[pltpu.VMEM((2, 1, F, D), F32), pltpu.SemaphoreType.DMA((2, 1, 2))],
        ),
        compiler_params=_cparams(("arbitrary",)),
        name="moe_down",
    )(plan, n_used, h, w_down)
    return y


def _combine_kernel(dest_ref, resid_ref, wts_ref, nw_ref, y_ref, *rest, tm, n_tok, keep_h):
    if keep_h:
        o_ref, n_ref, buf_ref, sem = rest
    else:
        o_ref = None
        n_ref, buf_ref, sem = rest
    i = pl.program_id(0)
    n = pl.num_programs(0)

    def gather(tile, par):
        for t in range(tm):
            for slot in range(2):
                d = dest_ref[slot * n_tok + tile * tm + t]
                pltpu.make_async_copy(y_ref.at[pl.ds(d, 1)], buf_ref.at[par, slot, pl.ds(t, 1)],
                                      sem.at[par]).start(priority=slot)

    @pl.when(i == 0)
    def _():
        gather(i, 0)

    for par in range(2):
        @pl.when((i + 1 < n) & ((i + 1) % 2 == par))
        def _():
            gather(i + 1, par)

    for par in range(2):
        @pl.when(i % 2 == par)
        def _():
            for slot in range(2):
                pltpu.make_async_copy(y_ref.at[pl.ds(0, tm)], buf_ref.at[par, slot], sem.at[par]).wait()
            w = wts_ref[...]
            hn = resid_ref[...] + (w[:, 0:1] * buf_ref[par, 0] + w[:, 1:2] * buf_ref[par, 1])
            if keep_h:
                o_ref[...] = hn
            yn = hn * lax.rsqrt(jnp.mean(hn * hn, axis=-1, keepdims=True) + EPS)
            n_ref[...] = (yn * nw_ref[...]).astype(n_ref.dtype)


def _combine(resid, wts, y, dest_flat, next_norm_w, norm_dtype, keep_h, norm_shape=None, tm=128):
    T, D = resid.shape
    row_spec = pl.BlockSpec((tm, D), lambda i, d: (i, 0))
    if norm_shape is None:
        out_shape = [jax.ShapeDtypeStruct((T, D), norm_dtype)]
        out_specs = [row_spec]
    else:
        per_seq = norm_shape[1] // tm
        out_shape = [jax.ShapeDtypeStruct(norm_shape, norm_dtype)]
        out_specs = [pl.BlockSpec((None, tm, D), lambda i, d: (i // per_seq, i % per_seq, 0))]
    if keep_h:
        out_shape.insert(0, jax.ShapeDtypeStruct((T, D), F32))
        out_specs.insert(0, row_spec)
    grid_spec = pltpu.PrefetchScalarGridSpec(
        num_scalar_prefetch=1,
        grid=(T // tm,),
        in_specs=[row_spec,
                  pl.BlockSpec((tm, ROUTER_LANES), lambda i, d: (i, 0)),
                  pl.BlockSpec((1, D), lambda i, d: (0, 0)),
                  pl.BlockSpec(memory_space=pl.ANY)],
        out_specs=tuple(out_specs),
        scratch_shapes=[pltpu.VMEM((2, 2, tm, D), F32), pltpu.SemaphoreType.DMA((2,))],
    )
    outs = pl.pallas_call(
        functools.partial(_combine_kernel, tm=tm, n_tok=T, keep_h=keep_h),
        out_shape=tuple(out_shape),
        grid_spec=grid_spec,
        compiler_params=_cparams(("arbitrary",)),
        name="moe_combine",
    )(dest_flat, resid, wts, next_norm_w.reshape(1, D), y)
    return (outs[0], outs[1]) if keep_h else (None, outs[0])


def _hier_moe(h, norm_w, w_group, b_group, w_expert, b_expert, w_gate, w_up, w_down, layer,
              next_norm_w, norm_dtype, keep_h, norm_shape=None):
    T, D = h.shape
    RB = MOE_ROWS
    n_blk = (T * 2) // RB + N_EXPERTS
    n_rows = n_blk * RB
    ids, wts, cnt = _router(h, norm_w, w_group, b_group, w_expert, b_expert)
    counts = cnt[0, N_GROUPS:N_GROUPS + N_EXPERTS].astype(I32)
    plan, n_used, start_padded, zero_blk = _moe_plan(counts, n_blk)
    is_e = ids[None, 0:2] == jnp.arange(N_EXPERTS, dtype=I32)[:, None, None]
    dest = (jnp.sum(jnp.where(is_e, start_padded[:, None, None], 0), axis=0) + ids[2:4]).reshape(-1)
    xs = _dispatch(h, norm_w, dest, zero_blk, n_rows)
    y = _expert_mlp(xs, plan, n_used, w_gate, w_up, w_down, layer)
    return _combine(h, wts, y, dest, next_norm_w, norm_dtype, keep_h, norm_shape)


def _compress_kernel(x_ref, w1_ref, w2_ref, pe_ref, o_ref, xf_ref):
    d = B_D
    half = CMP_STRIDE * d
    nch = x_ref.shape[0] // CMP_STRIDE
    xf_ref[...] = x_ref[...].astype(F32)
    w1 = w1_ref[...].astype(BF16)
    a = jnp.zeros((nch, w1.shape[1]), F32)
    b = jnp.zeros((nch, w1.shape[1]), F32)
    for l in range(CMP_STRIDE):
        xl = xf_ref[pl.ds(l, nch, stride=CMP_STRIDE), :].astype(BF16)
        a = a + jnp.dot(xl, w1[l * d:(l + 1) * d], preferred_element_type=F32)
        b = b + jnp.dot(xl, w1[half + l * d:half + (l + 1) * d], preferred_element_type=F32)
    pe = jnp.broadcast_to(pe_ref[...], (8, 2 * half)).astype(BF16)
    pt = jnp.dot(pe, w1, preferred_element_type=F32)[0:1]
    hid = a + pltpu.roll(b, b.shape[0] - 1, 0) + pt
    g = jax.nn.gelu(hid)
    o_ref[...] = jnp.dot(g.astype(BF16), w2_ref[...].astype(BF16),
                         preferred_element_type=F32).astype(o_ref.dtype)


def _compress(projm, w1, w2, pe_flat, B, S):
    G, d = B_GROUPS, B_D
    nch = S // CMP_STRIDE
    width = CMP_STRIDE * d
    hid = w1.shape[-1]
    cb = B_HEADS
    return pl.pallas_call(
        _compress_kernel,
        out_shape=jax.ShapeDtypeStruct((2, B, G, nch, d), BF16),
        grid=(2, B, G),
        in_specs=[pl.BlockSpec((S, d), lambda kv, b, g: (b, cb + kv * G + g)),
                  pl.BlockSpec((None, 2 * width, hid), lambda kv, b, g: (kv, 0, 0)),
                  pl.BlockSpec((None, hid, d), lambda kv, b, g: (kv, 0, 0)),
                  pl.BlockSpec((None, 1, 2 * width), lambda kv, b, g: (kv, 0, 0))],
        out_specs=pl.BlockSpec((None, None, None, nch, d), lambda kv, b, g: (kv, b, g, 0, 0)),
        scratch_shapes=[pltpu.VMEM((S, d), F32)],
        compiler_params=_cparams(("arbitrary", "arbitrary", "arbitrary")),
        name="nsa_compress",
    )(projm, w1, w2, pe_flat)


def _cmp_select_kernel(slopes_ref, q_ref, kc_ref, vc_ref, ov_ref, o_ref, sel_ref, need_ref, *, tq, n_cmp,
                       n_blocks, n_rank):
    g = pl.program_id(1)
    i = pl.program_id(2)
    d = B_D
    nch = kc_ref.shape[0]
    pos = i * tq + lax.broadcasted_iota(I32, (1, tq), 1)
    c = lax.broadcasted_iota(I32, (nch, 1), 0)
    dist_i = pos - (c * CMP_STRIDE + (CMP_LEN - 1))
    valid = (dist_i >= 0) & (c < n_cmp)
    dist = dist_i.astype(F32)
    any_valid = jnp.where(pos >= CMP_LEN - 1, 1.0, 0.0)
    kc = kc_ref[...]
    vc = vc_ref[...]
    psum = jnp.zeros((nch, tq), F32)
    for hh in range(B_HPG):
        s = _nt_dot(kc, q_ref[:, hh * d:(hh + 1) * d])
        s = jnp.where(valid, s - (slopes_ref[g * B_HPG + hh] * LOG2E) * dist, NEG)
        e = jnp.exp2(s - jnp.max(s, axis=0, keepdims=True))
        p = (e / jnp.sum(e, axis=0, keepdims=True)) * any_valid
        oT = lax.dot_general(vc, p.astype(BF16), (((0,), (0,)), ((), ())), preferred_element_type=F32)
        o_ref[:, hh * d:(hh + 1) * d] = oT.T.astype(o_ref.dtype)
        psum = psum + p

    p_hi = psum.astype(BF16)
    p_lo = (psum - p_hi.astype(F32)).astype(BF16)
    ov = ov_ref[...]
    imp = jnp.dot(ov, p_hi, preferred_element_type=F32) + jnp.dot(ov, p_lo, preferred_element_type=F32)
    imp = imp[:n_rank]

    blk = lax.broadcasted_iota(I32, (n_rank, 1), 0)
    cur = lax.shift_right_logical(pos, SLC_BLOCK.bit_length() - 1)
    forced = (blk == 0) | (blk == cur) | (blk == cur - 1)
    causal = blk * SLC_BLOCK <= pos
    score = jnp.where(forced, FORCE, jnp.where(causal, imp, -FORCE))
    score = jnp.where(blk < n_blocks, score, -2.0 * FORCE)
    blk_f = blk.astype(F32)
    remaining = score
    sel_add = jnp.full(score.shape, NEG, F32)
    for _ in range(SLC_TOPN):
        top = jnp.max(remaining, axis=0, keepdims=True)
        first = jnp.min(jnp.where(remaining == top, blk_f, float(n_rank)), axis=0, keepdims=True)
        pick = blk_f == first
        sel_add = jnp.where(pick, 0.0, sel_add)
        remaining = jnp.where(pick, -4.0 * FORCE, remaining)
    pad = jnp.zeros((128 - n_rank, tq), F32)
    sel_t = jnp.concatenate([jnp.where(blk < n_blocks, sel_add, 0.0), pad], axis=0) if n_rank < 128 else sel_add
    sel_ref[...] = sel_t.T.astype(sel_ref.dtype)
    per_tile = SLC_KEY_TILE // SLC_BLOCK
    blk_any = jnp.max(jnp.where(sel_add == 0.0, 1, 0), axis=1, keepdims=True)
    rows = []
    for j in range(8):
        if (j + 1) * per_tile <= n_rank:
            hit = jnp.max(blk_any[j * per_tile:(j + 1) * per_tile], axis=0, keepdims=True)
            rows.append(jnp.broadcast_to(hit, (1, 128)))
        else:
            rows.append(jnp.zeros((1, 128), I32))
    need_ref[...] = jnp.concatenate(rows, axis=0)


def _cmp_select(projm, cmp_kv, overlap, B, S, tq=512):
    T = B * S
    G, d = B_GROUPS, B_D
    nq = S // tq
    nch = cmp_kv.shape[3]
    n_cmp = (S - CMP_LEN) // CMP_STRIDE + 1
    n_blocks = S // SLC_BLOCK
    assert SLC_TOPN <= n_blocks <= FEAT_BLK
    slopes = _alibi_slopes(B_HEADS)
    grid_spec = pltpu.PrefetchScalarGridSpec(
        num_scalar_prefetch=1,
        grid=(B, G, nq),
        in_specs=[pl.BlockSpec((tq, B_HPG * d), lambda b, g, i, sl: (b * nq + i, g)),
                  pl.BlockSpec((None, None, None, nch, d), lambda b, g, i, sl: (0, b, g, 0, 0)),
                  pl.BlockSpec((None, None, None, nch, d), lambda b, g, i, sl: (1, b, g, 0, 0)),
                  pl.BlockSpec((128, nch), lambda b, g, i, sl: (0, 0))],
        out_specs=(pl.BlockSpec((tq, B_HPG * d), lambda b, g, i, sl: (b * nq + i, g)),
                   pl.BlockSpec((None, None, tq, 128), lambda b, g, i, sl: (b, g, i, 0)),
                   pl.BlockSpec((None, None, None, 8, 128), lambda b, g, i, sl: (b, g, i, 0, 0))),
    )
    assert S // SLC_KEY_TILE <= 8
    return pl.pallas_call(
        functools.partial(_cmp_select_kernel, tq=tq, n_cmp=n_cmp, n_blocks=n_blocks,
                          n_rank=-(-n_blocks // 8) * 8),
        out_shape=(jax.ShapeDtypeStruct((T, B_HEADS * d), BF16),
                   jax.ShapeDtypeStruct((B, G, S, 128), BF16),
                   jax.ShapeDtypeStruct((B, G, nq, 8, 128), I32)),
        grid_spec=grid_spec,
        compiler_params=_cparams(("parallel", "parallel", "arbitrary")),
        name="nsa_cmp_select",
    )(slopes, projm, cmp_kv, cmp_kv, overlap)


def _head_units(k_ref, v_ref, feat_ref, qa_ref, j, mask, qc):
    n_heads, tq, _ = qa_ref.shape
    k0 = pl.multiple_of(j * tq, tq)
    v = v_ref[pl.ds(k0, tq), :]

    def score_fn(hh, c):
        def fn():
            ka = jnp.concatenate([k_ref[pl.ds(k0, tq), :], feat_ref[pl.ds(k0, tq), :]], axis=1)
            sT = _nt_dot(ka, qa_ref[hh, c * qc:(c + 1) * qc, :])
            return sT if mask is None else sT + mask[:, c * qc:(c + 1) * qc]
        return fn

    return [(score_fn(hh, c), v, hh * (tq // qc) + c) for hh in range(n_heads) for c in range(tq // qc)]


def _write_heads_T(o_ref, acc_ref, l_ref):
    tq = o_ref.shape[0]
    qc = l_ref.shape[-1]
    for hh in range(B_HPG):
        for c in range(tq // qc):
            u = hh * (tq // qc) + c
            o_ref[c * qc:(c + 1) * qc, hh * B_D:(hh + 1) * B_D] = (acc_ref[u] / l_ref[u]).T.astype(o_ref.dtype)


def _flash_state(n_heads, tq, dv):
    qc = min(ATTN_QC, tq)
    n_units = n_heads * (tq // qc)
    return [pltpu.VMEM((n_units, 1, qc), F32), pltpu.VMEM((n_units, 1, qc), F32),
            pltpu.VMEM((n_units, dv, qc), F32)]


def _stage_queries(q_ref, qa_ref, slopes_ref, g, q0, sel_add):
    tq = q_ref.shape[0]
    lane = lax.broadcasted_iota(I32, (tq, 128), 1)
    for hh in range(B_HPG):
        coef = jnp.broadcast_to(_alibi_query_features(slopes_ref[g * B_HPG + hh] * LOG2E, q0), (tq, 128))
        if sel_add is not None:
            coef = jnp.where(lane < FEAT_BLK, sel_add, coef)
        qa_ref[hh] = jnp.concatenate([q_ref[:, hh * B_D:(hh + 1) * B_D], coef.astype(BF16)], axis=1)


def _slc_attn_kernel(slopes_ref, need_ref, q_ref, k_ref, v_ref, feat_ref, sel_ref, o_ref, qa_ref, m_ref, l_ref,
                     acc_ref, *, tq):
    b = pl.program_id(0)
    g = pl.program_id(1)
    i = pl.program_id(2)
    n = pl.num_programs(2)
    _flash_init(m_ref, l_ref, acc_ref)
    _stage_queries(q_ref, qa_ref, slopes_ref, g, i * tq, sel_ref[...].astype(F32))

    qc = m_ref.shape[-1]

    def units(j, causal=None):
        return _head_units(k_ref, v_ref, feat_ref, qa_ref, j, causal, qc)

    need_base = ((b * pl.num_programs(1) + g) * n + i) * n

    def needed(j):
        return (j == 0) | (need_ref[need_base + jnp.maximum(j, 0)] > 0)

    def body(p, carry):
        j0, j1 = 2 * p, 2 * p + 1
        n0, n1 = needed(j0), needed(j1)

        @pl.when(n0 & n1)
        def _():
            _flash_units(units(j0) + units(j1), m_ref, l_ref, acc_ref)

        @pl.when(n0 & jnp.logical_not(n1))
        def _():
            _flash_units(units(j0), m_ref, l_ref, acc_ref)

        @pl.when(jnp.logical_not(n0) & n1)
        def _():
            _flash_units(units(j1), m_ref, l_ref, acc_ref)

        return carry

    lax.fori_loop(0, lax.shift_right_logical(i, 1), body, 0)

    @pl.when((i % 2 == 1) & needed(i - 1))
    def _():
        _flash_units(units(i - 1), m_ref, l_ref, acc_ref)

    _flash_units(units(i, _tri_mask(tq, tq, True)), m_ref, l_ref, acc_ref)
    _write_heads_T(o_ref, acc_ref, l_ref)


def _slc_attention(projm, sel, need, B, S):
    T = B * S
    G, d = B_GROUPS, B_D
    tq = SLC_KEY_TILE
    n = S // tq
    slopes = _alibi_slopes(B_HEADS)
    ksb = (B_HEADS * d + 2 * G * d) // d
    vsb = ksb + G
    grid_spec = pltpu.PrefetchScalarGridSpec(
        num_scalar_prefetch=2,
        grid=(B, G, n),
        in_specs=[
            pl.BlockSpec((tq, B_HPG * d), lambda b, g, i, sl, nd: (b * n + i, g)),
            pl.BlockSpec((S, d), lambda b, g, i, sl, nd: (b, ksb + g)),
            pl.BlockSpec((S, d), lambda b, g, i, sl, nd: (b, vsb + g)),
            pl.BlockSpec((S, 128), lambda b, g, i, sl, nd: (0, 0)),
            pl.BlockSpec((None, None, tq, 128), lambda b, g, i, sl, nd: (b, g, i, 0)),
        ],
        out_specs=pl.BlockSpec((tq, B_HPG * d), lambda b, g, i, sl, nd: (b * n + i, g)),
        scratch_shapes=[pltpu.VMEM((B_HPG, tq, 2 * d), BF16)] + _flash_state(B_HPG, tq, d),
    )
    return pl.pallas_call(
        functools.partial(_slc_attn_kernel, tq=tq),
        out_shape=jax.ShapeDtypeStruct((T, B_HEADS * d), BF16),
        grid_spec=grid_spec,
        compiler_params=_cparams(("parallel", "parallel", "arbitrary")),
        name="nsa_selected_attention",
    )(slopes, need.reshape(-1), projm, projm, projm, _key_features(S), sel)


def _win_attn_kernel(slopes_ref, q_ref, k_ref, v_ref, feat_ref, gl_ref, oc_ref, os_ref, o_ref, qa_ref, m_ref,
                     l_ref, acc_ref, *, tq):
    g = pl.program_id(1)
    i = pl.program_id(2)
    _flash_init(m_ref, l_ref, acc_ref)
    _stage_queries(q_ref, qa_ref, slopes_ref, g, i * tq, None)

    qc = m_ref.shape[-1]

    def units(j, mask):
        return _head_units(k_ref, v_ref, feat_ref, qa_ref, j, mask, qc)

    @pl.when(i > 0)
    def _():
        _flash_units(units(i - 1, _tri_mask(tq, tq, False)) + units(i, _tri_mask(tq, tq, True)),
                     m_ref, l_ref, acc_ref)

    @pl.when(i == 0)
    def _():
        _flash_units(units(i, _tri_mask(tq, tq, True)), m_ref, l_ref, acc_ref)

    n_chunks = tq // qc
    for hh in range(B_HPG):
        row = (g * B_HPG + hh) * 3
        for c in range(n_chunks):
            qs = slice(c * qc, (c + 1) * qc)
            cols = slice(hh * B_D, (hh + 1) * B_D)
            u = hh * n_chunks + c
            gates = [jax.nn.sigmoid(gl_ref[pl.ds(row + br, 1), qs]) for br in range(3)]
            mix = (gates[0] * oc_ref[qs, cols].astype(F32).T + gates[1] * os_ref[qs, cols].astype(F32).T
                   + gates[2] * (acc_ref[u] / l_ref[u]))
            o_ref[qs, cols] = mix.T.astype(o_ref.dtype)


def _win_attention(projm, gate_logits_t, o_cmp, o_slc, B, S):
    T = B * S
    G, d = B_GROUPS, B_D
    tq = WINDOW
    n = S // tq
    slopes = _alibi_slopes(B_HEADS)
    kwb = (B_HEADS * d + 4 * G * d) // d
    vwb = kwb + G
    grid_spec = pltpu.PrefetchScalarGridSpec(
        num_scalar_prefetch=1,
        grid=(B, G, n),
        in_specs=[
            pl.BlockSpec((tq, B_HPG * d), lambda b, g, i, sl: (b * n + i, g)),
            pl.BlockSpec((S, d), lambda b, g, i, sl: (b, kwb + g)),
            pl.BlockSpec((S, d), lambda b, g, i, sl: (b, vwb + g)),
            pl.BlockSpec((S, 128), lambda b, g, i, sl: (0, 0)),
            pl.BlockSpec((128, tq), lambda b, g, i, sl: (0, b * n + i)),
            pl.BlockSpec((tq, B_HPG * d), lambda b, g, i, sl: (b * n + i, g)),
            pl.BlockSpec((tq, B_HPG * d), lambda b, g, i, sl: (b * n + i, g)),
        ],
        out_specs=pl.BlockSpec((tq, B_HPG * d), lambda b, g, i, sl: (b * n + i, g)),
        scratch_shapes=[pltpu.VMEM((B_HPG, tq, 2 * d), BF16)] + _flash_state(B_HPG, tq, d),
    )
    return pl.pallas_call(
        functools.partial(_win_attn_kernel, tq=tq),
        out_shape=jax.ShapeDtypeStruct((T, B_HEADS * d), BF16),
        grid_spec=grid_spec,
        compiler_params=_cparams(("parallel", "parallel", "arbitrary")),
        name="nsa_window_attention",
    )(slopes, projm, projm, projm, _key_features(S), gate_logits_t, o_cmp, o_slc)


def _overlap_matrix(S, n_rows):
    nc = (S - CMP_LEN) // CMP_STRIDE + 1
    nsb = S // SLC_BLOCK
    cs = np.arange(nc) * CMP_STRIDE
    ss = np.arange(nsb) * SLC_BLOCK
    ov = np.clip(np.minimum(cs[:, None] + CMP_LEN, ss[None, :] + SLC_BLOCK)
                 - np.maximum(cs[:, None], ss[None, :]), 0, None) / CMP_LEN
    out = np.zeros((128, n_rows), np.float32)
    out[:nsb, :nc] = ov.T
    return jnp.asarray(out, BF16)


def _nsa_attention(xn, w_in, cmp_pos, cmp_w1, cmp_w2, B, S):
    T = B * S
    H, G, d = B_HEADS, B_GROUPS, B_D
    n_main = H * d + 6 * G * d
    col_scale = jnp.concatenate([jnp.full((H * d,), d ** -0.5 * LOG2E, F32), jnp.ones((n_main - H * d,), F32)])
    w_t = jnp.swapaxes(w_in, 0, 1)
    projm = _matmul(xn, w_t, n_main, BF16, col_scale=col_scale, w_is_transposed=True)
    w_gate_t = jnp.pad(w_t[n_main:], ((0, 128 - 3 * H), (0, 0)))
    gate_logits = _matmul(xn, w_gate_t, 128, F32, tn=128, w_is_transposed=True)

    cmp_kv = _compress(projm, cmp_w1, cmp_w2, cmp_pos.reshape(2, 1, CMP_LEN * d), B, S)
    o_cmp, sel, need = _cmp_select(projm, cmp_kv, _overlap_matrix(S, S // CMP_STRIDE), B, S)
    n_t = S // SLC_KEY_TILE
    need = need[:, :, :, :n_t, 0].reshape(B, G, n_t, -1, n_t).max(axis=3)
    o_slc = _slc_attention(projm, sel, need, B, S)
    return _win_attention(projm, gate_logits.T, o_cmp, o_slc, B, S)


def kernel(x, attn_norm_w, ffn_norm_w, final_norm_w, a_w_in, a_lambda, a_subln_w, a_w_out, b_w_in,
           b_cmp_pos, b_cmp_w1, b_cmp_w2, b_w_out, moe_w_group, moe_b_group, moe_w_expert,
           moe_b_expert, moe_w_gate, moe_w_up, moe_w_down):
    B, S, D = x.shape
    T = B * S
    depth = attn_norm_w.shape[0]
    h = x
    xn = _rmsnorm(h, attn_norm_w[0], BF16)
    for i in range(depth):
        j = i // 2
        if i % 2 == 0:
            lambda_init = 0.8 - 0.6 * math.exp(-0.3 * i)
            n_q = A_HEADS * 2 * A_DH
            n_in = 2 * n_q + A_HEADS * A_DV
            col_scale = jnp.concatenate([jnp.full((n_q,), A_DH ** -0.5 * LOG2E, F32),
                                         jnp.ones((n_in - n_q,), F32)])
            qkv = _matmul(xn, a_w_in[j], n_in, BF16, col_scale=col_scale)
            o = _diff_attention(qkv, a_lambda[j], a_subln_w[j], lambda_init, B, S)
            h = _matmul(o, a_w_out[j], D, F32, resid=h)
        else:
            o = _nsa_attention(xn, b_w_in[j], b_cmp_pos[j], b_cmp_w1[j], b_cmp_w2[j], B, S)
            h = _matmul(o, b_w_out[j], D, F32, resid=h)
        last = i == depth - 1
        h, xn = _hier_moe(h, ffn_norm_w[i], moe_w_group[i], moe_b_group[i], moe_w_expert[i], moe_b_expert[i],
                          moe_w_gate, moe_w_up, moe_w_down, i,
                          final_norm_w if last else attn_norm_w[i + 1], F32 if last else BF16, not last,
                          (B, S, D) if last else None)
    return xn
```

```python
import functools
import math

import numpy as np
import jax
import jax.numpy as jnp
from jax import lax
from jax.experimental import pallas as pl
from jax.experimental.pallas import tpu as pltpu

F32 = jnp.float32
BF16 = jnp.bfloat16
I32 = jnp.int32

EPS = 1e-6
NEG = -1e30
FORCE = 1e4
LOG2E = math.log2(math.e)

A_HEADS = 8
A_DH = 128
A_DV = 256
B_HEADS = 16
B_GROUPS = 4
B_HPG = 4
B_D = 128
CMP_LEN = 32
CMP_STRIDE = 16
SLC_BLOCK = 64
SLC_TOPN = 16
WINDOW = 512
N_GROUPS = 4
EPG = 8
N_EXPERTS = 32
D_EXPERT = 1024
ROUTER_LANES = 128
MOE_ROWS = 256
ATTN_QC = 512
SLC_KEY_TILE = 512

VMEM_LIMIT = 56 * 1024 * 1024


def _cparams(sem, flags=None):
    return pltpu.CompilerParams(dimension_semantics=sem, vmem_limit_bytes=VMEM_LIMIT, flags=flags)


def _alibi_slopes(n):
    return jnp.asarray(np.array([2.0 ** (-8.0 * (i + 1) / n) for i in range(n)], dtype=np.float32))


def _nt_dot(a, b):
    return lax.dot_general(a, b, (((1,), (1,)), ((), ())), preferred_element_type=F32)


def _rms_kernel(x_ref, w_ref, o_ref):
    x = x_ref[...]
    y = x * lax.rsqrt(jnp.mean(x * x, axis=-1, keepdims=True) + EPS)
    o_ref[...] = (y * w_ref[...]).astype(o_ref.dtype)


def _rows_spec(x, tm, tn, col_arg):
    if x.ndim == 2:
        return pl.BlockSpec((tm, tn), lambda *g: (g[-1], col_arg(g)))
    per_seq = x.shape[1] // tm
    return pl.BlockSpec((None, tm, tn), lambda *g: (g[-1] // per_seq, g[-1] % per_seq, col_arg(g)))


def _rmsnorm(x, w, out_dtype, tm=512):
    D = x.shape[-1]
    T = x.size // D
    return pl.pallas_call(
        _rms_kernel,
        out_shape=jax.ShapeDtypeStruct((T, D), out_dtype),
        grid=(T // tm,),
        in_specs=[_rows_spec(x, tm, D, lambda g: 0),
                  pl.BlockSpec((1, D), lambda i: (0, 0))],
        out_specs=pl.BlockSpec((tm, D), lambda i: (i, 0)),
        compiler_params=_cparams(("parallel",)),
        name="rmsnorm",
    )(x, w.reshape(1, D))


def _mm_kernel(*refs, has_scale, has_resid, w_is_transposed):
    a_ref, w_ref = refs[0], refs[1]
    k = 2
    scale_ref = resid_ref = None
    if has_scale:
        scale_ref = refs[k]
        k += 1
    if has_resid:
        resid_ref = refs[k]
        k += 1
    o_ref, wbf_ref = refs[k], refs[k + 1]

    @pl.when(pl.program_id(1) == 0)
    def _():
        wbf_ref[...] = w_ref[...].astype(BF16)

    if w_is_transposed:
        acc = _nt_dot(a_ref[...], wbf_ref[...])
    else:
        acc = jnp.dot(a_ref[...], wbf_ref[...], preferred_element_type=F32)
    if has_scale:
        acc = acc * scale_ref[...]
    if has_resid:
        acc = acc + resid_ref[...]
    o_ref[...] = acc.astype(o_ref.dtype)


def _matmul(a, w, n_out, out_dtype, *, col_scale=None, resid=None, tm=1024, tn=1024, w_is_transposed=False):
    M, K = a.shape
    if w_is_transposed:
        assert M % tm == 0 and n_out % tn == 0 and w.shape[1] == K and w.shape[0] >= n_out
        w_spec, w_block = pl.BlockSpec((tn, K), lambda j, i: (j, 0)), (tn, K)
    else:
        assert M % tm == 0 and n_out % tn == 0 and w.shape[0] == K and w.shape[1] >= n_out
        w_spec, w_block = pl.BlockSpec((K, tn), lambda j, i: (0, j)), (K, tn)
    in_specs = [pl.BlockSpec((tm, K), lambda j, i: (i, 0)), w_spec]
    args = [a, w]
    if col_scale is not None:
        in_specs.append(pl.BlockSpec((1, tn), lambda j, i: (0, j)))
        args.append(col_scale.reshape(1, n_out))
    if resid is not None:
        in_specs.append(_rows_spec(resid, tm, tn, lambda g: g[0]))
        args.append(resid)
    return pl.pallas_call(
        functools.partial(_mm_kernel, has_scale=col_scale is not None, has_resid=resid is not None,
                          w_is_transposed=w_is_transposed),
        out_shape=jax.ShapeDtypeStruct((M, n_out), out_dtype),
        grid=(n_out // tn, M // tm),
        in_specs=in_specs,
        out_specs=pl.BlockSpec((tm, tn), lambda j, i: (i, j)),
        scratch_shapes=[pltpu.VMEM(w_block, BF16)],
        compiler_params=_cparams(("parallel", "arbitrary")),
        name="matmul",
    )(*args)


FEAT_BLK = 64
FEAT_POS = 67
FEAT_ONE = 70


def _key_features(S):
    assert S // SLC_BLOCK <= FEAT_BLK
    j = np.arange(S)
    f = np.zeros((S, 128), np.float32)
    f[j, j // SLC_BLOCK] = 1.0
    f[:, FEAT_BLK:FEAT_BLK + 3] = (j // SLC_BLOCK)[:, None]
    f[:, FEAT_POS:FEAT_POS + 3] = (j % SLC_BLOCK)[:, None]
    f[:, FEAT_ONE:FEAT_ONE + 3] = 1.0
    return jnp.asarray(f, BF16)


def _alibi_query_features(slope2, q0):
    lane = lax.broadcasted_iota(I32, (1, 128), 1)
    base = jnp.where((lane >= FEAT_BLK) & (lane < FEAT_BLK + 3), slope2 * float(SLC_BLOCK),
                     jnp.where((lane >= FEAT_POS) & (lane < FEAT_POS + 3), slope2,
                               jnp.where((lane >= FEAT_ONE) & (lane < FEAT_ONE + 3),
                                         -slope2 * q0.astype(F32), 0.0)))
    hi = base.astype(BF16).astype(F32)
    r1 = base - hi
    lo = r1.astype(BF16).astype(F32)
    lo2 = (r1 - lo).astype(BF16).astype(F32)
    first = (lane == FEAT_BLK) | (lane == FEAT_POS) | (lane == FEAT_ONE)
    second = (lane == FEAT_BLK + 1) | (lane == FEAT_POS + 1) | (lane == FEAT_ONE + 1)
    return jnp.where(first, hi, jnp.where(second, lo, lo2))


def _flash_init(m_ref, l_ref, acc_ref):
    m_ref[...] = jnp.full(m_ref.shape, NEG, F32)
    l_ref[...] = jnp.zeros(l_ref.shape, F32)
    acc_ref[...] = jnp.zeros(acc_ref.shape, F32)


def _flash_probs(sT, m_ref, l_ref, u):
    m_old = m_ref[u]
    m_new = jnp.maximum(m_old, jnp.max(sT, axis=0, keepdims=True))
    alpha = jnp.exp2(m_old - m_new)
    p = jnp.exp2(sT - m_new)
    l_ref[u] = alpha * l_ref[u] + jnp.sum(p, axis=0, keepdims=True)
    m_ref[u] = m_new
    return p.astype(BF16), alpha


def _flash_accumulate(p, alpha, v, acc_ref, u):
    pv = lax.dot_general(v, p, (((0,), (0,)), ((), ())), preferred_element_type=F32)
    acc_ref[u] = alpha * acc_ref[u] + pv


def _flash_units(units, m_ref, l_ref, acc_ref):
    n = len(units)
    scores, probs = {}, {}
    for s in range(n + 2):
        if s < n:
            scores[s] = units[s][0]()
        if 1 <= s <= n:
            probs[s - 1] = _flash_probs(scores.pop(s - 1), m_ref, l_ref, units[s - 1][2])
        if 2 <= s <= n + 1:
            p, alpha = probs.pop(s - 2)
            _flash_accumulate(p, alpha, units[s - 2][1], acc_ref, units[s - 2][2])


def _tri_mask(tk, tq, keep_upper):
    r = lax.broadcasted_iota(I32, (tk, tq), 0)
    c = lax.broadcasted_iota(I32, (tk, tq), 1)
    keep = (r <= c) if keep_upper else (r > c)
    return jnp.where(keep, 0.0, NEG)


def _diff_attn_kernel(slopes_ref, q1_ref, q2_ref, k1_ref, k2_ref, v_ref, feat_ref, lam_ref, sw_ref, o_ref,
                      qa_ref, m_ref, l_ref, acc_ref, *, tq, lambda_init):
    h = pl.program_id(1)
    i = pl.program_id(2)
    _flash_init(m_ref, l_ref, acc_ref)
    qfeat = jnp.broadcast_to(_alibi_query_features(slopes_ref[h] * LOG2E, i * tq), (tq, 128)).astype(BF16)
    for c, q_ref in enumerate((q1_ref, q2_ref)):
        qa_ref[c] = jnp.concatenate([q_ref[...], qfeat], axis=1)

    def tile_units(j, mask):
        k0 = pl.multiple_of(j * tq, tq)
        v = v_ref[pl.ds(k0, tq), :]
        feat = feat_ref[pl.ds(k0, tq), :]
        qc = m_ref.shape[-1]

        def score_fn(c, k_ref, x):
            def fn():
                ka = jnp.concatenate([k_ref[pl.ds(k0, tq), :], feat], axis=1)
                sT = _nt_dot(ka, qa_ref[c, x * qc:(x + 1) * qc, :])
                return sT if mask is None else sT + mask[:, x * qc:(x + 1) * qc]
            return fn

        return [(score_fn(c, k_ref, x), v, c * (tq // qc) + x)
                for c, k_ref in enumerate((k1_ref, k2_ref)) for x in range(tq // qc)]

    def body(p, carry):
        _flash_units(tile_units(2 * p, None) + tile_units(2 * p + 1, None), m_ref, l_ref, acc_ref)
        return carry

    lax.fori_loop(0, lax.shift_right_logical(i, 1), body, 0)

    @pl.when(i % 2 == 1)
    def _():
        _flash_units(tile_units(i - 1, None), m_ref, l_ref, acc_ref)

    _flash_units(tile_units(i, _tri_mask(tq, tq, True)), m_ref, l_ref, acc_ref)

    lam = lam_ref[...]
    lmbda = (jnp.exp(jnp.sum(lam[0:1] * lam[1:2], axis=-1, keepdims=True))
             - jnp.exp(jnp.sum(lam[2:3] * lam[3:4], axis=-1, keepdims=True)) + lambda_init)
    qc = m_ref.shape[-1]
    nx = tq // qc
    for x in range(nx):
        oT = acc_ref[x] / l_ref[x] - lmbda * (acc_ref[nx + x] / l_ref[nx + x])
        o = oT.T
        y = o * lax.rsqrt(jnp.mean(o * o, axis=-1, keepdims=True) + EPS)
        y = (y * sw_ref[...]) * (1.0 - lambda_init)
        o_ref[x * qc:(x + 1) * qc, :] = y.astype(o_ref.dtype)


def _diff_attention(qkv, lam, subln_w, lambda_init, B, S, tq=512):
    T = B * S
    H, dh, dv = A_HEADS, A_DH, A_DV
    n = S // tq
    slopes = _alibi_slopes(H)
    kb = H * 2
    vb = (2 * H * 2 * dh) // dv
    grid_spec = pltpu.PrefetchScalarGridSpec(
        num_scalar_prefetch=1,
        grid=(B, H, n),
        in_specs=[
            pl.BlockSpec((tq, dh), lambda b, h, i, sl: (b * n + i, 2 * h)),
            pl.BlockSpec((tq, dh), lambda b, h, i, sl: (b * n + i, 2 * h + 1)),
            pl.BlockSpec((S, dh), lambda b, h, i, sl: (b, kb + 2 * h)),
            pl.BlockSpec((S, dh), lambda b, h, i, sl: (b, kb + 2 * h + 1)),
            pl.BlockSpec((S, dv), lambda b, h, i, sl: (b, vb + h)),
            pl.BlockSpec((S, 128), lambda b, h, i, sl: (0, 0)),
            pl.BlockSpec((4, dh), lambda b, h, i, sl: (0, 0)),
            pl.BlockSpec((1, dv), lambda b, h, i, sl: (0, 0)),
        ],
        out_specs=pl.BlockSpec((tq, dv), lambda b, h, i, sl: (b * n + i, h)),
        scratch_shapes=[pltpu.VMEM((2, tq, 2 * dh), BF16)] + _flash_state(2, tq, dv),
    )
    return pl.pallas_call(
        functools.partial(_diff_attn_kernel, tq=tq, lambda_init=lambda_init),
        out_shape=jax.ShapeDtypeStruct((T, H * dv), BF16),
        grid_spec=grid_spec,
        compiler_params=_cparams(("parallel", "parallel", "arbitrary")),
        name="diff_attention",
    )(slopes, qkv, qkv, qkv, qkv, qkv, _key_features(S), lam, subln_w.reshape(1, dv))


def _ffn_norm(h_ref, nw_ref):
    x = h_ref[...]
    return (x * lax.rsqrt(jnp.mean(x * x, axis=-1, keepdims=True) + EPS)) * nw_ref[...]


def _router_kernel(h_ref, nw_ref, wr_ref, br_ref, tri_ref, ids_ref, wts_ref, cnt_ref, base_ref):
    @pl.when(pl.program_id(0) == 0)
    def _():
        base_ref[...] = jnp.zeros(base_ref.shape, F32)

    xn = _ffn_norm(h_ref, nw_ref)
    w = wr_ref[...]
    x_hi = xn.astype(BF16)
    x_lo = (xn - x_hi.astype(F32)).astype(BF16)
    w_hi = w.astype(BF16)
    w_lo = (w - w_hi.astype(F32)).astype(BF16)
    logits = (jnp.dot(x_hi, w_hi, preferred_element_type=F32)
              + (jnp.dot(x_hi, w_lo, preferred_element_type=F32)
                 + jnp.dot(x_lo, w_hi, preferred_element_type=F32))) + br_ref[...]
    lane = lax.broadcasted_iota(I32, logits.shape, 1)
    big = jnp.int32(1 << 20)
    is_g = lane < N_GROUPS
    is_e = (lane >= N_GROUPS) & (lane < N_GROUPS + N_EXPERTS)

    lg = jnp.where(is_g, logits, NEG)
    mg = jnp.max(lg, axis=-1, keepdims=True)
    g_top = jnp.min(jnp.where(lg == mg, lane, big), axis=-1, keepdims=True)
    wg = 1.0 / jnp.sum(jnp.where(is_g, jnp.exp(lg - mg), 0.0), axis=-1, keepdims=True)

    in_grp = is_e & (lax.shift_right_logical(lane - N_GROUPS, 3) == g_top)
    le = jnp.where(in_grp, logits, NEG)
    me = jnp.max(le, axis=-1, keepdims=True)
    ee = jnp.where(in_grp, jnp.exp(le - me), 0.0)
    pe = ee / jnp.sum(ee, axis=-1, keepdims=True)
    pe = jnp.where(in_grp, pe, -1.0)
    p1 = jnp.max(pe, axis=-1, keepdims=True)
    i1 = jnp.min(jnp.where(pe == p1, lane, big), axis=-1, keepdims=True)
    pe2 = jnp.where(lane == i1, -1.0, pe)
    p2 = jnp.max(pe2, axis=-1, keepdims=True)
    i2 = jnp.min(jnp.where(pe2 == p2, lane, big), axis=-1, keepdims=True)
    den = p1 + p2
    w0 = wg * p1 / den
    w1 = wg * p2 / den

    oh1 = lane == i1
    oh2 = lane == i2
    both = jnp.where(oh1 | oh2, 1.0, 0.0)
    before = jnp.dot(tri_ref[...], both.astype(BF16), preferred_element_type=F32) + base_ref[...]
    pos0 = jnp.sum(jnp.where(oh1, before, 0.0), axis=-1, keepdims=True).astype(I32)
    pos1 = jnp.sum(jnp.where(oh2, before, 0.0), axis=-1, keepdims=True).astype(I32)
    total = base_ref[...] + jnp.sum(both, axis=0, keepdims=True)
    base_ref[...] = total
    cnt_ref[...] = total

    ids = jnp.where(lane == 0, i1 - N_GROUPS,
                    jnp.where(lane == 1, i2 - N_GROUPS,
                              jnp.where(lane == 2, pos0, jnp.where(lane == 3, pos1, 0))))
    ids_ref[...] = ids.T[:8]
    wts_ref[...] = jnp.where(lane == 0, w0, jnp.where(lane == 1, w1, 0.0))


def _router(h, norm_w, w_group, b_group, w_expert, b_expert, tm=512):
    T, D = h.shape
    pad = ROUTER_LANES - N_GROUPS - N_EXPERTS
    wr = jnp.concatenate([w_group, w_expert, jnp.zeros((D, pad), F32)], axis=1)
    br = jnp.concatenate([b_group, b_expert, jnp.zeros((pad,), F32)]).reshape(1, ROUTER_LANES)
    tri = jnp.asarray(np.tril(np.ones((tm, tm), np.float32), -1), BF16)
    return pl.pallas_call(
        _router_kernel,
        out_shape=(jax.ShapeDtypeStruct((8, T), I32),
                   jax.ShapeDtypeStruct((T, ROUTER_LANES), F32),
                   jax.ShapeDtypeStruct((1, ROUTER_LANES), F32)),
        grid=(T // tm,),
        in_specs=[pl.BlockSpec((tm, D), lambda i: (i, 0)),
                  pl.BlockSpec((1, D), lambda i: (0, 0)),
                  pl.BlockSpec((D, ROUTER_LANES), lambda i: (0, 0)),
                  pl.BlockSpec((1, ROUTER_LANES), lambda i: (0, 0)),
                  pl.BlockSpec((tm, tm), lambda i: (0, 0))],
        out_specs=(pl.BlockSpec((8, tm), lambda i: (0, i)),
                   pl.BlockSpec((tm, ROUTER_LANES), lambda i: (i, 0)),
                   pl.BlockSpec((1, ROUTER_LANES), lambda i: (0, 0))),
        scratch_shapes=[pltpu.VMEM((1, ROUTER_LANES), F32)],
        compiler_params=_cparams(("arbitrary",)),
        name="moe_router",
    )(h, norm_w.reshape(1, D), wr, br, tri)


def _dispatch_kernel(dest_ref, zero_ref, h_ref, nw_ref, xs_ref, xn_ref, zbuf_ref, sem, zsem, *, tm, n_tok):
    xn_ref[...] = _ffn_norm(h_ref, nw_ref)

    @pl.when(pl.program_id(0) == 0)
    def _():
        zbuf_ref[...] = jnp.zeros(zbuf_ref.shape, zbuf_ref.dtype)
        rb = zbuf_ref.shape[0]

        def fill(r):
            return pltpu.make_async_copy(zbuf_ref, xs_ref.at[pl.ds(r * rb, rb)], zsem)

        for r in range(zero_ref.shape[0]):
            @pl.when(zero_ref[r] == 1)
            def _():
                fill(r).start()
        for r in range(zero_ref.shape[0]):
            @pl.when(zero_ref[r] == 1)
            def _():
                fill(r).wait()

    base = pl.program_id(0) * tm
    for t in range(tm):
        for slot in range(2):
            d = dest_ref[slot * n_tok + base + t]
            pltpu.make_async_copy(xn_ref.at[pl.ds(t, 1)], xs_ref.at[pl.ds(d, 1)], sem).start(priority=slot)
    for slot in range(2):
        pltpu.make_async_copy(xn_ref, xs_ref.at[pl.ds(0, tm)], sem).wait()


def _dispatch(h, norm_w, dest_flat, zero_blk, n_rows, tm=256):
    T, D = h.shape
    grid_spec = pltpu.PrefetchScalarGridSpec(
        num_scalar_prefetch=2,
        grid=(T // tm,),
        in_specs=[pl.BlockSpec((tm, D), lambda i, d, z: (i, 0)),
                  pl.BlockSpec((1, D), lambda i, d, z: (0, 0))],
        out_specs=pl.BlockSpec(memory_space=pl.ANY),
        scratch_shapes=[pltpu.VMEM((tm, D), F32), pltpu.VMEM((MOE_ROWS, D), F32),
                        pltpu.SemaphoreType.DMA, pltpu.SemaphoreType.DMA],
    )
    return pl.pallas_call(
        functools.partial(_dispatch_kernel, tm=tm, n_tok=T),
        out_shape=jax.ShapeDtypeStruct((n_rows, D), F32),
        grid_spec=grid_spec,
        compiler_params=_cparams(("arbitrary",)),
        name="moe_dispatch",
    )(dest_flat, zero_blk, h, norm_w.reshape(1, D))


def _moe_plan(counts, n_blk):
    RB = MOE_ROWS
    nb_e = (counts + RB - 1) // RB
    end_b = jnp.cumsum(nb_e)
    start_b = end_b - nb_e
    n_used = end_b[-1:].astype(I32)
    r = jnp.arange(n_blk, dtype=I32)
    be = jnp.minimum(jnp.sum((end_b[None, :] <= r[:, None]).astype(I32), axis=1), N_EXPERTS - 1)
    active = nb_e > 0
    ordinal = jnp.cumsum(active.astype(I32)) - 1
    idx = jnp.arange(N_EXPERTS, dtype=I32)
    later = jnp.where(active[None, :] & (idx[None, :] > idx[:, None]), idx[None, :], N_EXPERTS)
    nxt_e = jnp.min(later, axis=1)
    nxt_e = jnp.where(nxt_e == N_EXPERTS, -1, nxt_e)
    first = ((r == start_b[be]) & (r < n_used[0])).astype(I32)
    plan = jnp.stack([be, first, ordinal[be] % 2, nxt_e[be]]).astype(I32)
    zero_blk = ((r == end_b[be] - 1) | (r >= n_used[0])).astype(I32)
    return plan, n_used, (start_b * RB).astype(I32), zero_blk


def _stream_expert_weights(plan_ref, r, w_hbms, wbuf, sem, layer):
    e, slot, nxt = plan_ref[0, r], plan_ref[2, r], plan_ref[3, r]

    def copies(expert, s):
        rows = wbuf.shape[2] // 2
        return [(pltpu.make_async_copy(w.at[layer, expert, pl.ds(half * rows, rows)],
                                       wbuf.at[s, k, pl.ds(half * rows, rows)], sem.at[s, k, half]), half)
                for k, w in enumerate(w_hbms) for half in range(2)]

    @pl.when(plan_ref[1, r] == 1)
    def _():
        @pl.when(r == 0)
        def _():
            for c, half in copies(e, slot):
                c.start(priority=half)

        for c, _ in copies(e, slot):
            c.wait()

        @pl.when(nxt >= 0)
        def _():
            for c, half in copies(nxt, 1 - slot):
                c.start(priority=half)

    return slot


def _gate_up_kernel(plan_ref, nu_ref, x_ref, wg_hbm, wu_hbm, h_ref, wbuf, sem, *, layer):
    r = pl.program_id(0)

    @pl.when(r < nu_ref[0])
    def _():
        slot = _stream_expert_weights(plan_ref, r, (wg_hbm, wu_hbm), wbuf, sem, layer)
        x = x_ref[...]
        g = jnp.dot(x, wbuf[slot, 0], preferred_element_type=F32)
        u = jnp.dot(x, wbuf[slot, 1], preferred_element_type=F32)
        h_ref[...] = (jax.nn.silu(g) * u).astype(h_ref.dtype)

    @pl.when(r >= nu_ref[0])
    def _():
        h_ref[...] = jnp.zeros(h_ref.shape, h_ref.dtype)


def _down_kernel(plan_ref, nu_ref, h_ref, wd_hbm, y_ref, wbuf, sem, *, layer):
    r = pl.program_id(0)

    @pl.when(r < nu_ref[0])
    def _():
        slot = _stream_expert_weights(plan_ref, r, (wd_hbm,), wbuf, sem, layer)
        y_ref[...] = jnp.dot(h_ref[...].astype(F32), wbuf[slot, 0], preferred_element_type=F32)

    @pl.when(r >= nu_ref[0])
    def _():
        y_ref[...] = jnp.zeros(y_ref.shape, y_ref.dtype)


def _expert_mlp(xs, plan, n_used, w_gate, w_up, w_down, layer):
    n_rows, D = xs.shape
    RB = MOE_ROWS
    n_blk = n_rows // RB
    F = D_EXPERT

    def used_blk(r, plan, nu):
        return (jnp.minimum(r, nu[0] - 1), 0)

    h = pl.pallas_call(
        functools.partial(_gate_up_kernel, layer=layer),
        out_shape=jax.ShapeDtypeStruct((n_rows, F), BF16),
        grid_spec=pltpu.PrefetchScalarGridSpec(
            num_scalar_prefetch=2,
            grid=(n_blk,),
            in_specs=[pl.BlockSpec((RB, D), used_blk),
                      pl.BlockSpec(memory_space=pl.ANY),
                      pl.BlockSpec(memory_space=pl.ANY)],
            out_specs=pl.BlockSpec((RB, F), lambda r, plan, nu: (r, 0)),
            scratch_shapes=[pltpu.VMEM((2, 2, D, F), F32), pltpu.SemaphoreType.DMA((2, 2, 2))],
        ),
        compiler_params=_cparams(("arbitrary",)),
        name="moe_gate_up",
    )(plan, n_used, xs, w_gate, w_up)
    y = pl.pallas_call(
        functools.partial(_down_kernel, layer=layer),
        out_shape=jax.ShapeDtypeStruct((n_rows, D), F32),
        grid_spec=pltpu.PrefetchScalarGridSpec(
            num_scalar_prefetch=2,
            grid=(n_blk,),
            in_specs=[pl.BlockSpec((RB, F), used_blk),
                      pl.BlockSpec(memory_space=pl.ANY)],
            out_specs=pl.BlockSpec((RB, D), lambda r, plan, nu: (r, 0)),
            scratch_shapes=[pltpu.VMEM((2, 1, F, D), F32), pltpu.SemaphoreType.DMA((2, 1, 2))],
        ),
        compiler_params=_cparams(("arbitrary",)),
        name="moe_down",
    )(plan, n_used, h, w_down)
    return y


def _combine_kernel(dest_ref, resid_ref, wts_ref, nw_ref, y_ref, *rest, tm, n_tok, keep_h):
    if keep_h:
        o_ref, n_ref, buf_ref, sem = rest
    else:
        o_ref = None
        n_ref, buf_ref, sem = rest
    i = pl.program_id(0)
    n = pl.num_programs(0)

    def gather(tile, par):
        for t in range(tm):
            for slot in range(2):
                d = dest_ref[slot * n_tok + tile * tm + t]
                pltpu.make_async_copy(y_ref.at[pl.ds(d, 1)], buf_ref.at[par, slot, pl.ds(t, 1)],
                                      sem.at[par]).start(priority=slot)

    @pl.when(i == 0)
    def _():
        gather(i, 0)

    for par in range(2):
        @pl.when((i + 1 < n) & ((i + 1) % 2 == par))
        def _():
            gather(i + 1, par)

    for par in range(2):
        @pl.when(i % 2 == par)
        def _():
            for slot in range(2):
                pltpu.make_async_copy(y_ref.at[pl.ds(0, tm)], buf_ref.at[par, slot], sem.at[par]).wait()
            w = wts_ref[...]
            hn = resid_ref[...] + (w[:, 0:1] * buf_ref[par, 0] + w[:, 1:2] * buf_ref[par, 1])
            if keep_h:
                o_ref[...] = hn
            yn = hn * lax.rsqrt(jnp.mean(hn * hn, axis=-1, keepdims=True) + EPS)
            n_ref[...] = (yn * nw_ref[...]).astype(n_ref.dtype)


def _combine(resid, wts, y, dest_flat, next_norm_w, norm_dtype, keep_h, norm_shape=None, tm=128):
    T, D = resid.shape
    row_spec = pl.BlockSpec((tm, D), lambda i, d: (i, 0))
    if norm_shape is None:
        out_shape = [jax.ShapeDtypeStruct((T, D), norm_dtype)]
        out_specs = [row_spec]
    else:
        per_seq = norm_shape[1] // tm
        out_shape = [jax.ShapeDtypeStruct(norm_shape, norm_dtype)]
        out_specs = [pl.BlockSpec((None, tm, D), lambda i, d: (i // per_seq, i % per_seq, 0))]
    if keep_h:
        out_shape.insert(0, jax.ShapeDtypeStruct((T, D), F32))
        out_specs.insert(0, row_spec)
    grid_spec = pltpu.PrefetchScalarGridSpec(
        num_scalar_prefetch=1,
        grid=(T // tm,),
        in_specs=[row_spec,
                  pl.BlockSpec((tm, ROUTER_LANES), lambda i, d: (i, 0)),
                  pl.BlockSpec((1, D), lambda i, d: (0, 0)),
                  pl.BlockSpec(memory_space=pl.ANY)],
        out_specs=tuple(out_specs),
        scratch_shapes=[pltpu.VMEM((2, 2, tm, D), F32), pltpu.SemaphoreType.DMA((2,))],
    )
    outs = pl.pallas_call(
        functools.partial(_combine_kernel, tm=tm, n_tok=T, keep_h=keep_h),
        out_shape=tuple(out_shape),
        grid_spec=grid_spec,
        compiler_params=_cparams(("arbitrary",)),
        name="moe_combine",
    )(dest_flat, resid, wts, next_norm_w.reshape(1, D), y)
    return (outs[0], outs[1]) if keep_h else (None, outs[0])


def _hier_moe(h, norm_w, w_group, b_group, w_expert, b_expert, w_gate, w_up, w_down, layer,
              next_norm_w, norm_dtype, keep_h, norm_shape=None):
    T, D = h.shape
    RB = MOE_ROWS
    n_blk = (T * 2) // RB + N_EXPERTS
    n_rows = n_blk * RB
    ids, wts, cnt = _router(h, norm_w, w_group, b_group, w_expert, b_expert)
    counts = cnt[0, N_GROUPS:N_GROUPS + N_EXPERTS].astype(I32)
    plan, n_used, start_padded, zero_blk = _moe_plan(counts, n_blk)
    is_e = ids[None, 0:2] == jnp.arange(N_EXPERTS, dtype=I32)[:, None, None]
    dest = (jnp.sum(jnp.where(is_e, start_padded[:, None, None], 0), axis=0) + ids[2:4]).reshape(-1)
    xs = _dispatch(h, norm_w, dest, zero_blk, n_rows)
    y = _expert_mlp(xs, plan, n_used, w_gate, w_up, w_down, layer)
    return _combine(h, wts, y, dest, next_norm_w, norm_dtype, keep_h, norm_shape)


def _compress_kernel(x_ref, w1_ref, w2_ref, pe_ref, o_ref, xf_ref):
    d = B_D
    half = CMP_STRIDE * d
    nch = x_ref.shape[0] // CMP_STRIDE
    xf_ref[...] = x_ref[...].astype(F32)
    w1 = w1_ref[...].astype(BF16)
    a = jnp.zeros((nch, w1.shape[1]), F32)
    b = jnp.zeros((nch, w1.shape[1]), F32)
    for l in range(CMP_STRIDE):
        xl = xf_ref[pl.ds(l, nch, stride=CMP_STRIDE), :].astype(BF16)
        a = a + jnp.dot(xl, w1[l * d:(l + 1) * d], preferred_element_type=F32)
        b = b + jnp.dot(xl, w1[half + l * d:half + (l + 1) * d], preferred_element_type=F32)
    pe = jnp.broadcast_to(pe_ref[...], (8, 2 * half)).astype(BF16)
    pt = jnp.dot(pe, w1, preferred_element_type=F32)[0:1]
    hid = a + pltpu.roll(b, b.shape[0] - 1, 0) + pt
    g = jax.nn.gelu(hid)
    o_ref[...] = jnp.dot(g.astype(BF16), w2_ref[...].astype(BF16),
                         preferred_element_type=F32).astype(o_ref.dtype)


def _compress(projm, w1, w2, pe_flat, B, S):
    G, d = B_GROUPS, B_D
    nch = S // CMP_STRIDE
    width = CMP_STRIDE * d
    hid = w1.shape[-1]
    cb = B_HEADS
    return pl.pallas_call(
        _compress_kernel,
        out_shape=jax.ShapeDtypeStruct((2, B, G, nch, d), BF16),
        grid=(2, B, G),
        in_specs=[pl.BlockSpec((S, d), lambda kv, b, g: (b, cb + kv * G + g)),
                  pl.BlockSpec((None, 2 * width, hid), lambda kv, b, g: (kv, 0, 0)),
                  pl.BlockSpec((None, hid, d), lambda kv, b, g: (kv, 0, 0)),
                  pl.BlockSpec((None, 1, 2 * width), lambda kv, b, g: (kv, 0, 0))],
        out_specs=pl.BlockSpec((None, None, None, nch, d), lambda kv, b, g: (kv, b, g, 0, 0)),
        scratch_shapes=[pltpu.VMEM((S, d), F32)],
        compiler_params=_cparams(("arbitrary", "arbitrary", "arbitrary")),
        name="nsa_compress",
    )(projm, w1, w2, pe_flat)


def _cmp_select_kernel(slopes_ref, q_ref, kc_ref, vc_ref, ov_ref, o_ref, sel_ref, need_ref, *, tq, n_cmp,
                       n_blocks, n_rank):
    g = pl.program_id(1)
    i = pl.program_id(2)
    d = B_D
    nch = kc_ref.shape[0]
    pos = i * tq + lax.broadcasted_iota(I32, (1, tq), 1)
    c = lax.broadcasted_iota(I32, (nch, 1), 0)
    dist_i = pos - (c * CMP_STRIDE + (CMP_LEN - 1))
    valid = (dist_i >= 0) & (c < n_cmp)
    dist = dist_i.astype(F32)
    any_valid = jnp.where(pos >= CMP_LEN - 1, 1.0, 0.0)
    kc = kc_ref[...]
    vc = vc_ref[...]
    psum = jnp.zeros((nch, tq), F32)
    for hh in range(B_HPG):
        s = _nt_dot(kc, q_ref[:, hh * d:(hh + 1) * d])
        s = jnp.where(valid, s - (slopes_ref[g * B_HPG + hh] * LOG2E) * dist, NEG)
        e = jnp.exp2(s - jnp.max(s, axis=0, keepdims=True))
        p = (e / jnp.sum(e, axis=0, keepdims=True)) * any_valid
        oT = lax.dot_general(vc, p.astype(BF16), (((0,), (0,)), ((), ())), preferred_element_type=F32)
        o_ref[:, hh * d:(hh + 1) * d] = oT.T.astype(o_ref.dtype)
        psum = psum + p

    p_hi = psum.astype(BF16)
    p_lo = (psum - p_hi.astype(F32)).astype(BF16)
    ov = ov_ref[...]
    imp = jnp.dot(ov, p_hi, preferred_element_type=F32) + jnp.dot(ov, p_lo, preferred_element_type=F32)
    imp = imp[:n_rank]

    blk = lax.broadcasted_iota(I32, (n_rank, 1), 0)
    cur = lax.shift_right_logical(pos, SLC_BLOCK.bit_length() - 1)
    forced = (blk == 0) | (blk == cur) | (blk == cur - 1)
    causal = blk * SLC_BLOCK <= pos
    score = jnp.where(forced, FORCE, jnp.where(causal, imp, -FORCE))
    score = jnp.where(blk < n_blocks, score, -2.0 * FORCE)
    blk_f = blk.astype(F32)
    remaining = score
    sel_add = jnp.full(score.shape, NEG, F32)
    for _ in range(SLC_TOPN):
        top = jnp.max(remaining, axis=0, keepdims=True)
        first = jnp.min(jnp.where(remaining == top, blk_f, float(n_rank)), axis=0, keepdims=True)
        pick = blk_f == first
        sel_add = jnp.where(pick, 0.0, sel_add)
        remaining = jnp.where(pick, -4.0 * FORCE, remaining)
    pad = jnp.zeros((128 - n_rank, tq), F32)
    sel_t = jnp.concatenate([jnp.where(blk < n_blocks, sel_add, 0.0), pad], axis=0) if n_rank < 128 else sel_add
    sel_ref[...] = sel_t.T.astype(sel_ref.dtype)
    per_tile = SLC_KEY_TILE // SLC_BLOCK
    chosen = jnp.where(sel_add == 0.0, 1, 0)
    for qh in range(tq // SLC_KEY_TILE):
        blk_any = jnp.max(chosen[:, qh * SLC_KEY_TILE:(qh + 1) * SLC_KEY_TILE], axis=1, keepdims=True)
        rows = []
        for j in range(8):
            if (j + 1) * per_tile <= n_rank:
                hit = jnp.max(blk_any[j * per_tile:(j + 1) * per_tile], axis=0, keepdims=True)
                rows.append(jnp.broadcast_to(hit, (1, 128)))
            else:
                rows.append(jnp.zeros((1, 128), I32))
        need_ref[qh] = jnp.concatenate(rows, axis=0)


def _cmp_select(projm, cmp_kv, overlap, B, S, tq=1024):
    T = B * S
    G, d = B_GROUPS, B_D
    nq = S // tq
    nch = cmp_kv.shape[3]
    n_cmp = (S - CMP_LEN) // CMP_STRIDE + 1
    n_blocks = S // SLC_BLOCK
    assert SLC_TOPN <= n_blocks <= FEAT_BLK
    slopes = _alibi_slopes(B_HEADS)
    grid_spec = pltpu.PrefetchScalarGridSpec(
        num_scalar_prefetch=1,
        grid=(B, G, nq),
        in_specs=[pl.BlockSpec((tq, B_HPG * d), lambda b, g, i, sl: (b * nq + i, g)),
                  pl.BlockSpec((None, None, None, nch, d), lambda b, g, i, sl: (0, b, g, 0, 0)),
                  pl.BlockSpec((None, None, None, nch, d), lambda b, g, i, sl: (1, b, g, 0, 0)),
                  pl.BlockSpec((128, nch), lambda b, g, i, sl: (0, 0))],
        out_specs=(pl.BlockSpec((tq, B_HPG * d), lambda b, g, i, sl: (b * nq + i, g)),
                   pl.BlockSpec((None, None, tq, 128), lambda b, g, i, sl: (b, g, i, 0)),
                   pl.BlockSpec((None, None, None, tq // SLC_KEY_TILE, 8, 128),
                                lambda b, g, i, sl: (b, g, i, 0, 0, 0))),
    )
    assert S // SLC_KEY_TILE <= 8 and tq % SLC_KEY_TILE == 0
    return pl.pallas_call(
        functools.partial(_cmp_select_kernel, tq=tq, n_cmp=n_cmp, n_blocks=n_blocks,
                          n_rank=-(-n_blocks // 8) * 8),
        out_shape=(jax.ShapeDtypeStruct((T, B_HEADS * d), BF16),
                   jax.ShapeDtypeStruct((B, G, S, 128), BF16),
                   jax.ShapeDtypeStruct((B, G, nq, tq // SLC_KEY_TILE, 8, 128), I32)),
        grid_spec=grid_spec,
        compiler_params=_cparams(("parallel", "parallel", "arbitrary")),
        name="nsa_cmp_select",
    )(slopes, projm, cmp_kv, cmp_kv, overlap)


def _head_units(k_ref, v_ref, feat_ref, qa_ref, j, mask, qc):
    n_heads, tq, _ = qa_ref.shape
    k0 = pl.multiple_of(j * tq, tq)
    v = v_ref[pl.ds(k0, tq), :]

    def score_fn(hh, c):
        def fn():
            ka = jnp.concatenate([k_ref[pl.ds(k0, tq), :], feat_ref[pl.ds(k0, tq), :]], axis=1)
            sT = _nt_dot(ka, qa_ref[hh, c * qc:(c + 1) * qc, :])
            return sT if mask is None else sT + mask[:, c * qc:(c + 1) * qc]
        return fn

    return [(score_fn(hh, c), v, hh * (tq // qc) + c) for hh in range(n_heads) for c in range(tq // qc)]


def _write_heads_T(o_ref, acc_ref, l_ref):
    tq = o_ref.shape[0]
    qc = l_ref.shape[-1]
    for hh in range(B_HPG):
        for c in range(tq // qc):
            u = hh * (tq // qc) + c
            o_ref[c * qc:(c + 1) * qc, hh * B_D:(hh + 1) * B_D] = (acc_ref[u] / l_ref[u]).T.astype(o_ref.dtype)


def _flash_state(n_heads, tq, dv):
    qc = min(ATTN_QC, tq)
    n_units = n_heads * (tq // qc)
    return [pltpu.VMEM((n_units, 1, qc), F32), pltpu.VMEM((n_units, 1, qc), F32),
            pltpu.VMEM((n_units, dv, qc), F32)]


def _stage_queries(q_ref, qa_ref, slopes_ref, g, q0, sel_add):
    tq = q_ref.shape[0]
    lane = lax.broadcasted_iota(I32, (tq, 128), 1)
    for hh in range(B_HPG):
        coef = jnp.broadcast_to(_alibi_query_features(slopes_ref[g * B_HPG + hh] * LOG2E, q0), (tq, 128))
        if sel_add is not None:
            coef = jnp.where(lane < FEAT_BLK, sel_add, coef)
        qa_ref[hh] = jnp.concatenate([q_ref[:, hh * B_D:(hh + 1) * B_D], coef.astype(BF16)], axis=1)


def _slc_attn_kernel(slopes_ref, need_ref, q_ref, k_ref, v_ref, feat_ref, sel_ref, o_ref, qa_ref, m_ref, l_ref,
                     acc_ref, *, tq):
    b = pl.program_id(0)
    g = pl.program_id(1)
    i = pl.program_id(2)
    n = pl.num_programs(2)
    _flash_init(m_ref, l_ref, acc_ref)
    _stage_queries(q_ref, qa_ref, slopes_ref, g, i * tq, sel_ref[...].astype(F32))

    qc = m_ref.shape[-1]

    def units(j, causal=None):
        return _head_units(k_ref, v_ref, feat_ref, qa_ref, j, causal, qc)

    need_base = ((b * pl.num_programs(1) + g) * n + i) * n

    def needed(j):
        return (j == 0) | (need_ref[need_base + jnp.maximum(j, 0)] > 0)

    def body(p, carry):
        j0, j1 = 2 * p, 2 * p + 1
        n0, n1 = needed(j0), needed(j1)

        @pl.when(n0 & n1)
        def _():
            _flash_units(units(j0) + units(j1), m_ref, l_ref, acc_ref)

        @pl.when(n0 & jnp.logical_not(n1))
        def _():
            _flash_units(units(j0), m_ref, l_ref, acc_ref)

        @pl.when(jnp.logical_not(n0) & n1)
        def _():
            _flash_units(units(j1), m_ref, l_ref, acc_ref)

        return carry

    lax.fori_loop(0, lax.shift_right_logical(i, 1), body, 0)

    @pl.when((i % 2 == 1) & needed(i - 1))
    def _():
        _flash_units(units(i - 1), m_ref, l_ref, acc_ref)

    _flash_units(units(i, _tri_mask(tq, tq, True)), m_ref, l_ref, acc_ref)
    _write_heads_T(o_ref, acc_ref, l_ref)


def _slc_attention(projm, sel, need, B, S):
    T = B * S
    G, d = B_GROUPS, B_D
    tq = SLC_KEY_TILE
    n = S // tq
    slopes = _alibi_slopes(B_HEADS)
    ksb = (B_HEADS * d + 2 * G * d) // d
    vsb = ksb + G
    grid_spec = pltpu.PrefetchScalarGridSpec(
        num_scalar_prefetch=2,
        grid=(B, G, n),
        in_specs=[
            pl.BlockSpec((tq, B_HPG * d), lambda b, g, i, sl, nd: (b * n + i, g)),
            pl.BlockSpec((S, d), lambda b, g, i, sl, nd: (b, ksb + g)),
            pl.BlockSpec((S, d), lambda b, g, i, sl, nd: (b, vsb + g)),
            pl.BlockSpec((S, 128), lambda b, g, i, sl, nd: (0, 0)),
            pl.BlockSpec((None, None, tq, 128), lambda b, g, i, sl, nd: (b, g, i, 0)),
        ],
        out_specs=pl.BlockSpec((tq, B_HPG * d), lambda b, g, i, sl, nd: (b * n + i, g)),
        scratch_shapes=[pltpu.VMEM((B_HPG, tq, 2 * d), BF16)] + _flash_state(B_HPG, tq, d),
    )
    return pl.pallas_call(
        functools.partial(_slc_attn_kernel, tq=tq),
        out_shape=jax.ShapeDtypeStruct((T, B_HEADS * d), BF16),
        grid_spec=grid_spec,
        compiler_params=_cparams(("parallel", "parallel", "arbitrary")),
        name="nsa_selected_attention",
    )(slopes, need.reshape(-1), projm, projm, projm, _key_features(S), sel)


def _win_attn_kernel(slopes_ref, q_ref, k_ref, v_ref, feat_ref, gl_ref, oc_ref, os_ref, o_ref, qa_ref, m_ref,
                     l_ref, acc_ref, *, tq):
    g = pl.program_id(1)
    i = pl.program_id(2)
    _flash_init(m_ref, l_ref, acc_ref)
    _stage_queries(q_ref, qa_ref, slopes_ref, g, i * tq, None)

    qc = m_ref.shape[-1]

    def units(j, mask):
        return _head_units(k_ref, v_ref, feat_ref, qa_ref, j, mask, qc)

    @pl.when(i > 0)
    def _():
        _flash_units(units(i - 1, _tri_mask(tq, tq, False)) + units(i, _tri_mask(tq, tq, True)),
                     m_ref, l_ref, acc_ref)

    @pl.when(i == 0)
    def _():
        _flash_units(units(i, _tri_mask(tq, tq, True)), m_ref, l_ref, acc_ref)

    n_chunks = tq // qc
    for hh in range(B_HPG):
        row = (g * B_HPG + hh) * 3
        for c in range(n_chunks):
            qs = slice(c * qc, (c + 1) * qc)
            cols = slice(hh * B_D, (hh + 1) * B_D)
            u = hh * n_chunks + c
            gates = [jax.nn.sigmoid(gl_ref[pl.ds(row + br, 1), qs]) for br in range(3)]
            mix = (gates[0] * oc_ref[qs, cols].astype(F32).T + gates[1] * os_ref[qs, cols].astype(F32).T
                   + gates[2] * (acc_ref[u] / l_ref[u]))
            o_ref[qs, cols] = mix.T.astype(o_ref.dtype)


def _win_attention(projm, gate_logits_t, o_cmp, o_slc, B, S):
    T = B * S
    G, d = B_GROUPS, B_D
    tq = WINDOW
    n = S // tq
    slopes = _alibi_slopes(B_HEADS)
    kwb = (B_HEADS * d + 4 * G * d) // d
    vwb = kwb + G
    grid_spec = pltpu.PrefetchScalarGridSpec(
        num_scalar_prefetch=1,
        grid=(B, G, n),
        in_specs=[
            pl.BlockSpec((tq, B_HPG * d), lambda b, g, i, sl: (b * n + i, g)),
            pl.BlockSpec((S, d), lambda b, g, i, sl: (b, kwb + g)),
            pl.BlockSpec((S, d), lambda b, g, i, sl: (b, vwb + g)),
            pl.BlockSpec((S, 128), lambda b, g, i, sl: (0, 0)),
            pl.BlockSpec((128, tq), lambda b, g, i, sl: (0, b * n + i)),
            pl.BlockSpec((tq, B_HPG * d), lambda b, g, i, sl: (b * n + i, g)),
            pl.BlockSpec((tq, B_HPG * d), lambda b, g, i, sl: (b * n + i, g)),
        ],
        out_specs=pl.BlockSpec((tq, B_HPG * d), lambda b, g, i, sl: (b * n + i, g)),
        scratch_shapes=[pltpu.VMEM((B_HPG, tq, 2 * d), BF16)] + _flash_state(B_HPG, tq, d),
    )
    return pl.pallas_call(
        functools.partial(_win_attn_kernel, tq=tq),
        out_shape=jax.ShapeDtypeStruct((T, B_HEADS * d), BF16),
        grid_spec=grid_spec,
        compiler_params=_cparams(("parallel", "parallel", "arbitrary")),
        name="nsa_window_attention",
    )(slopes, projm, projm, projm, _key_features(S), gate_logits_t, o_cmp, o_slc)


def _overlap_matrix(S, n_rows):
    nc = (S - CMP_LEN) // CMP_STRIDE + 1
    nsb = S // SLC_BLOCK
    cs = np.arange(nc) * CMP_STRIDE
    ss = np.arange(nsb) * SLC_BLOCK
    ov = np.clip(np.minimum(cs[:, None] + CMP_LEN, ss[None, :] + SLC_BLOCK)
                 - np.maximum(cs[:, None], ss[None, :]), 0, None) / CMP_LEN
    out = np.zeros((128, n_rows), np.float32)
    out[:nsb, :nc] = ov.T
    return jnp.asarray(out, BF16)


def _nsa_attention(xn, w_in, cmp_pos, cmp_w1, cmp_w2, B, S):
    T = B * S
    H, G, d = B_HEADS, B_GROUPS, B_D
    n_main = H * d + 6 * G * d
    col_scale = jnp.concatenate([jnp.full((H * d,), d ** -0.5 * LOG2E, F32), jnp.ones((n_main - H * d,), F32)])
    w_t = jnp.swapaxes(w_in, 0, 1)
    projm = _matmul(xn, w_t, n_main, BF16, col_scale=col_scale, w_is_transposed=True, tm=2048)
    w_gate_t = jnp.pad(w_t[n_main:], ((0, 128 - 3 * H), (0, 0)))
    gate_logits = _matmul(xn, w_gate_t, 128, F32, tn=128, w_is_transposed=True)

    cmp_kv = _compress(projm, cmp_w1, cmp_w2, cmp_pos.reshape(2, 1, CMP_LEN * d), B, S)
    o_cmp, sel, need = _cmp_select(projm, cmp_kv, _overlap_matrix(S, S // CMP_STRIDE), B, S)
    n_t = S // SLC_KEY_TILE
    need = need[:, :, :, :, :n_t, 0].reshape(B, G, n_t, n_t)
    o_slc = _slc_attention(projm, sel, need, B, S)
    return _win_attention(projm, gate_logits.T, o_cmp, o_slc, B, S)


def kernel(x, attn_norm_w, ffn_norm_w, final_norm_w, a_w_in, a_lambda, a_subln_w, a_w_out, b_w_in,
           b_cmp_pos, b_cmp_w1, b_cmp_w2, b_w_out, moe_w_group, moe_b_group, moe_w_expert,
           moe_b_expert, moe_w_gate, moe_w_up, moe_w_down):
    B, S, D = x.shape
    T = B * S
    depth = attn_norm_w.shape[0]
    h = x
    xn = _rmsnorm(h, attn_norm_w[0], BF16)
    for i in range(depth):
        j = i // 2
        if i % 2 == 0:
            lambda_init = 0.8 - 0.6 * math.exp(-0.3 * i)
            n_q = A_HEADS * 2 * A_DH
            n_in = 2 * n_q + A_HEADS * A_DV
            col_scale = jnp.concatenate([jnp.full((n_q,), A_DH ** -0.5 * LOG2E, F32),
                                         jnp.ones((n_in - n_q,), F32)])
            qkv = _matmul(xn, a_w_in[j], n_in, BF16, col_scale=col_scale, tm=2048)
            o = _diff_attention(qkv, a_lambda[j], a_subln_w[j], lambda_init, B, S)
            h = _matmul(o, a_w_out[j], D, F32, resid=h)
        else:
            o = _nsa_attention(xn, b_w_in[j], b_cmp_pos[j], b_cmp_w1[j], b_cmp_w2[j], B, S)
            h = _matmul(o, b_w_out[j], D, F32, resid=h)
        last = i == depth - 1
        h, xn = _hier_moe(h, ffn_norm_w[i], moe_w_group[i], moe_b_group[i], moe_w_expert[i], moe_b_expert[i],
                          moe_w_gate, moe_w_up, moe_w_down, i,
                          final_norm_w if last else attn_norm_w[i + 1], F32 if last else BF16, not last,
                          (B, S, D) if last else None)
    return xn
```

```python
import functools
import math

import numpy as np
import jax
import jax.numpy as jnp
from jax import lax
from jax.experimental import pallas as pl
from jax.experimental.pallas import tpu as pltpu

F32 = jnp.float32
BF16 = jnp.bfloat16
I32 = jnp.int32

EPS = 1e-6
NEG = -1e30
FORCE = 1e4
LOG2E = math.log2(math.e)

A_HEADS = 8
A_DH = 128
A_DV = 256
B_HEADS = 16
B_GROUPS = 4
B_HPG = 4
B_D = 128
CMP_LEN = 32
CMP_STRIDE = 16
SLC_BLOCK = 64
SLC_TOPN = 16
WINDOW = 512
N_GROUPS = 4
EPG = 8
N_EXPERTS = 32
D_EXPERT = 1024
ROUTER_LANES = 128
MOE_ROWS = 256
ATTN_QC = 512
SLC_KEY_TILE = 512

VMEM_LIMIT = 56 * 1024 * 1024


def _cparams(sem, flags=None):
    return pltpu.CompilerParams(dimension_semantics=sem, vmem_limit_bytes=VMEM_LIMIT, flags=flags)


def _alibi_slopes(n):
    return jnp.asarray(np.array([2.0 ** (-8.0 * (i + 1) / n) for i in range(n)], dtype=np.float32))


def _nt_dot(a, b):
    return lax.dot_general(a, b, (((1,), (1,)), ((), ())), preferred_element_type=F32)


def _rms_kernel(x_ref, w_ref, o_ref):
    x = x_ref[...]
    y = x * lax.rsqrt(jnp.mean(x * x, axis=-1, keepdims=True) + EPS)
    o_ref[...] = (y * w_ref[...]).astype(o_ref.dtype)


def _rows_spec(x, tm, tn, col_arg):
    if x.ndim == 2:
        return pl.BlockSpec((tm, tn), lambda *g: (g[-1], col_arg(g)))
    per_seq = x.shape[1] // tm
    return pl.BlockSpec((None, tm, tn), lambda *g: (g[-1] // per_seq, g[-1] % per_seq, col_arg(g)))


def _rmsnorm(x, w, out_dtype, tm=512):
    D = x.shape[-1]
    T = x.size // D
    return pl.pallas_call(
        _rms_kernel,
        out_shape=jax.ShapeDtypeStruct((T, D), out_dtype),
        grid=(T // tm,),
        in_specs=[_rows_spec(x, tm, D, lambda g: 0),
                  pl.BlockSpec((1, D), lambda i: (0, 0))],
        out_specs=pl.BlockSpec((tm, D), lambda i: (i, 0)),
        compiler_params=_cparams(("parallel",)),
        name="rmsnorm",
    )(x, w.reshape(1, D))


def _mm_kernel(*refs, has_scale, has_resid, w_is_transposed):
    a_ref, w_ref = refs[0], refs[1]
    k = 2
    scale_ref = resid_ref = None
    if has_scale:
        scale_ref = refs[k]
        k += 1
    if has_resid:
        resid_ref = refs[k]
        k += 1
    o_ref, wbf_ref = refs[k], refs[k + 1]

    @pl.when(pl.program_id(1) == 0)
    def _():
        wbf_ref[...] = w_ref[...].astype(BF16)

    if w_is_transposed:
        acc = _nt_dot(a_ref[...], wbf_ref[...])
    else:
        acc = jnp.dot(a_ref[...], wbf_ref[...], preferred_element_type=F32)
    if has_scale:
        acc = acc * scale_ref[...]
    if has_resid:
        acc = acc + resid_ref[...]
    o_ref[...] = acc.astype(o_ref.dtype)


def _matmul(a, w, n_out, out_dtype, *, col_scale=None, resid=None, tm=1024, tn=1024, w_is_transposed=False):
    M, K = a.shape
    if w_is_transposed:
        assert M % tm == 0 and n_out % tn == 0 and w.shape[1] == K and w.shape[0] >= n_out
        w_spec, w_block = pl.BlockSpec((tn, K), lambda j, i: (j, 0)), (tn, K)
    else:
        assert M % tm == 0 and n_out % tn == 0 and w.shape[0] == K and w.shape[1] >= n_out
        w_spec, w_block = pl.BlockSpec((K, tn), lambda j, i: (0, j)), (K, tn)
    in_specs = [pl.BlockSpec((tm, K), lambda j, i: (i, 0)), w_spec]
    args = [a, w]
    if col_scale is not None:
        in_specs.append(pl.BlockSpec((1, tn), lambda j, i: (0, j)))
        args.append(col_scale.reshape(1, n_out))
    if resid is not None:
        in_specs.append(_rows_spec(resid, tm, tn, lambda g: g[0]))
        args.append(resid)
    return pl.pallas_call(
        functools.partial(_mm_kernel, has_scale=col_scale is not None, has_resid=resid is not None,
                          w_is_transposed=w_is_transposed),
        out_shape=jax.ShapeDtypeStruct((M, n_out), out_dtype),
        grid=(n_out // tn, M // tm),
        in_specs=in_specs,
        out_specs=pl.BlockSpec((tm, tn), lambda j, i: (i, j)),
        scratch_shapes=[pltpu.VMEM(w_block, BF16)],
        compiler_params=_cparams(("parallel", "arbitrary")),
        name="matmul",
    )(*args)


FEAT_BLK = 64
FEAT_POS = 67
FEAT_ONE = 70


def _key_features(S):
    assert S // SLC_BLOCK <= FEAT_BLK
    j = np.arange(S)
    f = np.zeros((S, 128), np.float32)
    f[j, j // SLC_BLOCK] = 1.0
    f[:, FEAT_BLK:FEAT_BLK + 3] = (j // SLC_BLOCK)[:, None]
    f[:, FEAT_POS:FEAT_POS + 3] = (j % SLC_BLOCK)[:, None]
    f[:, FEAT_ONE:FEAT_ONE + 3] = 1.0
    return jnp.asarray(f, BF16)


def _alibi_query_features(slope2, q0):
    lane = lax.broadcasted_iota(I32, (1, 128), 1)
    base = jnp.where((lane >= FEAT_BLK) & (lane < FEAT_BLK + 3), slope2 * float(SLC_BLOCK),
                     jnp.where((lane >= FEAT_POS) & (lane < FEAT_POS + 3), slope2,
                               jnp.where((lane >= FEAT_ONE) & (lane < FEAT_ONE + 3),
                                         -slope2 * q0.astype(F32), 0.0)))
    hi = base.astype(BF16).astype(F32)
    r1 = base - hi
    lo = r1.astype(BF16).astype(F32)
    lo2 = (r1 - lo).astype(BF16).astype(F32)
    first = (lane == FEAT_BLK) | (lane == FEAT_POS) | (lane == FEAT_ONE)
    second = (lane == FEAT_BLK + 1) | (lane == FEAT_POS + 1) | (lane == FEAT_ONE + 1)
    return jnp.where(first, hi, jnp.where(second, lo, lo2))


def _flash_init(m_ref, l_ref, acc_ref):
    m_ref[...] = jnp.full(m_ref.shape, NEG, F32)
    l_ref[...] = jnp.zeros(l_ref.shape, F32)
    acc_ref[...] = jnp.zeros(acc_ref.shape, F32)


def _flash_probs(sT, m_ref, l_ref, u):
    m_old = m_ref[u]
    m_new = jnp.maximum(m_old, jnp.max(sT, axis=0, keepdims=True))
    alpha = jnp.exp2(m_old - m_new)
    p = jnp.exp2(sT - m_new)
    l_ref[u] = alpha * l_ref[u] + jnp.sum(p, axis=0, keepdims=True)
    m_ref[u] = m_new
    return p.astype(BF16), alpha


def _flash_accumulate(p, alpha, v, acc_ref, u):
    pv = lax.dot_general(v, p, (((0,), (0,)), ((), ())), preferred_element_type=F32)
    acc_ref[u] = alpha * acc_ref[u] + pv


def _flash_units(units, m_ref, l_ref, acc_ref):
    n = len(units)
    scores, probs = {}, {}
    for s in range(n + 2):
        if s < n:
            scores[s] = units[s][0]()
        if 1 <= s <= n:
            probs[s - 1] = _flash_probs(scores.pop(s - 1), m_ref, l_ref, units[s - 1][2])
        if 2 <= s <= n + 1:
            p, alpha = probs.pop(s - 2)
            _flash_accumulate(p, alpha, units[s - 2][1], acc_ref, units[s - 2][2])


def _tri_mask(tk, tq, keep_upper):
    r = lax.broadcasted_iota(I32, (tk, tq), 0)
    c = lax.broadcasted_iota(I32, (tk, tq), 1)
    keep = (r <= c) if keep_upper else (r > c)
    return jnp.where(keep, 0.0, NEG)


def _diff_attn_kernel(slopes_ref, q1_ref, q2_ref, k1_ref, k2_ref, v_ref, feat_ref, lam_ref, sw_ref, o_ref,
                      qa_ref, m_ref, l_ref, acc_ref, *, tq, lambda_init):
    h = pl.program_id(1)
    i = pl.program_id(2)
    _flash_init(m_ref, l_ref, acc_ref)
    qfeat = jnp.broadcast_to(_alibi_query_features(slopes_ref[h] * LOG2E, i * tq), (tq, 128)).astype(BF16)
    for c, q_ref in enumerate((q1_ref, q2_ref)):
        qa_ref[c] = jnp.concatenate([q_ref[...], qfeat], axis=1)

    def tile_units(j, mask):
        k0 = pl.multiple_of(j * tq, tq)
        v = v_ref[pl.ds(k0, tq), :]
        feat = feat_ref[pl.ds(k0, tq), :]
        qc = m_ref.shape[-1]

        def score_fn(c, k_ref, x):
            def fn():
                ka = jnp.concatenate([k_ref[pl.ds(k0, tq), :], feat], axis=1)
                sT = _nt_dot(ka, qa_ref[c, x * qc:(x + 1) * qc, :])
                return sT if mask is None else sT + mask[:, x * qc:(x + 1) * qc]
            return fn

        return [(score_fn(c, k_ref, x), v, c * (tq // qc) + x)
                for c, k_ref in enumerate((k1_ref, k2_ref)) for x in range(tq // qc)]

    def body(p, carry):
        _flash_units(tile_units(2 * p, None) + tile_units(2 * p + 1, None), m_ref, l_ref, acc_ref)
        return carry

    lax.fori_loop(0, lax.shift_right_logical(i, 1), body, 0)

    @pl.when(i % 2 == 1)
    def _():
        _flash_units(tile_units(i - 1, None), m_ref, l_ref, acc_ref)

    _flash_units(tile_units(i, _tri_mask(tq, tq, True)), m_ref, l_ref, acc_ref)

    lam = lam_ref[...]
    lmbda = (jnp.exp(jnp.sum(lam[0:1] * lam[1:2], axis=-1, keepdims=True))
             - jnp.exp(jnp.sum(lam[2:3] * lam[3:4], axis=-1, keepdims=True)) + lambda_init)
    qc = m_ref.shape[-1]
    nx = tq // qc
    for x in range(nx):
        oT = acc_ref[x] / l_ref[x] - lmbda * (acc_ref[nx + x] / l_ref[nx + x])
        o = oT.T
        y = o * lax.rsqrt(jnp.mean(o * o, axis=-1, keepdims=True) + EPS)
        y = (y * sw_ref[...]) * (1.0 - lambda_init)
        o_ref[x * qc:(x + 1) * qc, :] = y.astype(o_ref.dtype)


def _diff_attention(qkv, lam, subln_w, lambda_init, B, S, tq=512):
    T = B * S
    H, dh, dv = A_HEADS, A_DH, A_DV
    n = S // tq
    slopes = _alibi_slopes(H)
    kb = H * 2
    vb = (2 * H * 2 * dh) // dv
    grid_spec = pltpu.PrefetchScalarGridSpec(
        num_scalar_prefetch=1,
        grid=(B, H, n),
        in_specs=[
            pl.BlockSpec((tq, dh), lambda b, h, i, sl: (b * n + i, 2 * h)),
            pl.BlockSpec((tq, dh), lambda b, h, i, sl: (b * n + i, 2 * h + 1)),
            pl.BlockSpec((S, dh), lambda b, h, i, sl: (b, kb + 2 * h)),
            pl.BlockSpec((S, dh), lambda b, h, i, sl: (b, kb + 2 * h + 1)),
            pl.BlockSpec((S, dv), lambda b, h, i, sl: (b, vb + h)),
            pl.BlockSpec((S, 128), lambda b, h, i, sl: (0, 0)),
            pl.BlockSpec((4, dh), lambda b, h, i, sl: (0, 0)),
            pl.BlockSpec((1, dv), lambda b, h, i, sl: (0, 0)),
        ],
        out_specs=pl.BlockSpec((tq, dv), lambda b, h, i, sl: (b * n + i, h)),
        scratch_shapes=[pltpu.VMEM((2, tq, 2 * dh), BF16)] + _flash_state(2, tq, dv),
    )
    return pl.pallas_call(
        functools.partial(_diff_attn_kernel, tq=tq, lambda_init=lambda_init),
        out_shape=jax.ShapeDtypeStruct((T, H * dv), BF16),
        grid_spec=grid_spec,
        compiler_params=_cparams(("parallel", "parallel", "arbitrary")),
        name="diff_attention",
    )(slopes, qkv, qkv, qkv, qkv, qkv, _key_features(S), lam, subln_w.reshape(1, dv))


def _ffn_norm(h_ref, nw_ref):
    x = h_ref[...]
    return (x * lax.rsqrt(jnp.mean(x * x, axis=-1, keepdims=True) + EPS)) * nw_ref[...]


def _router_kernel(h_ref, nw_ref, wr_ref, br_ref, tri_ref, ids_ref, wts_ref, cnt_ref, base_ref):
    @pl.when(pl.program_id(0) == 0)
    def _():
        base_ref[...] = jnp.zeros(base_ref.shape, F32)

    xn = _ffn_norm(h_ref, nw_ref)
    w = wr_ref[...]
    x_hi = xn.astype(BF16)
    x_lo = (xn - x_hi.astype(F32)).astype(BF16)
    w_hi = w.astype(BF16)
    w_lo = (w - w_hi.astype(F32)).astype(BF16)
    logits = (jnp.dot(x_hi, w_hi, preferred_element_type=F32)
              + (jnp.dot(x_hi, w_lo, preferred_element_type=F32)
                 + jnp.dot(x_lo, w_hi, preferred_element_type=F32))) + br_ref[...]
    lane = lax.broadcasted_iota(I32, logits.shape, 1)
    big = jnp.int32(1 << 20)
    is_g = lane < N_GROUPS
    is_e = (lane >= N_GROUPS) & (lane < N_GROUPS + N_EXPERTS)

    lg = jnp.where(is_g, logits, NEG)
    mg = jnp.max(lg, axis=-1, keepdims=True)
    g_top = jnp.min(jnp.where(lg == mg, lane, big), axis=-1, keepdims=True)
    wg = 1.0 / jnp.sum(jnp.where(is_g, jnp.exp(lg - mg), 0.0), axis=-1, keepdims=True)

    in_grp = is_e & (lax.shift_right_logical(lane - N_GROUPS, 3) == g_top)
    le = jnp.where(in_grp, logits, NEG)
    me = jnp.max(le, axis=-1, keepdims=True)
    ee = jnp.where(in_grp, jnp.exp(le - me), 0.0)
    pe = ee / jnp.sum(ee, axis=-1, keepdims=True)
    pe = jnp.where(in_grp, pe, -1.0)
    p1 = jnp.max(pe, axis=-1, keepdims=True)
    i1 = jnp.min(jnp.where(pe == p1, lane, big), axis=-1, keepdims=True)
    pe2 = jnp.where(lane == i1, -1.0, pe)
    p2 = jnp.max(pe2, axis=-1, keepdims=True)
    i2 = jnp.min(jnp.where(pe2 == p2, lane, big), axis=-1, keepdims=True)
    den = p1 + p2
    w0 = wg * p1 / den
    w1 = wg * p2 / den

    oh1 = lane == i1
    oh2 = lane == i2
    both = jnp.where(oh1 | oh2, 1.0, 0.0)
    before = jnp.dot(tri_ref[...], both.astype(BF16), preferred_element_type=F32) + base_ref[...]
    pos0 = jnp.sum(jnp.where(oh1, before, 0.0), axis=-1, keepdims=True).astype(I32)
    pos1 = jnp.sum(jnp.where(oh2, before, 0.0), axis=-1, keepdims=True).astype(I32)
    total = base_ref[...] + jnp.sum(both, axis=0, keepdims=True)
    base_ref[...] = total
    cnt_ref[...] = total

    ids = jnp.where(lane == 0, i1 - N_GROUPS,
                    jnp.where(lane == 1, i2 - N_GROUPS,
                              jnp.where(lane == 2, pos0, jnp.where(lane == 3, pos1, 0))))
    ids_ref[...] = ids.T[:8]
    wts_ref[...] = jnp.where(lane == 0, w0, jnp.where(lane == 1, w1, 0.0))


def _router(h, norm_w, w_group, b_group, w_expert, b_expert, tm=512):
    T, D = h.shape
    pad = ROUTER_LANES - N_GROUPS - N_EXPERTS
    wr = jnp.concatenate([w_group, w_expert, jnp.zeros((D, pad), F32)], axis=1)
    br = jnp.concatenate([b_group, b_expert, jnp.zeros((pad,), F32)]).reshape(1, ROUTER_LANES)
    tri = jnp.asarray(np.tril(np.ones((tm, tm), np.float32), -1), BF16)
    return pl.pallas_call(
        _router_kernel,
        out_shape=(jax.ShapeDtypeStruct((8, T), I32),
                   jax.ShapeDtypeStruct((T, ROUTER_LANES), F32),
                   jax.ShapeDtypeStruct((1, ROUTER_LANES), F32)),
        grid=(T // tm,),
        in_specs=[pl.BlockSpec((tm, D), lambda i: (i, 0)),
                  pl.BlockSpec((1, D), lambda i: (0, 0)),
                  pl.BlockSpec((D, ROUTER_LANES), lambda i: (0, 0)),
                  pl.BlockSpec((1, ROUTER_LANES), lambda i: (0, 0)),
                  pl.BlockSpec((tm, tm), lambda i: (0, 0))],
        out_specs=(pl.BlockSpec((8, tm), lambda i: (0, i)),
                   pl.BlockSpec((tm, ROUTER_LANES), lambda i: (i, 0)),
                   pl.BlockSpec((1, ROUTER_LANES), lambda i: (0, 0))),
        scratch_shapes=[pltpu.VMEM((1, ROUTER_LANES), F32)],
        compiler_params=_cparams(("arbitrary",)),
        name="moe_router",
    )(h, norm_w.reshape(1, D), wr, br, tri)


def _dispatch_kernel(dest_ref, zero_ref, h_ref, nw_ref, xs_ref, xn_ref, zbuf_ref, sem, zsem, *, tm, n_tok):
    xn_ref[...] = _ffn_norm(h_ref, nw_ref)

    @pl.when(pl.program_id(0) == 0)
    def _():
        zbuf_ref[...] = jnp.zeros(zbuf_ref.shape, zbuf_ref.dtype)
        rb = zbuf_ref.shape[0]

        def fill(r):
            return pltpu.make_async_copy(zbuf_ref, xs_ref.at[pl.ds(r * rb, rb)], zsem)

        for r in range(zero_ref.shape[0]):
            @pl.when(zero_ref[r] == 1)
            def _():
                fill(r).start()
        for r in range(zero_ref.shape[0]):
            @pl.when(zero_ref[r] == 1)
            def _():
                fill(r).wait()

    base = pl.program_id(0) * tm
    for t in range(tm):
        for slot in range(2):
            d = dest_ref[slot * n_tok + base + t]
            pltpu.make_async_copy(xn_ref.at[pl.ds(t, 1)], xs_ref.at[pl.ds(d, 1)], sem).start(priority=slot)
    for slot in range(2):
        pltpu.make_async_copy(xn_ref, xs_ref.at[pl.ds(0, tm)], sem).wait()


def _dispatch(h, norm_w, dest_flat, zero_blk, n_rows, tm=256):
    T, D = h.shape
    grid_spec = pltpu.PrefetchScalarGridSpec(
        num_scalar_prefetch=2,
        grid=(T // tm,),
        in_specs=[pl.BlockSpec((tm, D), lambda i, d, z: (i, 0)),
                  pl.BlockSpec((1, D), lambda i, d, z: (0, 0))],
        out_specs=pl.BlockSpec(memory_space=pl.ANY),
        scratch_shapes=[pltpu.VMEM((tm, D), F32), pltpu.VMEM((MOE_ROWS, D), F32),
                        pltpu.SemaphoreType.DMA, pltpu.SemaphoreType.DMA],
    )
    return pl.pallas_call(
        functools.partial(_dispatch_kernel, tm=tm, n_tok=T),
        out_shape=jax.ShapeDtypeStruct((n_rows, D), F32),
        grid_spec=grid_spec,
        compiler_params=_cparams(("arbitrary",)),
        name="moe_dispatch",
    )(dest_flat, zero_blk, h, norm_w.reshape(1, D))


def _moe_plan(counts, n_blk):
    RB = MOE_ROWS
    nb_e = (counts + RB - 1) // RB
    end_b = jnp.cumsum(nb_e)
    start_b = end_b - nb_e
    n_used = end_b[-1:].astype(I32)
    r = jnp.arange(n_blk, dtype=I32)
    be = jnp.minimum(jnp.sum((end_b[None, :] <= r[:, None]).astype(I32), axis=1), N_EXPERTS - 1)
    active = nb_e > 0
    ordinal = jnp.cumsum(active.astype(I32)) - 1
    idx = jnp.arange(N_EXPERTS, dtype=I32)
    later = jnp.where(active[None, :] & (idx[None, :] > idx[:, None]), idx[None, :], N_EXPERTS)
    nxt_e = jnp.min(later, axis=1)
    nxt_e = jnp.where(nxt_e == N_EXPERTS, -1, nxt_e)
    first = ((r == start_b[be]) & (r < n_used[0])).astype(I32)
    plan = jnp.stack([be, first, ordinal[be] % 2, nxt_e[be]]).astype(I32)
    zero_blk = ((r == end_b[be] - 1) | (r >= n_used[0])).astype(I32)
    return plan, n_used, (start_b * RB).astype(I32), zero_blk


def _stream_expert_weights(plan_ref, r, w_hbms, wbuf, sem, layer):
    e, slot, nxt = plan_ref[0, r], plan_ref[2, r], plan_ref[3, r]

    def copies(expert, s):
        rows = wbuf.shape[2] // 2
        return [(pltpu.make_async_copy(w.at[layer, expert, pl.ds(half * rows, rows)],
                                       wbuf.at[s, k, pl.ds(half * rows, rows)], sem.at[s, k, half]), half)
                for k, w in enumerate(w_hbms) for half in range(2)]

    @pl.when(plan_ref[1, r] == 1)
    def _():
        @pl.when(r == 0)
        def _():
            for c, half in copies(e, slot):
                c.start(priority=half)

        for c, _ in copies(e, slot):
            c.wait()

        @pl.when(nxt >= 0)
        def _():
            for c, half in copies(nxt, 1 - slot):
                c.start(priority=half)

    return slot


def _gate_up_kernel(plan_ref, nu_ref, x_ref, wg_hbm, wu_hbm, h_ref, wbuf, sem, *, layer):
    r = pl.program_id(0)

    @pl.when(r < nu_ref[0])
    def _():
        slot = _stream_expert_weights(plan_ref, r, (wg_hbm, wu_hbm), wbuf, sem, layer)
        x = x_ref[...]
        g = jnp.dot(x, wbuf[slot, 0], preferred_element_type=F32)
        u = jnp.dot(x, wbuf[slot, 1], preferred_element_type=F32)
        h_ref[...] = (jax.nn.silu(g) * u).astype(h_ref.dtype)

    @pl.when(r >= nu_ref[0])
    def _():
        h_ref[...] = jnp.zeros(h_ref.shape, h_ref.dtype)


def _down_kernel(plan_ref, nu_ref, h_ref, wd_hbm, y_ref, wbuf, sem, *, layer):
    r = pl.program_id(0)

    @pl.when(r < nu_ref[0])
    def _():
        slot = _stream_expert_weights(plan_ref, r, (wd_hbm,), wbuf, sem, layer)
        y_ref[...] = jnp.dot(h_ref[...].astype(F32), wbuf[slot, 0], preferred_element_type=F32)

    @pl.when(r >= nu_ref[0])
    def _():
        y_ref[...] = jnp.zeros(y_ref.shape, y_ref.dtype)


def _expert_mlp(xs, plan, n_used, w_gate, w_up, w_down, layer):
    n_rows, D = xs.shape
    RB = MOE_ROWS
    n_blk = n_rows // RB
    F = D_EXPERT

    def used_blk(r, plan, nu):
        return (jnp.minimum(r, nu[0] - 1), 0)

    h = pl.pallas_call(
        functools.partial(_gate_up_kernel, layer=layer),
        out_shape=jax.ShapeDtypeStruct((n_rows, F), BF16),
        grid_spec=pltpu.PrefetchScalarGridSpec(
            num_scalar_prefetch=2,
            grid=(n_blk,),
            in_specs=[pl.BlockSpec((RB, D), used_blk),
                      pl.BlockSpec(memory_space=pl.ANY),
                      pl.BlockSpec(memory_space=pl.ANY)],
            out_specs=pl.BlockSpec((RB, F), lambda r, plan, nu: (r, 0)),
            scratch_shapes=[pltpu.VMEM((2, 2, D, F), F32), pltpu.SemaphoreType.DMA((2, 2, 2))],
        ),
        compiler_params=_cparams(("arbitrary",)),
        name="moe_gate_up",
    )(plan, n_used, xs, w_gate, w_up)
    y = pl.pallas_call(
        functools.partial(_down_kernel, layer=layer),
        out_shape=jax.ShapeDtypeStruct((n_rows, D), F32),
        grid_spec=pltpu.PrefetchScalarGridSpec(
            num_scalar_prefetch=2,
            grid=(n_blk,),
            in_specs=[pl.BlockSpec((RB, F), used_blk),
                      pl.BlockSpec(memory_space=pl.ANY)],
            out_specs=pl.BlockSpec((RB, D), lambda r, plan, nu: (r, 0)),
            scratch_shapes=[pltpu.VMEM((2, 1, F, D), F32), pltpu.SemaphoreType.DMA((2, 1, 2))],
        ),
        compiler_params=_cparams(("arbitrary",)),
        name="moe_down",
    )(plan, n_used, h, w_down)
    return y


def _combine_kernel(dest_ref, resid_ref, wts_ref, nw_ref, y_ref, *rest, tm, n_tok, keep_h):
    if keep_h:
        o_ref, n_ref, buf_ref, sem = rest
    else:
        o_ref = None
        n_ref, buf_ref, sem = rest
    i = pl.program_id(0)
    n = pl.num_programs(0)

    def gather(tile, par):
        for t in range(tm):
            for slot in range(2):
                d = dest_ref[slot * n_tok + tile * tm + t]
                pltpu.make_async_copy(y_ref.at[pl.ds(d, 1)], buf_ref.at[par, slot, pl.ds(t, 1)],
                                      sem.at[par]).start(priority=slot)

    @pl.when(i == 0)
    def _():
        gather(i, 0)

    for par in range(2):
        @pl.when((i + 1 < n) & ((i + 1) % 2 == par))
        def _():
            gather(i + 1, par)

    for par in range(2):
        @pl.when(i % 2 == par)
        def _():
            for slot in range(2):
                pltpu.make_async_copy(y_ref.at[pl.ds(0, tm)], buf_ref.at[par, slot], sem.at[par]).wait()
            w = wts_ref[...]
            hn = resid_ref[...] + (w[:, 0:1] * buf_ref[par, 0] + w[:, 1:2] * buf_ref[par, 1])
            if keep_h:
                o_ref[...] = hn
            yn = hn * lax.rsqrt(jnp.mean(hn * hn, axis=-1, keepdims=True) + EPS)
            n_ref[...] = (yn * nw_ref[...]).astype(n_ref.dtype)


def _combine(resid, wts, y, dest_flat, next_norm_w, norm_dtype, keep_h, norm_shape=None, tm=256):
    T, D = resid.shape
    row_spec = pl.BlockSpec((tm, D), lambda i, d: (i, 0))
    if norm_shape is None:
        out_shape = [jax.ShapeDtypeStruct((T, D), norm_dtype)]
        out_specs = [row_spec]
    else:
        per_seq = norm_shape[1] // tm
        out_shape = [jax.ShapeDtypeStruct(norm_shape, norm_dtype)]
        out_specs = [pl.BlockSpec((None, tm, D), lambda i, d: (i // per_seq, i % per_seq, 0))]
    if keep_h:
        out_shape.insert(0, jax.ShapeDtypeStruct((T, D), F32))
        out_specs.insert(0, row_spec)
    grid_spec = pltpu.PrefetchScalarGridSpec(
        num_scalar_prefetch=1,
        grid=(T // tm,),
        in_specs=[row_spec,
                  pl.BlockSpec((tm, ROUTER_LANES), lambda i, d: (i, 0)),
                  pl.BlockSpec((1, D), lambda i, d: (0, 0)),
                  pl.BlockSpec(memory_space=pl.ANY)],
        out_specs=tuple(out_specs),
        scratch_shapes=[pltpu.VMEM((2, 2, tm, D), F32), pltpu.SemaphoreType.DMA((2,))],
    )
    outs = pl.pallas_call(
        functools.partial(_combine_kernel, tm=tm, n_tok=T, keep_h=keep_h),
        out_shape=tuple(out_shape),
        grid_spec=grid_spec,
        compiler_params=_cparams(("arbitrary",)),
        name="moe_combine",
    )(dest_flat, resid, wts, next_norm_w.reshape(1, D), y)
    return (outs[0], outs[1]) if keep_h else (None, outs[0])


def _hier_moe(h, norm_w, w_group, b_group, w_expert, b_expert, w_gate, w_up, w_down, layer,
              next_norm_w, norm_dtype, keep_h, norm_shape=None):
    T, D = h.shape
    RB = MOE_ROWS
    n_blk = (T * 2) // RB + N_EXPERTS
    n_rows = n_blk * RB
    ids, wts, cnt = _router(h, norm_w, w_group, b_group, w_expert, b_expert)
    counts = cnt[0, N_GROUPS:N_GROUPS + N_EXPERTS].astype(I32)
    plan, n_used, start_padded, zero_blk = _moe_plan(counts, n_blk)
    is_e = ids[None, 0:2] == jnp.arange(N_EXPERTS, dtype=I32)[:, None, None]
    dest = (jnp.sum(jnp.where(is_e, start_padded[:, None, None], 0), axis=0) + ids[2:4]).reshape(-1)
    xs = _dispatch(h, norm_w, dest, zero_blk, n_rows)
    y = _expert_mlp(xs, plan, n_used, w_gate, w_up, w_down, layer)
    return _combine(h, wts, y, dest, next_norm_w, norm_dtype, keep_h, norm_shape)


def _compress_kernel(x_ref, w1_ref, w2_ref, pe_ref, o_ref, xf_ref):
    d = B_D
    half = CMP_STRIDE * d
    nch = x_ref.shape[0] // CMP_STRIDE
    xf_ref[...] = x_ref[...].astype(F32)
    w1 = w1_ref[...].astype(BF16)
    a = jnp.zeros((nch, w1.shape[1]), F32)
    b = jnp.zeros((nch, w1.shape[1]), F32)
    for l in range(CMP_STRIDE):
        xl = xf_ref[pl.ds(l, nch, stride=CMP_STRIDE), :].astype(BF16)
        a = a + jnp.dot(xl, w1[l * d:(l + 1) * d], preferred_element_type=F32)
        b = b + jnp.dot(xl, w1[half + l * d:half + (l + 1) * d], preferred_element_type=F32)
    pe = jnp.broadcast_to(pe_ref[...], (8, 2 * half)).astype(BF16)
    pt = jnp.dot(pe, w1, preferred_element_type=F32)[0:1]
    hid = a + pltpu.roll(b, b.shape[0] - 1, 0) + pt
    g = jax.nn.gelu(hid)
    o_ref[...] = jnp.dot(g.astype(BF16), w2_ref[...].astype(BF16),
                         preferred_element_type=F32).astype(o_ref.dtype)


def _compress(projm, w1, w2, pe_flat, B, S):
    G, d = B_GROUPS, B_D
    nch = S // CMP_STRIDE
    width = CMP_STRIDE * d
    hid = w1.shape[-1]
    cb = B_HEADS
    return pl.pallas_call(
        _compress_kernel,
        out_shape=jax.ShapeDtypeStruct((2, B, G, nch, d), BF16),
        grid=(2, B, G),
        in_specs=[pl.BlockSpec((S, d), lambda kv, b, g: (b, cb + kv * G + g)),
                  pl.BlockSpec((None, 2 * width, hid), lambda kv, b, g: (kv, 0, 0)),
                  pl.BlockSpec((None, hid, d), lambda kv, b, g: (kv, 0, 0)),
                  pl.BlockSpec((None, 1, 2 * width), lambda kv, b, g: (kv, 0, 0))],
        out_specs=pl.BlockSpec((None, None, None, nch, d), lambda kv, b, g: (kv, b, g, 0, 0)),
        scratch_shapes=[pltpu.VMEM((S, d), F32)],
        compiler_params=_cparams(("arbitrary", "arbitrary", "arbitrary")),
        name="nsa_compress",
    )(projm, w1, w2, pe_flat)


def _cmp_select_kernel(slopes_ref, q_ref, kc_ref, vc_ref, ov_ref, o_ref, sel_ref, need_ref, *, tq, n_cmp,
                       n_blocks, n_rank):
    g = pl.program_id(1)
    i = pl.program_id(2)
    d = B_D
    nch = kc_ref.shape[0]
    pos = i * tq + lax.broadcasted_iota(I32, (1, tq), 1)
    c = lax.broadcasted_iota(I32, (nch, 1), 0)
    dist_i = pos - (c * CMP_STRIDE + (CMP_LEN - 1))
    valid = (dist_i >= 0) & (c < n_cmp)
    dist = dist_i.astype(F32)
    any_valid = jnp.where(pos >= CMP_LEN - 1, 1.0, 0.0)
    kc = kc_ref[...]
    vc = vc_ref[...]
    psum = jnp.zeros((nch, tq), F32)
    for hh in range(B_HPG):
        s = _nt_dot(kc, q_ref[:, hh * d:(hh + 1) * d])
        s = jnp.where(valid, s - (slopes_ref[g * B_HPG + hh] * LOG2E) * dist, NEG)
        e = jnp.exp2(s - jnp.max(s, axis=0, keepdims=True))
        p = (e / jnp.sum(e, axis=0, keepdims=True)) * any_valid
        oT = lax.dot_general(vc, p.astype(BF16), (((0,), (0,)), ((), ())), preferred_element_type=F32)
        o_ref[:, hh * d:(hh + 1) * d] = oT.T.astype(o_ref.dtype)
        psum = psum + p

    p_hi = psum.astype(BF16)
    p_lo = (psum - p_hi.astype(F32)).astype(BF16)
    ov = ov_ref[...]
    imp = jnp.dot(ov, p_hi, preferred_element_type=F32) + jnp.dot(ov, p_lo, preferred_element_type=F32)
    imp = imp[:n_rank]

    blk = lax.broadcasted_iota(I32, (n_rank, 1), 0)
    cur = lax.shift_right_logical(pos, SLC_BLOCK.bit_length() - 1)
    forced = (blk == 0) | (blk == cur) | (blk == cur - 1)
    causal = blk * SLC_BLOCK <= pos
    score = jnp.where(forced, FORCE, jnp.where(causal, imp, -FORCE))
    score = jnp.where(blk < n_blocks, score, -2.0 * FORCE)
    blk_f = blk.astype(F32)
    remaining = score
    sel_add = jnp.full(score.shape, NEG, F32)
    for _ in range(SLC_TOPN):
        top = jnp.max(remaining, axis=0, keepdims=True)
        first = jnp.min(jnp.where(remaining == top, blk_f, float(n_rank)), axis=0, keepdims=True)
        pick = blk_f == first
        sel_add = jnp.where(pick, 0.0, sel_add)
        remaining = jnp.where(pick, -4.0 * FORCE, remaining)
    pad = jnp.zeros((128 - n_rank, tq), F32)
    sel_t = jnp.concatenate([jnp.where(blk < n_blocks, sel_add, 0.0), pad], axis=0) if n_rank < 128 else sel_add
    sel_ref[...] = sel_t.T.astype(sel_ref.dtype)
    per_tile = SLC_KEY_TILE // SLC_BLOCK
    chosen = jnp.where(sel_add == 0.0, 1, 0)
    for qh in range(tq // SLC_KEY_TILE):
        blk_any = jnp.max(chosen[:, qh * SLC_KEY_TILE:(qh + 1) * SLC_KEY_TILE], axis=1, keepdims=True)
        rows = []
        for j in range(8):
            if (j + 1) * per_tile <= n_rank:
                hit = jnp.max(blk_any[j * per_tile:(j + 1) * per_tile], axis=0, keepdims=True)
                rows.append(jnp.broadcast_to(hit, (1, 128)))
            else:
                rows.append(jnp.zeros((1, 128), I32))
        need_ref[qh] = jnp.concatenate(rows, axis=0)


def _cmp_select(projm, cmp_kv, overlap, B, S, tq=1024):
    T = B * S
    G, d = B_GROUPS, B_D
    nq = S // tq
    nch = cmp_kv.shape[3]
    n_cmp = (S - CMP_LEN) // CMP_STRIDE + 1
    n_blocks = S // SLC_BLOCK
    assert SLC_TOPN <= n_blocks <= FEAT_BLK
    slopes = _alibi_slopes(B_HEADS)
    grid_spec = pltpu.PrefetchScalarGridSpec(
        num_scalar_prefetch=1,
        grid=(B, G, nq),
        in_specs=[pl.BlockSpec((tq, B_HPG * d), lambda b, g, i, sl: (b * nq + i, g)),
                  pl.BlockSpec((None, None, None, nch, d), lambda b, g, i, sl: (0, b, g, 0, 0)),
                  pl.BlockSpec((None, None, None, nch, d), lambda b, g, i, sl: (1, b, g, 0, 0)),
                  pl.BlockSpec((128, nch), lambda b, g, i, sl: (0, 0))],
        out_specs=(pl.BlockSpec((tq, B_HPG * d), lambda b, g, i, sl: (b * nq + i, g)),
                   pl.BlockSpec((None, None, tq, 128), lambda b, g, i, sl: (b, g, i, 0)),
                   pl.BlockSpec((None, None, None, tq // SLC_KEY_TILE, 8, 128),
                                lambda b, g, i, sl: (b, g, i, 0, 0, 0))),
    )
    assert S // SLC_KEY_TILE <= 8 and tq % SLC_KEY_TILE == 0
    return pl.pallas_call(
        functools.partial(_cmp_select_kernel, tq=tq, n_cmp=n_cmp, n_blocks=n_blocks,
                          n_rank=-(-n_blocks // 8) * 8),
        out_shape=(jax.ShapeDtypeStruct((T, B_HEADS * d), BF16),
                   jax.ShapeDtypeStruct((B, G, S, 128), BF16),
                   jax.ShapeDtypeStruct((B, G, nq, tq // SLC_KEY_TILE, 8, 128), I32)),
        grid_spec=grid_spec,
        compiler_params=_cparams(("parallel", "parallel", "arbitrary")),
        name="nsa_cmp_select",
    )(slopes, projm, cmp_kv, cmp_kv, overlap)


def _head_units(k_ref, v_ref, feat_ref, qa_ref, j, mask, qc):
    n_heads, tq, _ = qa_ref.shape
    k0 = pl.multiple_of(j * tq, tq)
    v = v_ref[pl.ds(k0, tq), :]

    def score_fn(hh, c):
        def fn():
            ka = jnp.concatenate([k_ref[pl.ds(k0, tq), :], feat_ref[pl.ds(k0, tq), :]], axis=1)
            sT = _nt_dot(ka, qa_ref[hh, c * qc:(c + 1) * qc, :])
            return sT if mask is None else sT + mask[:, c * qc:(c + 1) * qc]
        return fn

    return [(score_fn(hh, c), v, hh * (tq // qc) + c) for hh in range(n_heads) for c in range(tq // qc)]


def _write_heads_T(o_ref, acc_ref, l_ref):
    tq = o_ref.shape[0]
    qc = l_ref.shape[-1]
    for hh in range(B_HPG):
        for c in range(tq // qc):
            u = hh * (tq // qc) + c
            o_ref[c * qc:(c + 1) * qc, hh * B_D:(hh + 1) * B_D] = (acc_ref[u] / l_ref[u]).T.astype(o_ref.dtype)


def _flash_state(n_heads, tq, dv):
    qc = min(ATTN_QC, tq)
    n_units = n_heads * (tq // qc)
    return [pltpu.VMEM((n_units, 1, qc), F32), pltpu.VMEM((n_units, 1, qc), F32),
            pltpu.VMEM((n_units, dv, qc), F32)]


def _stage_queries(q_ref, qa_ref, slopes_ref, g, q0, sel_add):
    tq = q_ref.shape[0]
    lane = lax.broadcasted_iota(I32, (tq, 128), 1)
    for hh in range(B_HPG):
        coef = jnp.broadcast_to(_alibi_query_features(slopes_ref[g * B_HPG + hh] * LOG2E, q0), (tq, 128))
        if sel_add is not None:
            coef = jnp.where(lane < FEAT_BLK, sel_add, coef)
        qa_ref[hh] = jnp.concatenate([q_ref[:, hh * B_D:(hh + 1) * B_D], coef.astype(BF16)], axis=1)


def _slc_attn_kernel(slopes_ref, need_ref, q_ref, k_ref, v_ref, feat_ref, sel_ref, o_ref, qa_ref, m_ref, l_ref,
                     acc_ref, *, tq):
    b = pl.program_id(0)
    g = pl.program_id(1)
    i = pl.program_id(2)
    n = pl.num_programs(2)
    _flash_init(m_ref, l_ref, acc_ref)
    _stage_queries(q_ref, qa_ref, slopes_ref, g, i * tq, sel_ref[...].astype(F32))

    qc = m_ref.shape[-1]

    def units(j, causal=None):
        return _head_units(k_ref, v_ref, feat_ref, qa_ref, j, causal, qc)

    need_base = ((b * pl.num_programs(1) + g) * n + i) * n

    def needed(j):
        return (j == 0) | (need_ref[need_base + jnp.maximum(j, 0)] > 0)

    def body(p, carry):
        j0, j1 = 2 * p, 2 * p + 1
        n0, n1 = needed(j0), needed(j1)

        @pl.when(n0 & n1)
        def _():
            _flash_units(units(j0) + units(j1), m_ref, l_ref, acc_ref)

        @pl.when(n0 & jnp.logical_not(n1))
        def _():
            _flash_units(units(j0), m_ref, l_ref, acc_ref)

        @pl.when(jnp.logical_not(n0) & n1)
        def _():
            _flash_units(units(j1), m_ref, l_ref, acc_ref)

        return carry

    lax.fori_loop(0, lax.shift_right_logical(i, 1), body, 0)

    @pl.when((i % 2 == 1) & needed(i - 1))
    def _():
        _flash_units(units(i - 1), m_ref, l_ref, acc_ref)

    _flash_units(units(i, _tri_mask(tq, tq, True)), m_ref, l_ref, acc_ref)
    _write_heads_T(o_ref, acc_ref, l_ref)


def _slc_attention(projm, sel, need, B, S):
    T = B * S
    G, d = B_GROUPS, B_D
    tq = SLC_KEY_TILE
    n = S // tq
    slopes = _alibi_slopes(B_HEADS)
    ksb = (B_HEADS * d + 2 * G * d) // d
    vsb = ksb + G
    grid_spec = pltpu.PrefetchScalarGridSpec(
        num_scalar_prefetch=2,
        grid=(B, G, n),
        in_specs=[
            pl.BlockSpec((tq, B_HPG * d), lambda b, g, i, sl, nd: (b * n + i, g)),
            pl.BlockSpec((S, d), lambda b, g, i, sl, nd: (b, ksb + g)),
            pl.BlockSpec((S, d), lambda b, g, i, sl, nd: (b, vsb + g)),
            pl.BlockSpec((S, 128), lambda b, g, i, sl, nd: (0, 0)),
            pl.BlockSpec((None, None, tq, 128), lambda b, g, i, sl, nd: (b, g, i, 0)),
        ],
        out_specs=pl.BlockSpec((tq, B_HPG * d), lambda b, g, i, sl, nd: (b * n + i, g)),
        scratch_shapes=[pltpu.VMEM((B_HPG, tq, 2 * d), BF16)] + _flash_state(B_HPG, tq, d),
    )
    return pl.pallas_call(
        functools.partial(_slc_attn_kernel, tq=tq),
        out_shape=jax.ShapeDtypeStruct((T, B_HEADS * d), BF16),
        grid_spec=grid_spec,
        compiler_params=_cparams(("parallel", "parallel", "arbitrary")),
        name="nsa_selected_attention",
    )(slopes, need.reshape(-1), projm, projm, projm, _key_features(S), sel)


def _win_attn_kernel(slopes_ref, q_ref, k_ref, v_ref, feat_ref, gl_ref, oc_ref, os_ref, o_ref, qa_ref, m_ref,
                     l_ref, acc_ref, *, tq):
    g = pl.program_id(1)
    i = pl.program_id(2)
    _flash_init(m_ref, l_ref, acc_ref)
    _stage_queries(q_ref, qa_ref, slopes_ref, g, i * tq, None)

    qc = m_ref.shape[-1]

    def units(j, mask):
        return _head_units(k_ref, v_ref, feat_ref, qa_ref, j, mask, qc)

    @pl.when(i > 0)
    def _():
        _flash_units(units(i - 1, _tri_mask(tq, tq, False)) + units(i, _tri_mask(tq, tq, True)),
                     m_ref, l_ref, acc_ref)

    @pl.when(i == 0)
    def _():
        _flash_units(units(i, _tri_mask(tq, tq, True)), m_ref, l_ref, acc_ref)

    n_chunks = tq // qc
    for hh in range(B_HPG):
        row = (g * B_HPG + hh) * 3
        for c in range(n_chunks):
            qs = slice(c * qc, (c + 1) * qc)
            cols = slice(hh * B_D, (hh + 1) * B_D)
            u = hh * n_chunks + c
            gates = [jax.nn.sigmoid(gl_ref[pl.ds(row + br, 1), qs]) for br in range(3)]
            mix = (gates[0] * oc_ref[qs, cols].astype(F32).T + gates[1] * os_ref[qs, cols].astype(F32).T
                   + gates[2] * (acc_ref[u] / l_ref[u]))
            o_ref[qs, cols] = mix.T.astype(o_ref.dtype)


def _win_attention(projm, gate_logits_t, o_cmp, o_slc, B, S):
    T = B * S
    G, d = B_GROUPS, B_D
    tq = WINDOW
    n = S // tq
    slopes = _alibi_slopes(B_HEADS)
    kwb = (B_HEADS * d + 4 * G * d) // d
    vwb = kwb + G
    grid_spec = pltpu.PrefetchScalarGridSpec(
        num_scalar_prefetch=1,
        grid=(B, G, n),
        in_specs=[
            pl.BlockSpec((tq, B_HPG * d), lambda b, g, i, sl: (b * n + i, g)),
            pl.BlockSpec((S, d), lambda b, g, i, sl: (b, kwb + g)),
            pl.BlockSpec((S, d), lambda b, g, i, sl: (b, vwb + g)),
            pl.BlockSpec((S, 128), lambda b, g, i, sl: (0, 0)),
            pl.BlockSpec((128, tq), lambda b, g, i, sl: (0, b * n + i)),
            pl.BlockSpec((tq, B_HPG * d), lambda b, g, i, sl: (b * n + i, g)),
            pl.BlockSpec((tq, B_HPG * d), lambda b, g, i, sl: (b * n + i, g)),
        ],
        out_specs=pl.BlockSpec((tq, B_HPG * d), lambda b, g, i, sl: (b * n + i, g)),
        scratch_shapes=[pltpu.VMEM((B_HPG, tq, 2 * d), BF16)] + _flash_state(B_HPG, tq, d),
    )
    return pl.pallas_call(
        functools.partial(_win_attn_kernel, tq=tq),
        out_shape=jax.ShapeDtypeStruct((T, B_HEADS * d), BF16),
        grid_spec=grid_spec,
        compiler_params=_cparams(("parallel", "parallel", "arbitrary")),
        name="nsa_window_attention",
    )(slopes, projm, projm, projm, _key_features(S), gate_logits_t, o_cmp, o_slc)


def _overlap_matrix(S, n_rows):
    nc = (S - CMP_LEN) // CMP_STRIDE + 1
    nsb = S // SLC_BLOCK
    cs = np.arange(nc) * CMP_STRIDE
    ss = np.arange(nsb) * SLC_BLOCK
    ov = np.clip(np.minimum(cs[:, None] + CMP_LEN, ss[None, :] + SLC_BLOCK)
                 - np.maximum(cs[:, None], ss[None, :]), 0, None) / CMP_LEN
    out = np.zeros((128, n_rows), np.float32)
    out[:nsb, :nc] = ov.T
    return jnp.asarray(out, BF16)


def _nsa_attention(xn, w_in, cmp_pos, cmp_w1, cmp_w2, B, S):
    T = B * S
    H, G, d = B_HEADS, B_GROUPS, B_D
    n_main = H * d + 6 * G * d
    col_scale = jnp.concatenate([jnp.full((H * d,), d ** -0.5 * LOG2E, F32), jnp.ones((n_main - H * d,), F32)])
    w_t = jnp.swapaxes(w_in, 0, 1)
    projm = _matmul(xn, w_t, n_main, BF16, col_scale=col_scale, w_is_transposed=True, tm=2048)
    w_gate_t = jnp.pad(w_t[n_main:], ((0, 128 - 3 * H), (0, 0)))
    gate_logits = _matmul(xn, w_gate_t, 128, F32, tn=128, w_is_transposed=True)

    cmp_kv = _compress(projm, cmp_w1, cmp_w2, cmp_pos.reshape(2, 1, CMP_LEN * d), B, S)
    o_cmp, sel, need = _cmp_select(projm, cmp_kv, _overlap_matrix(S, S // CMP_STRIDE), B, S)
    n_t = S // SLC_KEY_TILE
    need = need[:, :, :, :, :n_t, 0].reshape(B, G, n_t, n_t)
    o_slc = _slc_attention(projm, sel, need, B, S)
    return _win_attention(projm, gate_logits.T, o_cmp, o_slc, B, S)


def kernel(x, attn_norm_w, ffn_norm_w, final_norm_w, a_w_in, a_lambda, a_subln_w, a_w_out, b_w_in,
           b_cmp_pos, b_cmp_w1, b_cmp_w2, b_w_out, moe_w_group, moe_b_group, moe_w_expert,
           moe_b_expert, moe_w_gate, moe_w_up, moe_w_down):
    B, S, D = x.shape
    T = B * S
    depth = attn_norm_w.shape[0]
    h = x
    xn = _rmsnorm(h, attn_norm_w[0], BF16)
    for i in range(depth):
        j = i // 2
        if i % 2 == 0:
            lambda_init = 0.8 - 0.6 * math.exp(-0.3 * i)
            n_q = A_HEADS * 2 * A_DH
            n_in = 2 * n_q + A_HEADS * A_DV
            col_scale = jnp.concatenate([jnp.full((n_q,), A_DH ** -0.5 * LOG2E, F32),
                                         jnp.ones((n_in - n_q,), F32)])
            qkv = _matmul(xn, a_w_in[j], n_in, BF16, col_scale=col_scale, tm=2048)
            o = _diff_attention(qkv, a_lambda[j], a_subln_w[j], lambda_init, B, S)
            h = _matmul(o, a_w_out[j], D, F32, resid=h)
        else:
            o = _nsa_attention(xn, b_w_in[j], b_cmp_pos[j], b_cmp_w1[j], b_cmp_w2[j], B, S)
            h = _matmul(o, b_w_out[j], D, F32, resid=h)
        last = i == depth - 1
        h, xn = _hier_moe(h, ffn_norm_w[i], moe_w_group[i], moe_b_group[i], moe_w_expert[i], moe_b_expert[i],
                          moe_w_gate, moe_w_up, moe_w_down, i,
                          final_norm_w if last else attn_norm_w[i + 1], F32 if last else BF16, not last,
                          (B, S, D) if last else None)
    return xn
```

```python
import functools
import math

import numpy as np
import jax
import jax.numpy as jnp
from jax import lax
from jax.experimental import pallas as pl
from jax.experimental.pallas import tpu as pltpu

F32 = jnp.float32
BF16 = jnp.bfloat16
I32 = jnp.int32

EPS = 1e-6
NEG = -1e30
FORCE = 1e4
LOG2E = math.log2(math.e)

A_HEADS = 8
A_DH = 128
A_DV = 256
B_HEADS = 16
B_GROUPS = 4
B_HPG = 4
B_D = 128
CMP_LEN = 32
CMP_STRIDE = 16
SLC_BLOCK = 64
SLC_TOPN = 16
WINDOW = 512
N_GROUPS = 4
EPG = 8
N_EXPERTS = 32
D_EXPERT = 1024
ROUTER_LANES = 128
MOE_ROWS = 256
ATTN_QC = 512
SLC_KEY_TILE = 512

VMEM_LIMIT = 56 * 1024 * 1024


def _cparams(sem, flags=None):
    return pltpu.CompilerParams(dimension_semantics=sem, vmem_limit_bytes=VMEM_LIMIT, flags=flags)


def _alibi_slopes(n):
    return jnp.asarray(np.array([2.0 ** (-8.0 * (i + 1) / n) for i in range(n)], dtype=np.float32))


def _nt_dot(a, b):
    return lax.dot_general(a, b, (((1,), (1,)), ((), ())), preferred_element_type=F32)


def _rms_kernel(x_ref, w_ref, o_ref):
    x = x_ref[...]
    y = x * lax.rsqrt(jnp.mean(x * x, axis=-1, keepdims=True) + EPS)
    o_ref[...] = (y * w_ref[...]).astype(o_ref.dtype)


def _rows_spec(x, tm, tn, col_arg):
    if x.ndim == 2:
        return pl.BlockSpec((tm, tn), lambda *g: (g[-1], col_arg(g)))
    per_seq = x.shape[1] // tm
    return pl.BlockSpec((None, tm, tn), lambda *g: (g[-1] // per_seq, g[-1] % per_seq, col_arg(g)))


def _rmsnorm(x, w, out_dtype, tm=512):
    D = x.shape[-1]
    T = x.size // D
    return pl.pallas_call(
        _rms_kernel,
        out_shape=jax.ShapeDtypeStruct((T, D), out_dtype),
        grid=(T // tm,),
        in_specs=[_rows_spec(x, tm, D, lambda g: 0),
                  pl.BlockSpec((1, D), lambda i: (0, 0))],
        out_specs=pl.BlockSpec((tm, D), lambda i: (i, 0)),
        compiler_params=_cparams(("parallel",)),
        name="rmsnorm",
    )(x, w.reshape(1, D))


def _mm_kernel(*refs, has_scale, has_resid, w_is_transposed):
    a_ref, w_ref = refs[0], refs[1]
    k = 2
    scale_ref = resid_ref = None
    if has_scale:
        scale_ref = refs[k]
        k += 1
    if has_resid:
        resid_ref = refs[k]
        k += 1
    o_ref, wbf_ref = refs[k], refs[k + 1]

    @pl.when(pl.program_id(1) == 0)
    def _():
        wbf_ref[...] = w_ref[...].astype(BF16)

    if w_is_transposed:
        acc = _nt_dot(a_ref[...], wbf_ref[...])
    else:
        acc = jnp.dot(a_ref[...], wbf_ref[...], preferred_element_type=F32)
    if has_scale:
        acc = acc * scale_ref[...]
    if has_resid:
        acc = acc + resid_ref[...]
    o_ref[...] = acc.astype(o_ref.dtype)


def _matmul(a, w, n_out, out_dtype, *, col_scale=None, resid=None, tm=1024, tn=1024, w_is_transposed=False):
    M, K = a.shape
    if w_is_transposed:
        assert M % tm == 0 and n_out % tn == 0 and w.shape[1] == K and w.shape[0] >= n_out
        w_spec, w_block = pl.BlockSpec((tn, K), lambda j, i: (j, 0)), (tn, K)
    else:
        assert M % tm == 0 and n_out % tn == 0 and w.shape[0] == K and w.shape[1] >= n_out
        w_spec, w_block = pl.BlockSpec((K, tn), lambda j, i: (0, j)), (K, tn)
    in_specs = [pl.BlockSpec((tm, K), lambda j, i: (i, 0)), w_spec]
    args = [a, w]
    if col_scale is not None:
        in_specs.append(pl.BlockSpec((1, tn), lambda j, i: (0, j)))
        args.append(col_scale.reshape(1, n_out))
    if resid is not None:
        in_specs.append(_rows_spec(resid, tm, tn, lambda g: g[0]))
        args.append(resid)
    return pl.pallas_call(
        functools.partial(_mm_kernel, has_scale=col_scale is not None, has_resid=resid is not None,
                          w_is_transposed=w_is_transposed),
        out_shape=jax.ShapeDtypeStruct((M, n_out), out_dtype),
        grid=(n_out // tn, M // tm),
        in_specs=in_specs,
        out_specs=pl.BlockSpec((tm, tn), lambda j, i: (i, j)),
        scratch_shapes=[pltpu.VMEM(w_block, BF16)],
        compiler_params=_cparams(("parallel", "arbitrary")),
        name="matmul",
    )(*args)


FEAT_BLK = 64
FEAT_POS = 67
FEAT_ONE = 70


def _key_features(S):
    assert S // SLC_BLOCK <= FEAT_BLK
    j = np.arange(S)
    f = np.zeros((S, 128), np.float32)
    f[j, j // SLC_BLOCK] = 1.0
    f[:, FEAT_BLK:FEAT_BLK + 3] = (j // SLC_BLOCK)[:, None]
    f[:, FEAT_POS:FEAT_POS + 3] = (j % SLC_BLOCK)[:, None]
    f[:, FEAT_ONE:FEAT_ONE + 3] = 1.0
    return jnp.asarray(f, BF16)


def _alibi_query_features(slope2, q0):
    lane = lax.broadcasted_iota(I32, (1, 128), 1)
    base = jnp.where((lane >= FEAT_BLK) & (lane < FEAT_BLK + 3), slope2 * float(SLC_BLOCK),
                     jnp.where((lane >= FEAT_POS) & (lane < FEAT_POS + 3), slope2,
                               jnp.where((lane >= FEAT_ONE) & (lane < FEAT_ONE + 3),
                                         -slope2 * q0.astype(F32), 0.0)))
    hi = base.astype(BF16).astype(F32)
    r1 = base - hi
    lo = r1.astype(BF16).astype(F32)
    lo2 = (r1 - lo).astype(BF16).astype(F32)
    first = (lane == FEAT_BLK) | (lane == FEAT_POS) | (lane == FEAT_ONE)
    second = (lane == FEAT_BLK + 1) | (lane == FEAT_POS + 1) | (lane == FEAT_ONE + 1)
    return jnp.where(first, hi, jnp.where(second, lo, lo2))


def _flash_init(m_ref, l_ref, acc_ref):
    m_ref[...] = jnp.full(m_ref.shape, NEG, F32)
    l_ref[...] = jnp.zeros(l_ref.shape, F32)
    acc_ref[...] = jnp.zeros(acc_ref.shape, F32)


def _flash_probs(sT, m_ref, l_ref, u):
    m_old = m_ref[u]
    m_new = jnp.maximum(m_old, jnp.max(sT, axis=0, keepdims=True))
    alpha = jnp.exp2(m_old - m_new)
    p = jnp.exp2(sT - m_new)
    l_ref[u] = alpha * l_ref[u] + jnp.sum(p, axis=0, keepdims=True)
    m_ref[u] = m_new
    return p.astype(BF16), alpha


def _flash_accumulate(p, alpha, v, acc_ref, u):
    pv = lax.dot_general(v, p, (((0,), (0,)), ((), ())), preferred_element_type=F32)
    acc_ref[u] = alpha * acc_ref[u] + pv


def _flash_units(units, m_ref, l_ref, acc_ref):
    n = len(units)
    scores, probs = {}, {}
    for s in range(n + 3):
        if s < n:
            scores[s] = units[s][0]()
        if 2 <= s <= n + 1:
            probs[s - 2] = _flash_probs(scores.pop(s - 2), m_ref, l_ref, units[s - 2][2])
        if 3 <= s <= n + 2:
            p, alpha = probs.pop(s - 3)
            _flash_accumulate(p, alpha, units[s - 3][1], acc_ref, units[s - 3][2])


def _tri_mask(tk, tq, keep_upper):
    r = lax.broadcasted_iota(I32, (tk, tq), 0)
    c = lax.broadcasted_iota(I32, (tk, tq), 1)
    keep = (r <= c) if keep_upper else (r > c)
    return jnp.where(keep, 0.0, NEG)


def _diff_attn_kernel(slopes_ref, q1_ref, q2_ref, k1_ref, k2_ref, v_ref, feat_ref, lam_ref, sw_ref, o_ref,
                      qa_ref, m_ref, l_ref, acc_ref, *, tq, lambda_init):
    h = pl.program_id(1)
    i = pl.program_id(2)
    _flash_init(m_ref, l_ref, acc_ref)
    qfeat = jnp.broadcast_to(_alibi_query_features(slopes_ref[h] * LOG2E, i * tq), (tq, 128)).astype(BF16)
    for c, q_ref in enumerate((q1_ref, q2_ref)):
        qa_ref[c] = jnp.concatenate([q_ref[...], qfeat], axis=1)

    def tile_units(j, mask):
        k0 = pl.multiple_of(j * tq, tq)
        v = v_ref[pl.ds(k0, tq), :]
        feat = feat_ref[pl.ds(k0, tq), :]
        qc = m_ref.shape[-1]

        def score_fn(c, k_ref, x):
            def fn():
                ka = jnp.concatenate([k_ref[pl.ds(k0, tq), :], feat], axis=1)
                sT = _nt_dot(ka, qa_ref[c, x * qc:(x + 1) * qc, :])
                return sT if mask is None else sT + mask[:, x * qc:(x + 1) * qc]
            return fn

        return [(score_fn(c, k_ref, x), v, c * (tq // qc) + x)
                for c, k_ref in enumerate((k1_ref, k2_ref)) for x in range(tq // qc)]

    def body(p, carry):
        _flash_units(tile_units(2 * p, None) + tile_units(2 * p + 1, None), m_ref, l_ref, acc_ref)
        return carry

    lax.fori_loop(0, lax.shift_right_logical(i, 1), body, 0)

    @pl.when(i % 2 == 1)
    def _():
        _flash_units(tile_units(i - 1, None), m_ref, l_ref, acc_ref)

    _flash_units(tile_units(i, _tri_mask(tq, tq, True)), m_ref, l_ref, acc_ref)

    lam = lam_ref[...]
    lmbda = (jnp.exp(jnp.sum(lam[0:1] * lam[1:2], axis=-1, keepdims=True))
             - jnp.exp(jnp.sum(lam[2:3] * lam[3:4], axis=-1, keepdims=True)) + lambda_init)
    qc = m_ref.shape[-1]
    nx = tq // qc
    for x in range(nx):
        oT = acc_ref[x] / l_ref[x] - lmbda * (acc_ref[nx + x] / l_ref[nx + x])
        o = oT.T
        y = o * lax.rsqrt(jnp.mean(o * o, axis=-1, keepdims=True) + EPS)
        y = (y * sw_ref[...]) * (1.0 - lambda_init)
        o_ref[x * qc:(x + 1) * qc, :] = y.astype(o_ref.dtype)


def _diff_attention(qkv, lam, subln_w, lambda_init, B, S, tq=512):
    T = B * S
    H, dh, dv = A_HEADS, A_DH, A_DV
    n = S // tq
    slopes = _alibi_slopes(H)
    kb = H * 2
    vb = (2 * H * 2 * dh) // dv
    grid_spec = pltpu.PrefetchScalarGridSpec(
        num_scalar_prefetch=1,
        grid=(B, H, n),
        in_specs=[
            pl.BlockSpec((tq, dh), lambda b, h, i, sl: (b * n + i, 2 * h)),
            pl.BlockSpec((tq, dh), lambda b, h, i, sl: (b * n + i, 2 * h + 1)),
            pl.BlockSpec((S, dh), lambda b, h, i, sl: (b, kb + 2 * h)),
            pl.BlockSpec((S, dh), lambda b, h, i, sl: (b, kb + 2 * h + 1)),
            pl.BlockSpec((S, dv), lambda b, h, i, sl: (b, vb + h)),
            pl.BlockSpec((S, 128), lambda b, h, i, sl: (0, 0)),
            pl.BlockSpec((4, dh), lambda b, h, i, sl: (0, 0)),
            pl.BlockSpec((1, dv), lambda b, h, i, sl: (0, 0)),
        ],
        out_specs=pl.BlockSpec((tq, dv), lambda b, h, i, sl: (b * n + i, h)),
        scratch_shapes=[pltpu.VMEM((2, tq, 2 * dh), BF16)] + _flash_state(2, tq, dv),
    )
    return pl.pallas_call(
        functools.partial(_diff_attn_kernel, tq=tq, lambda_init=lambda_init),
        out_shape=jax.ShapeDtypeStruct((T, H * dv), BF16),
        grid_spec=grid_spec,
        compiler_params=_cparams(("parallel", "parallel", "arbitrary")),
        name="diff_attention",
    )(slopes, qkv, qkv, qkv, qkv, qkv, _key_features(S), lam, subln_w.reshape(1, dv))


def _ffn_norm(h_ref, nw_ref):
    x = h_ref[...]
    return (x * lax.rsqrt(jnp.mean(x * x, axis=-1, keepdims=True) + EPS)) * nw_ref[...]


def _router_kernel(h_ref, nw_ref, wr_ref, br_ref, tri_ref, ids_ref, wts_ref, cnt_ref, base_ref):
    @pl.when(pl.program_id(0) == 0)
    def _():
        base_ref[...] = jnp.zeros(base_ref.shape, F32)

    xn = _ffn_norm(h_ref, nw_ref)
    w = wr_ref[...]
    x_hi = xn.astype(BF16)
    x_lo = (xn - x_hi.astype(F32)).astype(BF16)
    w_hi = w.astype(BF16)
    w_lo = (w - w_hi.astype(F32)).astype(BF16)
    logits = (jnp.dot(x_hi, w_hi, preferred_element_type=F32)
              + (jnp.dot(x_hi, w_lo, preferred_element_type=F32)
                 + jnp.dot(x_lo, w_hi, preferred_element_type=F32))) + br_ref[...]
    lane = lax.broadcasted_iota(I32, logits.shape, 1)
    big = jnp.int32(1 << 20)
    is_g = lane < N_GROUPS
    is_e = (lane >= N_GROUPS) & (lane < N_GROUPS + N_EXPERTS)

    lg = jnp.where(is_g, logits, NEG)
    mg = jnp.max(lg, axis=-1, keepdims=True)
    g_top = jnp.min(jnp.where(lg == mg, lane, big), axis=-1, keepdims=True)
    wg = 1.0 / jnp.sum(jnp.where(is_g, jnp.exp(lg - mg), 0.0), axis=-1, keepdims=True)

    in_grp = is_e & (lax.shift_right_logical(lane - N_GROUPS, 3) == g_top)
    le = jnp.where(in_grp, logits, NEG)
    me = jnp.max(le, axis=-1, keepdims=True)
    ee = jnp.where(in_grp, jnp.exp(le - me), 0.0)
    pe = ee / jnp.sum(ee, axis=-1, keepdims=True)
    pe = jnp.where(in_grp, pe, -1.0)
    p1 = jnp.max(pe, axis=-1, keepdims=True)
    i1 = jnp.min(jnp.where(pe == p1, lane, big), axis=-1, keepdims=True)
    pe2 = jnp.where(lane == i1, -1.0, pe)
    p2 = jnp.max(pe2, axis=-1, keepdims=True)
    i2 = jnp.min(jnp.where(pe2 == p2, lane, big), axis=-1, keepdims=True)
    den = p1 + p2
    w0 = wg * p1 / den
    w1 = wg * p2 / den

    oh1 = lane == i1
    oh2 = lane == i2
    both = jnp.where(oh1 | oh2, 1.0, 0.0)
    before = jnp.dot(tri_ref[...], both.astype(BF16), preferred_element_type=F32) + base_ref[...]
    pos0 = jnp.sum(jnp.where(oh1, before, 0.0), axis=-1, keepdims=True).astype(I32)
    pos1 = jnp.sum(jnp.where(oh2, before, 0.0), axis=-1, keepdims=True).astype(I32)
    total = base_ref[...] + jnp.sum(both, axis=0, keepdims=True)
    base_ref[...] = total
    cnt_ref[...] = total

    ids = jnp.where(lane == 0, i1 - N_GROUPS,
                    jnp.where(lane == 1, i2 - N_GROUPS,
                              jnp.where(lane == 2, pos0, jnp.where(lane == 3, pos1, 0))))
    ids_ref[...] = ids.T[:8]
    wts_ref[...] = jnp.where(lane == 0, w0, jnp.where(lane == 1, w1, 0.0))


def _router(h, norm_w, w_group, b_group, w_expert, b_expert, tm=512):
    T, D = h.shape
    pad = ROUTER_LANES - N_GROUPS - N_EXPERTS
    wr = jnp.concatenate([w_group, w_expert, jnp.zeros((D, pad), F32)], axis=1)
    br = jnp.concatenate([b_group, b_expert, jnp.zeros((pad,), F32)]).reshape(1, ROUTER_LANES)
    tri = jnp.asarray(np.tril(np.ones((tm, tm), np.float32), -1), BF16)
    return pl.pallas_call(
        _router_kernel,
        out_shape=(jax.ShapeDtypeStruct((8, T), I32),
                   jax.ShapeDtypeStruct((T, ROUTER_LANES), F32),
                   jax.ShapeDtypeStruct((1, ROUTER_LANES), F32)),
        grid=(T // tm,),
        in_specs=[pl.BlockSpec((tm, D), lambda i: (i, 0)),
                  pl.BlockSpec((1, D), lambda i: (0, 0)),
                  pl.BlockSpec((D, ROUTER_LANES), lambda i: (0, 0)),
                  pl.BlockSpec((1, ROUTER_LANES), lambda i: (0, 0)),
                  pl.BlockSpec((tm, tm), lambda i: (0, 0))],
        out_specs=(pl.BlockSpec((8, tm), lambda i: (0, i)),
                   pl.BlockSpec((tm, ROUTER_LANES), lambda i: (i, 0)),
                   pl.BlockSpec((1, ROUTER_LANES), lambda i: (0, 0))),
        scratch_shapes=[pltpu.VMEM((1, ROUTER_LANES), F32)],
        compiler_params=_cparams(("arbitrary",)),
        name="moe_router",
    )(h, norm_w.reshape(1, D), wr, br, tri)


def _dispatch_kernel(dest_ref, zero_ref, h_ref, nw_ref, xs_ref, xn_ref, zbuf_ref, sem, zsem, *, tm, n_tok):
    xn_ref[...] = _ffn_norm(h_ref, nw_ref)

    @pl.when(pl.program_id(0) == 0)
    def _():
        zbuf_ref[...] = jnp.zeros(zbuf_ref.shape, zbuf_ref.dtype)
        rb = zbuf_ref.shape[0]

        def fill(r):
            return pltpu.make_async_copy(zbuf_ref, xs_ref.at[pl.ds(r * rb, rb)], zsem)

        for r in range(zero_ref.shape[0]):
            @pl.when(zero_ref[r] == 1)
            def _():
                fill(r).start()
        for r in range(zero_ref.shape[0]):
            @pl.when(zero_ref[r] == 1)
            def _():
                fill(r).wait()

    base = pl.program_id(0) * tm
    for t in range(tm):
        for slot in range(2):
            d = dest_ref[slot * n_tok + base + t]
            pltpu.make_async_copy(xn_ref.at[pl.ds(t, 1)], xs_ref.at[pl.ds(d, 1)], sem).start(priority=slot)
    for slot in range(2):
        pltpu.make_async_copy(xn_ref, xs_ref.at[pl.ds(0, tm)], sem).wait()


def _dispatch(h, norm_w, dest_flat, zero_blk, n_rows, tm=256):
    T, D = h.shape
    grid_spec = pltpu.PrefetchScalarGridSpec(
        num_scalar_prefetch=2,
        grid=(T // tm,),
        in_specs=[pl.BlockSpec((tm, D), lambda i, d, z: (i, 0)),
                  pl.BlockSpec((1, D), lambda i, d, z: (0, 0))],
        out_specs=pl.BlockSpec(memory_space=pl.ANY),
        scratch_shapes=[pltpu.VMEM((tm, D), F32), pltpu.VMEM((MOE_ROWS, D), F32),
                        pltpu.SemaphoreType.DMA, pltpu.SemaphoreType.DMA],
    )
    return pl.pallas_call(
        functools.partial(_dispatch_kernel, tm=tm, n_tok=T),
        out_shape=jax.ShapeDtypeStruct((n_rows, D), F32),
        grid_spec=grid_spec,
        compiler_params=_cparams(("arbitrary",)),
        name="moe_dispatch",
    )(dest_flat, zero_blk, h, norm_w.reshape(1, D))


def _moe_plan(counts, n_blk):
    RB = MOE_ROWS
    nb_e = (counts + RB - 1) // RB
    end_b = jnp.cumsum(nb_e)
    start_b = end_b - nb_e
    n_used = end_b[-1:].astype(I32)
    r = jnp.arange(n_blk, dtype=I32)
    be = jnp.minimum(jnp.sum((end_b[None, :] <= r[:, None]).astype(I32), axis=1), N_EXPERTS - 1)
    active = nb_e > 0
    ordinal = jnp.cumsum(active.astype(I32)) - 1
    idx = jnp.arange(N_EXPERTS, dtype=I32)
    later = jnp.where(active[None, :] & (idx[None, :] > idx[:, None]), idx[None, :], N_EXPERTS)
    nxt_e = jnp.min(later, axis=1)
    nxt_e = jnp.where(nxt_e == N_EXPERTS, -1, nxt_e)
    first = ((r == start_b[be]) & (r < n_used[0])).astype(I32)
    plan = jnp.stack([be, first, ordinal[be] % 2, nxt_e[be]]).astype(I32)
    zero_blk = ((r == end_b[be] - 1) | (r >= n_used[0])).astype(I32)
    return plan, n_used, (start_b * RB).astype(I32), zero_blk


def _stream_expert_weights(plan_ref, r, w_hbms, wbuf, sem, layer):
    e, slot, nxt = plan_ref[0, r], plan_ref[2, r], plan_ref[3, r]

    def copies(expert, s):
        rows = wbuf.shape[2] // 2
        return [(pltpu.make_async_copy(w.at[layer, expert, pl.ds(half * rows, rows)],
                                       wbuf.at[s, k, pl.ds(half * rows, rows)], sem.at[s, k, half]), half)
                for k, w in enumerate(w_hbms) for half in range(2)]

    @pl.when(plan_ref[1, r] == 1)
    def _():
        @pl.when(r == 0)
        def _():
            for c, half in copies(e, slot):
                c.start(priority=half)

        for c, _ in copies(e, slot):
            c.wait()

        @pl.when(nxt >= 0)
        def _():
            for c, half in copies(nxt, 1 - slot):
                c.start(priority=half)

    return slot


def _gate_up_kernel(plan_ref, nu_ref, x_ref, wg_hbm, wu_hbm, h_ref, wbuf, sem, *, layer):
    r = pl.program_id(0)

    @pl.when(r < nu_ref[0])
    def _():
        slot = _stream_expert_weights(plan_ref, r, (wg_hbm, wu_hbm), wbuf, sem, layer)
        x = x_ref[...]
        g = jnp.dot(x, wbuf[slot, 0], preferred_element_type=F32)
        u = jnp.dot(x, wbuf[slot, 1], preferred_element_type=F32)
        h_ref[...] = (jax.nn.silu(g) * u).astype(h_ref.dtype)

    @pl.when(r >= nu_ref[0])
    def _():
        h_ref[...] = jnp.zeros(h_ref.shape, h_ref.dtype)


def _down_kernel(plan_ref, nu_ref, h_ref, wd_hbm, y_ref, wbuf, sem, *, layer):
    r = pl.program_id(0)

    @pl.when(r < nu_ref[0])
    def _():
        slot = _stream_expert_weights(plan_ref, r, (wd_hbm,), wbuf, sem, layer)
        y_ref[...] = jnp.dot(h_ref[...].astype(F32), wbuf[slot, 0], preferred_element_type=F32)

    @pl.when(r >= nu_ref[0])
    def _():
        y_ref[...] = jnp.zeros(y_ref.shape, y_ref.dtype)


def _expert_mlp(xs, plan, n_used, w_gate, w_up, w_down, layer):
    n_rows, D = xs.shape
    RB = MOE_ROWS
    n_blk = n_rows // RB
    F = D_EXPERT

    def used_blk(r, plan, nu):
        return (jnp.minimum(r, nu[0] - 1), 0)

    h = pl.pallas_call(
        functools.partial(_gate_up_kernel, layer=layer),
        out_shape=jax.ShapeDtypeStruct((n_rows, F), BF16),
        grid_spec=pltpu.PrefetchScalarGridSpec(
            num_scalar_prefetch=2,
            grid=(n_blk,),
            in_specs=[pl.BlockSpec((RB, D), used_blk),
                      pl.BlockSpec(memory_space=pl.ANY),
                      pl.BlockSpec(memory_space=pl.ANY)],
            out_specs=pl.BlockSpec((RB, F), lambda r, plan, nu: (r, 0)),
            scratch_shapes=[pltpu.VMEM((2, 2, D, F), F32), pltpu.SemaphoreType.DMA((2, 2, 2))],
        ),
        compiler_params=_cparams(("arbitrary",)),
        name="moe_gate_up",
    )(plan, n_used, xs, w_gate, w_up)
    y = pl.pallas_call(
        functools.partial(_down_kernel, layer=layer),
        out_shape=jax.ShapeDtypeStruct((n_rows, D), F32),
        grid_spec=pltpu.PrefetchScalarGridSpec(
            num_scalar_prefetch=2,
            grid=(n_blk,),
            in_specs=[pl.BlockSpec((RB, F), used_blk),
                      pl.BlockSpec(memory_space=pl.ANY)],
            out_specs=pl.BlockSpec((RB, D), lambda r, plan, nu: (r, 0)),
            scratch_shapes=[pltpu.VMEM((2, 1, F, D), F32), pltpu.SemaphoreType.DMA((2, 1, 2))],
        ),
        compiler_params=_cparams(("arbitrary",)),
        name="moe_down",
    )(plan, n_used, h, w_down)
    return y


def _combine_kernel(dest_ref, resid_ref, wts_ref, nw_ref, y_ref, *rest, tm, n_tok, keep_h):
    if keep_h:
        o_ref, n_ref, buf_ref, sem = rest
    else:
        o_ref = None
        n_ref, buf_ref, sem = rest
    i = pl.program_id(0)
    n = pl.num_programs(0)

    def gather(tile, par):
        for t in range(tm):
            for slot in range(2):
                d = dest_ref[slot * n_tok + tile * tm + t]
                pltpu.make_async_copy(y_ref.at[pl.ds(d, 1)], buf_ref.at[par, slot, pl.ds(t, 1)],
                                      sem.at[par]).start(priority=slot)

    @pl.when(i == 0)
    def _():
        gather(i, 0)

    for par in range(2):
        @pl.when((i + 1 < n) & ((i + 1) % 2 == par))
        def _():
            gather(i + 1, par)

    for par in range(2):
        @pl.when(i % 2 == par)
        def _():
            for slot in range(2):
                pltpu.make_async_copy(y_ref.at[pl.ds(0, tm)], buf_ref.at[par, slot], sem.at[par]).wait()
            w = wts_ref[...]
            hn = resid_ref[...] + (w[:, 0:1] * buf_ref[par, 0] + w[:, 1:2] * buf_ref[par, 1])
            if keep_h:
                o_ref[...] = hn
            yn = hn * lax.rsqrt(jnp.mean(hn * hn, axis=-1, keepdims=True) + EPS)
            n_ref[...] = (yn * nw_ref[...]).astype(n_ref.dtype)


def _combine(resid, wts, y, dest_flat, next_norm_w, norm_dtype, keep_h, norm_shape=None, tm=256):
    T, D = resid.shape
    row_spec = pl.BlockSpec((tm, D), lambda i, d: (i, 0))
    if norm_shape is None:
        out_shape = [jax.ShapeDtypeStruct((T, D), norm_dtype)]
        out_specs = [row_spec]
    else:
        per_seq = norm_shape[1] // tm
        out_shape = [jax.ShapeDtypeStruct(norm_shape, norm_dtype)]
        out_specs = [pl.BlockSpec((None, tm, D), lambda i, d: (i // per_seq, i % per_seq, 0))]
    if keep_h:
        out_shape.insert(0, jax.ShapeDtypeStruct((T, D), F32))
        out_specs.insert(0, row_spec)
    grid_spec = pltpu.PrefetchScalarGridSpec(
        num_scalar_prefetch=1,
        grid=(T // tm,),
        in_specs=[row_spec,
                  pl.BlockSpec((tm, ROUTER_LANES), lambda i, d: (i, 0)),
                  pl.BlockSpec((1, D), lambda i, d: (0, 0)),
                  pl.BlockSpec(memory_space=pl.ANY)],
        out_specs=tuple(out_specs),
        scratch_shapes=[pltpu.VMEM((2, 2, tm, D), F32), pltpu.SemaphoreType.DMA((2,))],
    )
    outs = pl.pallas_call(
        functools.partial(_combine_kernel, tm=tm, n_tok=T, keep_h=keep_h),
        out_shape=tuple(out_shape),
        grid_spec=grid_spec,
        compiler_params=_cparams(("arbitrary",)),
        name="moe_combine",
    )(dest_flat, resid, wts, next_norm_w.reshape(1, D), y)
    return (outs[0], outs[1]) if keep_h else (None, outs[0])


def _hier_moe(h, norm_w, w_group, b_group, w_expert, b_expert, w_gate, w_up, w_down, layer,
              next_norm_w, norm_dtype, keep_h, norm_shape=None):
    T, D = h.shape
    RB = MOE_ROWS
    n_blk = (T * 2) // RB + N_EXPERTS
    n_rows = n_blk * RB
    ids, wts, cnt = _router(h, norm_w, w_group, b_group, w_expert, b_expert)
    counts = cnt[0, N_GROUPS:N_GROUPS + N_EXPERTS].astype(I32)
    plan, n_used, start_padded, zero_blk = _moe_plan(counts, n_blk)
    is_e = ids[None, 0:2] == jnp.arange(N_EXPERTS, dtype=I32)[:, None, None]
    dest = (jnp.sum(jnp.where(is_e, start_padded[:, None, None], 0), axis=0) + ids[2:4]).reshape(-1)
    xs = _dispatch(h, norm_w, dest, zero_blk, n_rows)
    y = _expert_mlp(xs, plan, n_used, w_gate, w_up, w_down, layer)
    return _combine(h, wts, y, dest, next_norm_w, norm_dtype, keep_h, norm_shape)


def _compress_kernel(x_ref, w1_ref, w2_ref, pe_ref, o_ref, xf_ref):
    d = B_D
    half = CMP_STRIDE * d
    nch = x_ref.shape[0] // CMP_STRIDE
    xf_ref[...] = x_ref[...].astype(F32)
    w1 = w1_ref[...].astype(BF16)
    a = jnp.zeros((nch, w1.shape[1]), F32)
    b = jnp.zeros((nch, w1.shape[1]), F32)
    for l in range(CMP_STRIDE):
        xl = xf_ref[pl.ds(l, nch, stride=CMP_STRIDE), :].astype(BF16)
        a = a + jnp.dot(xl, w1[l * d:(l + 1) * d], preferred_element_type=F32)
        b = b + jnp.dot(xl, w1[half + l * d:half + (l + 1) * d], preferred_element_type=F32)
    pe = jnp.broadcast_to(pe_ref[...], (8, 2 * half)).astype(BF16)
    pt = jnp.dot(pe, w1, preferred_element_type=F32)[0:1]
    hid = a + pltpu.roll(b, b.shape[0] - 1, 0) + pt
    g = jax.nn.gelu(hid)
    o_ref[...] = jnp.dot(g.astype(BF16), w2_ref[...].astype(BF16),
                         preferred_element_type=F32).astype(o_ref.dtype)


def _compress(projm, w1, w2, pe_flat, B, S):
    G, d = B_GROUPS, B_D
    nch = S // CMP_STRIDE
    width = CMP_STRIDE * d
    hid = w1.shape[-1]
    cb = B_HEADS
    return pl.pallas_call(
        _compress_kernel,
        out_shape=jax.ShapeDtypeStruct((2, B, G, nch, d), BF16),
        grid=(2, B, G),
        in_specs=[pl.BlockSpec((S, d), lambda kv, b, g: (b, cb + kv * G + g)),
                  pl.BlockSpec((None, 2 * width, hid), lambda kv, b, g: (kv, 0, 0)),
                  pl.BlockSpec((None, hid, d), lambda kv, b, g: (kv, 0, 0)),
                  pl.BlockSpec((None, 1, 2 * width), lambda kv, b, g: (kv, 0, 0))],
        out_specs=pl.BlockSpec((None, None, None, nch, d), lambda kv, b, g: (kv, b, g, 0, 0)),
        scratch_shapes=[pltpu.VMEM((S, d), F32)],
        compiler_params=_cparams(("arbitrary", "arbitrary", "arbitrary")),
        name="nsa_compress",
    )(projm, w1, w2, pe_flat)


def _cmp_select_kernel(slopes_ref, q_ref, kc_ref, vc_ref, ov_ref, o_ref, sel_ref, need_ref, *, tq, n_cmp,
                       n_blocks, n_rank):
    g = pl.program_id(1)
    i = pl.program_id(2)
    d = B_D
    nch = kc_ref.shape[0]
    pos = i * tq + lax.broadcasted_iota(I32, (1, tq), 1)
    c = lax.broadcasted_iota(I32, (nch, 1), 0)
    dist_i = pos - (c * CMP_STRIDE + (CMP_LEN - 1))
    valid = (dist_i >= 0) & (c < n_cmp)
    dist = dist_i.astype(F32)
    any_valid = jnp.where(pos >= CMP_LEN - 1, 1.0, 0.0)
    kc = kc_ref[...]
    vc = vc_ref[...]
    psum = jnp.zeros((nch, tq), F32)
    for hh in range(B_HPG):
        s = _nt_dot(kc, q_ref[:, hh * d:(hh + 1) * d])
        s = jnp.where(valid, s - (slopes_ref[g * B_HPG + hh] * LOG2E) * dist, NEG)
        e = jnp.exp2(s - jnp.max(s, axis=0, keepdims=True))
        p = (e / jnp.sum(e, axis=0, keepdims=True)) * any_valid
        oT = lax.dot_general(vc, p.astype(BF16), (((0,), (0,)), ((), ())), preferred_element_type=F32)
        o_ref[:, hh * d:(hh + 1) * d] = oT.T.astype(o_ref.dtype)
        psum = psum + p

    p_hi = psum.astype(BF16)
    p_lo = (psum - p_hi.astype(F32)).astype(BF16)
    ov = ov_ref[...]
    imp = jnp.dot(ov, p_hi, preferred_element_type=F32) + jnp.dot(ov, p_lo, preferred_element_type=F32)
    imp = imp[:n_rank]

    blk = lax.broadcasted_iota(I32, (n_rank, 1), 0)
    cur = lax.shift_right_logical(pos, SLC_BLOCK.bit_length() - 1)
    forced = (blk == 0) | (blk == cur) | (blk == cur - 1)
    causal = blk * SLC_BLOCK <= pos
    score = jnp.where(forced, FORCE, jnp.where(causal, imp, -FORCE))
    score = jnp.where(blk < n_blocks, score, -2.0 * FORCE)
    blk_f = blk.astype(F32)
    remaining = score
    sel_add = jnp.full(score.shape, NEG, F32)
    for _ in range(SLC_TOPN):
        top = jnp.max(remaining, axis=0, keepdims=True)
        first = jnp.min(jnp.where(remaining == top, blk_f, float(n_rank)), axis=0, keepdims=True)
        pick = blk_f == first
        sel_add = jnp.where(pick, 0.0, sel_add)
        remaining = jnp.where(pick, -4.0 * FORCE, remaining)
    pad = jnp.zeros((128 - n_rank, tq), F32)
    sel_t = jnp.concatenate([jnp.where(blk < n_blocks, sel_add, 0.0), pad], axis=0) if n_rank < 128 else sel_add
    sel_ref[...] = sel_t.T.astype(sel_ref.dtype)
    per_tile = SLC_KEY_TILE // SLC_BLOCK
    chosen = jnp.where(sel_add == 0.0, 1, 0)
    for qh in range(tq // SLC_KEY_TILE):
        blk_any = jnp.max(chosen[:, qh * SLC_KEY_TILE:(qh + 1) * SLC_KEY_TILE], axis=1, keepdims=True)
        rows = []
        for j in range(8):
            if (j + 1) * per_tile <= n_rank:
                hit = jnp.max(blk_any[j * per_tile:(j + 1) * per_tile], axis=0, keepdims=True)
                rows.append(jnp.broadcast_to(hit, (1, 128)))
            else:
                rows.append(jnp.zeros((1, 128), I32))
        need_ref[qh] = jnp.concatenate(rows, axis=0)


def _cmp_select(projm, cmp_kv, overlap, B, S, tq=1024):
    T = B * S
    G, d = B_GROUPS, B_D
    nq = S // tq
    nch = cmp_kv.shape[3]
    n_cmp = (S - CMP_LEN) // CMP_STRIDE + 1
    n_blocks = S // SLC_BLOCK
    assert SLC_TOPN <= n_blocks <= FEAT_BLK
    slopes = _alibi_slopes(B_HEADS)
    grid_spec = pltpu.PrefetchScalarGridSpec(
        num_scalar_prefetch=1,
        grid=(B, G, nq),
        in_specs=[pl.BlockSpec((tq, B_HPG * d), lambda b, g, i, sl: (b * nq + i, g)),
                  pl.BlockSpec((None, None, None, nch, d), lambda b, g, i, sl: (0, b, g, 0, 0)),
                  pl.BlockSpec((None, None, None, nch, d), lambda b, g, i, sl: (1, b, g, 0, 0)),
                  pl.BlockSpec((128, nch), lambda b, g, i, sl: (0, 0))],
        out_specs=(pl.BlockSpec((tq, B_HPG * d), lambda b, g, i, sl: (b * nq + i, g)),
                   pl.BlockSpec((None, None, tq, 128), lambda b, g, i, sl: (b, g, i, 0)),
                   pl.BlockSpec((None, None, None, tq // SLC_KEY_TILE, 8, 128),
                                lambda b, g, i, sl: (b, g, i, 0, 0, 0))),
    )
    assert S // SLC_KEY_TILE <= 8 and tq % SLC_KEY_TILE == 0
    return pl.pallas_call(
        functools.partial(_cmp_select_kernel, tq=tq, n_cmp=n_cmp, n_blocks=n_blocks,
                          n_rank=-(-n_blocks // 8) * 8),
        out_shape=(jax.ShapeDtypeStruct((T, B_HEADS * d), BF16),
                   jax.ShapeDtypeStruct((B, G, S, 128), BF16),
                   jax.ShapeDtypeStruct((B, G, nq, tq // SLC_KEY_TILE, 8, 128), I32)),
        grid_spec=grid_spec,
        compiler_params=_cparams(("parallel", "parallel", "arbitrary")),
        name="nsa_cmp_select",
    )(slopes, projm, cmp_kv, cmp_kv, overlap)


def _head_units(k_ref, v_ref, feat_ref, qa_ref, j, mask, qc):
    n_heads, tq, _ = qa_ref.shape
    k0 = pl.multiple_of(j * tq, tq)
    v = v_ref[pl.ds(k0, tq), :]

    def score_fn(hh, c):
        def fn():
            ka = jnp.concatenate([k_ref[pl.ds(k0, tq), :], feat_ref[pl.ds(k0, tq), :]], axis=1)
            sT = _nt_dot(ka, qa_ref[hh, c * qc:(c + 1) * qc, :])
            return sT if mask is None else sT + mask[:, c * qc:(c + 1) * qc]
        return fn

    return [(score_fn(hh, c), v, hh * (tq // qc) + c) for hh in range(n_heads) for c in range(tq // qc)]


def _write_heads_T(o_ref, acc_ref, l_ref):
    tq = o_ref.shape[0]
    qc = l_ref.shape[-1]
    for hh in range(B_HPG):
        for c in range(tq // qc):
            u = hh * (tq // qc) + c
            o_ref[c * qc:(c + 1) * qc, hh * B_D:(hh + 1) * B_D] = (acc_ref[u] / l_ref[u]).T.astype(o_ref.dtype)


def _flash_state(n_heads, tq, dv):
    qc = min(ATTN_QC, tq)
    n_units = n_heads * (tq // qc)
    return [pltpu.VMEM((n_units, 1, qc), F32), pltpu.VMEM((n_units, 1, qc), F32),
            pltpu.VMEM((n_units, dv, qc), F32)]


def _stage_queries(q_ref, qa_ref, slopes_ref, g, q0, sel_add):
    tq = q_ref.shape[0]
    lane = lax.broadcasted_iota(I32, (tq, 128), 1)
    for hh in range(B_HPG):
        coef = jnp.broadcast_to(_alibi_query_features(slopes_ref[g * B_HPG + hh] * LOG2E, q0), (tq, 128))
        if sel_add is not None:
            coef = jnp.where(lane < FEAT_BLK, sel_add, coef)
        qa_ref[hh] = jnp.concatenate([q_ref[:, hh * B_D:(hh + 1) * B_D], coef.astype(BF16)], axis=1)


def _slc_attn_kernel(slopes_ref, need_ref, q_ref, k_ref, v_ref, feat_ref, sel_ref, o_ref, qa_ref, m_ref, l_ref,
                     acc_ref, *, tq):
    b = pl.program_id(0)
    g = pl.program_id(1)
    i = pl.program_id(2)
    n = pl.num_programs(2)
    _flash_init(m_ref, l_ref, acc_ref)
    _stage_queries(q_ref, qa_ref, slopes_ref, g, i * tq, sel_ref[...].astype(F32))

    qc = m_ref.shape[-1]

    def units(j, causal=None):
        return _head_units(k_ref, v_ref, feat_ref, qa_ref, j, causal, qc)

    need_base = ((b * pl.num_programs(1) + g) * n + i) * n

    def needed(j):
        return (j == 0) | (need_ref[need_base + jnp.maximum(j, 0)] > 0)

    def body(p, carry):
        j0, j1 = 2 * p, 2 * p + 1
        n0, n1 = needed(j0), needed(j1)

        @pl.when(n0 & n1)
        def _():
            _flash_units(units(j0) + units(j1), m_ref, l_ref, acc_ref)

        @pl.when(n0 & jnp.logical_not(n1))
        def _():
            _flash_units(units(j0), m_ref, l_ref, acc_ref)

        @pl.when(jnp.logical_not(n0) & n1)
        def _():
            _flash_units(units(j1), m_ref, l_ref, acc_ref)

        return carry

    lax.fori_loop(0, lax.shift_right_logical(i, 1), body, 0)

    @pl.when((i % 2 == 1) & needed(i - 1))
    def _():
        _flash_units(units(i - 1), m_ref, l_ref, acc_ref)

    _flash_units(units(i, _tri_mask(tq, tq, True)), m_ref, l_ref, acc_ref)
    _write_heads_T(o_ref, acc_ref, l_ref)


def _slc_attention(projm, sel, need, B, S):
    T = B * S
    G, d = B_GROUPS, B_D
    tq = SLC_KEY_TILE
    n = S // tq
    slopes = _alibi_slopes(B_HEADS)
    ksb = (B_HEADS * d + 2 * G * d) // d
    vsb = ksb + G
    grid_spec = pltpu.PrefetchScalarGridSpec(
        num_scalar_prefetch=2,
        grid=(B, G, n),
        in_specs=[
            pl.BlockSpec((tq, B_HPG * d), lambda b, g, i, sl, nd: (b * n + i, g)),
            pl.BlockSpec((S, d), lambda b, g, i, sl, nd: (b, ksb + g)),
            pl.BlockSpec((S, d), lambda b, g, i, sl, nd: (b, vsb + g)),
            pl.BlockSpec((S, 128), lambda b, g, i, sl, nd: (0, 0)),
            pl.BlockSpec((None, None, tq, 128), lambda b, g, i, sl, nd: (b, g, i, 0)),
        ],
        out_specs=pl.BlockSpec((tq, B_HPG * d), lambda b, g, i, sl, nd: (b * n + i, g)),
        scratch_shapes=[pltpu.VMEM((B_HPG, tq, 2 * d), BF16)] + _flash_state(B_HPG, tq, d),
    )
    return pl.pallas_call(
        functools.partial(_slc_attn_kernel, tq=tq),
        out_shape=jax.ShapeDtypeStruct((T, B_HEADS * d), BF16),
        grid_spec=grid_spec,
        compiler_params=_cparams(("parallel", "parallel", "arbitrary")),
        name="nsa_selected_attention",
    )(slopes, need.reshape(-1), projm, projm, projm, _key_features(S), sel)


def _win_attn_kernel(slopes_ref, q_ref, k_ref, v_ref, feat_ref, gl_ref, oc_ref, os_ref, o_ref, qa_ref, m_ref,
                     l_ref, acc_ref, *, tq):
    g = pl.program_id(1)
    i = pl.program_id(2)
    _flash_init(m_ref, l_ref, acc_ref)
    _stage_queries(q_ref, qa_ref, slopes_ref, g, i * tq, None)

    qc = m_ref.shape[-1]

    def units(j, mask):
        return _head_units(k_ref, v_ref, feat_ref, qa_ref, j, mask, qc)

    @pl.when(i > 0)
    def _():
        _flash_units(units(i - 1, _tri_mask(tq, tq, False)) + units(i, _tri_mask(tq, tq, True)),
                     m_ref, l_ref, acc_ref)

    @pl.when(i == 0)
    def _():
        _flash_units(units(i, _tri_mask(tq, tq, True)), m_ref, l_ref, acc_ref)

    n_chunks = tq // qc
    for hh in range(B_HPG):
        row = (g * B_HPG + hh) * 3
        for c in range(n_chunks):
            qs = slice(c * qc, (c + 1) * qc)
            cols = slice(hh * B_D, (hh + 1) * B_D)
            u = hh * n_chunks + c
            gates = [jax.nn.sigmoid(gl_ref[pl.ds(row + br, 1), qs]) for br in range(3)]
            mix = (gates[0] * oc_ref[qs, cols].astype(F32).T + gates[1] * os_ref[qs, cols].astype(F32).T
                   + gates[2] * (acc_ref[u] / l_ref[u]))
            o_ref[qs, cols] = mix.T.astype(o_ref.dtype)


def _win_attention(projm, gate_logits_t, o_cmp, o_slc, B, S):
    T = B * S
    G, d = B_GROUPS, B_D
    tq = WINDOW
    n = S // tq
    slopes = _alibi_slopes(B_HEADS)
    kwb = (B_HEADS * d + 4 * G * d) // d
    vwb = kwb + G
    grid_spec = pltpu.PrefetchScalarGridSpec(
        num_scalar_prefetch=1,
        grid=(B, G, n),
        in_specs=[
            pl.BlockSpec((tq, B_HPG * d), lambda b, g, i, sl: (b * n + i, g)),
            pl.BlockSpec((S, d), lambda b, g, i, sl: (b, kwb + g)),
            pl.BlockSpec((S, d), lambda b, g, i, sl: (b, vwb + g)),
            pl.BlockSpec((S, 128), lambda b, g, i, sl: (0, 0)),
            pl.BlockSpec((128, tq), lambda b, g, i, sl: (0, b * n + i)),
            pl.BlockSpec((tq, B_HPG * d), lambda b, g, i, sl: (b * n + i, g)),
            pl.BlockSpec((tq, B_HPG * d), lambda b, g, i, sl: (b * n + i, g)),
        ],
        out_specs=pl.BlockSpec((tq, B_HPG * d), lambda b, g, i, sl: (b * n + i, g)),
        scratch_shapes=[pltpu.VMEM((B_HPG, tq, 2 * d), BF16)] + _flash_state(B_HPG, tq, d),
    )
    return pl.pallas_call(
        functools.partial(_win_attn_kernel, tq=tq),
        out_shape=jax.ShapeDtypeStruct((T, B_HEADS * d), BF16),
        grid_spec=grid_spec,
        compiler_params=_cparams(("parallel", "parallel", "arbitrary")),
        name="nsa_window_attention",
    )(slopes, projm, projm, projm, _key_features(S), gate_logits_t, o_cmp, o_slc)


def _overlap_matrix(S, n_rows):
    nc = (S - CMP_LEN) // CMP_STRIDE + 1
    nsb = S // SLC_BLOCK
    cs = np.arange(nc) * CMP_STRIDE
    ss = np.arange(nsb) * SLC_BLOCK
    ov = np.clip(np.minimum(cs[:, None] + CMP_LEN, ss[None, :] + SLC_BLOCK)
                 - np.maximum(cs[:, None], ss[None, :]), 0, None) / CMP_LEN
    out = np.zeros((128, n_rows), np.float32)
    out[:nsb, :nc] = ov.T
    return jnp.asarray(out, BF16)


def _nsa_attention(xn, w_in, cmp_pos, cmp_w1, cmp_w2, B, S):
    T = B * S
    H, G, d = B_HEADS, B_GROUPS, B_D
    n_main = H * d + 6 * G * d
    col_scale = jnp.concatenate([jnp.full((H * d,), d ** -0.5 * LOG2E, F32), jnp.ones((n_main - H * d,), F32)])
    w_t = jnp.swapaxes(w_in, 0, 1)
    projm = _matmul(xn, w_t, n_main, BF16, col_scale=col_scale, w_is_transposed=True, tm=2048)
    w_gate_t = jnp.pad(w_t[n_main:], ((0, 128 - 3 * H), (0, 0)))
    gate_logits = _matmul(xn, w_gate_t, 128, F32, tn=128, w_is_transposed=True)

    cmp_kv = _compress(projm, cmp_w1, cmp_w2, cmp_pos.reshape(2, 1, CMP_LEN * d), B, S)
    o_cmp, sel, need = _cmp_select(projm, cmp_kv, _overlap_matrix(S, S // CMP_STRIDE), B, S)
    n_t = S // SLC_KEY_TILE
    need = need[:, :, :, :, :n_t, 0].reshape(B, G, n_t, n_t)
    o_slc = _slc_attention(projm, sel, need, B, S)
    return _win_attention(projm, gate_logits.T, o_cmp, o_slc, B, S)


def kernel(x, attn_norm_w, ffn_norm_w, final_norm_w, a_w_in, a_lambda, a_subln_w, a_w_out, b_w_in,
           b_cmp_pos, b_cmp_w1, b_cmp_w2, b_w_out, moe_w_group, moe_b_group, moe_w_expert,
           moe_b_expert, moe_w_gate, moe_w_up, moe_w_down):
    B, S, D = x.shape
    T = B * S
    depth = attn_norm_w.shape[0]
    h = x
    xn = _rmsnorm(h, attn_norm_w[0], BF16)
    for i in range(depth):
        j = i // 2
        if i % 2 == 0:
            lambda_init = 0.8 - 0.6 * math.exp(-0.3 * i)
            n_q = A_HEADS * 2 * A_DH
            n_in = 2 * n_q + A_HEADS * A_DV
            col_scale = jnp.concatenate([jnp.full((n_q,), A_DH ** -0.5 * LOG2E, F32),
                                         jnp.ones((n_in - n_q,), F32)])
            qkv = _matmul(xn, a_w_in[j], n_in, BF16, col_scale=col_scale, tm=2048)
            o = _diff_attention(qkv, a_lambda[j], a_subln_w[j], lambda_init, B, S)
            h = _matmul(o, a_w_out[j], D, F32, resid=h)
        else:
            o = _nsa_attention(xn, b_w_in[j], b_cmp_pos[j], b_cmp_w1[j], b_cmp_w2[j], B, S)
            h = _matmul(o, b_w_out[j], D, F32, resid=h)
        last = i == depth - 1
        h, xn = _hier_moe(h, ffn_norm_w[i], moe_w_group[i], moe_b_group[i], moe_w_expert[i], moe_b_expert[i],
                          moe_w_gate, moe_w_up, moe_w_down, i,
                          final_norm_w if last else attn_norm_w[i + 1], F32 if last else BF16, not last,
                          (B, S, D) if last else None)
    return xn
```

```python
import functools
import math

import numpy as np
import jax
import jax.numpy as jnp
from jax import lax
from jax.experimental import pallas as pl
from jax.experimental.pallas import tpu as pltpu

F32 = jnp.float32
BF16 = jnp.bfloat16
I32 = jnp.int32

EPS = 1e-6
NEG = -1e30
FORCE = 1e4
LOG2E = math.log2(math.e)

A_HEADS = 8
A_DH = 128
A_DV = 256
B_HEADS = 16
B_GROUPS = 4
B_HPG = 4
B_D = 128
CMP_LEN = 32
CMP_STRIDE = 16
SLC_BLOCK = 64
SLC_TOPN = 16
WINDOW = 512
N_GROUPS = 4
EPG = 8
N_EXPERTS = 32
D_EXPERT = 1024
ROUTER_LANES = 128
MOE_ROWS = 256
ATTN_QC = 512
SLC_KEY_TILE = 512

VMEM_LIMIT = 56 * 1024 * 1024


def _cparams(sem, flags=None):
    return pltpu.CompilerParams(dimension_semantics=sem, vmem_limit_bytes=VMEM_LIMIT, flags=flags)


def _alibi_slopes(n):
    return jnp.asarray(np.array([2.0 ** (-8.0 * (i + 1) / n) for i in range(n)], dtype=np.float32))


def _nt_dot(a, b):
    return lax.dot_general(a, b, (((1,), (1,)), ((), ())), preferred_element_type=F32)


def _rms_kernel(x_ref, w_ref, o_ref):
    x = x_ref[...]
    y = x * lax.rsqrt(jnp.mean(x * x, axis=-1, keepdims=True) + EPS)
    o_ref[...] = (y * w_ref[...]).astype(o_ref.dtype)


def _rows_spec(x, tm, tn, col_arg):
    if x.ndim == 2:
        return pl.BlockSpec((tm, tn), lambda *g: (g[-1], col_arg(g)))
    per_seq = x.shape[1] // tm
    return pl.BlockSpec((None, tm, tn), lambda *g: (g[-1] // per_seq, g[-1] % per_seq, col_arg(g)))


def _rmsnorm(x, w, out_dtype, tm=512):
    D = x.shape[-1]
    T = x.size // D
    return pl.pallas_call(
        _rms_kernel,
        out_shape=jax.ShapeDtypeStruct((T, D), out_dtype),
        grid=(T // tm,),
        in_specs=[_rows_spec(x, tm, D, lambda g: 0),
                  pl.BlockSpec((1, D), lambda i: (0, 0))],
        out_specs=pl.BlockSpec((tm, D), lambda i: (i, 0)),
        compiler_params=_cparams(("parallel",)),
        name="rmsnorm",
    )(x, w.reshape(1, D))


def _mm_kernel(*refs, has_scale, has_resid, w_is_transposed):
    a_ref, w_ref = refs[0], refs[1]
    k = 2
    scale_ref = resid_ref = None
    if has_scale:
        scale_ref = refs[k]
        k += 1
    if has_resid:
        resid_ref = refs[k]
        k += 1
    o_ref, wbf_ref = refs[k], refs[k + 1]

    @pl.when(pl.program_id(1) == 0)
    def _():
        wbf_ref[...] = w_ref[...].astype(BF16)

    if w_is_transposed:
        acc = _nt_dot(a_ref[...], wbf_ref[...])
    else:
        acc = jnp.dot(a_ref[...], wbf_ref[...], preferred_element_type=F32)
    if has_scale:
        acc = acc * scale_ref[...]
    if has_resid:
        acc = acc + resid_ref[...]
    o_ref[...] = acc.astype(o_ref.dtype)


def _matmul(a, w, n_out, out_dtype, *, col_scale=None, resid=None, tm=1024, tn=1024, w_is_transposed=False):
    M, K = a.shape
    if w_is_transposed:
        assert M % tm == 0 and n_out % tn == 0 and w.shape[1] == K and w.shape[0] >= n_out
        w_spec, w_block = pl.BlockSpec((tn, K), lambda j, i: (j, 0)), (tn, K)
    else:
        assert M % tm == 0 and n_out % tn == 0 and w.shape[0] == K and w.shape[1] >= n_out
        w_spec, w_block = pl.BlockSpec((K, tn), lambda j, i: (0, j)), (K, tn)
    in_specs = [pl.BlockSpec((tm, K), lambda j, i: (i, 0)), w_spec]
    args = [a, w]
    if col_scale is not None:
        in_specs.append(pl.BlockSpec((1, tn), lambda j, i: (0, j)))
        args.append(col_scale.reshape(1, n_out))
    if resid is not None:
        in_specs.append(_rows_spec(resid, tm, tn, lambda g: g[0]))
        args.append(resid)
    return pl.pallas_call(
        functools.partial(_mm_kernel, has_scale=col_scale is not None, has_resid=resid is not None,
                          w_is_transposed=w_is_transposed),
        out_shape=jax.ShapeDtypeStruct((M, n_out), out_dtype),
        grid=(n_out // tn, M // tm),
        in_specs=in_specs,
        out_specs=pl.BlockSpec((tm, tn), lambda j, i: (i, j)),
        scratch_shapes=[pltpu.VMEM(w_block, BF16)],
        compiler_params=_cparams(("parallel", "arbitrary")),
        name="matmul",
    )(*args)


FEAT_BLK = 64
FEAT_POS = 67
FEAT_ONE = 70


def _key_features(S):
    assert S // SLC_BLOCK <= FEAT_BLK
    j = np.arange(S)
    f = np.zeros((S, 128), np.float32)
    f[j, j // SLC_BLOCK] = 1.0
    f[:, FEAT_BLK:FEAT_BLK + 3] = (j // SLC_BLOCK)[:, None]
    f[:, FEAT_POS:FEAT_POS + 3] = (j % SLC_BLOCK)[:, None]
    f[:, FEAT_ONE:FEAT_ONE + 3] = 1.0
    return jnp.asarray(f, BF16)


def _alibi_query_features(slope2, q0):
    lane = lax.broadcasted_iota(I32, (1, 128), 1)
    base = jnp.where((lane >= FEAT_BLK) & (lane < FEAT_BLK + 3), slope2 * float(SLC_BLOCK),
                     jnp.where((lane >= FEAT_POS) & (lane < FEAT_POS + 3), slope2,
                               jnp.where((lane >= FEAT_ONE) & (lane < FEAT_ONE + 3),
                                         -slope2 * q0.astype(F32), 0.0)))
    hi = base.astype(BF16).astype(F32)
    r1 = base - hi
    lo = r1.astype(BF16).astype(F32)
    lo2 = (r1 - lo).astype(BF16).astype(F32)
    first = (lane == FEAT_BLK) | (lane == FEAT_POS) | (lane == FEAT_ONE)
    second = (lane == FEAT_BLK + 1) | (lane == FEAT_POS + 1) | (lane == FEAT_ONE + 1)
    return jnp.where(first, hi, jnp.where(second, lo, lo2))


def _flash_init(m_ref, l_ref, acc_ref):
    m_ref[...] = jnp.full(m_ref.shape, NEG, F32)
    l_ref[...] = jnp.zeros(l_ref.shape, F32)
    acc_ref[...] = jnp.zeros(acc_ref.shape, F32)


def _flash_probs(sT, m_ref, l_ref, u):
    m_old = m_ref[u]
    m_new = jnp.maximum(m_old, jnp.max(sT, axis=0, keepdims=True))
    alpha = jnp.exp2(m_old - m_new)
    p = jnp.exp2(sT - m_new)
    l_ref[u] = alpha * l_ref[u] + jnp.sum(p, axis=0, keepdims=True)
    m_ref[u] = m_new
    return p.astype(BF16), alpha


def _flash_accumulate(p, alpha, v, acc_ref, u):
    pv = lax.dot_general(v, p, (((0,), (0,)), ((), ())), preferred_element_type=F32)
    acc_ref[u] = alpha * acc_ref[u] + pv


def _flash_units(units, m_ref, l_ref, acc_ref):
    n = len(units)
    scores, probs = {}, {}
    for s in range(n + 3):
        if s < n:
            scores[s] = units[s][0]()
        if 2 <= s <= n + 1:
            probs[s - 2] = _flash_probs(scores.pop(s - 2), m_ref, l_ref, units[s - 2][2])
        if 3 <= s <= n + 2:
            p, alpha = probs.pop(s - 3)
            _flash_accumulate(p, alpha, units[s - 3][1], acc_ref, units[s - 3][2])


def _tri_mask(tk, tq, keep_upper):
    r = lax.broadcasted_iota(I32, (tk, tq), 0)
    c = lax.broadcasted_iota(I32, (tk, tq), 1)
    keep = (r <= c) if keep_upper else (r > c)
    return jnp.where(keep, 0.0, NEG)


def _diff_attn_kernel(slopes_ref, q1_ref, q2_ref, k1_ref, k2_ref, v_ref, feat_ref, lam_ref, sw_ref, o_ref,
                      qa_ref, m_ref, l_ref, acc_ref, *, tq, lambda_init):
    h = pl.program_id(1)
    i = pl.program_id(2)
    _flash_init(m_ref, l_ref, acc_ref)
    qfeat = jnp.broadcast_to(_alibi_query_features(slopes_ref[h] * LOG2E, i * tq), (tq, 128)).astype(BF16)
    for c, q_ref in enumerate((q1_ref, q2_ref)):
        qa_ref[c] = jnp.concatenate([q_ref[...], qfeat], axis=1)

    def tile_units(j, mask):
        k0 = pl.multiple_of(j * tq, tq)
        v = v_ref[pl.ds(k0, tq), :]
        feat = feat_ref[pl.ds(k0, tq), :]
        qc = m_ref.shape[-1]

        def score_fn(c, k_ref, x):
            def fn():
                ka = jnp.concatenate([k_ref[pl.ds(k0, tq), :], feat], axis=1)
                sT = _nt_dot(ka, qa_ref[c, x * qc:(x + 1) * qc, :])
                return sT if mask is None else sT + mask[:, x * qc:(x + 1) * qc]
            return fn

        return [(score_fn(c, k_ref, x), v, c * (tq // qc) + x)
                for c, k_ref in enumerate((k1_ref, k2_ref)) for x in range(tq // qc)]

    def body(p, carry):
        _flash_units(tile_units(2 * p, None) + tile_units(2 * p + 1, None), m_ref, l_ref, acc_ref)
        return carry

    lax.fori_loop(0, lax.shift_right_logical(i, 1), body, 0)

    @pl.when(i % 2 == 1)
    def _():
        _flash_units(tile_units(i - 1, None) + tile_units(i, _tri_mask(tq, tq, True)), m_ref, l_ref, acc_ref)

    @pl.when(i % 2 == 0)
    def _():
        _flash_units(tile_units(i, _tri_mask(tq, tq, True)), m_ref, l_ref, acc_ref)

    lam = lam_ref[...]
    lmbda = (jnp.exp(jnp.sum(lam[0:1] * lam[1:2], axis=-1, keepdims=True))
             - jnp.exp(jnp.sum(lam[2:3] * lam[3:4], axis=-1, keepdims=True)) + lambda_init)
    qc = m_ref.shape[-1]
    nx = tq // qc
    for x in range(nx):
        oT = acc_ref[x] / l_ref[x] - lmbda * (acc_ref[nx + x] / l_ref[nx + x])
        o = oT.T
        y = o * lax.rsqrt(jnp.mean(o * o, axis=-1, keepdims=True) + EPS)
        y = (y * sw_ref[...]) * (1.0 - lambda_init)
        o_ref[x * qc:(x + 1) * qc, :] = y.astype(o_ref.dtype)


def _diff_attention(qkv, lam, subln_w, lambda_init, B, S, tq=512):
    T = B * S
    H, dh, dv = A_HEADS, A_DH, A_DV
    n = S // tq
    slopes = _alibi_slopes(H)
    kb = H * 2
    vb = (2 * H * 2 * dh) // dv
    grid_spec = pltpu.PrefetchScalarGridSpec(
        num_scalar_prefetch=1,
        grid=(B, H, n),
        in_specs=[
            pl.BlockSpec((tq, dh), lambda b, h, i, sl: (b * n + i, 2 * h)),
            pl.BlockSpec((tq, dh), lambda b, h, i, sl: (b * n + i, 2 * h + 1)),
            pl.BlockSpec((S, dh), lambda b, h, i, sl: (b, kb + 2 * h)),
            pl.BlockSpec((S, dh), lambda b, h, i, sl: (b, kb + 2 * h + 1)),
            pl.BlockSpec((S, dv), lambda b, h, i, sl: (b, vb + h)),
            pl.BlockSpec((S, 128), lambda b, h, i, sl: (0, 0)),
            pl.BlockSpec((4, dh), lambda b, h, i, sl: (0, 0)),
            pl.BlockSpec((1, dv), lambda b, h, i, sl: (0, 0)),
        ],
        out_specs=pl.BlockSpec((tq, dv), lambda b, h, i, sl: (b * n + i, h)),
        scratch_shapes=[pltpu.VMEM((2, tq, 2 * dh), BF16)] + _flash_state(2, tq, dv),
    )
    return pl.pallas_call(
        functools.partial(_diff_attn_kernel, tq=tq, lambda_init=lambda_init),
        out_shape=jax.ShapeDtypeStruct((T, H * dv), BF16),
        grid_spec=grid_spec,
        compiler_params=_cparams(("parallel", "parallel", "arbitrary")),
        name="diff_attention",
    )(slopes, qkv, qkv, qkv, qkv, qkv, _key_features(S), lam, subln_w.reshape(1, dv))


def _ffn_norm(h_ref, nw_ref):
    x = h_ref[...]
    return (x * lax.rsqrt(jnp.mean(x * x, axis=-1, keepdims=True) + EPS)) * nw_ref[...]


def _router_kernel(h_ref, nw_ref, wr_ref, br_ref, tri_ref, ids_ref, wts_ref, cnt_ref, base_ref):
    @pl.when(pl.program_id(0) == 0)
    def _():
        base_ref[...] = jnp.zeros(base_ref.shape, F32)

    xn = _ffn_norm(h_ref, nw_ref)
    w = wr_ref[...]
    x_hi = xn.astype(BF16)
    x_lo = (xn - x_hi.astype(F32)).astype(BF16)
    w_hi = w.astype(BF16)
    w_lo = (w - w_hi.astype(F32)).astype(BF16)
    logits = (jnp.dot(x_hi, w_hi, preferred_element_type=F32)
              + (jnp.dot(x_hi, w_lo, preferred_element_type=F32)
                 + jnp.dot(x_lo, w_hi, preferred_element_type=F32))) + br_ref[...]
    lane = lax.broadcasted_iota(I32, logits.shape, 1)
    big = jnp.int32(1 << 20)
    is_g = lane < N_GROUPS
    is_e = (lane >= N_GROUPS) & (lane < N_GROUPS + N_EXPERTS)

    lg = jnp.where(is_g, logits, NEG)
    mg = jnp.max(lg, axis=-1, keepdims=True)
    g_top = jnp.min(jnp.where(lg == mg, lane, big), axis=-1, keepdims=True)
    wg = 1.0 / jnp.sum(jnp.where(is_g, jnp.exp(lg - mg), 0.0), axis=-1, keepdims=True)

    in_grp = is_e & (lax.shift_right_logical(lane - N_GROUPS, 3) == g_top)
    le = jnp.where(in_grp, logits, NEG)
    me = jnp.max(le, axis=-1, keepdims=True)
    ee = jnp.where(in_grp, jnp.exp(le - me), 0.0)
    pe = ee / jnp.sum(ee, axis=-1, keepdims=True)
    pe = jnp.where(in_grp, pe, -1.0)
    p1 = jnp.max(pe, axis=-1, keepdims=True)
    i1 = jnp.min(jnp.where(pe == p1, lane, big), axis=-1, keepdims=True)
    pe2 = jnp.where(lane == i1, -1.0, pe)
    p2 = jnp.max(pe2, axis=-1, keepdims=True)
    i2 = jnp.min(jnp.where(pe2 == p2, lane, big), axis=-1, keepdims=True)
    den = p1 + p2
    w0 = wg * p1 / den
    w1 = wg * p2 / den

    oh1 = lane == i1
    oh2 = lane == i2
    both = jnp.where(oh1 | oh2, 1.0, 0.0)
    before = jnp.dot(tri_ref[...], both.astype(BF16), preferred_element_type=F32) + base_ref[...]
    pos0 = jnp.sum(jnp.where(oh1, before, 0.0), axis=-1, keepdims=True).astype(I32)
    pos1 = jnp.sum(jnp.where(oh2, before, 0.0), axis=-1, keepdims=True).astype(I32)
    total = base_ref[...] + jnp.sum(both, axis=0, keepdims=True)
    base_ref[...] = total
    cnt_ref[...] = total

    ids = jnp.where(lane == 0, i1 - N_GROUPS,
                    jnp.where(lane == 1, i2 - N_GROUPS,
                              jnp.where(lane == 2, pos0, jnp.where(lane == 3, pos1, 0))))
    ids_ref[...] = ids.T[:8]
    wts_ref[...] = jnp.where(lane == 0, w0, jnp.where(lane == 1, w1, 0.0))


def _router(h, norm_w, w_group, b_group, w_expert, b_expert, tm=512):
    T, D = h.shape
    pad = ROUTER_LANES - N_GROUPS - N_EXPERTS
    wr = jnp.concatenate([w_group, w_expert, jnp.zeros((D, pad), F32)], axis=1)
    br = jnp.concatenate([b_group, b_expert, jnp.zeros((pad,), F32)]).reshape(1, ROUTER_LANES)
    tri = jnp.asarray(np.tril(np.ones((tm, tm), np.float32), -1), BF16)
    return pl.pallas_call(
        _router_kernel,
        out_shape=(jax.ShapeDtypeStruct((8, T), I32),
                   jax.ShapeDtypeStruct((T, ROUTER_LANES), F32),
                   jax.ShapeDtypeStruct((1, ROUTER_LANES), F32)),
        grid=(T // tm,),
        in_specs=[pl.BlockSpec((tm, D), lambda i: (i, 0)),
                  pl.BlockSpec((1, D), lambda i: (0, 0)),
                  pl.BlockSpec((D, ROUTER_LANES), lambda i: (0, 0)),
                  pl.BlockSpec((1, ROUTER_LANES), lambda i: (0, 0)),
                  pl.BlockSpec((tm, tm), lambda i: (0, 0))],
        out_specs=(pl.BlockSpec((8, tm), lambda i: (0, i)),
                   pl.BlockSpec((tm, ROUTER_LANES), lambda i: (i, 0)),
                   pl.BlockSpec((1, ROUTER_LANES), lambda i: (0, 0))),
        scratch_shapes=[pltpu.VMEM((1, ROUTER_LANES), F32)],
        compiler_params=_cparams(("arbitrary",)),
        name="moe_router",
    )(h, norm_w.reshape(1, D), wr, br, tri)


def _dispatch_kernel(dest_ref, zero_ref, h_ref, nw_ref, xs_ref, xn_ref, zbuf_ref, sem, zsem, *, tm, n_tok):
    xn_ref[...] = _ffn_norm(h_ref, nw_ref)

    @pl.when(pl.program_id(0) == 0)
    def _():
        zbuf_ref[...] = jnp.zeros(zbuf_ref.shape, zbuf_ref.dtype)
        rb = zbuf_ref.shape[0]

        def fill(r):
            return pltpu.make_async_copy(zbuf_ref, xs_ref.at[pl.ds(r * rb, rb)], zsem)

        for r in range(zero_ref.shape[0]):
            @pl.when(zero_ref[r] == 1)
            def _():
                fill(r).start()
        for r in range(zero_ref.shape[0]):
            @pl.when(zero_ref[r] == 1)
            def _():
                fill(r).wait()

    base = pl.program_id(0) * tm
    for t in range(tm):
        for slot in range(2):
            d = dest_ref[slot * n_tok + base + t]
            pltpu.make_async_copy(xn_ref.at[pl.ds(t, 1)], xs_ref.at[pl.ds(d, 1)], sem).start(priority=slot)
    for slot in range(2):
        pltpu.make_async_copy(xn_ref, xs_ref.at[pl.ds(0, tm)], sem).wait()


def _dispatch(h, norm_w, dest_flat, zero_blk, n_rows, tm=256):
    T, D = h.shape
    grid_spec = pltpu.PrefetchScalarGridSpec(
        num_scalar_prefetch=2,
        grid=(T // tm,),
        in_specs=[pl.BlockSpec((tm, D), lambda i, d, z: (i, 0)),
                  pl.BlockSpec((1, D), lambda i, d, z: (0, 0))],
        out_specs=pl.BlockSpec(memory_space=pl.ANY),
        scratch_shapes=[pltpu.VMEM((tm, D), F32), pltpu.VMEM((MOE_ROWS, D), F32),
                        pltpu.SemaphoreType.DMA, pltpu.SemaphoreType.DMA],
    )
    return pl.pallas_call(
        functools.partial(_dispatch_kernel, tm=tm, n_tok=T),
        out_shape=jax.ShapeDtypeStruct((n_rows, D), F32),
        grid_spec=grid_spec,
        compiler_params=_cparams(("arbitrary",)),
        name="moe_dispatch",
    )(dest_flat, zero_blk, h, norm_w.reshape(1, D))


def _moe_plan(counts, n_blk):
    RB = MOE_ROWS
    nb_e = (counts + RB - 1) // RB
    end_b = jnp.cumsum(nb_e)
    start_b = end_b - nb_e
    n_used = end_b[-1:].astype(I32)
    r = jnp.arange(n_blk, dtype=I32)
    be = jnp.minimum(jnp.sum((end_b[None, :] <= r[:, None]).astype(I32), axis=1), N_EXPERTS - 1)
    active = nb_e > 0
    ordinal = jnp.cumsum(active.astype(I32)) - 1
    idx = jnp.arange(N_EXPERTS, dtype=I32)
    later = jnp.where(active[None, :] & (idx[None, :] > idx[:, None]), idx[None, :], N_EXPERTS)
    nxt_e = jnp.min(later, axis=1)
    nxt_e = jnp.where(nxt_e == N_EXPERTS, -1, nxt_e)
    first = ((r == start_b[be]) & (r < n_used[0])).astype(I32)
    plan = jnp.stack([be, first, ordinal[be] % 2, nxt_e[be]]).astype(I32)
    zero_blk = ((r == end_b[be] - 1) | (r >= n_used[0])).astype(I32)
    return plan, n_used, (start_b * RB).astype(I32), zero_blk


def _stream_expert_weights(plan_ref, r, w_hbms, wbuf, sem, layer):
    e, slot, nxt = plan_ref[0, r], plan_ref[2, r], plan_ref[3, r]

    def copies(expert, s):
        rows = wbuf.shape[2] // 2
        return [(pltpu.make_async_copy(w.at[layer, expert, pl.ds(half * rows, rows)],
                                       wbuf.at[s, k, pl.ds(half * rows, rows)], sem.at[s, k, half]), half)
                for k, w in enumerate(w_hbms) for half in range(2)]

    @pl.when(plan_ref[1, r] == 1)
    def _():
        @pl.when(r == 0)
        def _():
            for c, half in copies(e, slot):
                c.start(priority=half)

        for c, _ in copies(e, slot):
            c.wait()

        @pl.when(nxt >= 0)
        def _():
            for c, half in copies(nxt, 1 - slot):
                c.start(priority=half)

    return slot


def _gate_up_kernel(plan_ref, nu_ref, x_ref, wg_hbm, wu_hbm, h_ref, wbuf, sem, *, layer):
    r = pl.program_id(0)

    @pl.when(r < nu_ref[0])
    def _():
        slot = _stream_expert_weights(plan_ref, r, (wg_hbm, wu_hbm), wbuf, sem, layer)
        x = x_ref[...]
        g = jnp.dot(x, wbuf[slot, 0], preferred_element_type=F32)
        u = jnp.dot(x, wbuf[slot, 1], preferred_element_type=F32)
        h_ref[...] = (jax.nn.silu(g) * u).astype(h_ref.dtype)

    @pl.when(r >= nu_ref[0])
    def _():
        h_ref[...] = jnp.zeros(h_ref.shape, h_ref.dtype)


def _down_kernel(plan_ref, nu_ref, h_ref, wd_hbm, y_ref, wbuf, sem, *, layer):
    r = pl.program_id(0)

    @pl.when(r < nu_ref[0])
    def _():
        slot = _stream_expert_weights(plan_ref, r, (wd_hbm,), wbuf, sem, layer)
        y_ref[...] = jnp.dot(h_ref[...].astype(F32), wbuf[slot, 0], preferred_element_type=F32)

    @pl.when(r >= nu_ref[0])
    def _():
        y_ref[...] = jnp.zeros(y_ref.shape, y_ref.dtype)


def _expert_mlp(xs, plan, n_used, w_gate, w_up, w_down, layer):
    n_rows, D = xs.shape
    RB = MOE_ROWS
    n_blk = n_rows // RB
    F = D_EXPERT

    def used_blk(r, plan, nu):
        return (jnp.minimum(r, nu[0] - 1), 0)

    h = pl.pallas_call(
        functools.partial(_gate_up_kernel, layer=layer),
        out_shape=jax.ShapeDtypeStruct((n_rows, F), BF16),
        grid_spec=pltpu.PrefetchScalarGridSpec(
            num_scalar_prefetch=2,
            grid=(n_blk,),
            in_specs=[pl.BlockSpec((RB, D), used_blk),
                      pl.BlockSpec(memory_space=pl.ANY),
                      pl.BlockSpec(memory_space=pl.ANY)],
            out_specs=pl.BlockSpec((RB, F), lambda r, plan, nu: (r, 0)),
            scratch_shapes=[pltpu.VMEM((2, 2, D, F), F32), pltpu.SemaphoreType.DMA((2, 2, 2))],
        ),
        compiler_params=_cparams(("arbitrary",)),
        name="moe_gate_up",
    )(plan, n_used, xs, w_gate, w_up)
    y = pl.pallas_call(
        functools.partial(_down_kernel, layer=layer),
        out_shape=jax.ShapeDtypeStruct((n_rows, D), F32),
        grid_spec=pltpu.PrefetchScalarGridSpec(
            num_scalar_prefetch=2,
            grid=(n_blk,),
            in_specs=[pl.BlockSpec((RB, F), used_blk),
                      pl.BlockSpec(memory_space=pl.ANY)],
            out_specs=pl.BlockSpec((RB, D), lambda r, plan, nu: (r, 0)),
            scratch_shapes=[pltpu.VMEM((2, 1, F, D), F32), pltpu.SemaphoreType.DMA((2, 1, 2))],
        ),
        compiler_params=_cparams(("arbitrary",)),
        name="moe_down",
    )(plan, n_used, h, w_down)
    return y


def _combine_kernel(dest_ref, resid_ref, wts_ref, nw_ref, y_ref, *rest, tm, n_tok, keep_h):
    if keep_h:
        o_ref, n_ref, buf_ref, sem = rest
    else:
        o_ref = None
        n_ref, buf_ref, sem = rest
    i = pl.program_id(0)
    n = pl.num_programs(0)

    def gather(tile, par):
        for t in range(tm):
            for slot in range(2):
                d = dest_ref[slot * n_tok + tile * tm + t]
                pltpu.make_async_copy(y_ref.at[pl.ds(d, 1)], buf_ref.at[par, slot, pl.ds(t, 1)],
                                      sem.at[par]).start(priority=slot)

    @pl.when(i == 0)
    def _():
        gather(i, 0)

    for par in range(2):
        @pl.when((i + 1 < n) & ((i + 1) % 2 == par))
        def _():
            gather(i + 1, par)

    for par in range(2):
        @pl.when(i % 2 == par)
        def _():
            for slot in range(2):
                pltpu.make_async_copy(y_ref.at[pl.ds(0, tm)], buf_ref.at[par, slot], sem.at[par]).wait()
            w = wts_ref[...]
            hn = resid_ref[...] + (w[:, 0:1] * buf_ref[par, 0] + w[:, 1:2] * buf_ref[par, 1])
            if keep_h:
                o_ref[...] = hn
            yn = hn * lax.rsqrt(jnp.mean(hn * hn, axis=-1, keepdims=True) + EPS)
            n_ref[...] = (yn * nw_ref[...]).astype(n_ref.dtype)


def _combine(resid, wts, y, dest_flat, next_norm_w, norm_dtype, keep_h, norm_shape=None, tm=256):
    T, D = resid.shape
    row_spec = pl.BlockSpec((tm, D), lambda i, d: (i, 0))
    if norm_shape is None:
        out_shape = [jax.ShapeDtypeStruct((T, D), norm_dtype)]
        out_specs = [row_spec]
    else:
        per_seq = norm_shape[1] // tm
        out_shape = [jax.ShapeDtypeStruct(norm_shape, norm_dtype)]
        out_specs = [pl.BlockSpec((None, tm, D), lambda i, d: (i // per_seq, i % per_seq, 0))]
    if keep_h:
        out_shape.insert(0, jax.ShapeDtypeStruct((T, D), F32))
        out_specs.insert(0, row_spec)
    grid_spec = pltpu.PrefetchScalarGridSpec(
        num_scalar_prefetch=1,
        grid=(T // tm,),
        in_specs=[row_spec,
                  pl.BlockSpec((tm, ROUTER_LANES), lambda i, d: (i, 0)),
                  pl.BlockSpec((1, D), lambda i, d: (0, 0)),
                  pl.BlockSpec(memory_space=pl.ANY)],
        out_specs=tuple(out_specs),
        scratch_shapes=[pltpu.VMEM((2, 2, tm, D), F32), pltpu.SemaphoreType.DMA((2,))],
    )
    outs = pl.pallas_call(
        functools.partial(_combine_kernel, tm=tm, n_tok=T, keep_h=keep_h),
        out_shape=tuple(out_shape),
        grid_spec=grid_spec,
        compiler_params=_cparams(("arbitrary",)),
        name="moe_combine",
    )(dest_flat, resid, wts, next_norm_w.reshape(1, D), y)
    return (outs[0], outs[1]) if keep_h else (None, outs[0])


def _hier_moe(h, norm_w, w_group, b_group, w_expert, b_expert, w_gate, w_up, w_down, layer,
              next_norm_w, norm_dtype, keep_h, norm_shape=None):
    T, D = h.shape
    RB = MOE_ROWS
    n_blk = (T * 2) // RB + N_EXPERTS
    n_rows = n_blk * RB
    ids, wts, cnt = _router(h, norm_w, w_group, b_group, w_expert, b_expert)
    counts = cnt[0, N_GROUPS:N_GROUPS + N_EXPERTS].astype(I32)
    plan, n_used, start_padded, zero_blk = _moe_plan(counts, n_blk)
    is_e = ids[None, 0:2] == jnp.arange(N_EXPERTS, dtype=I32)[:, None, None]
    dest = (jnp.sum(jnp.where(is_e, start_padded[:, None, None], 0), axis=0) + ids[2:4]).reshape(-1)
    xs = _dispatch(h, norm_w, dest, zero_blk, n_rows)
    y = _expert_mlp(xs, plan, n_used, w_gate, w_up, w_down, layer)
    return _combine(h, wts, y, dest, next_norm_w, norm_dtype, keep_h, norm_shape)


def _compress_kernel(x_ref, w1_ref, w2_ref, pe_ref, o_ref, xf_ref):
    d = B_D
    half = CMP_STRIDE * d
    nch = x_ref.shape[0] // CMP_STRIDE
    xf_ref[...] = x_ref[...].astype(F32)
    w1 = w1_ref[...].astype(BF16)
    a = jnp.zeros((nch, w1.shape[1]), F32)
    b = jnp.zeros((nch, w1.shape[1]), F32)
    for l in range(CMP_STRIDE):
        xl = xf_ref[pl.ds(l, nch, stride=CMP_STRIDE), :].astype(BF16)
        a = a + jnp.dot(xl, w1[l * d:(l + 1) * d], preferred_element_type=F32)
        b = b + jnp.dot(xl, w1[half + l * d:half + (l + 1) * d], preferred_element_type=F32)
    pe = jnp.broadcast_to(pe_ref[...], (8, 2 * half)).astype(BF16)
    pt = jnp.dot(pe, w1, preferred_element_type=F32)[0:1]
    hid = a + pltpu.roll(b, b.shape[0] - 1, 0) + pt
    g = jax.nn.gelu(hid)
    o_ref[...] = jnp.dot(g.astype(BF16), w2_ref[...].astype(BF16),
                         preferred_element_type=F32).astype(o_ref.dtype)


def _compress(projm, w1, w2, pe_flat, B, S):
    G, d = B_GROUPS, B_D
    nch = S // CMP_STRIDE
    width = CMP_STRIDE * d
    hid = w1.shape[-1]
    cb = B_HEADS
    return pl.pallas_call(
        _compress_kernel,
        out_shape=jax.ShapeDtypeStruct((2, B, G, nch, d), BF16),
        grid=(2, B, G),
        in_specs=[pl.BlockSpec((S, d), lambda kv, b, g: (b, cb + kv * G + g)),
                  pl.BlockSpec((None, 2 * width, hid), lambda kv, b, g: (kv, 0, 0)),
                  pl.BlockSpec((None, hid, d), lambda kv, b, g: (kv, 0, 0)),
                  pl.BlockSpec((None, 1, 2 * width), lambda kv, b, g: (kv, 0, 0))],
        out_specs=pl.BlockSpec((None, None, None, nch, d), lambda kv, b, g: (kv, b, g, 0, 0)),
        scratch_shapes=[pltpu.VMEM((S, d), F32)],
        compiler_params=_cparams(("arbitrary", "arbitrary", "arbitrary")),
        name="nsa_compress",
    )(projm, w1, w2, pe_flat)


def _cmp_select_kernel(slopes_ref, q_ref, kc_ref, vc_ref, ov_ref, o_ref, sel_ref, need_ref, *, tq, n_cmp,
                       n_blocks, n_rank):
    g = pl.program_id(1)
    i = pl.program_id(2)
    d = B_D
    nch = kc_ref.shape[0]
    pos = i * tq + lax.broadcasted_iota(I32, (1, tq), 1)
    c = lax.broadcasted_iota(I32, (nch, 1), 0)
    dist_i = pos - (c * CMP_STRIDE + (CMP_LEN - 1))
    valid = (dist_i >= 0) & (c < n_cmp)
    dist = dist_i.astype(F32)
    any_valid = jnp.where(pos >= CMP_LEN - 1, 1.0, 0.0)
    kc = kc_ref[...]
    vc = vc_ref[...]
    psum = jnp.zeros((nch, tq), F32)
    for hh in range(B_HPG):
        s = _nt_dot(kc, q_ref[:, hh * d:(hh + 1) * d])
        s = jnp.where(valid, s - (slopes_ref[g * B_HPG + hh] * LOG2E) * dist, NEG)
        e = jnp.exp2(s - jnp.max(s, axis=0, keepdims=True))
        p = (e / jnp.sum(e, axis=0, keepdims=True)) * any_valid
        oT = lax.dot_general(vc, p.astype(BF16), (((0,), (0,)), ((), ())), preferred_element_type=F32)
        o_ref[:, hh * d:(hh + 1) * d] = oT.T.astype(o_ref.dtype)
        psum = psum + p

    p_hi = psum.astype(BF16)
    p_lo = (psum - p_hi.astype(F32)).astype(BF16)
    ov = ov_ref[...]
    imp = jnp.dot(ov, p_hi, preferred_element_type=F32) + jnp.dot(ov, p_lo, preferred_element_type=F32)
    imp = imp[:n_rank]

    blk = lax.broadcasted_iota(I32, (n_rank, 1), 0)
    cur = lax.shift_right_logical(pos, SLC_BLOCK.bit_length() - 1)
    forced = (blk == 0) | (blk == cur) | (blk == cur - 1)
    causal = blk * SLC_BLOCK <= pos
    score = jnp.where(forced, FORCE, jnp.where(causal, imp, -FORCE))
    score = jnp.where(blk < n_blocks, score, -2.0 * FORCE)
    blk_f = blk.astype(F32)
    remaining = score
    sel_add = jnp.full(score.shape, NEG, F32)
    for _ in range(SLC_TOPN):
        top = jnp.max(remaining, axis=0, keepdims=True)
        first = jnp.min(jnp.where(remaining == top, blk_f, float(n_rank)), axis=0, keepdims=True)
        pick = blk_f == first
        sel_add = jnp.where(pick, 0.0, sel_add)
        remaining = jnp.where(pick, -4.0 * FORCE, remaining)
    pad = jnp.zeros((128 - n_rank, tq), F32)
    sel_t = jnp.concatenate([jnp.where(blk < n_blocks, sel_add, 0.0), pad], axis=0) if n_rank < 128 else sel_add
    sel_ref[...] = sel_t.T.astype(sel_ref.dtype)
    per_tile = SLC_KEY_TILE // SLC_BLOCK
    chosen = jnp.where(sel_add == 0.0, 1, 0)
    for qh in range(tq // SLC_KEY_TILE):
        blk_any = jnp.max(chosen[:, qh * SLC_KEY_TILE:(qh + 1) * SLC_KEY_TILE], axis=1, keepdims=True)
        rows = []
        for j in range(8):
            if (j + 1) * per_tile <= n_rank:
                hit = jnp.max(blk_any[j * per_tile:(j + 1) * per_tile], axis=0, keepdims=True)
                rows.append(jnp.broadcast_to(hit, (1, 128)))
            else:
                rows.append(jnp.zeros((1, 128), I32))
        need_ref[qh] = jnp.concatenate(rows, axis=0)


def _cmp_select(projm, cmp_kv, overlap, B, S, tq=1024):
    T = B * S
    G, d = B_GROUPS, B_D
    nq = S // tq
    nch = cmp_kv.shape[3]
    n_cmp = (S - CMP_LEN) // CMP_STRIDE + 1
    n_blocks = S // SLC_BLOCK
    assert SLC_TOPN <= n_blocks <= FEAT_BLK
    slopes = _alibi_slopes(B_HEADS)
    grid_spec = pltpu.PrefetchScalarGridSpec(
        num_scalar_prefetch=1,
        grid=(B, G, nq),
        in_specs=[pl.BlockSpec((tq, B_HPG * d), lambda b, g, i, sl: (b * nq + i, g)),
                  pl.BlockSpec((None, None, None, nch, d), lambda b, g, i, sl: (0, b, g, 0, 0)),
                  pl.BlockSpec((None, None, None, nch, d), lambda b, g, i, sl: (1, b, g, 0, 0)),
                  pl.BlockSpec((128, nch), lambda b, g, i, sl: (0, 0))],
        out_specs=(pl.BlockSpec((tq, B_HPG * d), lambda b, g, i, sl: (b * nq + i, g)),
                   pl.BlockSpec((None, None, tq, 128), lambda b, g, i, sl: (b, g, i, 0)),
                   pl.BlockSpec((None, None, None, tq // SLC_KEY_TILE, 8, 128),
                                lambda b, g, i, sl: (b, g, i, 0, 0, 0))),
    )
    assert S // SLC_KEY_TILE <= 8 and tq % SLC_KEY_TILE == 0
    return pl.pallas_call(
        functools.partial(_cmp_select_kernel, tq=tq, n_cmp=n_cmp, n_blocks=n_blocks,
                          n_rank=-(-n_blocks // 8) * 8),
        out_shape=(jax.ShapeDtypeStruct((T, B_HEADS * d), BF16),
                   jax.ShapeDtypeStruct((B, G, S, 128), BF16),
                   jax.ShapeDtypeStruct((B, G, nq, tq // SLC_KEY_TILE, 8, 128), I32)),
        grid_spec=grid_spec,
        compiler_params=_cparams(("parallel", "parallel", "arbitrary")),
        name="nsa_cmp_select",
    )(slopes, projm, cmp_kv, cmp_kv, overlap)


def _head_units(k_ref, v_ref, feat_ref, qa_ref, j, mask, qc):
    n_heads, tq, _ = qa_ref.shape
    k0 = pl.multiple_of(j * tq, tq)
    v = v_ref[pl.ds(k0, tq), :]

    def score_fn(hh, c):
        def fn():
            ka = jnp.concatenate([k_ref[pl.ds(k0, tq), :], feat_ref[pl.ds(k0, tq), :]], axis=1)
            sT = _nt_dot(ka, qa_ref[hh, c * qc:(c + 1) * qc, :])
            return sT if mask is None else sT + mask[:, c * qc:(c + 1) * qc]
        return fn

    return [(score_fn(hh, c), v, hh * (tq // qc) + c) for hh in range(n_heads) for c in range(tq // qc)]


def _write_heads_T(o_ref, acc_ref, l_ref):
    tq = o_ref.shape[0]
    qc = l_ref.shape[-1]
    for hh in range(B_HPG):
        for c in range(tq // qc):
            u = hh * (tq // qc) + c
            o_ref[c * qc:(c + 1) * qc, hh * B_D:(hh + 1) * B_D] = (acc_ref[u] / l_ref[u]).T.astype(o_ref.dtype)


def _flash_state(n_heads, tq, dv):
    qc = min(ATTN_QC, tq)
    n_units = n_heads * (tq // qc)
    return [pltpu.VMEM((n_units, 1, qc), F32), pltpu.VMEM((n_units, 1, qc), F32),
            pltpu.VMEM((n_units, dv, qc), F32)]


def _stage_queries(q_ref, qa_ref, slopes_ref, g, q0, sel_add):
    tq = q_ref.shape[0]
    lane = lax.broadcasted_iota(I32, (tq, 128), 1)
    for hh in range(B_HPG):
        coef = jnp.broadcast_to(_alibi_query_features(slopes_ref[g * B_HPG + hh] * LOG2E, q0), (tq, 128))
        if sel_add is not None:
            coef = jnp.where(lane < FEAT_BLK, sel_add, coef)
        qa_ref[hh] = jnp.concatenate([q_ref[:, hh * B_D:(hh + 1) * B_D], coef.astype(BF16)], axis=1)


def _slc_attn_kernel(slopes_ref, need_ref, q_ref, k_ref, v_ref, feat_ref, sel_ref, o_ref, qa_ref, m_ref, l_ref,
                     acc_ref, *, tq):
    b = pl.program_id(0)
    g = pl.program_id(1)
    i = pl.program_id(2)
    n = pl.num_programs(2)
    _flash_init(m_ref, l_ref, acc_ref)
    _stage_queries(q_ref, qa_ref, slopes_ref, g, i * tq, sel_ref[...].astype(F32))

    qc = m_ref.shape[-1]

    def units(j, causal=None):
        return _head_units(k_ref, v_ref, feat_ref, qa_ref, j, causal, qc)

    need_base = ((b * pl.num_programs(1) + g) * n + i) * n

    def needed(j):
        return (j == 0) | (need_ref[need_base + jnp.maximum(j, 0)] > 0)

    def body(p, carry):
        j0, j1 = 2 * p, 2 * p + 1
        n0, n1 = needed(j0), needed(j1)

        @pl.when(n0 & n1)
        def _():
            _flash_units(units(j0) + units(j1), m_ref, l_ref, acc_ref)

        @pl.when(n0 & jnp.logical_not(n1))
        def _():
            _flash_units(units(j0), m_ref, l_ref, acc_ref)

        @pl.when(jnp.logical_not(n0) & n1)
        def _():
            _flash_units(units(j1), m_ref, l_ref, acc_ref)

        return carry

    lax.fori_loop(0, lax.shift_right_logical(i, 1), body, 0)

    with_left = (i % 2 == 1) & needed(i - 1)

    @pl.when(with_left)
    def _():
        _flash_units(units(i - 1) + units(i, _tri_mask(tq, tq, True)), m_ref, l_ref, acc_ref)

    @pl.when(jnp.logical_not(with_left))
    def _():
        _flash_units(units(i, _tri_mask(tq, tq, True)), m_ref, l_ref, acc_ref)

    _write_heads_T(o_ref, acc_ref, l_ref)


def _slc_attention(projm, sel, need, B, S):
    T = B * S
    G, d = B_GROUPS, B_D
    tq = SLC_KEY_TILE
    n = S // tq
    slopes = _alibi_slopes(B_HEADS)
    ksb = (B_HEADS * d + 2 * G * d) // d
    vsb = ksb + G
    grid_spec = pltpu.PrefetchScalarGridSpec(
        num_scalar_prefetch=2,
        grid=(B, G, n),
        in_specs=[
            pl.BlockSpec((tq, B_HPG * d), lambda b, g, i, sl, nd: (b * n + i, g)),
            pl.BlockSpec((S, d), lambda b, g, i, sl, nd: (b, ksb + g)),
            pl.BlockSpec((S, d), lambda b, g, i, sl, nd: (b, vsb + g)),
            pl.BlockSpec((S, 128), lambda b, g, i, sl, nd: (0, 0)),
            pl.BlockSpec((None, None, tq, 128), lambda b, g, i, sl, nd: (b, g, i, 0)),
        ],
        out_specs=pl.BlockSpec((tq, B_HPG * d), lambda b, g, i, sl, nd: (b * n + i, g)),
        scratch_shapes=[pltpu.VMEM((B_HPG, tq, 2 * d), BF16)] + _flash_state(B_HPG, tq, d),
    )
    return pl.pallas_call(
        functools.partial(_slc_attn_kernel, tq=tq),
        out_shape=jax.ShapeDtypeStruct((T, B_HEADS * d), BF16),
        grid_spec=grid_spec,
        compiler_params=_cparams(("parallel", "parallel", "arbitrary")),
        name="nsa_selected_attention",
    )(slopes, need.reshape(-1), projm, projm, projm, _key_features(S), sel)


def _win_attn_kernel(slopes_ref, q_ref, k_ref, v_ref, feat_ref, gl_ref, oc_ref, os_ref, o_ref, qa_ref, m_ref,
                     l_ref, acc_ref, *, tq):
    g = pl.program_id(1)
    i = pl.program_id(2)
    _flash_init(m_ref, l_ref, acc_ref)
    _stage_queries(q_ref, qa_ref, slopes_ref, g, i * tq, None)

    qc = m_ref.shape[-1]

    def units(j, mask):
        return _head_units(k_ref, v_ref, feat_ref, qa_ref, j, mask, qc)

    @pl.when(i > 0)
    def _():
        _flash_units(units(i - 1, _tri_mask(tq, tq, False)) + units(i, _tri_mask(tq, tq, True)),
                     m_ref, l_ref, acc_ref)

    @pl.when(i == 0)
    def _():
        _flash_units(units(i, _tri_mask(tq, tq, True)), m_ref, l_ref, acc_ref)

    n_chunks = tq // qc
    for hh in range(B_HPG):
        row = (g * B_HPG + hh) * 3
        for c in range(n_chunks):
            qs = slice(c * qc, (c + 1) * qc)
            cols = slice(hh * B_D, (hh + 1) * B_D)
            u = hh * n_chunks + c
            gates = [jax.nn.sigmoid(gl_ref[pl.ds(row + br, 1), qs]) for br in range(3)]
            mix = (gates[0] * oc_ref[qs, cols].astype(F32).T + gates[1] * os_ref[qs, cols].astype(F32).T
                   + gates[2] * (acc_ref[u] / l_ref[u]))
            o_ref[qs, cols] = mix.T.astype(o_ref.dtype)


def _win_attention(projm, gate_logits_t, o_cmp, o_slc, B, S):
    T = B * S
    G, d = B_GROUPS, B_D
    tq = WINDOW
    n = S // tq
    slopes = _alibi_slopes(B_HEADS)
    kwb = (B_HEADS * d + 4 * G * d) // d
    vwb = kwb + G
    grid_spec = pltpu.PrefetchScalarGridSpec(
        num_scalar_prefetch=1,
        grid=(B, G, n),
        in_specs=[
            pl.BlockSpec((tq, B_HPG * d), lambda b, g, i, sl: (b * n + i, g)),
            pl.BlockSpec((S, d), lambda b, g, i, sl: (b, kwb + g)),
            pl.BlockSpec((S, d), lambda b, g, i, sl: (b, vwb + g)),
            pl.BlockSpec((S, 128), lambda b, g, i, sl: (0, 0)),
            pl.BlockSpec((128, tq), lambda b, g, i, sl: (0, b * n + i)),
            pl.BlockSpec((tq, B_HPG * d), lambda b, g, i, sl: (b * n + i, g)),
            pl.BlockSpec((tq, B_HPG * d), lambda b, g, i, sl: (b * n + i, g)),
        ],
        out_specs=pl.BlockSpec((tq, B_HPG * d), lambda b, g, i, sl: (b * n + i, g)),
        scratch_shapes=[pltpu.VMEM((B_HPG, tq, 2 * d), BF16)] + _flash_state(B_HPG, tq, d),
    )
    return pl.pallas_call(
        functools.partial(_win_attn_kernel, tq=tq),
        out_shape=jax.ShapeDtypeStruct((T, B_HEADS * d), BF16),
        grid_spec=grid_spec,
        compiler_params=_cparams(("parallel", "parallel", "arbitrary")),
        name="nsa_window_attention",
    )(slopes, projm, projm, projm, _key_features(S), gate_logits_t, o_cmp, o_slc)


def _overlap_matrix(S, n_rows):
    nc = (S - CMP_LEN) // CMP_STRIDE + 1
    nsb = S // SLC_BLOCK
    cs = np.arange(nc) * CMP_STRIDE
    ss = np.arange(nsb) * SLC_BLOCK
    ov = np.clip(np.minimum(cs[:, None] + CMP_LEN, ss[None, :] + SLC_BLOCK)
                 - np.maximum(cs[:, None], ss[None, :]), 0, None) / CMP_LEN
    out = np.zeros((128, n_rows), np.float32)
    out[:nsb, :nc] = ov.T
    return jnp.asarray(out, BF16)


def _nsa_attention(xn, w_in, cmp_pos, cmp_w1, cmp_w2, B, S):
    T = B * S
    H, G, d = B_HEADS, B_GROUPS, B_D
    n_main = H * d + 6 * G * d
    col_scale = jnp.concatenate([jnp.full((H * d,), d ** -0.5 * LOG2E, F32), jnp.ones((n_main - H * d,), F32)])
    w_t = jnp.swapaxes(w_in, 0, 1)
    projm = _matmul(xn, w_t, n_main, BF16, col_scale=col_scale, w_is_transposed=True, tm=2048)
    w_gate_t = jnp.pad(w_t[n_main:], ((0, 128 - 3 * H), (0, 0)))
    gate_logits = _matmul(xn, w_gate_t, 128, F32, tn=128, w_is_transposed=True)

    cmp_kv = _compress(projm, cmp_w1, cmp_w2, cmp_pos.reshape(2, 1, CMP_LEN * d), B, S)
    o_cmp, sel, need = _cmp_select(projm, cmp_kv, _overlap_matrix(S, S // CMP_STRIDE), B, S)
    n_t = S // SLC_KEY_TILE
    need = need[:, :, :, :, :n_t, 0].reshape(B, G, n_t, n_t)
    o_slc = _slc_attention(projm, sel, need, B, S)
    return _win_attention(projm, gate_logits.T, o_cmp, o_slc, B, S)


def kernel(x, attn_norm_w, ffn_norm_w, final_norm_w, a_w_in, a_lambda, a_subln_w, a_w_out, b_w_in,
           b_cmp_pos, b_cmp_w1, b_cmp_w2, b_w_out, moe_w_group, moe_b_group, moe_w_expert,
           moe_b_expert, moe_w_gate, moe_w_up, moe_w_down):
    B, S, D = x.shape
    T = B * S
    depth = attn_norm_w.shape[0]
    h = x
    xn = _rmsnorm(h, attn_norm_w[0], BF16)
    for i in range(depth):
        j = i // 2
        if i % 2 == 0:
            lambda_init = 0.8 - 0.6 * math.exp(-0.3 * i)
            n_q = A_HEADS * 2 * A_DH
            n_in = 2 * n_q + A_HEADS * A_DV
            col_scale = jnp.concatenate([jnp.full((n_q,), A_DH ** -0.5 * LOG2E, F32),
                                         jnp.ones((n_in - n_q,), F32)])
            qkv = _matmul(xn, a_w_in[j], n_in, BF16, col_scale=col_scale, tm=2048)
            o = _diff_attention(qkv, a_lambda[j], a_subln_w[j], lambda_init, B, S)
            h = _matmul(o, a_w_out[j], D, F32, resid=h)
        else:
            o = _nsa_attention(xn, b_w_in[j], b_cmp_pos[j], b_cmp_w1[j], b_cmp_w2[j], B, S)
            h = _matmul(o, b_w_out[j], D, F32, resid=h)
        last = i == depth - 1
        h, xn = _hier_moe(h, ffn_norm_w[i], moe_w_group[i], moe_b_group[i], moe_w_expert[i], moe_b_expert[i],
                          moe_w_gate, moe_w_up, moe_w_down, i,
                          final_norm_w if last else attn_norm_w[i + 1], F32 if last else BF16, not last,
                          (B, S, D) if last else None)
    return xn
```
